```python
import jax, jax.numpy as jnp
from jax import lax
import numpy as np

D_MODEL = 2048
BATCH = 8
SEQ = 2048
DEPTH = 4

HEAD_DIM = 128
DIL_GROUPS = ((128, 1), (512, 4), (2048, 16))
HEADS_PER_DIL = 4
N_DIL_HEADS = HEADS_PER_DIL * len(DIL_GROUPS)
N_FOX_HEADS = 4
N_HEADS = N_DIL_HEADS + N_FOX_HEADS
ATTN_WIDTH = N_HEADS * HEAD_DIM
BRANCH_A_WIDTH = HEADS_PER_DIL * HEAD_DIM
BRANCH_B_WIDTH = N_FOX_HEADS * HEAD_DIM
IN_COLS = 3 * ATTN_WIDTH + N_FOX_HEADS
D_FF = 4 * D_MODEL
PLE_DIM = 256
ROPE_THETA = 500000.0
ROPE_DIM = HEAD_DIM // 4
BLOCK = 128
NORM_EPS = 1e-6

kernel_name = "hybrid_dilated_fox_gated_block"


def rms_norm(x, g):
    xf = x.astype(jnp.float32)
    y = xf * lax.rsqrt(jnp.mean(xf * xf, axis=-1, keepdims=True) + NORM_EPS)
    return (y * g.astype(jnp.float32)).astype(x.dtype)


def partial_rope(x):
    S = x.shape[1]
    half = ROPE_DIM // 2
    inv = ROPE_THETA ** (-jnp.arange(half, dtype=jnp.float32) / half)
    ang = jnp.arange(S, dtype=jnp.float32)[:, None] * inv[None, :]
    cos = jnp.cos(ang)[None, :, None, :]
    sin = jnp.sin(ang)[None, :, None, :]
    xr = x[..., :ROPE_DIM].astype(jnp.float32)
    x1, x2 = xr[..., :half], xr[..., half:]
    rot = jnp.concatenate([x1 * cos - x2 * sin, x2 * cos + x1 * sin], axis=-1).astype(x.dtype)
    return jnp.concatenate([rot, x[..., ROPE_DIM:]], axis=-1)


def dilated_group_attention(q, k, v, window, dilation):
    B, S, H, Dh = q.shape
    span = window // dilation
    L = S // dilation
    nb = -(-L // BLOCK)
    Lp = nb * BLOCK

    def to_blocks(t):
        t = t.reshape(B, L, dilation, H, Dh).transpose(0, 2, 1, 3, 4)
        t = jnp.pad(t, ((0, 0), (0, 0), (0, Lp - L), (0, 0), (0, 0)))
        return t.reshape(B, dilation, nb, BLOCK, H, Dh)

    def with_prev(t):
        prev = jnp.pad(t[:, :, :-1], ((0, 0), (0, 0), (1, 0), (0, 0), (0, 0), (0, 0)))
        return jnp.concatenate([prev, t], axis=3)

    qb = to_blocks(q)
    kb = with_prev(to_blocks(k))
    vb = with_prev(to_blocks(v))
    s = jnp.einsum("brnqhd,brnkhd->brnhqk", qb, kb).astype(jnp.float32) * (Dh ** -0.5)
    qi = jnp.arange(BLOCK)[:, None]
    ki = jnp.arange(2 * BLOCK)[None, :]
    dist = BLOCK + qi - ki
    band = (dist >= 0) & (dist <= span)
    key_exists = (jnp.arange(nb) > 0)[:, None, None] | (ki >= BLOCK)[None]
    mask = (band[None] & key_exists)[:, None]
    s = jnp.where(mask, s, -jnp.inf)
    lse = jax.nn.logsumexp(s, axis=-1)
    prob = jnp.exp(s - lse[..., None]).astype(v.dtype)
    o = jnp.einsum("brnhqk,brnkhd->brnqhd", prob, vb)
    o = o.reshape(B, dilation, Lp, H, Dh)[:, :, :L].transpose(0, 2, 1, 3, 4).reshape(B, S, H, Dh)
    lse = lse.transpose(0, 1, 2, 4, 3).reshape(B, dilation, Lp, H)[:, :, :L]
    lse = lse.transpose(0, 2, 1, 3).reshape(B, S, H)
    return o, lse


def dilated_mixture(q, k, v):
    outs, lses = [], []
    for g, (window, dilation) in enumerate(DIL_GROUPS):
        sl = slice(g * HEADS_PER_DIL, (g + 1) * HEADS_PER_DIL)
        o, l = dilated_group_attention(q[:, :, sl], k[:, :, sl], v[:, :, sl], window, dilation)
        outs.append(o)
        lses.append(l)
    o = jnp.stack(outs, axis=0)
    w = jax.nn.softmax(jnp.stack(lses, axis=0), axis=0)
    return jnp.sum(w[..., None].astype(o.dtype) * o, axis=0)


def forgetting_attention(q, k, v, f_logit):
    B, S, H, Dh = q.shape
    nb = S // BLOCK
    c = jnp.cumsum(jax.nn.log_sigmoid(f_logit.astype(jnp.float32)), axis=1)
    c_keys = c.transpose(0, 2, 1)[:, :, None, :]
    qb = q.reshape(B, nb, BLOCK, H, Dh).transpose(1, 0, 2, 3, 4)
    cb = c.reshape(B, nb, BLOCK, H).transpose(1, 0, 3, 2)
    kpos = jnp.arange(S)
    scale = Dh ** -0.5

    def one_block(args):
        j, qj, cj = args
        s = jnp.einsum("bqhd,bkhd->bhqk", qj, k).astype(jnp.float32) * scale
        s = s + cj[..., None] - c_keys
        qpos = j * BLOCK + jnp.arange(BLOCK)
        s = jnp.where(kpos[None, :] <= qpos[:, None], s, -jnp.inf)
        prob = jax.nn.softmax(s, axis=-1).astype(v.dtype)
        return jnp.einsum("bhqk,bkhd->bqhd", prob, v)

    out = lax.map(one_block, (jnp.arange(nb), qb, cb))
    return out.transpose(1, 0, 2, 3, 4).reshape(B, S, H, Dh)


def hybrid_layer(h, p_i, g_mix, w_in, b_f, w_gate, b_gate, w_br_a, w_br_b, w_o,
                 g_mlp, w_up, w_down, g_ple, w_ple, w_ple_gate):
    B, S, _ = h.shape
    u = rms_norm(h, g_mix)
    z = u @ w_in
    q = z[..., :ATTN_WIDTH].reshape(B, S, N_HEADS, HEAD_DIM)
    k = z[..., ATTN_WIDTH:2 * ATTN_WIDTH].reshape(B, S, N_HEADS, HEAD_DIM)
    v = z[..., 2 * ATTN_WIDTH:3 * ATTN_WIDTH].reshape(B, S, N_HEADS, HEAD_DIM)
    f_logit = z[..., 3 * ATTN_WIDTH:] + b_f

    ya = dilated_mixture(partial_rope(q[:, :, :N_DIL_HEADS]), partial_rope(k[:, :, :N_DIL_HEADS]),
                         v[:, :, :N_DIL_HEADS])
    ya = ya.reshape(B, S, BRANCH_A_WIDTH) @ w_br_a
    yb = forgetting_attention(q[:, :, N_DIL_HEADS:], k[:, :, N_DIL_HEADS:], v[:, :, N_DIL_HEADS:], f_logit)
    yb = yb.reshape(B, S, BRANCH_B_WIDTH) @ w_br_b

    gates = jax.nn.sigmoid(u @ w_gate + b_gate)
    merged = gates[..., :D_MODEL] * ya + gates[..., D_MODEL:] * yb
    h = h + merged @ w_o

    m = rms_norm(h, g_mlp)
    h = h + jnp.square(jax.nn.relu(m @ w_up)) @ w_down

    ple_gate = jax.nn.sigmoid(rms_norm(h, g_ple) @ w_ple_gate)
    h = h + ple_gate * (p_i @ w_ple)
    return h


def _fwd_setup_inputs(seed: int = 0) -> dict:
    key = jax.random.key(seed)
    ks = jax.random.split(key, 20)
    f32 = jnp.float32

    def w(k, shape, fan_in):
        return jax.random.normal(k, shape, f32) * (fan_in ** -0.5)

    def gain(k, shape):
        return 1.0 + 0.02 * jax.random.normal(k, shape, f32)

    return {
        "x": jax.random.normal(ks[0], (BATCH, SEQ, D_MODEL), f32),
        "p": jax.random.normal(ks[1], (DEPTH, BATCH, SEQ, PLE_DIM), f32),
        "g_mix": gain(ks[2], (DEPTH, D_MODEL)),
        "w_in": w(ks[3], (DEPTH, D_MODEL, IN_COLS), D_MODEL),
        "b_f": 3.0 + 0.1 * jax.random.normal(ks[4], (DEPTH, N_FOX_HEADS), f32),
        "w_gate": w(ks[5], (DEPTH, D_MODEL, 2 * D_MODEL), D_MODEL),
        "b_gate": 0.1 * jax.random.normal(ks[6], (DEPTH, 2 * D_MODEL), f32),
        "w_br_a": w(ks[7], (DEPTH, BRANCH_A_WIDTH, D_MODEL), BRANCH_A_WIDTH),
        "w_br_b": w(ks[8], (DEPTH, BRANCH_B_WIDTH, D_MODEL), BRANCH_B_WIDTH),
        "w_o": w(ks[9], (DEPTH, D_MODEL, D_MODEL), D_MODEL),
        "g_mlp": gain(ks[10], (DEPTH, D_MODEL)),
        "w_up": w(ks[11], (DEPTH, D_MODEL, D_FF), D_MODEL),
        "w_down": w(ks[12], (DEPTH, D_FF, D_MODEL), D_FF),
        "g_ple": gain(ks[13], (DEPTH, D_MODEL)),
        "w_ple": w(ks[14], (DEPTH, PLE_DIM, D_MODEL), PLE_DIM),
        "w_ple_gate": w(ks[15], (DEPTH, D_MODEL, D_MODEL), D_MODEL),
        "g_final": gain(ks[16], (D_MODEL,)),
    }


def _fwd_reference(x, p, g_mix, w_in, b_f, w_gate, b_gate, w_br_a, w_br_b, w_o,
              g_mlp, w_up, w_down, g_ple, w_ple, w_ple_gate, g_final):
    h = x
    for i in range(DEPTH):
        h = hybrid_layer(h, p[i], g_mix[i], w_in[i], b_f[i], w_gate[i], b_gate[i], w_br_a[i], w_br_b[i],
                         w_o[i], g_mlp[i], w_up[i], w_down[i], g_ple[i], w_ple[i], w_ple_gate[i])
    return rms_norm(h, g_final)


import jax as _jax
import jax.numpy as _jnp

TWIN_FORMAT = 'train_step'
FWD_PARAMS = ['x', 'p', 'g_mix', 'w_in', 'b_f', 'w_gate', 'b_gate', 'w_br_a', 'w_br_b', 'w_o', 'g_mlp', 'w_up', 'w_down', 'g_ple', 'w_ple', 'w_ple_gate', 'g_final']
TWIN_WEIGHTS = ['g_mix', 'w_in', 'b_f', 'w_gate', 'b_gate', 'w_br_a', 'w_br_b', 'w_o', 'g_mlp', 'w_up', 'w_down', 'g_ple', 'w_ple', 'w_ple_gate', 'g_final']
TWIN_DIFF_INPUT = 'x'
TWIN_INPUTS = ['x', 'p', 'g_mix', 'w_in', 'b_f', 'w_gate', 'b_gate', 'w_br_a', 'w_br_b', 'w_o', 'g_mlp', 'w_up', 'w_down', 'g_ple', 'w_ple', 'w_ple_gate', 'g_final', 'loss_target', 'm_g_mix', 'm_w_in', 'm_b_f', 'm_w_gate', 'm_b_gate', 'm_w_br_a', 'm_w_br_b', 'm_w_o', 'm_g_mlp', 'm_w_up', 'm_w_down', 'm_g_ple', 'm_w_ple', 'm_w_ple_gate', 'm_g_final', 'v_g_mix', 'v_w_in', 'v_b_f', 'v_w_gate', 'v_b_gate', 'v_w_br_a', 'v_w_br_b', 'v_w_o', 'v_g_mlp', 'v_w_up', 'v_w_down', 'v_g_ple', 'v_w_ple', 'v_w_ple_gate', 'v_g_final']
TWIN_OUTPUTS = ['loss', 'grad_x', 'grad_g_mix', 'grad_w_in', 'grad_b_f', 'grad_w_gate', 'grad_b_gate', 'grad_w_br_a', 'grad_w_br_b', 'grad_w_o', 'grad_g_mlp', 'grad_w_up', 'grad_w_down', 'grad_g_ple', 'grad_w_ple', 'grad_w_ple_gate', 'grad_g_final', 'delta_g_mix', 'delta_w_in', 'delta_b_f', 'delta_w_gate', 'delta_b_gate', 'delta_w_br_a', 'delta_w_br_b', 'delta_w_o', 'delta_g_mlp', 'delta_w_up', 'delta_w_down', 'delta_g_ple', 'delta_w_ple', 'delta_w_ple_gate', 'delta_g_final', 'new_m_g_mix', 'new_m_w_in', 'new_m_b_f', 'new_m_w_gate', 'new_m_b_gate', 'new_m_w_br_a', 'new_m_w_br_b', 'new_m_w_o', 'new_m_g_mlp', 'new_m_w_up', 'new_m_w_down', 'new_m_g_ple', 'new_m_w_ple', 'new_m_w_ple_gate', 'new_m_g_final', 'new_v_g_mix', 'new_v_w_in', 'new_v_b_f', 'new_v_w_gate', 'new_v_b_gate', 'new_v_w_br_a', 'new_v_w_br_b', 'new_v_w_o', 'new_v_g_mlp', 'new_v_w_up', 'new_v_w_down', 'new_v_g_ple', 'new_v_w_ple', 'new_v_w_ple_gate', 'new_v_g_final']
TWIN_LEAF_KINDS = {'loss': 'loss', 'grad_x': 'grad_x', 'grad_g_mix': 'grad_w', 'grad_w_in': 'grad_w', 'grad_b_f': 'grad_w', 'grad_w_gate': 'grad_w', 'grad_b_gate': 'grad_w', 'grad_w_br_a': 'grad_w', 'grad_w_br_b': 'grad_w', 'grad_w_o': 'grad_w', 'grad_g_mlp': 'grad_w', 'grad_w_up': 'grad_w', 'grad_w_down': 'grad_w', 'grad_g_ple': 'grad_w', 'grad_w_ple': 'grad_w', 'grad_w_ple_gate': 'grad_w', 'grad_g_final': 'grad_w', 'delta_g_mix': 'delta_w', 'delta_w_in': 'delta_w', 'delta_b_f': 'delta_w', 'delta_w_gate': 'delta_w', 'delta_b_gate': 'delta_w', 'delta_w_br_a': 'delta_w', 'delta_w_br_b': 'delta_w', 'delta_w_o': 'delta_w', 'delta_g_mlp': 'delta_w', 'delta_w_up': 'delta_w', 'delta_w_down': 'delta_w', 'delta_g_ple': 'delta_w', 'delta_w_ple': 'delta_w', 'delta_w_ple_gate': 'delta_w', 'delta_g_final': 'delta_w', 'new_m_g_mix': 'new_m', 'new_m_w_in': 'new_m', 'new_m_b_f': 'new_m', 'new_m_w_gate': 'new_m', 'new_m_b_gate': 'new_m', 'new_m_w_br_a': 'new_m', 'new_m_w_br_b': 'new_m', 'new_m_w_o': 'new_m', 'new_m_g_mlp': 'new_m', 'new_m_w_up': 'new_m', 'new_m_w_down': 'new_m', 'new_m_g_ple': 'new_m', 'new_m_w_ple': 'new_m', 'new_m_w_ple_gate': 'new_m', 'new_m_g_final': 'new_m', 'new_v_g_mix': 'new_v', 'new_v_w_in': 'new_v', 'new_v_b_f': 'new_v', 'new_v_w_gate': 'new_v', 'new_v_b_gate': 'new_v', 'new_v_w_br_a': 'new_v', 'new_v_w_br_b': 'new_v', 'new_v_w_o': 'new_v', 'new_v_g_mlp': 'new_v', 'new_v_w_up': 'new_v', 'new_v_w_down': 'new_v', 'new_v_g_ple': 'new_v', 'new_v_w_ple': 'new_v', 'new_v_w_ple_gate': 'new_v', 'new_v_g_final': 'new_v'}


def _forward(args):
    return _fwd_reference(*[args[k] for k in FWD_PARAMS])


def _output_shape():
    out = _jax.eval_shape(lambda: _forward(_fwd_setup_inputs(0)))
    return out.shape, out.dtype

N_MICROBATCH = 1
ADAM_LR = 0.001
ADAM_B1 = 0.9
ADAM_B2 = 0.999
ADAM_EPS = 1e-08
ADAM_WD = 0.01
ADAM_STEP = 10
PER_EXAMPLE_BATCH_AXIS = {'x': 0, 'p': 1, 'loss_target': 0}
SHARED_INPUTS = []
_WEIGHT_DTYPES = {'g_mix': _jnp.float32, 'w_in': _jnp.float32, 'b_f': _jnp.float32, 'w_gate': _jnp.float32, 'b_gate': _jnp.float32, 'w_br_a': _jnp.float32, 'w_br_b': _jnp.float32, 'w_o': _jnp.float32, 'g_mlp': _jnp.float32, 'w_up': _jnp.float32, 'w_down': _jnp.float32, 'g_ple': _jnp.float32, 'w_ple': _jnp.float32, 'w_ple_gate': _jnp.float32, 'g_final': _jnp.float32}
MOMENT_SCALE = {'g_mix': 1.548904e-02, 'w_in': 8.604801e-03, 'b_f': 2.998523e-01, 'w_gate': 2.944716e-03, 'b_gate': 3.028894e-03, 'w_br_a': 5.977164e-03, 'w_br_b': 9.704437e-03, 'w_o': 1.103679e-02, 'g_mlp': 4.405600e-02, 'w_up': 2.167653e-02, 'w_down': 4.231588e-02, 'g_ple': 6.266008e-03, 'w_ple': 1.613240e-02, 'w_ple_gate': 6.378489e-03, 'g_final': 8.218083e+00}


def _to_microbatches(a, axis):
    t = _jnp.moveaxis(a, axis, 0)
    t = t.reshape((N_MICROBATCH, t.shape[0] // N_MICROBATCH) + t.shape[1:])
    return _jnp.moveaxis(t, 1, axis + 1)


def setup_inputs(seed: int = 0) -> dict:
    inp = _fwd_setup_inputs(seed)
    key = _jax.random.fold_in(_jax.random.key(seed), 7919)
    shape, _ = _output_shape()
    out = dict(inp)
    out["loss_target"] = _jax.random.normal(_jax.random.fold_in(key, 0), shape, _jnp.float32)
    for i, name in enumerate(TWIN_WEIGHTS):
        w = inp[name].astype(_jnp.float32)
        if MOMENT_SCALE is None:
            s = _jnp.sqrt(_jnp.mean(_jnp.square(w)) + 1e-30)
        else:
            s = MOMENT_SCALE[name]
        km, kv = _jax.random.split(_jax.random.fold_in(key, i + 1))
        out[name] = w
        out["m_" + name] = s * _jax.random.normal(km, w.shape, _jnp.float32)
        out["v_" + name] = (s * s) * _jax.random.uniform(kv, w.shape, _jnp.float32, 0.5, 1.5)
    if N_MICROBATCH > 1:
        for name, axis in PER_EXAMPLE_BATCH_AXIS.items():
            out[name] = _to_microbatches(out[name], axis)
    return {'x': out['x'], 'p': out['p'], 'g_mix': out['g_mix'], 'w_in': out['w_in'], 'b_f': out['b_f'], 'w_gate': out['w_gate'], 'b_gate': out['b_gate'], 'w_br_a': out['w_br_a'], 'w_br_b': out['w_br_b'], 'w_o': out['w_o'], 'g_mlp': out['g_mlp'], 'w_up': out['w_up'], 'w_down': out['w_down'], 'g_ple': out['g_ple'], 'w_ple': out['w_ple'], 'w_ple_gate': out['w_ple_gate'], 'g_final': out['g_final'], 'loss_target': out['loss_target'], 'm_g_mix': out['m_g_mix'], 'm_w_in': out['m_w_in'], 'm_b_f': out['m_b_f'], 'm_w_gate': out['m_w_gate'], 'm_b_gate': out['m_b_gate'], 'm_w_br_a': out['m_w_br_a'], 'm_w_br_b': out['m_w_br_b'], 'm_w_o': out['m_w_o'], 'm_g_mlp': out['m_g_mlp'], 'm_w_up': out['m_w_up'], 'm_w_down': out['m_w_down'], 'm_g_ple': out['m_g_ple'], 'm_w_ple': out['m_w_ple'], 'm_w_ple_gate': out['m_w_ple_gate'], 'm_g_final': out['m_g_final'], 'v_g_mix': out['v_g_mix'], 'v_w_in': out['v_w_in'], 'v_b_f': out['v_b_f'], 'v_w_gate': out['v_w_gate'], 'v_b_gate': out['v_b_gate'], 'v_w_br_a': out['v_w_br_a'], 'v_w_br_b': out['v_w_br_b'], 'v_w_o': out['v_w_o'], 'v_g_mlp': out['v_g_mlp'], 'v_w_up': out['v_w_up'], 'v_w_down': out['v_w_down'], 'v_g_ple': out['v_g_ple'], 'v_w_ple': out['v_w_ple'], 'v_w_ple_gate': out['v_w_ple_gate'], 'v_g_final': out['v_g_final']}


def _loss(weights, diff, rest, loss_target):
    with _jax.named_scope("forward"):
        args = {**rest, TWIN_DIFF_INPUT: diff, **{k: w.astype(_WEIGHT_DTYPES[k]) for k, w in weights.items()}}
        y = _forward(args)
    with _jax.named_scope("loss_head"):
        err = _jnp.square(y.astype(_jnp.float32) - loss_target)
        return 0.5 * _jnp.sum(_jnp.mean(err, axis=-1)) if err.ndim else 0.5 * err


def _adamw(w, g, m, v):
    m = ADAM_B1 * m + (1.0 - ADAM_B1) * g
    v = ADAM_B2 * v + (1.0 - ADAM_B2) * _jnp.square(g)
    m_hat = m / (1.0 - ADAM_B1 ** ADAM_STEP)
    v_hat = v / (1.0 - ADAM_B2 ** ADAM_STEP)
    delta = -ADAM_LR * (m_hat / (_jnp.sqrt(v_hat) + ADAM_EPS) + ADAM_WD * w)
    return delta, m, v


def reference(x, p, g_mix, w_in, b_f, w_gate, b_gate, w_br_a, w_br_b, w_o, g_mlp, w_up, w_down, g_ple, w_ple, w_ple_gate, g_final, loss_target, m_g_mix, m_w_in, m_b_f, m_w_gate, m_b_gate, m_w_br_a, m_w_br_b, m_w_o, m_g_mlp, m_w_up, m_w_down, m_g_ple, m_w_ple, m_w_ple_gate, m_g_final, v_g_mix, v_w_in, v_b_f, v_w_gate, v_b_gate, v_w_br_a, v_w_br_b, v_w_o, v_g_mlp, v_w_up, v_w_down, v_g_ple, v_w_ple, v_w_ple_gate, v_g_final):
    given = dict(x=x, p=p, g_mix=g_mix, w_in=w_in, b_f=b_f, w_gate=w_gate, b_gate=b_gate, w_br_a=w_br_a, w_br_b=w_br_b, w_o=w_o, g_mlp=g_mlp, w_up=w_up, w_down=w_down, g_ple=g_ple, w_ple=w_ple, w_ple_gate=w_ple_gate, g_final=g_final, loss_target=loss_target, m_g_mix=m_g_mix, m_w_in=m_w_in, m_b_f=m_b_f, m_w_gate=m_w_gate, m_b_gate=m_b_gate, m_w_br_a=m_w_br_a, m_w_br_b=m_w_br_b, m_w_o=m_w_o, m_g_mlp=m_g_mlp, m_w_up=m_w_up, m_w_down=m_w_down, m_g_ple=m_g_ple, m_w_ple=m_w_ple, m_w_ple_gate=m_w_ple_gate, m_g_final=m_g_final, v_g_mix=v_g_mix, v_w_in=v_w_in, v_b_f=v_b_f, v_w_gate=v_w_gate, v_b_gate=v_b_gate, v_w_br_a=v_w_br_a, v_w_br_b=v_w_br_b, v_w_o=v_w_o, v_g_mlp=v_g_mlp, v_w_up=v_w_up, v_w_down=v_w_down, v_g_ple=v_g_ple, v_w_ple=v_w_ple, v_w_ple_gate=v_w_ple_gate, v_g_final=v_g_final)
    weights = {n: given[n] for n in TWIN_WEIGHTS}
    shared = {n: given[n] for n in SHARED_INPUTS}
    per_example = {n: given[n] for n in ['x', 'p']}
    grad_fn = _jax.value_and_grad(_loss, argnums=(0, 1))

    def one_microbatch(ex, loss_target):
        ex = dict(ex)
        diff = ex.pop(TWIN_DIFF_INPUT)
        return grad_fn(weights, diff, {**shared, **ex}, loss_target)

    if N_MICROBATCH == 1:
        loss, (grad_w, grad_x) = one_microbatch(per_example, given["loss_target"])
    else:
        def body(carry, xs):
            loss_sum, grad_sum = carry
            l_k, (gw_k, gx_k) = one_microbatch(xs[0], xs[1])
            with _jax.named_scope("update"):
                return (loss_sum + l_k, _jax.tree.map(_jnp.add, grad_sum, gw_k)), gx_k

        init = (_jnp.zeros((), _jnp.float32), _jax.tree.map(_jnp.zeros_like, weights))
        (loss, grad_w), grad_x = _jax.lax.scan(body, init, (per_example, given["loss_target"]))
    with _jax.named_scope("update"):
        delta_w, new_m, new_v = {}, {}, {}
        for n in TWIN_WEIGHTS:
            delta_w[n], new_m[n], new_v[n] = _adamw(weights[n], grad_w[n], given["m_" + n], given["v_" + n])
    return (loss, grad_x, *[grad_w[n] for n in TWIN_WEIGHTS], *[delta_w[n] for n in TWIN_WEIGHTS],
            *[new_m[n] for n in TWIN_WEIGHTS], *[new_v[n] for n in TWIN_WEIGHTS])
```

```python
import functools

import jax
import jax.numpy as jnp
from jax import lax
from jax.experimental import pallas as pl
from jax.experimental.pallas import tpu as pltpu

F32 = jnp.float32
BF16 = jnp.bfloat16

HEAD_DIM = 128
N_HEADS = 16
N_DIL_HEADS = 12
N_FOX_HEADS = 4
HEADS_PER_DIL = 4
DILATIONS = (1, 4, 16)
BLOCK = 128
ATTN_WIDTH = N_HEADS * HEAD_DIM
DIL_WIDTH = N_DIL_HEADS * HEAD_DIM
FOX_WIDTH = N_FOX_HEADS * HEAD_DIM
ROPE_THETA = 500000.0
ROPE_HALF = 16
NORM_EPS = 1e-6
SCALE = HEAD_DIM ** -0.5
NEG = -1e30

ADAM_LR = 0.001
ADAM_B1 = 0.9
ADAM_B2 = 0.999
ADAM_EPS = 1e-08
ADAM_WD = 0.01
ADAM_STEP = 10

N_CHIPS = 4
V7X_VMEM_LIMIT_BYTES = 56 * 1024 * 1024
LANES = 128
MESH = pl.DeviceIdType.MESH
ANY = pl.BlockSpec(memory_space=pl.ANY)


def _params(sem):
    return pltpu.CompilerParams(dimension_semantics=sem, vmem_limit_bytes=V7X_VMEM_LIMIT_BYTES)


def _tile(n, pref):
    if n <= pref:
        return n
    t = (pref // LANES) * LANES
    while t > LANES and n % t:
        t -= LANES
    assert n % t == 0, (n, pref)
    return t


def _mm(a, b, *, mode, name, out_dtypes=(BF16,), epilogue=None, extras=(), b_chunked=False,
        out_chunks=0, ti=2048, tj=512, tc=2048):
    if mode == "tn":
        C, I = a.shape
    else:
        I, C = a.shape
    if b_chunked:
        nch, d0, n = b.shape
        if mode == "nn":
            assert d0 == C
            J = nch * n
        else:
            assert mode == "nt" and nch * n == C
            J = d0
    elif mode == "nt":
        J = b.shape[0]
        assert b.shape[1] == C
    else:
        assert b.shape[0] == C
        J = b.shape[1]
    ti, tc = _tile(I, ti), _tile(C, tc)
    if b_chunked and mode == "nn":
        tj = _tile(n, tj)
    elif out_chunks:
        tj = _tile(J // out_chunks, tj)
    else:
        tj = _tile(J, tj)
    if b_chunked and mode == "nt":
        tc = _tile(n, tc)
    ni, nj, nc = I // ti, J // tj, C // tc

    if mode == "tn":
        a_spec = pl.BlockSpec((tc, ti), lambda i, j, c: (c, i))
        dims = (((0,), (0,)), ((), ()))
    else:
        a_spec = pl.BlockSpec((ti, tc), lambda i, j, c: (i, c))
        dims = (((1,), (0,)), ((), ())) if mode == "nn" else (((1,), (1,)), ((), ()))
    if mode == "nt":
        if b_chunked:
            cb = n // tc
            b_spec = pl.BlockSpec((None, tj, tc), lambda i, j, c: (c // cb, j, c % cb))
        else:
            b_spec = pl.BlockSpec((tj, tc), lambda i, j, c: (j, c))
    else:
        if b_chunked:
            jb = n // tj
            b_spec = pl.BlockSpec((None, tc, tj), lambda i, j, c: (j // jb, c, j % jb))
        else:
            b_spec = pl.BlockSpec((tc, tj), lambda i, j, c: (c, j))
    extra_specs = []
    for arr, kind in extras:
        if kind == "tile":
            assert arr.shape == (I, J), (arr.shape, I, J)
            extra_specs.append(pl.BlockSpec((ti, tj), lambda i, j, c: (i, j)))
        else:
            assert arr.shape == (1, J)
            extra_specs.append(pl.BlockSpec((1, tj), lambda i, j, c: (0, j)))
    if out_chunks:
        ob = (J // out_chunks) // tj
        out_spec = pl.BlockSpec((None, ti, tj), lambda i, j, c: (j // ob, i, j % ob))
        out_shape = [jax.ShapeDtypeStruct((out_chunks, I, J // out_chunks), d) for d in out_dtypes]
    else:
        out_spec = pl.BlockSpec((ti, tj), lambda i, j, c: (i, j))
        out_shape = [jax.ShapeDtypeStruct((I, J), d) for d in out_dtypes]
    ne, no = len(extras), len(out_dtypes)
    if epilogue is None:
        epilogue = lambda acc: (acc,)

    def body(a_ref, b_ref, *rest):
        extra_refs, out_refs = rest[:ne], rest[ne:ne + no]

        def finish(acc):
            outs = epilogue(acc, *[r[...] for r in extra_refs])
            for o_ref, val in zip(out_refs, outs):
                o_ref[...] = val.astype(o_ref.dtype)

        part = lax.dot_general(a_ref[...], b_ref[...], dims, preferred_element_type=F32)
        if nc == 1:
            finish(part)
        else:
            acc_ref = rest[-1]
            k = pl.program_id(2)

            @pl.when(k == 0)
            def _():
                acc_ref[...] = part

            @pl.when(k > 0)
            def _():
                acc_ref[...] += part

            @pl.when(k == nc - 1)
            def _():
                finish(acc_ref[...])

    outs = pl.pallas_call(
        body, name=name, grid=(ni, nj, nc),
        in_specs=[a_spec, b_spec] + extra_specs,
        out_specs=[out_spec] * no, out_shape=out_shape,
        scratch_shapes=[pltpu.VMEM((ti, tj), F32)] if nc > 1 else [],
        compiler_params=_params(("parallel", "parallel", "arbitrary")),
    )(a, b, *[e[0] for e in extras])
    return outs[0] if no == 1 else tuple(outs)


def _ew(fn, ins, outs, grid, name):
    n_in = len(ins)
    has_acc = any(o[3] for o in outs)
    assert not has_acc or len(grid) == 1

    def body(*refs):
        vals = fn(*[r[...] for r in refs[:n_in]])
        for o_ref, o, val in zip(refs[n_in:], outs, vals):
            if o[3]:
                step = pl.program_id(0)

                @pl.when(step == 0)
                def _(o_ref=o_ref, val=val):
                    o_ref[...] = val

                @pl.when(step > 0)
                def _(o_ref=o_ref, val=val):
                    o_ref[...] += val
            else:
                o_ref[...] = val.astype(o_ref.dtype)

    sem = ("arbitrary",) if has_acc else ("parallel",) * len(grid)
    res = pl.pallas_call(
        body, name=name, grid=grid,
        in_specs=[i[1] for i in ins], out_specs=[o[2] for o in outs],
        out_shape=[jax.ShapeDtypeStruct(o[0], o[1]) for o in outs],
        compiler_params=_params(sem),
    )(*[i[0] for i in ins])
    return res[0] if len(outs) == 1 else tuple(res)


def _rows(tr, w):
    return pl.BlockSpec((tr, w), lambda i: (i, 0))


def _bcast(w):
    return pl.BlockSpec((1, w), lambda i: (0, 0))


def _row_tile(S, width_bytes):
    tr = 512
    while tr > 16 and tr * width_bytes > 2 * 1024 * 1024:
        tr //= 2
    return min(tr, S)


def _rms_fwd(h, g, name):
    S, D = h.shape
    tr = _row_tile(S, D * 4)

    def fn(x, gg):
        r = lax.rsqrt(jnp.mean(x * x, axis=-1, keepdims=True) + NORM_EPS)
        return (x * r * gg,)

    return _ew(fn, [(h, _rows(tr, D)), (g, _bcast(D))], [((S, D), BF16, _rows(tr, D), False)],
               (S // tr,), name)


def _rms_bwd(x, g, dy, dres, name):
    S, D = x.shape
    tr = _row_tile(S, D * 4)

    def fn(xv, gg, dyv, dr):
        r = lax.rsqrt(jnp.mean(xv * xv, axis=-1, keepdims=True) + NORM_EPS)
        dyf = dyv.astype(F32)
        gy = dyf * gg
        dx = r * gy - xv * (r * r * r) * jnp.mean(xv * gy, axis=-1, keepdims=True)
        tot = dr + dx
        dg = jnp.sum(dyf * xv * r, axis=0, keepdims=True)
        return tot, tot, dg

    return _ew(fn, [(x, _rows(tr, D)), (g, _bcast(D)), (dy, _rows(tr, D)), (dres, _rows(tr, D))],
               [((S, D), F32, _rows(tr, D), False), ((S, D), BF16, _rows(tr, D), False),
                ((1, D), F32, _bcast(D), True)], (S // tr,), name)


def _loss_head(h, g, target, name):
    S, D = h.shape
    tr = _row_tile(S, D * 4)

    def fn(xv, gg, tgt):
        r = lax.rsqrt(jnp.mean(xv * xv, axis=-1, keepdims=True) + NORM_EPS)
        y = xv * r * gg
        e = y - tgt
        loss = 0.5 * jnp.sum(jnp.mean(e * e, axis=-1, keepdims=True), axis=0, keepdims=True)
        dy = e * (1.0 / D)
        gy = dy * gg
        dx = r * gy - xv * (r * r * r) * jnp.mean(xv * gy, axis=-1, keepdims=True)
        dg = jnp.sum(dy * xv * r, axis=0, keepdims=True)
        return dx, jnp.broadcast_to(loss, (1, LANES)), dg

    return _ew(fn, [(h, _rows(tr, D)), (g, _bcast(D)), (target, _rows(tr, D))],
               [((S, D), F32, _rows(tr, D), False), ((1, LANES), F32, _bcast(LANES), True),
                ((1, D), F32, _bcast(D), True)], (S // tr,), name)


def _rope_tables(S):
    inv = ROPE_THETA ** (-jnp.arange(ROPE_HALF, dtype=F32) / ROPE_HALF)
    ang = jnp.arange(S, dtype=F32)[:, None] * inv[None, :]
    cos, sin = jnp.cos(ang), jnp.sin(ang)
    rest = HEAD_DIM - 2 * ROPE_HALF
    ctab = jnp.concatenate([cos, cos, jnp.ones((S, rest), F32)], axis=1)
    stab = jnp.concatenate([-sin, sin, jnp.zeros((S, rest), F32)], axis=1)
    return ctab, stab


def _swap_halves(x):
    lane = lax.broadcasted_iota(jnp.int32, x.shape, 1)
    return jnp.where(lane < ROPE_HALF, pltpu.roll(x, HEAD_DIM - ROPE_HALF, 1), pltpu.roll(x, ROPE_HALF, 1))


def _rope(q_src, q_col0, k_src, k_col0, ctab, stab, sign, name):
    S = q_src.shape[0]
    tr = min(512, S)

    def fn(q, k, ct, st):
        outs = []
        for v in (q, k):
            vf = v.astype(F32)
            outs.append(vf * ct + sign * _swap_halves(vf) * st)
        return tuple(outs)

    head = lambda c0: pl.BlockSpec((tr, HEAD_DIM), lambda i, h: (i, c0 + h))
    tab = pl.BlockSpec((tr, HEAD_DIM), lambda i, h: (i, 0))
    out = ((S, DIL_WIDTH), BF16, head(0), False)
    return _ew(fn, [(q_src, head(q_col0)), (k_src, head(k_col0)), (ctab, tab), (stab, tab)],
               [out, out], (S // tr, N_DIL_HEADS), name)


SW_BLOCKS_PER_STEP = 8


def _to_strided(x):
    S = x.shape[0]
    parts = []
    for g, d in enumerate(DILATIONS):
        xg = x[:, g * 512:(g + 1) * 512].reshape(S // d, d, HEADS_PER_DIL, HEAD_DIM)
        parts.append(xg.transpose(1, 2, 0, 3).reshape(-1, BLOCK, HEAD_DIM))
    return jnp.concatenate(parts, axis=0)


def _from_strided(y, S):
    per = y.shape[0] // len(DILATIONS)
    parts = []
    for g, d in enumerate(DILATIONS):
        yg = y[g * per:(g + 1) * per].reshape(d, HEADS_PER_DIL, S // d, HEAD_DIM)
        parts.append(yg.transpose(2, 0, 1, 3).reshape(S, HEADS_PER_DIL * HEAD_DIM))
    return jnp.stack(parts, axis=0)


def _seq_blocks(b0, per_group):
    g = b0 // per_group
    n0 = per_group // HEADS_PER_DIL
    return jnp.where(g == 0, n0, jnp.where(g == 1, n0 // 4, n0 // 16))


def _sw_masks():
    qi = lax.broadcasted_iota(jnp.int32, (BLOCK, BLOCK), 0)
    ki = lax.broadcasted_iota(jnp.int32, (BLOCK, BLOCK), 1)
    return qi >= ki, qi <= ki


def _sw_fwd(q, k, v, name):
    NB = q.shape[0]
    T = SW_BLOCKS_PER_STEP
    per_group = NB // len(DILATIONS)
    nt = (((1,), (1,)), ((), ()))

    def body(q_ref, k_ref, v_ref, kp_ref, vp_ref, o_ref, lse_ref):
        b0 = pl.program_id(0) * T
        nseq = _seq_blocks(b0, per_group)
        cur_mask, prev_mask = _sw_masks()
        for t in range(T):
            has_prev = ((b0 + t) & (nseq - 1)) != 0
            qt = q_ref[t]
            kp = kp_ref[0] if t == 0 else k_ref[t - 1]
            vp = vp_ref[0] if t == 0 else v_ref[t - 1]
            s_c = lax.dot_general(qt, k_ref[t], nt, preferred_element_type=F32) * SCALE
            s_p = lax.dot_general(qt, kp, nt, preferred_element_type=F32) * SCALE
            s_c = jnp.where(cur_mask, s_c, NEG)
            s_p = jnp.where(prev_mask, s_p, NEG) + jnp.where(has_prev, 0.0, NEG)
            m = jnp.maximum(jnp.max(s_c, axis=-1, keepdims=True), jnp.max(s_p, axis=-1, keepdims=True))
            p_c = jnp.exp(s_c - m)
            p_p = jnp.exp(s_p - m)
            l = jnp.sum(p_c, axis=-1, keepdims=True) + jnp.sum(p_p, axis=-1, keepdims=True)
            o = (jnp.dot(p_c.astype(BF16), v_ref[t], preferred_element_type=F32)
                 + jnp.dot(p_p.astype(BF16), vp, preferred_element_type=F32))
            o_ref[t] = (o / l).astype(o_ref.dtype)
            lse_ref[t] = jnp.broadcast_to(m + jnp.log(l), (BLOCK, HEAD_DIM))

    tile = pl.BlockSpec((T, BLOCK, HEAD_DIM), lambda i: (i, 0, 0))
    before = pl.BlockSpec((1, BLOCK, HEAD_DIM), lambda i: (jnp.maximum(i * T - 1, 0), 0, 0))
    return pl.pallas_call(
        body, name=name, grid=(NB // T,),
        in_specs=[tile, tile, tile, before, before], out_specs=[tile, tile],
        out_shape=[jax.ShapeDtypeStruct(q.shape, BF16), jax.ShapeDtypeStruct(q.shape, F32)],
        compiler_params=_params(("parallel",)),
    )(q, k, v, k, v)


def _sw_bwd(q, k, v, do, lse, tt, name):
    NB = q.shape[0]
    T = SW_BLOCKS_PER_STEP
    per_group = NB // len(DILATIONS)
    nt = (((1,), (1,)), ((), ()))
    tn = (((0,), (0,)), ((), ()))

    def body(q_ref, k_ref, v_ref, do_ref, lse_ref, tt_ref, kp_ref, vp_ref, qn_ref, don_ref, lsen_ref,
             ttn_ref, dq_ref, dk_ref, dv_ref):
        b0 = pl.program_id(0) * T
        nseq = _seq_blocks(b0, per_group)
        cur_mask, prev_mask = _sw_masks()

        def probs(qq, kk, lse_b, mask, gate):
            s = lax.dot_general(qq, kk, nt, preferred_element_type=F32) * SCALE
            return jnp.exp(jnp.where(mask, s, NEG) + gate - lse_b)

        for t in range(T):
            has_prev = jnp.where(((b0 + t) & (nseq - 1)) != 0, 0.0, NEG)
            has_next = jnp.where(((b0 + t + 1) & (nseq - 1)) != 0, 0.0, NEG)
            last = t == T - 1
            qt, kt, vt, dot = q_ref[t], k_ref[t], v_ref[t], do_ref[t]
            kp = kp_ref[0] if t == 0 else k_ref[t - 1]
            vp = vp_ref[0] if t == 0 else v_ref[t - 1]
            qn = qn_ref[0] if last else q_ref[t + 1]
            don = don_ref[0] if last else do_ref[t + 1]
            lsen = lsen_ref[0] if last else lse_ref[t + 1]
            ttn = ttn_ref[0] if last else tt_ref[t + 1]
            p_cc = probs(qt, kt, lse_ref[t], cur_mask, 0.0)
            p_cp = probs(qt, kp, lse_ref[t], prev_mask, has_prev)
            p_nc = probs(qn, kt, lsen, prev_mask, has_next)
            ds_cc = p_cc * (lax.dot_general(dot, vt, nt, preferred_element_type=F32) + tt_ref[t])
            ds_cp = p_cp * (lax.dot_general(dot, vp, nt, preferred_element_type=F32) + tt_ref[t])
            ds_nc = p_nc * (lax.dot_general(don, vt, nt, preferred_element_type=F32) + ttn)
            ds_cc, ds_cp, ds_nc = ds_cc.astype(BF16), ds_cp.astype(BF16), ds_nc.astype(BF16)
            dq = (jnp.dot(ds_cc, kt, preferred_element_type=F32)
                  + jnp.dot(ds_cp, kp, preferred_element_type=F32))
            dk = (lax.dot_general(ds_cc, qt, tn, preferred_element_type=F32)
                  + lax.dot_general(ds_nc, qn, tn, preferred_element_type=F32))
            dv = (lax.dot_general(p_cc.astype(BF16), dot, tn, preferred_element_type=F32)
                  + lax.dot_general(p_nc.astype(BF16), don, tn, preferred_element_type=F32))
            dq_ref[t] = (dq * SCALE).astype(BF16)
            dk_ref[t] = (dk * SCALE).astype(BF16)
            dv_ref[t] = dv.astype(BF16)

    tile = pl.BlockSpec((T, BLOCK, HEAD_DIM), lambda i: (i, 0, 0))
    before = pl.BlockSpec((1, BLOCK, HEAD_DIM), lambda i: (jnp.maximum(i * T - 1, 0), 0, 0))
    after = pl.BlockSpec((1, BLOCK, HEAD_DIM), lambda i: (jnp.minimum(i * T + T, NB - 1), 0, 0))
    out = jax.ShapeDtypeStruct(q.shape, BF16)
    return pl.pallas_call(
        body, name=name, grid=(NB // T,),
        in_specs=[tile] * 6 + [before, before, after, after, after, after],
        out_specs=[tile] * 3, out_shape=[out] * 3,
        compiler_params=_params(("parallel",)),
    )(q, k, v, do, lse, tt, k, v, q, do, lse, tt)


def _group_softmax(lse):
    m = jnp.max(lse, axis=0, keepdims=True)
    e = jnp.exp(lse - m)
    return e / jnp.sum(e, axis=0, keepdims=True)


def _mix_fwd(o, lse, name):
    G, S, W = o.shape
    tr = min(256, S)
    blk = pl.BlockSpec((G, tr, W), lambda i: (0, i, 0))

    def fn(ov, lv):
        return (jnp.sum(_group_softmax(lv) * ov.astype(F32), axis=0),)

    return _ew(fn, [(o, blk), (lse, blk)], [((S, W), BF16, _rows(tr, W), False)], (S // tr,), name)


def _mix_bwd(dya, ya, o, lse, name):
    G, S, W = o.shape
    tr = min(256, S)
    blk = pl.BlockSpec((G, tr, HEAD_DIM), lambda i, h: (0, i, h))
    row = pl.BlockSpec((tr, HEAD_DIM), lambda i, h: (i, h))

    def fn(dy, yv, ov, lv):
        w = _group_softmax(lv)
        dyf = dy.astype(F32)
        inner = jnp.sum(dyf * yv.astype(F32), axis=-1, keepdims=True)
        return w * dyf[None], -w * inner[None]

    return _ew(fn, [(dya, row), (ya, row), (o, blk), (lse, blk)],
               [((G, S, W), BF16, blk, False), ((G, S, W), F32, blk, False)],
               (S // tr, HEADS_PER_DIL), name)


CUM_BLOCK = 256


def _split3(x):
    hi = x.astype(BF16)
    r = x - hi.astype(F32)
    mid = r.astype(BF16)
    lo = (r - mid.astype(F32)).astype(BF16)
    return hi, mid, lo


def _tri_matmul(tri, x):
    return sum(jnp.dot(tri, part, preferred_element_type=F32) for part in _split3(x))


def _log_sigmoid(x):
    return jnp.minimum(x, 0.0) - jnp.log(1.0 + jnp.exp(-jnp.abs(x)))


def _fox_prep(f, b, name):
    S = f.shape[0]
    tb = min(CUM_BLOCK, S)

    def body(f_ref, b_ref, c_ref, carry):
        @pl.when(pl.program_id(0) == 0)
        def _():
            carry[...] = jnp.zeros_like(carry)

        ls = _log_sigmoid(f_ref[...] + b_ref[...])
        r = lax.broadcasted_iota(jnp.int32, (tb, tb), 0)
        cidx = lax.broadcasted_iota(jnp.int32, (tb, tb), 1)
        tri = jnp.where(r >= cidx, 1.0, 0.0).astype(BF16)
        c_ref[...] = _tri_matmul(tri, ls) + carry[...]
        carry[...] += jnp.sum(ls, axis=0, keepdims=True)

    return pl.pallas_call(
        body, name=name, grid=(S // tb,),
        in_specs=[_rows(tb, LANES), _bcast(LANES)], out_specs=_rows(tb, LANES),
        out_shape=jax.ShapeDtypeStruct((S, LANES), F32),
        scratch_shapes=[pltpu.VMEM((1, LANES), F32)],
        compiler_params=_params(("arbitrary",)),
    )(f, b)


def _fox_prep_bwd(dc, f, b, name):
    S = f.shape[0]
    tb = min(CUM_BLOCK, S)
    nb = S // tb

    def body(dc_ref, f_ref, b_ref, df_ref, db_ref, carry):
        @pl.when(pl.program_id(0) == 0)
        def _():
            carry[...] = jnp.zeros_like(carry)
            db_ref[...] = jnp.zeros_like(db_ref)

        r = lax.broadcasted_iota(jnp.int32, (tb, tb), 0)
        cidx = lax.broadcasted_iota(jnp.int32, (tb, tb), 1)
        tri = jnp.where(r <= cidx, 1.0, 0.0).astype(BF16)
        dcv = dc_ref[...]
        dls = _tri_matmul(tri, dcv) + carry[...]
        carry[...] += jnp.sum(dcv, axis=0, keepdims=True)
        z = f_ref[...] + b_ref[...]
        df = dls * (1.0 / (1.0 + jnp.exp(z)))
        df_ref[...] = df
        db_ref[...] += jnp.sum(df, axis=0, keepdims=True)

    rev = pl.BlockSpec((tb, LANES), lambda i: (nb - 1 - i, 0))
    return pl.pallas_call(
        body, name=name, grid=(nb,),
        in_specs=[rev, rev, _bcast(LANES)], out_specs=[rev, _bcast(LANES)],
        out_shape=[jax.ShapeDtypeStruct((S, LANES), F32), jax.ShapeDtypeStruct((1, LANES), F32)],
        scratch_shapes=[pltpu.VMEM((1, LANES), F32)],
        compiler_params=_params(("arbitrary",)),
    )(dc, f, b)


FOX_Q_TILE = 256
_FOX_Q0 = N_DIL_HEADS
_FOX_K0 = N_HEADS + N_DIL_HEADS
_FOX_V0 = 2 * N_HEADS + N_DIL_HEADS


def _fox_scores(q, k, cq, ck, iq, tq, S):
    nt = (((1,), (1,)), ((), ()))
    s = lax.dot_general(q, k, nt, preferred_element_type=F32) * SCALE + cq - ck
    qpos = iq * tq + lax.broadcasted_iota(jnp.int32, (tq, S), 0)
    kpos = lax.broadcasted_iota(jnp.int32, (tq, S), 1)
    return jnp.where(kpos <= qpos, s, NEG)


def _fox_fwd(z, cq, ck, name):
    S = z.shape[0]
    tq = min(FOX_Q_TILE, S)

    def body(q_ref, k_ref, v_ref, cq_ref, ck_ref, o_ref, lse_ref):
        s = _fox_scores(q_ref[...], k_ref[...], cq_ref[...], ck_ref[...], pl.program_id(1), tq, S)
        m = jnp.max(s, axis=-1, keepdims=True)
        p = jnp.exp(s - m)
        l = jnp.sum(p, axis=-1, keepdims=True)
        o = jnp.dot(p.astype(BF16), v_ref[...], preferred_element_type=F32)
        o_ref[...] = (o / l).astype(o_ref.dtype)
        lse_ref[...] = m + jnp.log(l)

    qblk = lambda c0: pl.BlockSpec((tq, HEAD_DIM), lambda h, i: (i, c0 + h))
    full = lambda c0: pl.BlockSpec((S, HEAD_DIM), lambda h, i: (0, c0 + h))
    col = pl.BlockSpec((None, tq, 1), lambda h, i: (h, i, 0))
    rowv = pl.BlockSpec((None, 1, S), lambda h, i: (h, 0, 0))
    return pl.pallas_call(
        body, name=name, grid=(N_FOX_HEADS, S // tq),
        in_specs=[qblk(_FOX_Q0), full(_FOX_K0), full(_FOX_V0), col, rowv],
        out_specs=[qblk(0), col],
        out_shape=[jax.ShapeDtypeStruct((S, FOX_WIDTH), BF16),
                   jax.ShapeDtypeStruct((N_FOX_HEADS, S, 1), F32)],
        compiler_params=_params(("parallel", "parallel")),
    )(z, z, z, cq, ck)


def _fox_bwd(z, cq, ck, lse, yb, dyb, name):
    S = z.shape[0]
    tq = min(FOX_Q_TILE, S)
    nq = S // tq
    nt = (((1,), (1,)), ((), ()))
    tn = (((0,), (0,)), ((), ()))

    def body(q_ref, k_ref, v_ref, cq_ref, ck_ref, lse_ref, o_ref, do_ref,
             dq_ref, dk_ref, dv_ref, dc_ref, dk_acc, dv_acc):
        i = pl.program_id(1)

        @pl.when(i == 0)
        def _():
            dk_acc[...] = jnp.zeros_like(dk_acc)
            dv_acc[...] = jnp.zeros_like(dv_acc)
            dc_ref[...] = jnp.zeros_like(dc_ref)

        q, k, v, do = q_ref[...], k_ref[...], v_ref[...], do_ref[...]
        s = _fox_scores(q, k, cq_ref[...], ck_ref[...], i, tq, S)
        p = jnp.exp(s - lse_ref[...])
        dp = lax.dot_general(do, v, nt, preferred_element_type=F32)
        ds = p * (dp - jnp.sum(p * dp, axis=-1, keepdims=True))
        dsb = ds.astype(BF16)
        dq_ref[...] = (jnp.dot(dsb, k, preferred_element_type=F32) * SCALE).astype(BF16)
        dk_acc[...] += lax.dot_general(dsb, q, tn, preferred_element_type=F32) * SCALE
        dv_acc[...] += lax.dot_general(p.astype(BF16), do, tn, preferred_element_type=F32)
        dc_ref[...] -= jnp.sum(ds, axis=0, keepdims=True)

        @pl.when(i == nq - 1)
        def _():
            dk_ref[...] = dk_acc[...].astype(BF16)
            dv_ref[...] = dv_acc[...].astype(BF16)

    qblk = lambda c0: pl.BlockSpec((tq, HEAD_DIM), lambda h, i: (i, c0 + h))
    full = lambda c0: pl.BlockSpec((S, HEAD_DIM), lambda h, i: (0, c0 + h))
    col = pl.BlockSpec((None, tq, 1), lambda h, i: (h, i, 0))
    rowv = pl.BlockSpec((None, 1, S), lambda h, i: (h, 0, 0))
    wide = jax.ShapeDtypeStruct((S, FOX_WIDTH), BF16)
    return pl.pallas_call(
        body, name=name, grid=(N_FOX_HEADS, nq),
        in_specs=[qblk(_FOX_Q0), full(_FOX_K0), full(_FOX_V0), col, rowv, col, qblk(0), qblk(0)],
        out_specs=[qblk(0), full(0), full(0), rowv],
        out_shape=[wide, wide, wide, jax.ShapeDtypeStruct((N_FOX_HEADS, 1, S), F32)],
        scratch_shapes=[pltpu.VMEM((S, HEAD_DIM), F32), pltpu.VMEM((S, HEAD_DIM), F32)],
        compiler_params=_params(("parallel", "arbitrary")),
    )(z, z, z, cq, ck, lse, yb, dyb)


def _sigmoid(x):
    return 1.0 / (1.0 + jnp.exp(-x))


def _merge_fwd(gates, a, bm, name):
    S, D = a.shape
    tr = _row_tile(S, D * 4)
    g1 = pl.BlockSpec((tr, D), lambda i: (i, 0))
    g2 = pl.BlockSpec((tr, D), lambda i: (i, 1))

    def fn(x1, x2, av, bv):
        return (x1.astype(F32) * av.astype(F32) + x2.astype(F32) * bv.astype(F32),)

    return _ew(fn, [(gates, g1), (gates, g2), (a, _rows(tr, D)), (bm, _rows(tr, D))],
               [((S, D), BF16, _rows(tr, D), False)], (S // tr,), name)


def _merge_bwd(dmerged, gates, a, bm, name):
    S, D = a.shape
    tr = _row_tile(S, D * 8)
    g1 = pl.BlockSpec((tr, D), lambda i: (i, 0))
    g2 = pl.BlockSpec((tr, D), lambda i: (i, 1))

    def fn(dm, x1, x2, av, bv):
        dm, x1, x2 = dm.astype(F32), x1.astype(F32), x2.astype(F32)
        dg1 = dm * av.astype(F32) * x1 * (1.0 - x1)
        dg2 = dm * bv.astype(F32) * x2 * (1.0 - x2)
        dgp = jnp.concatenate([dg1, dg2], axis=1)
        return dm * x1, dm * x2, dgp, jnp.sum(dgp, axis=0, keepdims=True)

    return _ew(fn, [(dmerged, _rows(tr, D)), (gates, g1), (gates, g2), (a, _rows(tr, D)), (bm, _rows(tr, D))],
               [((S, D), BF16, _rows(tr, D), False), ((S, D), BF16, _rows(tr, D), False),
                ((S, 2 * D), BF16, _rows(tr, 2 * D), False), ((1, 2 * D), F32, _bcast(2 * D), True)],
               (S // tr,), name)


def _ple_bwd(dh, pg, pe, name):
    S, D = dh.shape
    tr = _row_tile(S, D * 4)

    def fn(d, g, e):
        g, e = g.astype(F32), e.astype(F32)
        return d * g, d * e * g * (1.0 - g)

    spec = _rows(tr, D)
    return _ew(fn, [(dh, spec), (pg, spec), (pe, spec)],
               [((S, D), BF16, spec, False), ((S, D), BF16, spec, False)], (S // tr,), name)


def _position():
    x, y, c = lax.axis_index("x"), lax.axis_index("y"), lax.axis_index("c")
    chips = [(1 - x, y), (x, 1 - y), (1 - x, 1 - y)]
    return x, y, c, chips


def _remote(src, dst, send_sem, recv_sem, target):
    return pltpu.make_async_remote_copy(src_ref=src, dst_ref=dst, send_sem=send_sem, recv_sem=recv_sem,
                                        device_id=target, device_id_type=MESH)


def _allgather_chips(shards, name):
    n = len(shards)

    def body(*refs):
        ins, outs = refs[:n], refs[n:2 * n]
        send_sems, recv_sems, local_sems = refs[2 * n:]
        x, y, c, chips = _position()
        me = 2 * x + y
        sibling = (x, y, 1 - c)
        local, pending = [], []
        for a in range(n):
            rh = ins[a].shape[0] // 2
            mine = pl.ds(c * rh, rh)
            own = pltpu.make_async_copy(ins[a], outs[a].at[me], local_sems.at[a])
            own.start()
            local.append(own)
            for j, (cx, cy) in enumerate(chips):
                cp = _remote(ins[a].at[mine], outs[a].at[me, mine], send_sems.at[a, j], recv_sems.at[a, j],
                             (cx, cy, c))
                cp.start()
                pending.append(cp)
        for a in range(n):
            rh = ins[a].shape[0] // 2
            mine = pl.ds(c * rh, rh)
            for j, (cx, cy) in enumerate(chips):
                landed = outs[a].at[2 * cx + cy, mine]
                _remote(ins[a].at[mine], landed, send_sems.at[a, j], recv_sems.at[a, j], (cx, cy, c)).wait_recv()
                fwd = _remote(landed, landed, send_sems.at[a, 3 + j], recv_sems.at[a, 3 + j], sibling)
                fwd.start()
                pending.append(fwd)
        for a in range(n):
            rh = ins[a].shape[0] // 2
            other = pl.ds((1 - c) * rh, rh)
            for j, (cx, cy) in enumerate(chips):
                landed = outs[a].at[2 * cx + cy, other]
                _remote(landed, landed, send_sems.at[a, 3 + j], recv_sems.at[a, 3 + j], sibling).wait_recv()
        for cp in pending:
            cp.wait_send()
        for own in local:
            own.wait()

    return pl.pallas_call(
        body, name=name,
        in_specs=[ANY] * n, out_specs=[ANY] * n,
        out_shape=[jax.ShapeDtypeStruct((N_CHIPS,) + s.shape, s.dtype) for s in shards],
        scratch_shapes=[pltpu.SemaphoreType.DMA((n, 6)), pltpu.SemaphoreType.DMA((n, 6)),
                        pltpu.SemaphoreType.DMA((n,))],
    )(*shards)


def _pair_exchange(grads, name):
    n = len(grads)

    def body(*refs):
        ins, outs = refs[:n], refs[n:2 * n]
        send_sems, recv_sems = refs[2 * n:]
        x, y, c, _ = _position()
        copies = []
        for a in range(n):
            rh = ins[a].shape[1] // 2
            cp = _remote(ins[a].at[:, pl.ds((1 - c) * rh, rh)], outs[a], send_sems.at[a], recv_sems.at[a],
                         (x, y, 1 - c))
            cp.start()
            copies.append(cp)
        for cp in copies:
            cp.wait()

    return pl.pallas_call(
        body, name=name, in_specs=[ANY] * n, out_specs=[ANY] * n,
        out_shape=[jax.ShapeDtypeStruct((g.shape[0], g.shape[1] // 2, g.shape[2]), g.dtype) for g in grads],
        scratch_shapes=[pltpu.SemaphoreType.DMA((n,)), pltpu.SemaphoreType.DMA((n,))],
    )(*grads)


def _pair_sum(mine, theirs, c, name):
    nch, rh, cc = theirs.shape
    tr = _row_tile(rh, cc * 4)
    nb = rh // tr

    def body(c_ref, m_ref, t_ref, o_ref):
        o_ref[...] = (m_ref[...].astype(F32) + t_ref[...].astype(F32)).astype(o_ref.dtype)

    return pl.pallas_call(
        body, name=name,
        grid_spec=pltpu.PrefetchScalarGridSpec(
            num_scalar_prefetch=1, grid=(nch, nb),
            in_specs=[pl.BlockSpec((1, tr, cc), lambda k, i, c_ref: (k, c_ref[0] * nb + i, 0)),
                      pl.BlockSpec((1, tr, cc), lambda k, i, c_ref: (k, i, 0))],
            out_specs=pl.BlockSpec((1, tr, cc), lambda k, i, c_ref: (k, i, 0))),
        out_shape=jax.ShapeDtypeStruct(theirs.shape, BF16),
        compiler_params=_params(("parallel", "parallel")),
    )(c, mine, theirs)


def _chip_exchange(sums, name):
    n = len(sums)

    def body(*refs):
        ins, outs = refs[:n], refs[n:2 * n]
        send_sems, recv_sems = refs[2 * n:]
        x, y, c, chips = _position()
        copies = []
        for a in range(n):
            for j, (cx, cy) in enumerate(chips):
                cp = _remote(ins[a].at[2 * cx + cy], outs[a].at[j], send_sems.at[a, j], recv_sems.at[a, j],
                             (cx, cy, c))
                cp.start()
                copies.append(cp)
        for cp in copies:
            cp.wait()

    return pl.pallas_call(
        body, name=name, in_specs=[ANY] * n, out_specs=[ANY] * n,
        out_shape=[jax.ShapeDtypeStruct((3,) + s.shape[1:], s.dtype) for s in sums],
        scratch_shapes=[pltpu.SemaphoreType.DMA((n, 3)), pltpu.SemaphoreType.DMA((n, 3))],
    )(*sums)


def _chip_sum(own, others, me, name):
    _, rh, cc = own.shape
    tr = _row_tile(rh, cc * 4)

    def body(me_ref, o_ref, r_ref, g_ref):
        g_ref[...] = (o_ref[0].astype(F32) + r_ref[0].astype(F32)) + (r_ref[1].astype(F32) + r_ref[2].astype(F32))

    return pl.pallas_call(
        body, name=name,
        grid_spec=pltpu.PrefetchScalarGridSpec(
            num_scalar_prefetch=1, grid=(rh // tr,),
            in_specs=[pl.BlockSpec((1, tr, cc), lambda i, me_ref: (me_ref[0], i, 0)),
                      pl.BlockSpec((3, tr, cc), lambda i, me_ref: (0, i, 0))],
            out_specs=pl.BlockSpec((tr, cc), lambda i, me_ref: (i, 0))),
        out_shape=jax.ShapeDtypeStruct((rh, cc), F32),
        compiler_params=_params(("parallel",)),
    )(me, own, others)


def _half_exchange(halves, totals, layer, depth, name):
    n = len(halves)
    chained = totals is not None

    def body(*refs):
        ins = refs[:n]
        outs = refs[2 * n:3 * n] if chained else refs[n:2 * n]
        send_sems, recv_sems, local_sems = refs[-3:]
        x, y, c, _ = _position()
        copies = []
        for a in range(n):
            rh = ins[a].shape[0]
            dst = outs[a].at[layer, pl.ds(c * rh, rh)]
            own = pltpu.make_async_copy(ins[a], dst, local_sems.at[a])
            own.start()
            cp = _remote(ins[a], dst, send_sems.at[a], recv_sems.at[a], (x, y, 1 - c))
            cp.start()
            copies.append((own, cp, outs[a].at[layer, pl.ds((1 - c) * rh, rh)]))
        for a, (own, cp, theirs) in enumerate(copies):
            own.wait()
            cp.wait_send()
            _remote(ins[a], theirs, send_sems.at[a], recv_sems.at[a], (x, y, 1 - c)).wait_recv()

    out_shape = [jax.ShapeDtypeStruct((depth, 2 * h.shape[0], h.shape[1]), h.dtype) for h in halves]
    args = list(halves) + (list(totals) if chained else [])
    return pl.pallas_call(
        body, name=name, in_specs=[ANY] * len(args), out_specs=[ANY] * n, out_shape=out_shape,
        input_output_aliases={n + a: a for a in range(n)} if chained else {},
        scratch_shapes=[pltpu.SemaphoreType.DMA((n,)), pltpu.SemaphoreType.DMA((n,)),
                        pltpu.SemaphoreType.DMA((n,))],
    )(*args)


def _allgather_devices(v, name):
    m_per, n = v.shape

    def body(x_ref, out_ref, send_sems, recv_sems, local_sem):
        x, y, c, chips = _position()
        me, sibling = (x, y, c), (x, y, 1 - c)

        def rows(px, py, pc):
            return out_ref.at[pl.ds((4 * px + 2 * py + pc) * m_per, m_per), :]

        def copy(k, block, to, src=None):
            return _remote(rows(*block) if src is None else src, rows(*block), send_sems.at[k], recv_sems.at[k], to)

        mine = pltpu.make_async_copy(x_ref, rows(*me), local_sem)
        mine.start()
        first = [copy(0, me, sibling, src=x_ref)]
        first += [copy(1 + j, me, (*chip, c), src=x_ref) for j, chip in enumerate(chips)]
        for cp in first:
            cp.start()
        passed = [copy(4 + j, (*chip, c), sibling) for j, chip in enumerate(chips)]
        for j, chip in enumerate(chips):
            copy(1 + j, (*chip, c), me).wait_recv()
            passed[j].start()
        copy(0, sibling, me).wait_recv()
        for j, chip in enumerate(chips):
            copy(4 + j, (*chip, 1 - c), me).wait_recv()
        for cp in first + passed:
            cp.wait_send()
        mine.wait()

    vm = pl.BlockSpec(memory_space=pltpu.VMEM)
    return pl.pallas_call(
        body, name=name, in_specs=[vm], out_specs=vm,
        out_shape=jax.ShapeDtypeStruct((8 * m_per, n), v.dtype),
        scratch_shapes=[pltpu.SemaphoreType.DMA((7,)), pltpu.SemaphoreType.DMA((7,)), pltpu.SemaphoreType.DMA],
    )(v)


def _adamw_math(w, g, m, v):
    m = ADAM_B1 * m + (1.0 - ADAM_B1) * g
    v = ADAM_B2 * v + (1.0 - ADAM_B2) * (g * g)
    m_hat = m / (1.0 - ADAM_B1 ** ADAM_STEP)
    v_hat = v / (1.0 - ADAM_B2 ** ADAM_STEP)
    delta = -ADAM_LR * (m_hat / (jnp.sqrt(v_hat) + ADAM_EPS) + ADAM_WD * w)
    return delta, m, v


def _adamw(w, g, m, v, name):
    depth, r, cc = w.shape
    tr = _row_tile(r, cc * 4 * 2)
    spec = pl.BlockSpec((1, tr, cc), lambda l, i: (l, i, 0))

    def fn(wv, gv, mv, vv):
        return (gv,) + _adamw_math(wv, gv, mv, vv)

    out = (w.shape, F32, spec, False)
    return _ew(fn, [(w, spec), (g, spec), (m, spec), (v, spec)], [out] * 4, (depth, r // tr), name)


def _adamw_small(w, parts, m, v, name):
    M = w.shape[0]

    def body(w_ref, p_ref, m_ref, v_ref, g_ref, d_ref, nm_ref, nv_ref):
        g = p_ref[pl.ds(0, M), :]
        for k in range(1, 8):
            g = g + p_ref[pl.ds(k * M, M), :]
        d, nm, nv = _adamw_math(w_ref[...], g, m_ref[...], v_ref[...])
        g_ref[...] = g
        d_ref[...] = d
        nm_ref[...] = nm
        nv_ref[...] = nv

    vm = pl.BlockSpec(memory_space=pltpu.VMEM)
    return pl.pallas_call(
        body, name=name, in_specs=[vm] * 4, out_specs=[vm] * 4,
        out_shape=[jax.ShapeDtypeStruct(w.shape, F32)] * 4,
    )(w, parts, m, v)


def _layer_fwd(h0, p_l, W, small, tabs):
    S, D = h0.shape
    ctab, stab = tabs
    u = _rms_fwd(h0, small["g_mix"], "rms_mix")
    z = _mm(u, W["w_qkv"], mode="nn", name="mm_qkv")
    f = _mm(u, W["w_f"], mode="nn", name="mm_f", out_dtypes=(F32,))
    gates = _mm(u, W["w_gate"], mode="nn", name="mm_gate", b_chunked=True, extras=[(small["b_gate"], "row")],
                epilogue=lambda acc, b: (_sigmoid(acc + b),))
    qr, kr = _rope(z, 0, z, N_HEADS, ctab, stab, 1.0, "rope_fwd")
    qs, ks, vs = _to_strided(qr), _to_strided(kr), _to_strided(z[:, 2 * ATTN_WIDTH:2 * ATTN_WIDTH + DIL_WIDTH])
    o_s, lse_s = _sw_fwd(qs, ks, vs, "sw_fwd")
    o_g, lse_g = _from_strided(o_s, S), _from_strided(lse_s, S)
    ya = _mix_fwd(o_g, lse_g, "mix_fwd")
    a = _mm(ya, W["w_br_a"], mode="nn", name="mm_br_a", b_chunked=True)
    cum = _fox_prep(f, small["b_f"], "fox_prep")
    cq = cum[:, :N_FOX_HEADS].T[:, :, None]
    ck = cum[:, :N_FOX_HEADS].T[:, None, :]
    yb, lse_f = _fox_fwd(z, cq, ck, "fox_fwd")
    bm = _mm(yb, W["w_br_b"], mode="nn", name="mm_br_b", b_chunked=True)
    merged = _merge_fwd(gates, a, bm, "merge_fwd")
    h1 = _mm(merged, W["w_o"], mode="nn", name="mm_o", out_dtypes=(F32,), extras=[(h0, "tile")],
             epilogue=lambda acc, r: (acc + r,), tj=256)
    m = _rms_fwd(h1, small["g_mlp"], "rms_mlp")
    ra, act = _mm(m, W["w_up"], mode="nn", name="mm_up", b_chunked=True, out_dtypes=(BF16, BF16),
                  epilogue=lambda acc: (jnp.maximum(acc, 0.0), jnp.square(jnp.maximum(acc, 0.0))))
    h2 = _mm(act, W["w_down"], mode="nn", name="mm_down", out_dtypes=(F32,), extras=[(h1, "tile")],
             epilogue=lambda acc, r: (acc + r,), ti=1024)
    n = _rms_fwd(h2, small["g_ple"], "rms_ple")
    pg = _mm(n, W["w_ple_gate"], mode="nn", name="mm_ple_gate", epilogue=lambda acc: (_sigmoid(acc),))
    h3, pe = _mm(p_l, W["w_ple"], mode="nn", name="mm_ple", b_chunked=True, out_dtypes=(F32, BF16),
                 extras=[(h2, "tile"), (pg, "tile")], tj=256,
                 epilogue=lambda acc, r, g: (r + g.astype(F32) * acc, acc))
    saved = dict(h0=h0, u=u, z=z, f=f, gates=gates, qs=qs, ks=ks, vs=vs, lse_s=lse_s, o_g=o_g, lse_g=lse_g,
                 ya=ya, a=a, cq=cq, ck=ck, yb=yb, lse_f=lse_f, bm=bm, merged=merged, h1=h1, m=m, ra=ra,
                 act=act, h2=h2, n=n, pg=pg, pe=pe, p_l=p_l)
    return h3, saved


def _layer_bwd(dh3, sv, W, small, tabs):
    S, D = dh3.shape
    ctab, stab = tabs
    gw, gs = {}, {}
    tn = functools.partial(_mm, mode="tn", ti=512, tj=1024)
    dpe, dpg = _ple_bwd(dh3, sv["pg"], sv["pe"], "ple_bwd")
    gw["w_ple"] = tn(sv["p_l"], dpe, name="dw_ple", out_chunks=N_CHIPS)
    gw["w_ple_gate"] = tn(sv["n"], dpg, name="dw_ple_gate").reshape(N_CHIPS, D // N_CHIPS, D)
    dn = _mm(dpg, W["w_ple_gate"], mode="nt", name="mm_dn")
    dh2, dh2b, gs["g_ple"] = _rms_bwd(sv["h2"], small["g_ple"], dn, dh3, "rms_ple_bwd")
    da = _mm(dh2b, W["w_down"], mode="nt", name="mm_dact", extras=[(sv["ra"], "tile")],
             epilogue=lambda acc, r: (acc * (2.0 * r.astype(F32)),))
    FF = da.shape[1]
    gw["w_down"] = tn(sv["act"], dh2b, name="dw_down").reshape(N_CHIPS, FF // N_CHIPS, D)
    gw["w_up"] = tn(sv["m"], da, name="dw_up", out_chunks=N_CHIPS)
    dm = _mm(da, W["w_up"], mode="nt", name="mm_dm", b_chunked=True)
    dh1, dh1b, gs["g_mlp"] = _rms_bwd(sv["h1"], small["g_mlp"], dm, dh2, "rms_mlp_bwd")
    dmerged = _mm(dh1b, W["w_o"], mode="nt", name="mm_dmerged")
    gw["w_o"] = tn(sv["merged"], dh1b, name="dw_o").reshape(N_CHIPS, D // N_CHIPS, D)
    d_a, d_b, dgp, gs["b_gate"] = _merge_bwd(dmerged, sv["gates"], sv["a"], sv["bm"], "merge_bwd")
    gw["w_gate"] = tn(sv["u"], dgp, name="dw_gate", out_chunks=N_CHIPS)
    gw["w_br_a"] = tn(sv["ya"], d_a, name="dw_br_a", out_chunks=N_CHIPS, tj=512)
    gw["w_br_b"] = tn(sv["yb"], d_b, name="dw_br_b", out_chunks=N_CHIPS, tj=512)
    dya = _mm(d_a, W["w_br_a"], mode="nt", name="mm_dya", b_chunked=True)
    dyb = _mm(d_b, W["w_br_b"], mode="nt", name="mm_dyb", b_chunked=True)
    z = sv["z"]
    dq_f, dk_f, dv_f, dck = _fox_bwd(z, sv["cq"], sv["ck"], sv["lse_f"], sv["yb"], dyb, "fox_bwd")
    dc = jnp.pad(dck[:, 0, :].T, ((0, 0), (0, LANES - N_FOX_HEADS)))
    df, dbf = _fox_prep_bwd(dc, sv["f"], small["b_f"], "fox_prep_bwd")
    gs["b_f"] = dbf[:, :N_FOX_HEADS]
    lane = jnp.arange(LANES)[None, :] < N_FOX_HEADS
    dzf = jnp.where(lane, df, 0.0).astype(BF16)
    do_g, tt_g = _mix_bwd(dya, sv["ya"], sv["o_g"], sv["lse_g"], "mix_bwd")
    do_s = _to_strided(do_g.transpose(1, 0, 2).reshape(S, DIL_WIDTH))
    tt_s = _to_strided(tt_g.transpose(1, 0, 2).reshape(S, DIL_WIDTH))
    dq_s, dk_s, dv_s = _sw_bwd(sv["qs"], sv["ks"], sv["vs"], do_s, sv["lse_s"], tt_s, "sw_bwd")
    unstride = lambda t: _from_strided(t, S).transpose(1, 0, 2).reshape(S, DIL_WIDTH)
    dq_a, dk_a = _rope(unstride(dq_s), 0, unstride(dk_s), 0, ctab, stab, -1.0, "rope_bwd")
    dz = jnp.concatenate([dq_a, dq_f, dk_a, dk_f, unstride(dv_s), dv_f], axis=1)
    g_qkv = tn(sv["u"], dz, name="dw_qkv")
    g_f = tn(sv["u"], dzf, name="dw_f", tj=128)
    cols = W["w_in_cols"]
    g_in = jnp.concatenate([g_qkv, g_f[:, :N_FOX_HEADS]], axis=1)
    gw["w_in"] = g_in.reshape(D, N_CHIPS, cols).transpose(1, 0, 2)
    du = _mm(dzf, W["w_f"], mode="nt", name="mm_du_f", out_dtypes=(F32,))
    du = _mm(dgp, W["w_gate"], mode="nt", name="mm_du_gate", b_chunked=True, out_dtypes=(F32,),
             extras=[(du, "tile")], epilogue=lambda acc, r: (acc + r,), tj=256)
    du = _mm(dz, W["w_qkv"], mode="nt", name="mm_du_qkv", extras=[(du, "tile")],
             epilogue=lambda acc, r: (acc + r,), tj=256)
    dh0, _, gs["g_mix"] = _rms_bwd(sv["h0"], small["g_mix"], du, dh1, "rms_mix_bwd")
    return dh0, gw, gs


BIG = ("w_in", "w_gate", "w_br_a", "w_br_b", "w_o", "w_up", "w_down", "w_ple", "w_ple_gate")
SMALL = ("g_mix", "b_f", "b_gate", "g_mlp", "g_ple", "g_final")
ORDER = ("g_mix", "w_in", "b_f", "w_gate", "b_gate", "w_br_a", "w_br_b", "w_o", "g_mlp", "w_up", "w_down",
         "g_ple", "w_ple", "w_ple_gate", "g_final")


def _gathered_layer_weights(full, l, D):
    W = {}
    for name in ("w_gate", "w_br_a", "w_br_b", "w_up", "w_ple"):
        W[name] = full[name][:, l]
    for name in ("w_o", "w_down", "w_ple_gate"):
        t = full[name][:, l]
        W[name] = t.reshape(t.shape[0] * t.shape[1], t.shape[2])
    w_in = full["w_in"][:, l]
    cols = w_in.shape[2]
    w_in = w_in.transpose(1, 0, 2).reshape(D, N_CHIPS * cols)
    W["w_qkv"] = w_in[:, :3 * ATTN_WIDTH]
    W["w_f"] = jnp.pad(w_in[:, 3 * ATTN_WIDTH:], ((0, 0), (0, LANES - N_FOX_HEADS)))
    W["w_in_cols"] = cols
    return W


def _pack_rows(vals):
    flat = jnp.concatenate([v.reshape(-1) for v in vals])
    rows = -(-flat.shape[0] // (8 * LANES)) * 8
    return jnp.pad(flat, (0, rows * LANES - flat.shape[0])).reshape(rows, LANES)


def _unpack_rows(packed, shapes):
    flat = packed.reshape(-1)
    out, pos = [], 0
    for s in shapes:
        size = 1
        for dim in s:
            size *= dim
        out.append(flat[pos:pos + size].reshape(s))
        pos += size
    return out


def _local_step(x, p, small_w, full, loss_target):
    depth = p.shape[0]
    S, D = x.shape
    tabs = _rope_tables(S)
    h = x
    saved, weights, smalls = [], [], []
    for l in range(depth):
        W = _gathered_layer_weights(full, l, D)
        sm = dict(g_mix=small_w["g_mix"][l][None], g_mlp=small_w["g_mlp"][l][None],
                  g_ple=small_w["g_ple"][l][None], b_gate=small_w["b_gate"][l][None],
                  b_f=jnp.pad(small_w["b_f"][l][None], ((0, 0), (0, LANES - N_FOX_HEADS))))
        h, sv = _layer_fwd(h, p[l].astype(BF16), W, sm, tabs)
        saved.append(sv)
        weights.append(W)
        smalls.append(sm)
    dh, loss_row, dg_final = _loss_head(h, small_w["g_final"][None], loss_target, "loss_head")
    gws, gss = [None] * depth, [None] * depth
    for l in reversed(range(depth)):
        dh, gws[l], gss[l] = _layer_bwd(dh, saved[l], weights[l], smalls[l], tabs)
    return loss_row, dh, gws, gss, dg_final


def _reduce_layer(gw, totals, layer, depth, c_arr, me_arr):
    grads = [gw[n] for n in BIG]
    theirs = _pair_exchange(grads, "rs_pair_exchange")
    sums = [_pair_sum(g, t, c_arr, "rs_pair_sum") for g, t in zip(grads, theirs)]
    others = _chip_exchange(sums, "rs_chip_exchange")
    halves = [_chip_sum(s, o, me_arr, "rs_chip_sum") for s, o in zip(sums, others)]
    return _half_exchange(halves, totals, layer, depth, "rs_half_exchange")


def kernel(x, p, g_mix, w_in, b_f, w_gate, b_gate, w_br_a, w_br_b, w_o, g_mlp, w_up, w_down, g_ple, w_ple, w_ple_gate, g_final, loss_target, m_g_mix, m_w_in, m_b_f, m_w_gate, m_b_gate, m_w_br_a, m_w_br_b, m_w_o, m_g_mlp, m_w_up, m_w_down, m_g_ple, m_w_ple, m_w_ple_gate, m_g_final, v_g_mix, v_w_in, v_b_f, v_w_gate, v_b_gate, v_w_br_a, v_w_br_b, v_w_o, v_g_mlp, v_w_up, v_w_down, v_g_ple, v_w_ple, v_w_ple_gate, v_g_final):
    w = dict(g_mix=g_mix, w_in=w_in, b_f=b_f, w_gate=w_gate, b_gate=b_gate, w_br_a=w_br_a, w_br_b=w_br_b,
             w_o=w_o, g_mlp=g_mlp, w_up=w_up, w_down=w_down, g_ple=g_ple, w_ple=w_ple, w_ple_gate=w_ple_gate,
             g_final=g_final)
    m = dict(g_mix=m_g_mix, w_in=m_w_in, b_f=m_b_f, w_gate=m_w_gate, b_gate=m_b_gate, w_br_a=m_w_br_a,
             w_br_b=m_w_br_b, w_o=m_w_o, g_mlp=m_g_mlp, w_up=m_w_up, w_down=m_w_down, g_ple=m_g_ple,
             w_ple=m_w_ple, w_ple_gate=m_w_ple_gate, g_final=m_g_final)
    v = dict(g_mix=v_g_mix, w_in=v_w_in, b_f=v_b_f, w_gate=v_w_gate, b_gate=v_b_gate, w_br_a=v_w_br_a,
             w_br_b=v_w_br_b, w_o=v_w_o, g_mlp=v_g_mlp, w_up=v_w_up, w_down=v_w_down, g_ple=v_g_ple,
             w_ple=v_w_ple, w_ple_gate=v_w_ple_gate, g_final=v_g_final)
    depth = p.shape[0]
    cx, cy, cc = lax.axis_index("x"), lax.axis_index("y"), lax.axis_index("c")
    c_arr = jnp.reshape(cc, (1,)).astype(jnp.int32)
    me_arr = jnp.reshape(2 * cx + cy, (1,)).astype(jnp.int32)

    gathered = _allgather_chips([w[n].astype(BF16) for n in BIG], "allgather_weights")
    full = dict(zip(BIG, gathered))
    loss_row, grad_x, gws, gss, dg_final = _local_step(x[0], p[:, 0], w, full, loss_target[0])

    totals = None
    for l in reversed(range(depth)):
        totals = _reduce_layer(gws[l], totals, l, depth, c_arr, me_arr)
    big_out = {n: _adamw(w[n], t, m[n], v[n], "adamw") for n, t in zip(BIG, totals)}

    small_grads = [jnp.stack([gss[l][n][0] for l in range(depth)]) for n in SMALL[:-1]] + [dg_final[0]]
    shapes = [w[n].shape for n in SMALL]
    parts = _allgather_devices(_pack_rows(small_grads), "allgather_small")
    packed = _adamw_small(_pack_rows([w[n] for n in SMALL]), parts, _pack_rows([m[n] for n in SMALL]),
                          _pack_rows([v[n] for n in SMALL]), "adamw_small")
    small_out = {n: vals for n, vals in zip(SMALL, zip(*[_unpack_rows(t, shapes) for t in packed]))}

    loss = lax.psum(loss_row[0, 0], ("x", "y", "c"))
    out = {**big_out, **small_out}
    return (loss, grad_x[None], *[out[n][0] for n in ORDER], *[out[n][1] for n in ORDER],
            *[out[n][2] for n in ORDER], *[out[n][3] for n in ORDER])
```

```python
import functools

import jax
import jax.numpy as jnp
from jax import lax
from jax.experimental import pallas as pl
from jax.experimental.pallas import tpu as pltpu

F32 = jnp.float32
BF16 = jnp.bfloat16

HEAD_DIM = 128
N_HEADS = 16
N_DIL_HEADS = 12
N_FOX_HEADS = 4
HEADS_PER_DIL = 4
DILATIONS = (1, 4, 16)
BLOCK = 128
ATTN_WIDTH = N_HEADS * HEAD_DIM
DIL_WIDTH = N_DIL_HEADS * HEAD_DIM
FOX_WIDTH = N_FOX_HEADS * HEAD_DIM
ROPE_THETA = 500000.0
ROPE_HALF = 16
NORM_EPS = 1e-6
SCALE = HEAD_DIM ** -0.5
NEG = -1e30

ADAM_LR = 0.001
ADAM_B1 = 0.9
ADAM_B2 = 0.999
ADAM_EPS = 1e-08
ADAM_WD = 0.01
ADAM_STEP = 10

N_CHIPS = 4
V7X_VMEM_LIMIT_BYTES = 56 * 1024 * 1024
LANES = 128
MESH = pl.DeviceIdType.MESH
ANY = pl.BlockSpec(memory_space=pl.ANY)


def _params(sem):
    return pltpu.CompilerParams(dimension_semantics=sem, vmem_limit_bytes=V7X_VMEM_LIMIT_BYTES)


def _tile(n, pref):
    if n <= pref:
        return n
    t = (pref // LANES) * LANES
    while t > LANES and n % t:
        t -= LANES
    assert n % t == 0, (n, pref)
    return t


def _mm(a, b, *, mode, name, out_dtypes=(BF16,), epilogue=None, extras=(), b_chunked=False,
        out_chunks=0, ti=2048, tj=512, tc=2048):
    if mode == "tn":
        C, I = a.shape
    else:
        I, C = a.shape
    if b_chunked:
        nch, d0, n = b.shape
        if mode == "nn":
            assert d0 == C
            J = nch * n
        else:
            assert mode == "nt" and nch * n == C
            J = d0
    elif mode == "nt":
        J = b.shape[0]
        assert b.shape[1] == C
    else:
        assert b.shape[0] == C
        J = b.shape[1]
    ti, tc = _tile(I, ti), _tile(C, tc)
    if b_chunked and mode == "nn":
        tj = _tile(n, tj)
    elif out_chunks:
        tj = _tile(J // out_chunks, tj)
    else:
        tj = _tile(J, tj)
    if b_chunked and mode == "nt":
        tc = _tile(n, tc)
    ni, nj, nc = I // ti, J // tj, C // tc

    if mode == "tn":
        a_spec = pl.BlockSpec((tc, ti), lambda i, j, c: (c, i))
        dims = (((0,), (0,)), ((), ()))
    else:
        a_spec = pl.BlockSpec((ti, tc), lambda i, j, c: (i, c))
        dims = (((1,), (0,)), ((), ())) if mode == "nn" else (((1,), (1,)), ((), ()))
    if mode == "nt":
        if b_chunked:
            cb = n // tc
            b_spec = pl.BlockSpec((None, tj, tc), lambda i, j, c: (c // cb, j, c % cb))
        else:
            b_spec = pl.BlockSpec((tj, tc), lambda i, j, c: (j, c))
    else:
        if b_chunked:
            jb = n // tj
            b_spec = pl.BlockSpec((None, tc, tj), lambda i, j, c: (j // jb, c, j % jb))
        else:
            b_spec = pl.BlockSpec((tc, tj), lambda i, j, c: (c, j))
    extra_specs = []
    for arr, kind in extras:
        if kind == "tile":
            assert arr.shape == (I, J), (arr.shape, I, J)
            extra_specs.append(pl.BlockSpec((ti, tj), lambda i, j, c: (i, j)))
        else:
            assert arr.shape == (1, J)
            extra_specs.append(pl.BlockSpec((1, tj), lambda i, j, c: (0, j)))
    if out_chunks:
        ob = (J // out_chunks) // tj
        out_spec = pl.BlockSpec((None, ti, tj), lambda i, j, c: (j // ob, i, j % ob))
        out_shape = [jax.ShapeDtypeStruct((out_chunks, I, J // out_chunks), d) for d in out_dtypes]
    else:
        out_spec = pl.BlockSpec((ti, tj), lambda i, j, c: (i, j))
        out_shape = [jax.ShapeDtypeStruct((I, J), d) for d in out_dtypes]
    ne, no = len(extras), len(out_dtypes)
    if epilogue is None:
        epilogue = lambda acc: (acc,)

    def body(a_ref, b_ref, *rest):
        extra_refs, out_refs = rest[:ne], rest[ne:ne + no]

        def finish(acc):
            outs = epilogue(acc, *[r[...] for r in extra_refs])
            for o_ref, val in zip(out_refs, outs):
                o_ref[...] = val.astype(o_ref.dtype)

        part = lax.dot_general(a_ref[...], b_ref[...], dims, preferred_element_type=F32)
        if nc == 1:
            finish(part)
        else:
            acc_ref = rest[-1]
            k = pl.program_id(2)

            @pl.when(k == 0)
            def _():
                acc_ref[...] = part

            @pl.when(k > 0)
            def _():
                acc_ref[...] += part

            @pl.when(k == nc - 1)
            def _():
                finish(acc_ref[...])

    outs = pl.pallas_call(
        body, name=name, grid=(ni, nj, nc),
        in_specs=[a_spec, b_spec] + extra_specs,
        out_specs=[out_spec] * no, out_shape=out_shape,
        scratch_shapes=[pltpu.VMEM((ti, tj), F32)] if nc > 1 else [],
        compiler_params=_params(("parallel", "parallel", "arbitrary")),
    )(a, b, *[e[0] for e in extras])
    return outs[0] if no == 1 else tuple(outs)


def _ew(fn, ins, outs, grid, name):
    n_in = len(ins)
    has_acc = any(o[3] for o in outs)
    assert not has_acc or len(grid) == 1

    def body(*refs):
        vals = fn(*[r[...] for r in refs[:n_in]])
        for o_ref, o, val in zip(refs[n_in:], outs, vals):
            if o[3]:
                step = pl.program_id(0)

                @pl.when(step == 0)
                def _(o_ref=o_ref, val=val):
                    o_ref[...] = val

                @pl.when(step > 0)
                def _(o_ref=o_ref, val=val):
                    o_ref[...] += val
            else:
                o_ref[...] = val.astype(o_ref.dtype)

    sem = ("arbitrary",) if has_acc else ("parallel",) * len(grid)
    res = pl.pallas_call(
        body, name=name, grid=grid,
        in_specs=[i[1] for i in ins], out_specs=[o[2] for o in outs],
        out_shape=[jax.ShapeDtypeStruct(o[0], o[1]) for o in outs],
        compiler_params=_params(sem),
    )(*[i[0] for i in ins])
    return res[0] if len(outs) == 1 else tuple(res)


def _rows(tr, w):
    return pl.BlockSpec((tr, w), lambda i: (i, 0))


def _bcast(w):
    return pl.BlockSpec((1, w), lambda i: (0, 0))


def _row_tile(S, width_bytes):
    tr = 512
    while tr > 16 and tr * width_bytes > 2 * 1024 * 1024:
        tr //= 2
    return min(tr, S)


def _rms_fwd(h, g, name):
    S, D = h.shape
    tr = _row_tile(S, D * 4)

    def fn(x, gg):
        r = lax.rsqrt(jnp.mean(x * x, axis=-1, keepdims=True) + NORM_EPS)
        return (x * r * gg,)

    return _ew(fn, [(h, _rows(tr, D)), (g, _bcast(D))], [((S, D), BF16, _rows(tr, D), False)],
               (S // tr,), name)


def _rms_bwd(x, g, dy, dres, name):
    S, D = x.shape
    tr = _row_tile(S, D * 4)

    def fn(xv, gg, dyv, dr):
        r = lax.rsqrt(jnp.mean(xv * xv, axis=-1, keepdims=True) + NORM_EPS)
        dyf = dyv.astype(F32)
        gy = dyf * gg
        dx = r * gy - xv * (r * r * r) * jnp.mean(xv * gy, axis=-1, keepdims=True)
        tot = dr + dx
        dg = jnp.sum(dyf * xv * r, axis=0, keepdims=True)
        return tot, tot, dg

    return _ew(fn, [(x, _rows(tr, D)), (g, _bcast(D)), (dy, _rows(tr, D)), (dres, _rows(tr, D))],
               [((S, D), F32, _rows(tr, D), False), ((S, D), BF16, _rows(tr, D), False),
                ((1, D), F32, _bcast(D), True)], (S // tr,), name)


def _loss_head(h, g, target, name):
    S, D = h.shape
    tr = _row_tile(S, D * 4)

    def fn(xv, gg, tgt):
        r = lax.rsqrt(jnp.mean(xv * xv, axis=-1, keepdims=True) + NORM_EPS)
        y = xv * r * gg
        e = y - tgt
        loss = 0.5 * jnp.sum(jnp.mean(e * e, axis=-1, keepdims=True), axis=0, keepdims=True)
        dy = e * (1.0 / D)
        gy = dy * gg
        dx = r * gy - xv * (r * r * r) * jnp.mean(xv * gy, axis=-1, keepdims=True)
        dg = jnp.sum(dy * xv * r, axis=0, keepdims=True)
        return dx, jnp.broadcast_to(loss, (1, LANES)), dg

    return _ew(fn, [(h, _rows(tr, D)), (g, _bcast(D)), (target, _rows(tr, D))],
               [((S, D), F32, _rows(tr, D), False), ((1, LANES), F32, _bcast(LANES), True),
                ((1, D), F32, _bcast(D), True)], (S // tr,), name)


def _rope_tables(S):
    inv = ROPE_THETA ** (-jnp.arange(ROPE_HALF, dtype=F32) / ROPE_HALF)
    ang = jnp.arange(S, dtype=F32)[:, None] * inv[None, :]
    cos, sin = jnp.cos(ang), jnp.sin(ang)
    rest = HEAD_DIM - 2 * ROPE_HALF
    ctab = jnp.concatenate([cos, cos, jnp.ones((S, rest), F32)], axis=1)
    stab = jnp.concatenate([-sin, sin, jnp.zeros((S, rest), F32)], axis=1)
    return ctab, stab


def _swap_halves(x):
    lane = lax.broadcasted_iota(jnp.int32, x.shape, 1)
    return jnp.where(lane < ROPE_HALF, pltpu.roll(x, HEAD_DIM - ROPE_HALF, 1), pltpu.roll(x, ROPE_HALF, 1))


def _rope(q_src, q_col0, k_src, k_col0, ctab, stab, sign, name):
    S = q_src.shape[0]
    tr = min(512, S)

    def fn(q, k, ct, st):
        outs = []
        for v in (q, k):
            vf = v.astype(F32)
            outs.append(vf * ct + sign * _swap_halves(vf) * st)
        return tuple(outs)

    head = lambda c0: pl.BlockSpec((tr, HEAD_DIM), lambda i, h: (i, c0 + h))
    tab = pl.BlockSpec((tr, HEAD_DIM), lambda i, h: (i, 0))
    out = ((S, DIL_WIDTH), BF16, head(0), False)
    return _ew(fn, [(q_src, head(q_col0)), (k_src, head(k_col0)), (ctab, tab), (stab, tab)],
               [out, out], (S // tr, N_DIL_HEADS), name)


SW_BLOCKS_PER_STEP = 8


def _to_strided(x):
    S = x.shape[0]
    parts = []
    for g, d in enumerate(DILATIONS):
        xg = x[:, g * 512:(g + 1) * 512].reshape(S // d, d, HEADS_PER_DIL, HEAD_DIM)
        parts.append(xg.transpose(1, 2, 0, 3).reshape(-1, BLOCK, HEAD_DIM))
    return jnp.concatenate(parts, axis=0)


def _from_strided(y, S):
    per = y.shape[0] // len(DILATIONS)
    parts = []
    for g, d in enumerate(DILATIONS):
        yg = y[g * per:(g + 1) * per].reshape(d, HEADS_PER_DIL, S // d, HEAD_DIM)
        parts.append(yg.transpose(2, 0, 1, 3).reshape(S, HEADS_PER_DIL * HEAD_DIM))
    return jnp.stack(parts, axis=0)


def _seq_blocks(b0, per_group):
    g = b0 // per_group
    n0 = per_group // HEADS_PER_DIL
    return jnp.where(g == 0, n0, jnp.where(g == 1, n0 // 4, n0 // 16))


def _sw_masks():
    qi = lax.broadcasted_iota(jnp.int32, (BLOCK, BLOCK), 0)
    ki = lax.broadcasted_iota(jnp.int32, (BLOCK, BLOCK), 1)
    return qi >= ki, qi <= ki


def _sw_fwd(q, k, v, name):
    NB = q.shape[0]
    T = SW_BLOCKS_PER_STEP
    per_group = NB // len(DILATIONS)
    nt = (((1,), (1,)), ((), ()))

    def body(q_ref, k_ref, v_ref, kp_ref, vp_ref, o_ref, lse_ref):
        b0 = pl.program_id(0) * T
        nseq = _seq_blocks(b0, per_group)
        cur_mask, prev_mask = _sw_masks()
        for t in range(T):
            has_prev = ((b0 + t) & (nseq - 1)) != 0
            qt = q_ref[t]
            kp = kp_ref[0] if t == 0 else k_ref[t - 1]
            vp = vp_ref[0] if t == 0 else v_ref[t - 1]
            s_c = lax.dot_general(qt, k_ref[t], nt, preferred_element_type=F32) * SCALE
            s_p = lax.dot_general(qt, kp, nt, preferred_element_type=F32) * SCALE
            s_c = jnp.where(cur_mask, s_c, NEG)
            s_p = jnp.where(prev_mask, s_p, NEG) + jnp.where(has_prev, 0.0, NEG)
            m = jnp.maximum(jnp.max(s_c, axis=-1, keepdims=True), jnp.max(s_p, axis=-1, keepdims=True))
            p_c = jnp.exp(s_c - m)
            p_p = jnp.exp(s_p - m)
            l = jnp.sum(p_c, axis=-1, keepdims=True) + jnp.sum(p_p, axis=-1, keepdims=True)
            o = (jnp.dot(p_c.astype(BF16), v_ref[t], preferred_element_type=F32)
                 + jnp.dot(p_p.astype(BF16), vp, preferred_element_type=F32))
            o_ref[t] = (o / l).astype(o_ref.dtype)
            lse_ref[t] = jnp.broadcast_to(m + jnp.log(l), (BLOCK, HEAD_DIM))

    tile = pl.BlockSpec((T, BLOCK, HEAD_DIM), lambda i: (i, 0, 0))
    before = pl.BlockSpec((1, BLOCK, HEAD_DIM), lambda i: (jnp.maximum(i * T - 1, 0), 0, 0))
    return pl.pallas_call(
        body, name=name, grid=(NB // T,),
        in_specs=[tile, tile, tile, before, before], out_specs=[tile, tile],
        out_shape=[jax.ShapeDtypeStruct(q.shape, BF16), jax.ShapeDtypeStruct(q.shape, F32)],
        compiler_params=_params(("parallel",)),
    )(q, k, v, k, v)


def _sw_bwd(q, k, v, do, lse, tt, name):
    NB = q.shape[0]
    T = SW_BLOCKS_PER_STEP
    per_group = NB // len(DILATIONS)
    nt = (((1,), (1,)), ((), ()))
    tn = (((0,), (0,)), ((), ()))

    def body(q_ref, k_ref, v_ref, do_ref, lse_ref, tt_ref, kp_ref, vp_ref, qn_ref, don_ref, lsen_ref,
             ttn_ref, dq_ref, dk_ref, dv_ref):
        b0 = pl.program_id(0) * T
        nseq = _seq_blocks(b0, per_group)
        cur_mask, prev_mask = _sw_masks()

        def probs(qq, kk, lse_b, mask, gate):
            s = lax.dot_general(qq, kk, nt, preferred_element_type=F32) * SCALE
            return jnp.exp(jnp.where(mask, s, NEG) + gate - lse_b)

        for t in range(T):
            has_prev = jnp.where(((b0 + t) & (nseq - 1)) != 0, 0.0, NEG)
            has_next = jnp.where(((b0 + t + 1) & (nseq - 1)) != 0, 0.0, NEG)
            last = t == T - 1
            qt, kt, vt, dot = q_ref[t], k_ref[t], v_ref[t], do_ref[t]
            kp = kp_ref[0] if t == 0 else k_ref[t - 1]
            vp = vp_ref[0] if t == 0 else v_ref[t - 1]
            qn = qn_ref[0] if last else q_ref[t + 1]
            don = don_ref[0] if last else do_ref[t + 1]
            lsen = lsen_ref[0] if last else lse_ref[t + 1]
            ttn = ttn_ref[0] if last else tt_ref[t + 1]
            p_cc = probs(qt, kt, lse_ref[t], cur_mask, 0.0)
            p_cp = probs(qt, kp, lse_ref[t], prev_mask, has_prev)
            p_nc = probs(qn, kt, lsen, prev_mask, has_next)
            ds_cc = p_cc * (lax.dot_general(dot, vt, nt, preferred_element_type=F32) + tt_ref[t])
            ds_cp = p_cp * (lax.dot_general(dot, vp, nt, preferred_element_type=F32) + tt_ref[t])
            ds_nc = p_nc * (lax.dot_general(don, vt, nt, preferred_element_type=F32) + ttn)
            ds_cc, ds_cp, ds_nc = ds_cc.astype(BF16), ds_cp.astype(BF16), ds_nc.astype(BF16)
            dq = (jnp.dot(ds_cc, kt, preferred_element_type=F32)
                  + jnp.dot(ds_cp, kp, preferred_element_type=F32))
            dk = (lax.dot_general(ds_cc, qt, tn, preferred_element_type=F32)
                  + lax.dot_general(ds_nc, qn, tn, preferred_element_type=F32))
            dv = (lax.dot_general(p_cc.astype(BF16), dot, tn, preferred_element_type=F32)
                  + lax.dot_general(p_nc.astype(BF16), don, tn, preferred_element_type=F32))
            dq_ref[t] = (dq * SCALE).astype(BF16)
            dk_ref[t] = (dk * SCALE).astype(BF16)
            dv_ref[t] = dv.astype(BF16)

    tile = pl.BlockSpec((T, BLOCK, HEAD_DIM), lambda i: (i, 0, 0))
    before = pl.BlockSpec((1, BLOCK, HEAD_DIM), lambda i: (jnp.maximum(i * T - 1, 0), 0, 0))
    after = pl.BlockSpec((1, BLOCK, HEAD_DIM), lambda i: (jnp.minimum(i * T + T, NB - 1), 0, 0))
    out = jax.ShapeDtypeStruct(q.shape, BF16)
    return pl.pallas_call(
        body, name=name, grid=(NB // T,),
        in_specs=[tile] * 6 + [before, before, after, after, after, after],
        out_specs=[tile] * 3, out_shape=[out] * 3,
        compiler_params=_params(("parallel",)),
    )(q, k, v, do, lse, tt, k, v, q, do, lse, tt)


def _group_softmax(lse):
    m = jnp.max(lse, axis=0, keepdims=True)
    e = jnp.exp(lse - m)
    return e / jnp.sum(e, axis=0, keepdims=True)


def _mix_fwd(o, lse, name):
    G, S, W = o.shape
    tr = min(256, S)
    blk = pl.BlockSpec((G, tr, W), lambda i: (0, i, 0))

    def fn(ov, lv):
        return (jnp.sum(_group_softmax(lv) * ov.astype(F32), axis=0),)

    return _ew(fn, [(o, blk), (lse, blk)], [((S, W), BF16, _rows(tr, W), False)], (S // tr,), name)


def _mix_bwd(dya, ya, o, lse, name):
    G, S, W = o.shape
    tr = min(256, S)
    blk = pl.BlockSpec((G, tr, HEAD_DIM), lambda i, h: (0, i, h))
    row = pl.BlockSpec((tr, HEAD_DIM), lambda i, h: (i, h))

    def fn(dy, yv, ov, lv):
        w = _group_softmax(lv)
        dyf = dy.astype(F32)
        inner = jnp.sum(dyf * yv.astype(F32), axis=-1, keepdims=True)
        return w * dyf[None], -w * inner[None]

    return _ew(fn, [(dya, row), (ya, row), (o, blk), (lse, blk)],
               [((G, S, W), BF16, blk, False), ((G, S, W), F32, blk, False)],
               (S // tr, HEADS_PER_DIL), name)


CUM_BLOCK = 256


def _split3(x):
    hi = x.astype(BF16)
    r = x - hi.astype(F32)
    mid = r.astype(BF16)
    lo = (r - mid.astype(F32)).astype(BF16)
    return hi, mid, lo


def _tri_matmul(tri, x):
    return sum(jnp.dot(tri, part, preferred_element_type=F32) for part in _split3(x))


def _log_sigmoid(x):
    return jnp.minimum(x, 0.0) - jnp.log(1.0 + jnp.exp(-jnp.abs(x)))


def _fox_prep(f, b, name):
    S = f.shape[0]
    tb = min(CUM_BLOCK, S)

    def body(f_ref, b_ref, c_ref, carry):
        @pl.when(pl.program_id(0) == 0)
        def _():
            carry[...] = jnp.zeros_like(carry)

        ls = _log_sigmoid(f_ref[...] + b_ref[...])
        r = lax.broadcasted_iota(jnp.int32, (tb, tb), 0)
        cidx = lax.broadcasted_iota(jnp.int32, (tb, tb), 1)
        tri = jnp.where(r >= cidx, 1.0, 0.0).astype(BF16)
        c_ref[...] = _tri_matmul(tri, ls) + carry[...]
        carry[...] += jnp.sum(ls, axis=0, keepdims=True)

    return pl.pallas_call(
        body, name=name, grid=(S // tb,),
        in_specs=[_rows(tb, LANES), _bcast(LANES)], out_specs=_rows(tb, LANES),
        out_shape=jax.ShapeDtypeStruct((S, LANES), F32),
        scratch_shapes=[pltpu.VMEM((1, LANES), F32)],
        compiler_params=_params(("arbitrary",)),
    )(f, b)


def _fox_prep_bwd(dc, f, b, name):
    S = f.shape[0]
    tb = min(CUM_BLOCK, S)
    nb = S // tb

    def body(dc_ref, f_ref, b_ref, df_ref, db_ref, carry):
        @pl.when(pl.program_id(0) == 0)
        def _():
            carry[...] = jnp.zeros_like(carry)
            db_ref[...] = jnp.zeros_like(db_ref)

        r = lax.broadcasted_iota(jnp.int32, (tb, tb), 0)
        cidx = lax.broadcasted_iota(jnp.int32, (tb, tb), 1)
        tri = jnp.where(r <= cidx, 1.0, 0.0).astype(BF16)
        dcv = dc_ref[...]
        dls = _tri_matmul(tri, dcv) + carry[...]
        carry[...] += jnp.sum(dcv, axis=0, keepdims=True)
        z = f_ref[...] + b_ref[...]
        df = dls * (1.0 / (1.0 + jnp.exp(z)))
        df_ref[...] = df
        db_ref[...] += jnp.sum(df, axis=0, keepdims=True)

    rev = pl.BlockSpec((tb, LANES), lambda i: (nb - 1 - i, 0))
    return pl.pallas_call(
        body, name=name, grid=(nb,),
        in_specs=[rev, rev, _bcast(LANES)], out_specs=[rev, _bcast(LANES)],
        out_shape=[jax.ShapeDtypeStruct((S, LANES), F32), jax.ShapeDtypeStruct((1, LANES), F32)],
        scratch_shapes=[pltpu.VMEM((1, LANES), F32)],
        compiler_params=_params(("arbitrary",)),
    )(dc, f, b)


FOX_Q_TILE = 256
_FOX_Q0 = N_DIL_HEADS
_FOX_K0 = N_HEADS + N_DIL_HEADS
_FOX_V0 = 2 * N_HEADS + N_DIL_HEADS


def _fox_scores(q, k, cq, ck, iq, tq, S):
    nt = (((1,), (1,)), ((), ()))
    s = lax.dot_general(q, k, nt, preferred_element_type=F32) * SCALE + cq - ck
    qpos = iq * tq + lax.broadcasted_iota(jnp.int32, (tq, S), 0)
    kpos = lax.broadcasted_iota(jnp.int32, (tq, S), 1)
    return jnp.where(kpos <= qpos, s, NEG)


def _fox_fwd(z, cq, ck, name):
    S = z.shape[0]
    tq = min(FOX_Q_TILE, S)

    def body(q_ref, k_ref, v_ref, cq_ref, ck_ref, o_ref, lse_ref):
        s = _fox_scores(q_ref[...], k_ref[...], cq_ref[...], ck_ref[...], pl.program_id(1), tq, S)
        m = jnp.max(s, axis=-1, keepdims=True)
        p = jnp.exp(s - m)
        l = jnp.sum(p, axis=-1, keepdims=True)
        o = jnp.dot(p.astype(BF16), v_ref[...], preferred_element_type=F32)
        o_ref[...] = (o / l).astype(o_ref.dtype)
        lse_ref[...] = m + jnp.log(l)

    qblk = lambda c0: pl.BlockSpec((tq, HEAD_DIM), lambda h, i: (i, c0 + h))
    full = lambda c0: pl.BlockSpec((S, HEAD_DIM), lambda h, i: (0, c0 + h))
    col = pl.BlockSpec((None, tq, 1), lambda h, i: (h, i, 0))
    rowv = pl.BlockSpec((None, 1, S), lambda h, i: (h, 0, 0))
    return pl.pallas_call(
        body, name=name, grid=(N_FOX_HEADS, S // tq),
        in_specs=[qblk(_FOX_Q0), full(_FOX_K0), full(_FOX_V0), col, rowv],
        out_specs=[qblk(0), col],
        out_shape=[jax.ShapeDtypeStruct((S, FOX_WIDTH), BF16),
                   jax.ShapeDtypeStruct((N_FOX_HEADS, S, 1), F32)],
        compiler_params=_params(("parallel", "parallel")),
    )(z, z, z, cq, ck)


def _fox_bwd(z, cq, ck, lse, yb, dyb, name):
    S = z.shape[0]
    tq = min(FOX_Q_TILE, S)
    nq = S // tq
    nt = (((1,), (1,)), ((), ()))
    tn = (((0,), (0,)), ((), ()))

    def body(q_ref, k_ref, v_ref, cq_ref, ck_ref, lse_ref, o_ref, do_ref,
             dq_ref, dk_ref, dv_ref, dc_ref, dk_acc, dv_acc):
        i = pl.program_id(1)

        @pl.when(i == 0)
        def _():
            dk_acc[...] = jnp.zeros_like(dk_acc)
            dv_acc[...] = jnp.zeros_like(dv_acc)
            dc_ref[...] = jnp.zeros_like(dc_ref)

        q, k, v, do = q_ref[...], k_ref[...], v_ref[...], do_ref[...]
        s = _fox_scores(q, k, cq_ref[...], ck_ref[...], i, tq, S)
        p = jnp.exp(s - lse_ref[...])
        dp = lax.dot_general(do, v, nt, preferred_element_type=F32)
        ds = p * (dp - jnp.sum(p * dp, axis=-1, keepdims=True))
        dsb = ds.astype(BF16)
        dq_ref[...] = (jnp.dot(dsb, k, preferred_element_type=F32) * SCALE).astype(BF16)
        dk_acc[...] += lax.dot_general(dsb, q, tn, preferred_element_type=F32) * SCALE
        dv_acc[...] += lax.dot_general(p.astype(BF16), do, tn, preferred_element_type=F32)
        dc_ref[...] -= jnp.sum(ds, axis=0, keepdims=True)

        @pl.when(i == nq - 1)
        def _():
            dk_ref[...] = dk_acc[...].astype(BF16)
            dv_ref[...] = dv_acc[...].astype(BF16)

    qblk = lambda c0: pl.BlockSpec((tq, HEAD_DIM), lambda h, i: (i, c0 + h))
    full = lambda c0: pl.BlockSpec((S, HEAD_DIM), lambda h, i: (0, c0 + h))
    col = pl.BlockSpec((None, tq, 1), lambda h, i: (h, i, 0))
    rowv = pl.BlockSpec((None, 1, S), lambda h, i: (h, 0, 0))
    wide = jax.ShapeDtypeStruct((S, FOX_WIDTH), BF16)
    return pl.pallas_call(
        body, name=name, grid=(N_FOX_HEADS, nq),
        in_specs=[qblk(_FOX_Q0), full(_FOX_K0), full(_FOX_V0), col, rowv, col, qblk(0), qblk(0)],
        out_specs=[qblk(0), full(0), full(0), rowv],
        out_shape=[wide, wide, wide, jax.ShapeDtypeStruct((N_FOX_HEADS, 1, S), F32)],
        scratch_shapes=[pltpu.VMEM((S, HEAD_DIM), F32), pltpu.VMEM((S, HEAD_DIM), F32)],
        compiler_params=_params(("parallel", "arbitrary")),
    )(z, z, z, cq, ck, lse, yb, dyb)


def _sigmoid(x):
    return 1.0 / (1.0 + jnp.exp(-x))


def _merge_fwd(gates, a, bm, name):
    S, D = a.shape
    tr = _row_tile(S, D * 4)
    g1 = pl.BlockSpec((tr, D), lambda i: (i, 0))
    g2 = pl.BlockSpec((tr, D), lambda i: (i, 1))

    def fn(x1, x2, av, bv):
        return (x1.astype(F32) * av.astype(F32) + x2.astype(F32) * bv.astype(F32),)

    return _ew(fn, [(gates, g1), (gates, g2), (a, _rows(tr, D)), (bm, _rows(tr, D))],
               [((S, D), BF16, _rows(tr, D), False)], (S // tr,), name)


def _merge_bwd(dmerged, gates, a, bm, name):
    S, D = a.shape
    tr = _row_tile(S, D * 8)
    g1 = pl.BlockSpec((tr, D), lambda i: (i, 0))
    g2 = pl.BlockSpec((tr, D), lambda i: (i, 1))

    def fn(dm, x1, x2, av, bv):
        dm, x1, x2 = dm.astype(F32), x1.astype(F32), x2.astype(F32)
        dg1 = dm * av.astype(F32) * x1 * (1.0 - x1)
        dg2 = dm * bv.astype(F32) * x2 * (1.0 - x2)
        dgp = jnp.concatenate([dg1, dg2], axis=1)
        return dm * x1, dm * x2, dgp, jnp.sum(dgp, axis=0, keepdims=True)

    return _ew(fn, [(dmerged, _rows(tr, D)), (gates, g1), (gates, g2), (a, _rows(tr, D)), (bm, _rows(tr, D))],
               [((S, D), BF16, _rows(tr, D), False), ((S, D), BF16, _rows(tr, D), False),
                ((S, 2 * D), BF16, _rows(tr, 2 * D), False), ((1, 2 * D), F32, _bcast(2 * D), True)],
               (S // tr,), name)


def _ple_bwd(dh, pg, pe, name):
    S, D = dh.shape
    tr = _row_tile(S, D * 4)

    def fn(d, g, e):
        g, e = g.astype(F32), e.astype(F32)
        return d * g, d * e * g * (1.0 - g)

    spec = _rows(tr, D)
    return _ew(fn, [(dh, spec), (pg, spec), (pe, spec)],
               [((S, D), BF16, spec, False), ((S, D), BF16, spec, False)], (S // tr,), name)


def _position():
    x, y, c = lax.axis_index("x"), lax.axis_index("y"), lax.axis_index("c")
    chips = [(1 - x, y), (x, 1 - y), (1 - x, 1 - y)]
    return x, y, c, chips


def _remote(src, dst, send_sem, recv_sem, target):
    return pltpu.make_async_remote_copy(src_ref=src, dst_ref=dst, send_sem=send_sem, recv_sem=recv_sem,
                                        device_id=target, device_id_type=MESH)


def _place_shard(w, layer, me, name):
    _, r, cc = w.shape
    tr = _row_tile(r, cc * 4)

    def body(me_ref, w_ref, o_ref):
        o_ref[...] = w_ref[...].astype(o_ref.dtype)

    return pl.pallas_call(
        body, name=name,
        grid_spec=pltpu.PrefetchScalarGridSpec(
            num_scalar_prefetch=1, grid=(r // tr,),
            in_specs=[pl.BlockSpec((None, tr, cc), lambda i, me_ref: (layer, i, 0))],
            out_specs=pl.BlockSpec((None, tr, cc), lambda i, me_ref: (me_ref[0], i, 0))),
        out_shape=jax.ShapeDtypeStruct((N_CHIPS, r, cc), BF16),
        compiler_params=_params(("parallel",)),
    )(me, w)


def _allgather_chips(bufs, name):
    n = len(bufs)

    def body(*refs):
        outs = refs[n:2 * n]
        send_sems, recv_sems = refs[2 * n:]
        x, y, c, chips = _position()
        me = 2 * x + y
        sibling = (x, y, 1 - c)
        pending = []
        for a in range(n):
            rh = outs[a].shape[1] // 2
            mine = outs[a].at[me, pl.ds(c * rh, rh)]
            for j, (cx, cy) in enumerate(chips):
                cp = _remote(mine, mine, send_sems.at[a, j], recv_sems.at[a, j], (cx, cy, c))
                cp.start()
                pending.append(cp)
        for a in range(n):
            rh = outs[a].shape[1] // 2
            for j, (cx, cy) in enumerate(chips):
                landed = outs[a].at[2 * cx + cy, pl.ds(c * rh, rh)]
                _remote(landed, landed, send_sems.at[a, j], recv_sems.at[a, j], (cx, cy, c)).wait_recv()
                fwd = _remote(landed, landed, send_sems.at[a, 3 + j], recv_sems.at[a, 3 + j], sibling)
                fwd.start()
                pending.append(fwd)
        for a in range(n):
            rh = outs[a].shape[1] // 2
            for j, (cx, cy) in enumerate(chips):
                landed = outs[a].at[2 * cx + cy, pl.ds((1 - c) * rh, rh)]
                _remote(landed, landed, send_sems.at[a, 3 + j], recv_sems.at[a, 3 + j], sibling).wait_recv()
        for cp in pending:
            cp.wait_send()

    return pl.pallas_call(
        body, name=name,
        in_specs=[ANY] * n, out_specs=[ANY] * n,
        out_shape=[jax.ShapeDtypeStruct(b.shape, b.dtype) for b in bufs],
        input_output_aliases={a: a for a in range(n)},
        scratch_shapes=[pltpu.SemaphoreType.DMA((n, 6)), pltpu.SemaphoreType.DMA((n, 6))],
    )(*bufs)


def _pair_exchange(grads, name):
    n = len(grads)

    def body(*refs):
        ins, outs = refs[:n], refs[n:2 * n]
        send_sems, recv_sems = refs[2 * n:]
        x, y, c, _ = _position()
        copies = []
        for a in range(n):
            rh = ins[a].shape[1] // 2
            cp = _remote(ins[a].at[:, pl.ds((1 - c) * rh, rh)], outs[a], send_sems.at[a], recv_sems.at[a],
                         (x, y, 1 - c))
            cp.start()
            copies.append(cp)
        for cp in copies:
            cp.wait()

    return pl.pallas_call(
        body, name=name, in_specs=[ANY] * n, out_specs=[ANY] * n,
        out_shape=[jax.ShapeDtypeStruct((g.shape[0], g.shape[1] // 2, g.shape[2]), g.dtype) for g in grads],
        scratch_shapes=[pltpu.SemaphoreType.DMA((n,)), pltpu.SemaphoreType.DMA((n,))],
    )(*grads)


def _pair_sum(mine, theirs, c, name):
    nch, rh, cc = theirs.shape
    tr = _row_tile(rh, cc * 4)
    nb = rh // tr

    def body(c_ref, m_ref, t_ref, o_ref):
        o_ref[...] = (m_ref[...].astype(F32) + t_ref[...].astype(F32)).astype(o_ref.dtype)

    return pl.pallas_call(
        body, name=name,
        grid_spec=pltpu.PrefetchScalarGridSpec(
            num_scalar_prefetch=1, grid=(nch, nb),
            in_specs=[pl.BlockSpec((1, tr, cc), lambda k, i, c_ref: (k, c_ref[0] * nb + i, 0)),
                      pl.BlockSpec((1, tr, cc), lambda k, i, c_ref: (k, i, 0))],
            out_specs=pl.BlockSpec((1, tr, cc), lambda k, i, c_ref: (k, i, 0))),
        out_shape=jax.ShapeDtypeStruct(theirs.shape, BF16),
        compiler_params=_params(("parallel", "parallel")),
    )(c, mine, theirs)


def _chip_exchange(sums, name):
    n = len(sums)

    def body(*refs):
        ins, outs = refs[:n], refs[n:2 * n]
        send_sems, recv_sems = refs[2 * n:]
        x, y, c, chips = _position()
        copies = []
        for a in range(n):
            for j, (cx, cy) in enumerate(chips):
                cp = _remote(ins[a].at[2 * cx + cy], outs[a].at[j], send_sems.at[a, j], recv_sems.at[a, j],
                             (cx, cy, c))
                cp.start()
                copies.append(cp)
        for cp in copies:
            cp.wait()

    return pl.pallas_call(
        body, name=name, in_specs=[ANY] * n, out_specs=[ANY] * n,
        out_shape=[jax.ShapeDtypeStruct((3,) + s.shape[1:], s.dtype) for s in sums],
        scratch_shapes=[pltpu.SemaphoreType.DMA((n, 3)), pltpu.SemaphoreType.DMA((n, 3))],
    )(*sums)


def _chip_sum(own, others, me, c, total, layer, depth, name):
    _, rh, cc = own.shape
    tr = _row_tile(rh, cc * 4)
    nb = rh // tr
    chained = total is not None

    def body(me_ref, c_ref, o_ref, r_ref, *rest):
        g_ref = rest[-1]
        g_ref[...] = (o_ref[0].astype(F32) + r_ref[0].astype(F32)) + (r_ref[1].astype(F32) + r_ref[2].astype(F32))

    in_specs = [pl.BlockSpec((1, tr, cc), lambda i, me_ref, c_ref: (me_ref[0], i, 0)),
                pl.BlockSpec((3, tr, cc), lambda i, me_ref, c_ref: (0, i, 0))]
    args = [me, c, own, others]
    if chained:
        in_specs.append(ANY)
        args.append(total)
    return pl.pallas_call(
        body, name=name,
        grid_spec=pltpu.PrefetchScalarGridSpec(
            num_scalar_prefetch=2, grid=(nb,), in_specs=in_specs,
            out_specs=pl.BlockSpec((None, tr, cc), lambda i, me_ref, c_ref: (layer, c_ref[0] * nb + i, 0))),
        out_shape=jax.ShapeDtypeStruct((depth, 2 * rh, cc), F32),
        input_output_aliases={4: 0} if chained else {},
        compiler_params=_params(("parallel",)),
    )(*args)


def _half_exchange(totals, layer, name):
    n = len(totals)

    def body(*refs):
        outs = refs[n:2 * n]
        send_sems, recv_sems = refs[2 * n:]
        x, y, c, _ = _position()
        copies = []
        for a in range(n):
            rh = outs[a].shape[1] // 2
            mine = outs[a].at[layer, pl.ds(c * rh, rh)]
            cp = _remote(mine, mine, send_sems.at[a], recv_sems.at[a], (x, y, 1 - c))
            cp.start()
            copies.append(cp)
        for a, cp in enumerate(copies):
            rh = outs[a].shape[1] // 2
            theirs = outs[a].at[layer, pl.ds((1 - c) * rh, rh)]
            cp.wait_send()
            _remote(theirs, theirs, send_sems.at[a], recv_sems.at[a], (x, y, 1 - c)).wait_recv()

    return pl.pallas_call(
        body, name=name, in_specs=[ANY] * n, out_specs=[ANY] * n,
        out_shape=[jax.ShapeDtypeStruct(t.shape, t.dtype) for t in totals],
        input_output_aliases={a: a for a in range(n)},
        scratch_shapes=[pltpu.SemaphoreType.DMA((n,)), pltpu.SemaphoreType.DMA((n,))],
    )(*totals)


def _allgather_devices(v, name):
    m_per, n = v.shape

    def body(x_ref, out_ref, send_sems, recv_sems, local_sem):
        x, y, c, chips = _position()
        me, sibling = (x, y, c), (x, y, 1 - c)

        def rows(px, py, pc):
            return out_ref.at[pl.ds((4 * px + 2 * py + pc) * m_per, m_per), :]

        def copy(k, block, to, src=None):
            return _remote(rows(*block) if src is None else src, rows(*block), send_sems.at[k], recv_sems.at[k], to)

        mine = pltpu.make_async_copy(x_ref, rows(*me), local_sem)
        mine.start()
        first = [copy(0, me, sibling, src=x_ref)]
        first += [copy(1 + j, me, (*chip, c), src=x_ref) for j, chip in enumerate(chips)]
        for cp in first:
            cp.start()
        passed = [copy(4 + j, (*chip, c), sibling) for j, chip in enumerate(chips)]
        for j, chip in enumerate(chips):
            copy(1 + j, (*chip, c), me).wait_recv()
            passed[j].start()
        copy(0, sibling, me).wait_recv()
        for j, chip in enumerate(chips):
            copy(4 + j, (*chip, 1 - c), me).wait_recv()
        for cp in first + passed:
            cp.wait_send()
        mine.wait()

    vm = pl.BlockSpec(memory_space=pltpu.VMEM)
    return pl.pallas_call(
        body, name=name, in_specs=[vm], out_specs=vm,
        out_shape=jax.ShapeDtypeStruct((8 * m_per, n), v.dtype),
        scratch_shapes=[pltpu.SemaphoreType.DMA((7,)), pltpu.SemaphoreType.DMA((7,)), pltpu.SemaphoreType.DMA],
    )(v)


def _adamw_math(w, g, m, v):
    m = ADAM_B1 * m + (1.0 - ADAM_B1) * g
    v = ADAM_B2 * v + (1.0 - ADAM_B2) * (g * g)
    m_hat = m / (1.0 - ADAM_B1 ** ADAM_STEP)
    v_hat = v / (1.0 - ADAM_B2 ** ADAM_STEP)
    delta = -ADAM_LR * (m_hat / (jnp.sqrt(v_hat) + ADAM_EPS) + ADAM_WD * w)
    return delta, m, v


def _adamw(w, g, m, v, name):
    depth, r, cc = w.shape
    tr = _row_tile(r, cc * 4 * 2)
    spec = pl.BlockSpec((1, tr, cc), lambda l, i: (l, i, 0))

    def fn(wv, gv, mv, vv):
        return (gv,) + _adamw_math(wv, gv, mv, vv)

    out = (w.shape, F32, spec, False)
    return _ew(fn, [(w, spec), (g, spec), (m, spec), (v, spec)], [out] * 4, (depth, r // tr), name)


def _adamw_small(w, parts, m, v, name):
    M = w.shape[0]

    def body(w_ref, p_ref, m_ref, v_ref, g_ref, d_ref, nm_ref, nv_ref):
        g = p_ref[pl.ds(0, M), :]
        for k in range(1, 8):
            g = g + p_ref[pl.ds(k * M, M), :]
        d, nm, nv = _adamw_math(w_ref[...], g, m_ref[...], v_ref[...])
        g_ref[...] = g
        d_ref[...] = d
        nm_ref[...] = nm
        nv_ref[...] = nv

    vm = pl.BlockSpec(memory_space=pltpu.VMEM)
    return pl.pallas_call(
        body, name=name, in_specs=[vm] * 4, out_specs=[vm] * 4,
        out_shape=[jax.ShapeDtypeStruct(w.shape, F32)] * 4,
    )(w, parts, m, v)


def _layer_fwd(h0, p_l, W, small, tabs):
    S, D = h0.shape
    ctab, stab = tabs
    u = _rms_fwd(h0, small["g_mix"], "rms_mix")
    z = _mm(u, W["w_qkv"], mode="nn", name="mm_qkv")
    f = _mm(u, W["w_f"], mode="nn", name="mm_f", out_dtypes=(F32,))
    gates = _mm(u, W["w_gate"], mode="nn", name="mm_gate", b_chunked=True, extras=[(small["b_gate"], "row")],
                epilogue=lambda acc, b: (_sigmoid(acc + b),))
    qr, kr = _rope(z, 0, z, N_HEADS, ctab, stab, 1.0, "rope_fwd")
    qs, ks, vs = _to_strided(qr), _to_strided(kr), _to_strided(z[:, 2 * ATTN_WIDTH:2 * ATTN_WIDTH + DIL_WIDTH])
    o_s, lse_s = _sw_fwd(qs, ks, vs, "sw_fwd")
    o_g, lse_g = _from_strided(o_s, S), _from_strided(lse_s, S)
    ya = _mix_fwd(o_g, lse_g, "mix_fwd")
    a = _mm(ya, W["w_br_a"], mode="nn", name="mm_br_a", b_chunked=True)
    cum = _fox_prep(f, small["b_f"], "fox_prep")
    cq = cum[:, :N_FOX_HEADS].T[:, :, None]
    ck = cum[:, :N_FOX_HEADS].T[:, None, :]
    yb, lse_f = _fox_fwd(z, cq, ck, "fox_fwd")
    bm = _mm(yb, W["w_br_b"], mode="nn", name="mm_br_b", b_chunked=True)
    merged = _merge_fwd(gates, a, bm, "merge_fwd")
    h1 = _mm(merged, W["w_o"], mode="nn", name="mm_o", out_dtypes=(F32,), extras=[(h0, "tile")],
             epilogue=lambda acc, r: (acc + r,), tj=256)
    m = _rms_fwd(h1, small["g_mlp"], "rms_mlp")
    ra, act = _mm(m, W["w_up"], mode="nn", name="mm_up", b_chunked=True, out_dtypes=(BF16, BF16),
                  epilogue=lambda acc: (jnp.maximum(acc, 0.0), jnp.square(jnp.maximum(acc, 0.0))))
    h2 = _mm(act, W["w_down"], mode="nn", name="mm_down", out_dtypes=(F32,), extras=[(h1, "tile")],
             epilogue=lambda acc, r: (acc + r,), ti=1024)
    n = _rms_fwd(h2, small["g_ple"], "rms_ple")
    pg = _mm(n, W["w_ple_gate"], mode="nn", name="mm_ple_gate", epilogue=lambda acc: (_sigmoid(acc),))
    h3, pe = _mm(p_l, W["w_ple"], mode="nn", name="mm_ple", b_chunked=True, out_dtypes=(F32, BF16),
                 extras=[(h2, "tile"), (pg, "tile")], tj=256,
                 epilogue=lambda acc, r, g: (r + g.astype(F32) * acc, acc))
    saved = dict(h0=h0, u=u, z=z, f=f, gates=gates, qs=qs, ks=ks, vs=vs, lse_s=lse_s, o_g=o_g, lse_g=lse_g,
                 ya=ya, a=a, cq=cq, ck=ck, yb=yb, lse_f=lse_f, bm=bm, merged=merged, h1=h1, m=m, ra=ra,
                 act=act, h2=h2, n=n, pg=pg, pe=pe, p_l=p_l)
    return h3, saved


def _layer_bwd(dh3, sv, W, small, tabs):
    S, D = dh3.shape
    ctab, stab = tabs
    gw, gs = {}, {}
    tn = functools.partial(_mm, mode="tn", ti=512, tj=1024)
    dpe, dpg = _ple_bwd(dh3, sv["pg"], sv["pe"], "ple_bwd")
    gw["w_ple"] = tn(sv["p_l"], dpe, name="dw_ple", out_chunks=N_CHIPS)
    gw["w_ple_gate"] = tn(sv["n"], dpg, name="dw_ple_gate").reshape(N_CHIPS, D // N_CHIPS, D)
    dn = _mm(dpg, W["w_ple_gate"], mode="nt", name="mm_dn")
    dh2, dh2b, gs["g_ple"] = _rms_bwd(sv["h2"], small["g_ple"], dn, dh3, "rms_ple_bwd")
    da = _mm(dh2b, W["w_down"], mode="nt", name="mm_dact", extras=[(sv["ra"], "tile")],
             epilogue=lambda acc, r: (acc * (2.0 * r.astype(F32)),))
    FF = da.shape[1]
    gw["w_down"] = tn(sv["act"], dh2b, name="dw_down").reshape(N_CHIPS, FF // N_CHIPS, D)
    gw["w_up"] = tn(sv["m"], da, name="dw_up", out_chunks=N_CHIPS)
    dm = _mm(da, W["w_up"], mode="nt", name="mm_dm", b_chunked=True)
    dh1, dh1b, gs["g_mlp"] = _rms_bwd(sv["h1"], small["g_mlp"], dm, dh2, "rms_mlp_bwd")
    dmerged = _mm(dh1b, W["w_o"], mode="nt", name="mm_dmerged")
    gw["w_o"] = tn(sv["merged"], dh1b, name="dw_o").reshape(N_CHIPS, D // N_CHIPS, D)
    d_a, d_b, dgp, gs["b_gate"] = _merge_bwd(dmerged, sv["gates"], sv["a"], sv["bm"], "merge_bwd")
    gw["w_gate"] = tn(sv["u"], dgp, name="dw_gate", out_chunks=N_CHIPS)
    gw["w_br_a"] = tn(sv["ya"], d_a, name="dw_br_a", out_chunks=N_CHIPS, tj=512)
    gw["w_br_b"] = tn(sv["yb"], d_b, name="dw_br_b", out_chunks=N_CHIPS, tj=512)
    dya = _mm(d_a, W["w_br_a"], mode="nt", name="mm_dya", b_chunked=True)
    dyb = _mm(d_b, W["w_br_b"], mode="nt", name="mm_dyb", b_chunked=True)
    z = sv["z"]
    dq_f, dk_f, dv_f, dck = _fox_bwd(z, sv["cq"], sv["ck"], sv["lse_f"], sv["yb"], dyb, "fox_bwd")
    dc = jnp.pad(dck[:, 0, :].T, ((0, 0), (0, LANES - N_FOX_HEADS)))
    df, dbf = _fox_prep_bwd(dc, sv["f"], small["b_f"], "fox_prep_bwd")
    gs["b_f"] = dbf[:, :N_FOX_HEADS]
    lane = jnp.arange(LANES)[None, :] < N_FOX_HEADS
    dzf = jnp.where(lane, df, 0.0).astype(BF16)
    do_g, tt_g = _mix_bwd(dya, sv["ya"], sv["o_g"], sv["lse_g"], "mix_bwd")
    do_s = _to_strided(do_g.transpose(1, 0, 2).reshape(S, DIL_WIDTH))
    tt_s = _to_strided(tt_g.transpose(1, 0, 2).reshape(S, DIL_WIDTH))
    dq_s, dk_s, dv_s = _sw_bwd(sv["qs"], sv["ks"], sv["vs"], do_s, sv["lse_s"], tt_s, "sw_bwd")
    unstride = lambda t: _from_strided(t, S).transpose(1, 0, 2).reshape(S, DIL_WIDTH)
    dq_a, dk_a = _rope(unstride(dq_s), 0, unstride(dk_s), 0, ctab, stab, -1.0, "rope_bwd")
    dz = jnp.concatenate([dq_a, dq_f, dk_a, dk_f, unstride(dv_s), dv_f], axis=1)
    g_qkv = tn(sv["u"], dz, name="dw_qkv")
    g_f = tn(sv["u"], dzf, name="dw_f", tj=128)
    cols = W["w_in_cols"]
    g_in = jnp.concatenate([g_qkv, g_f[:, :N_FOX_HEADS]], axis=1)
    gw["w_in"] = g_in.reshape(D, N_CHIPS, cols).transpose(1, 0, 2)
    du = _mm(dzf, W["w_f"], mode="nt", name="mm_du_f", out_dtypes=(F32,))
    du = _mm(dgp, W["w_gate"], mode="nt", name="mm_du_gate", b_chunked=True, out_dtypes=(F32,),
             extras=[(du, "tile")], epilogue=lambda acc, r: (acc + r,), tj=256)
    du = _mm(dz, W["w_qkv"], mode="nt", name="mm_du_qkv", extras=[(du, "tile")],
             epilogue=lambda acc, r: (acc + r,), tj=256)
    dh0, _, gs["g_mix"] = _rms_bwd(sv["h0"], small["g_mix"], du, dh1, "rms_mix_bwd")
    return dh0, gw, gs


BIG = ("w_in", "w_gate", "w_br_a", "w_br_b", "w_o", "w_up", "w_down", "w_ple", "w_ple_gate")
SMALL = ("g_mix", "b_f", "b_gate", "g_mlp", "g_ple", "g_final")
ORDER = ("g_mix", "w_in", "b_f", "w_gate", "b_gate", "w_br_a", "w_br_b", "w_o", "g_mlp", "w_up", "w_down",
         "g_ple", "w_ple", "w_ple_gate", "g_final")


def _gathered_layer_weights(full, D):
    W = {}
    for name in ("w_gate", "w_br_a", "w_br_b", "w_up", "w_ple"):
        W[name] = full[name]
    for name in ("w_o", "w_down", "w_ple_gate"):
        t = full[name]
        W[name] = t.reshape(t.shape[0] * t.shape[1], t.shape[2])
    w_in = full["w_in"]
    cols = w_in.shape[2]
    w_in = w_in.transpose(1, 0, 2).reshape(D, N_CHIPS * cols)
    W["w_qkv"] = w_in[:, :3 * ATTN_WIDTH]
    W["w_f"] = jnp.pad(w_in[:, 3 * ATTN_WIDTH:], ((0, 0), (0, LANES - N_FOX_HEADS)))
    W["w_in_cols"] = cols
    return W


def _pack_rows(vals):
    flat = jnp.concatenate([v.reshape(-1) for v in vals])
    rows = -(-flat.shape[0] // (8 * LANES)) * 8
    return jnp.pad(flat, (0, rows * LANES - flat.shape[0])).reshape(rows, LANES)


def _unpack_rows(packed, shapes):
    flat = packed.reshape(-1)
    out, pos = [], 0
    for s in shapes:
        size = 1
        for dim in s:
            size *= dim
        out.append(flat[pos:pos + size].reshape(s))
        pos += size
    return out


def _local_step(x, p, small_w, layer_weights, loss_target):
    depth = p.shape[0]
    S, D = x.shape
    tabs = _rope_tables(S)
    h = x
    saved, weights, smalls = [], [], []
    for l in range(depth):
        W = _gathered_layer_weights(layer_weights(l), D)
        sm = dict(g_mix=small_w["g_mix"][l][None], g_mlp=small_w["g_mlp"][l][None],
                  g_ple=small_w["g_ple"][l][None], b_gate=small_w["b_gate"][l][None],
                  b_f=jnp.pad(small_w["b_f"][l][None], ((0, 0), (0, LANES - N_FOX_HEADS))))
        h, sv = _layer_fwd(h, p[l].astype(BF16), W, sm, tabs)
        saved.append(sv)
        weights.append(W)
        smalls.append(sm)
    dh, loss_row, dg_final = _loss_head(h, small_w["g_final"][None], loss_target, "loss_head")
    gws, gss = [None] * depth, [None] * depth
    for l in reversed(range(depth)):
        dh, gws[l], gss[l] = _layer_bwd(dh, saved[l], weights[l], smalls[l], tabs)
    return loss_row, dh, gws, gss, dg_final


def _reduce_layer(gw, totals, layer, depth, c_arr, me_arr):
    grads = [gw[n] for n in BIG]
    theirs = _pair_exchange(grads, "rs_pair_exchange")
    sums = [_pair_sum(g, t, c_arr, "rs_pair_sum") for g, t in zip(grads, theirs)]
    others = _chip_exchange(sums, "rs_chip_exchange")
    if totals is None:
        totals = [None] * len(grads)
    totals = [_chip_sum(s, o, me_arr, c_arr, t, layer, depth, "rs_chip_sum")
              for s, o, t in zip(sums, others, totals)]
    return _half_exchange(totals, layer, "rs_half_exchange")


def kernel(x, p, g_mix, w_in, b_f, w_gate, b_gate, w_br_a, w_br_b, w_o, g_mlp, w_up, w_down, g_ple, w_ple, w_ple_gate, g_final, loss_target, m_g_mix, m_w_in, m_b_f, m_w_gate, m_b_gate, m_w_br_a, m_w_br_b, m_w_o, m_g_mlp, m_w_up, m_w_down, m_g_ple, m_w_ple, m_w_ple_gate, m_g_final, v_g_mix, v_w_in, v_b_f, v_w_gate, v_b_gate, v_w_br_a, v_w_br_b, v_w_o, v_g_mlp, v_w_up, v_w_down, v_g_ple, v_w_ple, v_w_ple_gate, v_g_final):
    w = dict(g_mix=g_mix, w_in=w_in, b_f=b_f, w_gate=w_gate, b_gate=b_gate, w_br_a=w_br_a, w_br_b=w_br_b,
             w_o=w_o, g_mlp=g_mlp, w_up=w_up, w_down=w_down, g_ple=g_ple, w_ple=w_ple, w_ple_gate=w_ple_gate,
             g_final=g_final)
    m = dict(g_mix=m_g_mix, w_in=m_w_in, b_f=m_b_f, w_gate=m_w_gate, b_gate=m_b_gate, w_br_a=m_w_br_a,
             w_br_b=m_w_br_b, w_o=m_w_o, g_mlp=m_g_mlp, w_up=m_w_up, w_down=m_w_down, g_ple=m_g_ple,
             w_ple=m_w_ple, w_ple_gate=m_w_ple_gate, g_final=m_g_final)
    v = dict(g_mix=v_g_mix, w_in=v_w_in, b_f=v_b_f, w_gate=v_w_gate, b_gate=v_b_gate, w_br_a=v_w_br_a,
             w_br_b=v_w_br_b, w_o=v_w_o, g_mlp=v_g_mlp, w_up=v_w_up, w_down=v_w_down, g_ple=v_g_ple,
             w_ple=v_w_ple, w_ple_gate=v_w_ple_gate, g_final=v_g_final)
    depth = p.shape[0]
    cx, cy, cc = lax.axis_index("x"), lax.axis_index("y"), lax.axis_index("c")
    c_arr = jnp.reshape(cc, (1,)).astype(jnp.int32)
    me_arr = jnp.reshape(2 * cx + cy, (1,)).astype(jnp.int32)

    def layer_weights(l):
        placed = [_place_shard(w[n], l, me_arr, "place_shard") for n in BIG]
        return dict(zip(BIG, _allgather_chips(placed, "allgather_weights")))

    loss_row, grad_x, gws, gss, dg_final = _local_step(x[0], p[:, 0], w, layer_weights, loss_target[0])

    totals = None
    for l in reversed(range(depth)):
        totals = _reduce_layer(gws[l], totals, l, depth, c_arr, me_arr)
    big_out = {n: _adamw(w[n], t, m[n], v[n], "adamw") for n, t in zip(BIG, totals)}

    small_grads = [jnp.stack([gss[l][n][0] for l in range(depth)]) for n in SMALL[:-1]] + [dg_final[0]]
    shapes = [w[n].shape for n in SMALL]
    parts = _allgather_devices(_pack_rows(small_grads), "allgather_small")
    packed = _adamw_small(_pack_rows([w[n] for n in SMALL]), parts, _pack_rows([m[n] for n in SMALL]),
                          _pack_rows([v[n] for n in SMALL]), "adamw_small")
    small_out = {n: vals for n, vals in zip(SMALL, zip(*[_unpack_rows(t, shapes) for t in packed]))}

    loss = lax.psum(loss_row[0, 0], ("x", "y", "c"))
    out = {**big_out, **small_out}
    return (loss, grad_x[None], *[out[n][0] for n in ORDER], *[out[n][1] for n in ORDER],
            *[out[n][2] for n in ORDER], *[out[n][3] for n in ORDER])
```

```python
import functools

import jax
import jax.numpy as jnp
from jax import lax
from jax.experimental import pallas as pl
from jax.experimental.pallas import tpu as pltpu

F32 = jnp.float32
BF16 = jnp.bfloat16

HEAD_DIM = 128
N_HEADS = 16
N_DIL_HEADS = 12
N_FOX_HEADS = 4
HEADS_PER_DIL = 4
DILATIONS = (1, 4, 16)
BLOCK = 128
ATTN_WIDTH = N_HEADS * HEAD_DIM
DIL_WIDTH = N_DIL_HEADS * HEAD_DIM
FOX_WIDTH = N_FOX_HEADS * HEAD_DIM
ROPE_THETA = 500000.0
ROPE_HALF = 16
NORM_EPS = 1e-6
SCALE = HEAD_DIM ** -0.5
NEG = -1e30

ADAM_LR = 0.001
ADAM_B1 = 0.9
ADAM_B2 = 0.999
ADAM_EPS = 1e-08
ADAM_WD = 0.01
ADAM_STEP = 10

N_CHIPS = 4
V7X_VMEM_LIMIT_BYTES = 56 * 1024 * 1024
LANES = 128
MESH = pl.DeviceIdType.MESH
ANY = pl.BlockSpec(memory_space=pl.ANY)


def _params(sem):
    return pltpu.CompilerParams(dimension_semantics=sem, vmem_limit_bytes=V7X_VMEM_LIMIT_BYTES)


def _tile(n, pref):
    if n <= pref:
        return n
    t = (pref // LANES) * LANES
    while t > LANES and n % t:
        t -= LANES
    assert n % t == 0, (n, pref)
    return t


def _mm(a, b, *, mode, name, out_dtypes=(BF16,), epilogue=None, extras=(), b_chunked=False,
        out_chunks=0, ti=2048, tj=512, tc=2048):
    if mode == "tn":
        C, I = a.shape
    else:
        I, C = a.shape
    if b_chunked:
        nch, d0, n = b.shape
        if mode == "nn":
            assert d0 == C
            J = nch * n
        else:
            assert mode == "nt" and nch * n == C
            J = d0
    elif mode == "nt":
        J = b.shape[0]
        assert b.shape[1] == C
    else:
        assert b.shape[0] == C
        J = b.shape[1]
    ti, tc = _tile(I, ti), _tile(C, tc)
    if b_chunked and mode == "nn":
        tj = _tile(n, tj)
    elif out_chunks:
        tj = _tile(J // out_chunks, tj)
    else:
        tj = _tile(J, tj)
    if b_chunked and mode == "nt":
        tc = _tile(n, tc)
    ni, nj, nc = I // ti, J // tj, C // tc

    if mode == "tn":
        a_spec = pl.BlockSpec((tc, ti), lambda i, j, c: (c, i))
        dims = (((0,), (0,)), ((), ()))
    else:
        a_spec = pl.BlockSpec((ti, tc), lambda i, j, c: (i, c))
        dims = (((1,), (0,)), ((), ())) if mode == "nn" else (((1,), (1,)), ((), ()))
    if mode == "nt":
        if b_chunked:
            cb = n // tc
            b_spec = pl.BlockSpec((None, tj, tc), lambda i, j, c: (c // cb, j, c % cb))
        else:
            b_spec = pl.BlockSpec((tj, tc), lambda i, j, c: (j, c))
    else:
        if b_chunked:
            jb = n // tj
            b_spec = pl.BlockSpec((None, tc, tj), lambda i, j, c: (j // jb, c, j % jb))
        else:
            b_spec = pl.BlockSpec((tc, tj), lambda i, j, c: (c, j))
    extra_specs = []
    for arr, kind in extras:
        if kind == "tile":
            assert arr.shape == (I, J), (arr.shape, I, J)
            extra_specs.append(pl.BlockSpec((ti, tj), lambda i, j, c: (i, j)))
        else:
            assert arr.shape == (1, J)
            extra_specs.append(pl.BlockSpec((1, tj), lambda i, j, c: (0, j)))
    if out_chunks:
        ob = (J // out_chunks) // tj
        out_spec = pl.BlockSpec((None, ti, tj), lambda i, j, c: (j // ob, i, j % ob))
        out_shape = [jax.ShapeDtypeStruct((out_chunks, I, J // out_chunks), d) for d in out_dtypes]
    else:
        out_spec = pl.BlockSpec((ti, tj), lambda i, j, c: (i, j))
        out_shape = [jax.ShapeDtypeStruct((I, J), d) for d in out_dtypes]
    ne, no = len(extras), len(out_dtypes)
    if epilogue is None:
        epilogue = lambda acc: (acc,)

    def body(a_ref, b_ref, *rest):
        extra_refs, out_refs = rest[:ne], rest[ne:ne + no]

        def finish(acc):
            outs = epilogue(acc, *[r[...] for r in extra_refs])
            for o_ref, val in zip(out_refs, outs):
                o_ref[...] = val.astype(o_ref.dtype)

        part = lax.dot_general(a_ref[...], b_ref[...], dims, preferred_element_type=F32)
        if nc == 1:
            finish(part)
        else:
            acc_ref = rest[-1]
            k = pl.program_id(2)

            @pl.when(k == 0)
            def _():
                acc_ref[...] = part

            @pl.when(k > 0)
            def _():
                acc_ref[...] += part

            @pl.when(k == nc - 1)
            def _():
                finish(acc_ref[...])

    outs = pl.pallas_call(
        body, name=name, grid=(ni, nj, nc),
        in_specs=[a_spec, b_spec] + extra_specs,
        out_specs=[out_spec] * no, out_shape=out_shape,
        scratch_shapes=[pltpu.VMEM((ti, tj), F32)] if nc > 1 else [],
        compiler_params=_params(("parallel", "parallel", "arbitrary")),
    )(a, b, *[e[0] for e in extras])
    return outs[0] if no == 1 else tuple(outs)


def _ew(fn, ins, outs, grid, name):
    n_in = len(ins)
    has_acc = any(o[3] for o in outs)
    assert not has_acc or len(grid) == 1

    def body(*refs):
        vals = fn(*[r[...] for r in refs[:n_in]])
        for o_ref, o, val in zip(refs[n_in:], outs, vals):
            if o[3]:
                step = pl.program_id(0)

                @pl.when(step == 0)
                def _(o_ref=o_ref, val=val):
                    o_ref[...] = val

                @pl.when(step > 0)
                def _(o_ref=o_ref, val=val):
                    o_ref[...] += val
            else:
                o_ref[...] = val.astype(o_ref.dtype)

    sem = ("arbitrary",) if has_acc else ("parallel",) * len(grid)
    res = pl.pallas_call(
        body, name=name, grid=grid,
        in_specs=[i[1] for i in ins], out_specs=[o[2] for o in outs],
        out_shape=[jax.ShapeDtypeStruct(o[0], o[1]) for o in outs],
        compiler_params=_params(sem),
    )(*[i[0] for i in ins])
    return res[0] if len(outs) == 1 else tuple(res)


def _rows(tr, w):
    return pl.BlockSpec((tr, w), lambda i: (i, 0))


def _bcast(w):
    return pl.BlockSpec((1, w), lambda i: (0, 0))


def _row_tile(S, width_bytes):
    tr = 512
    while tr > 16 and tr * width_bytes > 2 * 1024 * 1024:
        tr //= 2
    return min(tr, S)


def _rms_fwd(h, g, name):
    S, D = h.shape
    tr = _row_tile(S, D * 4)

    def fn(x, gg):
        r = lax.rsqrt(jnp.mean(x * x, axis=-1, keepdims=True) + NORM_EPS)
        return (x * r * gg,)

    return _ew(fn, [(h, _rows(tr, D)), (g, _bcast(D))], [((S, D), BF16, _rows(tr, D), False)],
               (S // tr,), name)


def _rms_bwd(x, g, dy, dres, name):
    S, D = x.shape
    tr = _row_tile(S, D * 4)

    def fn(xv, gg, dyv, dr):
        r = lax.rsqrt(jnp.mean(xv * xv, axis=-1, keepdims=True) + NORM_EPS)
        dyf = dyv.astype(F32)
        gy = dyf * gg
        dx = r * gy - xv * (r * r * r) * jnp.mean(xv * gy, axis=-1, keepdims=True)
        tot = dr + dx
        dg = jnp.sum(dyf * xv * r, axis=0, keepdims=True)
        return tot, tot, dg

    return _ew(fn, [(x, _rows(tr, D)), (g, _bcast(D)), (dy, _rows(tr, D)), (dres, _rows(tr, D))],
               [((S, D), F32, _rows(tr, D), False), ((S, D), BF16, _rows(tr, D), False),
                ((1, D), F32, _bcast(D), True)], (S // tr,), name)


def _loss_head(h, g, target, name):
    S, D = h.shape
    tr = _row_tile(S, D * 4)

    def fn(xv, gg, tgt):
        r = lax.rsqrt(jnp.mean(xv * xv, axis=-1, keepdims=True) + NORM_EPS)
        y = xv * r * gg
        e = y - tgt
        loss = 0.5 * jnp.sum(jnp.mean(e * e, axis=-1, keepdims=True), axis=0, keepdims=True)
        dy = e * (1.0 / D)
        gy = dy * gg
        dx = r * gy - xv * (r * r * r) * jnp.mean(xv * gy, axis=-1, keepdims=True)
        dg = jnp.sum(dy * xv * r, axis=0, keepdims=True)
        return dx, jnp.broadcast_to(loss, (1, LANES)), dg

    return _ew(fn, [(h, _rows(tr, D)), (g, _bcast(D)), (target, _rows(tr, D))],
               [((S, D), F32, _rows(tr, D), False), ((1, LANES), F32, _bcast(LANES), True),
                ((1, D), F32, _bcast(D), True)], (S // tr,), name)


def _rope_tables(S):
    inv = ROPE_THETA ** (-jnp.arange(ROPE_HALF, dtype=F32) / ROPE_HALF)
    ang = jnp.arange(S, dtype=F32)[:, None] * inv[None, :]
    cos, sin = jnp.cos(ang), jnp.sin(ang)
    rest = HEAD_DIM - 2 * ROPE_HALF
    ctab = jnp.concatenate([cos, cos, jnp.ones((S, rest), F32)], axis=1)
    stab = jnp.concatenate([-sin, sin, jnp.zeros((S, rest), F32)], axis=1)
    return ctab, stab


def _swap_halves(x):
    lane = lax.broadcasted_iota(jnp.int32, x.shape, 1)
    return jnp.where(lane < ROPE_HALF, pltpu.roll(x, HEAD_DIM - ROPE_HALF, 1), pltpu.roll(x, ROPE_HALF, 1))


def _rope(q_src, q_col0, k_src, k_col0, ctab, stab, sign, name):
    S = q_src.shape[0]
    tr = min(512, S)

    def fn(q, k, ct, st):
        outs = []
        for v in (q, k):
            vf = v.astype(F32)
            outs.append(vf * ct + sign * _swap_halves(vf) * st)
        return tuple(outs)

    head = lambda c0: pl.BlockSpec((tr, HEAD_DIM), lambda i, h: (i, c0 + h))
    tab = pl.BlockSpec((tr, HEAD_DIM), lambda i, h: (i, 0))
    out = ((S, DIL_WIDTH), BF16, head(0), False)
    return _ew(fn, [(q_src, head(q_col0)), (k_src, head(k_col0)), (ctab, tab), (stab, tab)],
               [out, out], (S // tr, N_DIL_HEADS), name)


SW_BLOCKS_PER_STEP = 8


def _to_strided(x):
    S = x.shape[0]
    parts = []
    for g, d in enumerate(DILATIONS):
        xg = x[:, g * 512:(g + 1) * 512].reshape(S // d, d, HEADS_PER_DIL, HEAD_DIM)
        parts.append(xg.transpose(1, 2, 0, 3).reshape(-1, BLOCK, HEAD_DIM))
    return jnp.concatenate(parts, axis=0)


def _from_strided(y, S):
    per = y.shape[0] // len(DILATIONS)
    parts = []
    for g, d in enumerate(DILATIONS):
        yg = y[g * per:(g + 1) * per].reshape(d, HEADS_PER_DIL, S // d, HEAD_DIM)
        parts.append(yg.transpose(2, 0, 1, 3).reshape(S, HEADS_PER_DIL * HEAD_DIM))
    return jnp.stack(parts, axis=0)


def _seq_blocks(b0, per_group):
    g = b0 // per_group
    n0 = per_group // HEADS_PER_DIL
    return jnp.where(g == 0, n0, jnp.where(g == 1, n0 // 4, n0 // 16))


def _sw_masks():
    qi = lax.broadcasted_iota(jnp.int32, (BLOCK, BLOCK), 0)
    ki = lax.broadcasted_iota(jnp.int32, (BLOCK, BLOCK), 1)
    return qi >= ki, qi <= ki


def _sw_fwd(q, k, v, name):
    NB = q.shape[0]
    T = SW_BLOCKS_PER_STEP
    per_group = NB // len(DILATIONS)
    nt = (((1,), (1,)), ((), ()))

    def body(q_ref, k_ref, v_ref, kp_ref, vp_ref, o_ref, lse_ref):
        b0 = pl.program_id(0) * T
        nseq = _seq_blocks(b0, per_group)
        cur_mask, prev_mask = _sw_masks()
        for t in range(T):
            has_prev = ((b0 + t) & (nseq - 1)) != 0
            qt = q_ref[t]
            kp = kp_ref[0] if t == 0 else k_ref[t - 1]
            vp = vp_ref[0] if t == 0 else v_ref[t - 1]
            s_c = lax.dot_general(qt, k_ref[t], nt, preferred_element_type=F32) * SCALE
            s_p = lax.dot_general(qt, kp, nt, preferred_element_type=F32) * SCALE
            s_c = jnp.where(cur_mask, s_c, NEG)
            s_p = jnp.where(prev_mask, s_p, NEG) + jnp.where(has_prev, 0.0, NEG)
            m = jnp.maximum(jnp.max(s_c, axis=-1, keepdims=True), jnp.max(s_p, axis=-1, keepdims=True))
            p_c = jnp.exp(s_c - m)
            p_p = jnp.exp(s_p - m)
            l = jnp.sum(p_c, axis=-1, keepdims=True) + jnp.sum(p_p, axis=-1, keepdims=True)
            o = (jnp.dot(p_c.astype(BF16), v_ref[t], preferred_element_type=F32)
                 + jnp.dot(p_p.astype(BF16), vp, preferred_element_type=F32))
            o_ref[t] = (o / l).astype(o_ref.dtype)
            lse_ref[t] = jnp.broadcast_to(m + jnp.log(l), (BLOCK, HEAD_DIM))

    tile = pl.BlockSpec((T, BLOCK, HEAD_DIM), lambda i: (i, 0, 0))
    before = pl.BlockSpec((1, BLOCK, HEAD_DIM), lambda i: (jnp.maximum(i * T - 1, 0), 0, 0))
    return pl.pallas_call(
        body, name=name, grid=(NB // T,),
        in_specs=[tile, tile, tile, before, before], out_specs=[tile, tile],
        out_shape=[jax.ShapeDtypeStruct(q.shape, BF16), jax.ShapeDtypeStruct(q.shape, F32)],
        compiler_params=_params(("parallel",)),
    )(q, k, v, k, v)


def _sw_bwd(q, k, v, do, lse, tt, name):
    NB = q.shape[0]
    T = SW_BLOCKS_PER_STEP
    per_group = NB // len(DILATIONS)
    nt = (((1,), (1,)), ((), ()))
    tn = (((0,), (0,)), ((), ()))

    def body(q_ref, k_ref, v_ref, do_ref, lse_ref, tt_ref, kp_ref, vp_ref, qn_ref, don_ref, lsen_ref,
             ttn_ref, dq_ref, dk_ref, dv_ref):
        b0 = pl.program_id(0) * T
        nseq = _seq_blocks(b0, per_group)
        cur_mask, prev_mask = _sw_masks()

        def probs(qq, kk, lse_b, mask, gate):
            s = lax.dot_general(qq, kk, nt, preferred_element_type=F32) * SCALE
            return jnp.exp(jnp.where(mask, s, NEG) + gate - lse_b)

        for t in range(T):
            has_prev = jnp.where(((b0 + t) & (nseq - 1)) != 0, 0.0, NEG)
            has_next = jnp.where(((b0 + t + 1) & (nseq - 1)) != 0, 0.0, NEG)
            last = t == T - 1
            qt, kt, vt, dot = q_ref[t], k_ref[t], v_ref[t], do_ref[t]
            kp = kp_ref[0] if t == 0 else k_ref[t - 1]
            vp = vp_ref[0] if t == 0 else v_ref[t - 1]
            qn = qn_ref[0] if last else q_ref[t + 1]
            don = don_ref[0] if last else do_ref[t + 1]
            lsen = lsen_ref[0] if last else lse_ref[t + 1]
            ttn = ttn_ref[0] if last else tt_ref[t + 1]
            p_cc = probs(qt, kt, lse_ref[t], cur_mask, 0.0)
            p_cp = probs(qt, kp, lse_ref[t], prev_mask, has_prev)
            p_nc = probs(qn, kt, lsen, prev_mask, has_next)
            ds_cc = p_cc * (lax.dot_general(dot, vt, nt, preferred_element_type=F32) + tt_ref[t])
            ds_cp = p_cp * (lax.dot_general(dot, vp, nt, preferred_element_type=F32) + tt_ref[t])
            ds_nc = p_nc * (lax.dot_general(don, vt, nt, preferred_element_type=F32) + ttn)
            ds_cc, ds_cp, ds_nc = ds_cc.astype(BF16), ds_cp.astype(BF16), ds_nc.astype(BF16)
            dq = (jnp.dot(ds_cc, kt, preferred_element_type=F32)
                  + jnp.dot(ds_cp, kp, preferred_element_type=F32))
            dk = (lax.dot_general(ds_cc, qt, tn, preferred_element_type=F32)
                  + lax.dot_general(ds_nc, qn, tn, preferred_element_type=F32))
            dv = (lax.dot_general(p_cc.astype(BF16), dot, tn, preferred_element_type=F32)
                  + lax.dot_general(p_nc.astype(BF16), don, tn, preferred_element_type=F32))
            dq_ref[t] = (dq * SCALE).astype(BF16)
            dk_ref[t] = (dk * SCALE).astype(BF16)
            dv_ref[t] = dv.astype(BF16)

    tile = pl.BlockSpec((T, BLOCK, HEAD_DIM), lambda i: (i, 0, 0))
    before = pl.BlockSpec((1, BLOCK, HEAD_DIM), lambda i: (jnp.maximum(i * T - 1, 0), 0, 0))
    after = pl.BlockSpec((1, BLOCK, HEAD_DIM), lambda i: (jnp.minimum(i * T + T, NB - 1), 0, 0))
    out = jax.ShapeDtypeStruct(q.shape, BF16)
    return pl.pallas_call(
        body, name=name, grid=(NB // T,),
        in_specs=[tile] * 6 + [before, before, after, after, after, after],
        out_specs=[tile] * 3, out_shape=[out] * 3,
        compiler_params=_params(("parallel",)),
    )(q, k, v, do, lse, tt, k, v, q, do, lse, tt)


def _group_softmax(lse):
    m = jnp.max(lse, axis=0, keepdims=True)
    e = jnp.exp(lse - m)
    return e / jnp.sum(e, axis=0, keepdims=True)


def _mix_fwd(o, lse, name):
    G, S, W = o.shape
    tr = min(256, S)
    blk = pl.BlockSpec((G, tr, W), lambda i: (0, i, 0))

    def fn(ov, lv):
        return (jnp.sum(_group_softmax(lv) * ov.astype(F32), axis=0),)

    return _ew(fn, [(o, blk), (lse, blk)], [((S, W), BF16, _rows(tr, W), False)], (S // tr,), name)


def _mix_bwd(dya, ya, o, lse, name):
    G, S, W = o.shape
    tr = min(256, S)
    blk = pl.BlockSpec((G, tr, HEAD_DIM), lambda i, h: (0, i, h))
    row = pl.BlockSpec((tr, HEAD_DIM), lambda i, h: (i, h))

    def fn(dy, yv, ov, lv):
        w = _group_softmax(lv)
        dyf = dy.astype(F32)
        inner = jnp.sum(dyf * yv.astype(F32), axis=-1, keepdims=True)
        return w * dyf[None], -w * inner[None]

    return _ew(fn, [(dya, row), (ya, row), (o, blk), (lse, blk)],
               [((G, S, W), BF16, blk, False), ((G, S, W), F32, blk, False)],
               (S // tr, HEADS_PER_DIL), name)


CUM_BLOCK = 256


def _split3(x):
    hi = x.astype(BF16)
    r = x - hi.astype(F32)
    mid = r.astype(BF16)
    lo = (r - mid.astype(F32)).astype(BF16)
    return hi, mid, lo


def _tri_matmul(tri, x):
    return sum(jnp.dot(tri, part, preferred_element_type=F32) for part in _split3(x))


def _log_sigmoid(x):
    return jnp.minimum(x, 0.0) - jnp.log(1.0 + jnp.exp(-jnp.abs(x)))


def _fox_prep(f, b, name):
    S = f.shape[0]
    tb = min(CUM_BLOCK, S)

    def body(f_ref, b_ref, c_ref, carry):
        @pl.when(pl.program_id(0) == 0)
        def _():
            carry[...] = jnp.zeros_like(carry)

        ls = _log_sigmoid(f_ref[...] + b_ref[...])
        r = lax.broadcasted_iota(jnp.int32, (tb, tb), 0)
        cidx = lax.broadcasted_iota(jnp.int32, (tb, tb), 1)
        tri = jnp.where(r >= cidx, 1.0, 0.0).astype(BF16)
        c_ref[...] = _tri_matmul(tri, ls) + carry[...]
        carry[...] += jnp.sum(ls, axis=0, keepdims=True)

    return pl.pallas_call(
        body, name=name, grid=(S // tb,),
        in_specs=[_rows(tb, LANES), _bcast(LANES)], out_specs=_rows(tb, LANES),
        out_shape=jax.ShapeDtypeStruct((S, LANES), F32),
        scratch_shapes=[pltpu.VMEM((1, LANES), F32)],
        compiler_params=_params(("arbitrary",)),
    )(f, b)


def _fox_prep_bwd(dc, f, b, name):
    S = f.shape[0]
    tb = min(CUM_BLOCK, S)
    nb = S // tb

    def body(dc_ref, f_ref, b_ref, df_ref, db_ref, carry):
        @pl.when(pl.program_id(0) == 0)
        def _():
            carry[...] = jnp.zeros_like(carry)
            db_ref[...] = jnp.zeros_like(db_ref)

        r = lax.broadcasted_iota(jnp.int32, (tb, tb), 0)
        cidx = lax.broadcasted_iota(jnp.int32, (tb, tb), 1)
        tri = jnp.where(r <= cidx, 1.0, 0.0).astype(BF16)
        dcv = dc_ref[...]
        dls = _tri_matmul(tri, dcv) + carry[...]
        carry[...] += jnp.sum(dcv, axis=0, keepdims=True)
        z = f_ref[...] + b_ref[...]
        df = dls * (1.0 / (1.0 + jnp.exp(z)))
        df_ref[...] = df
        db_ref[...] += jnp.sum(df, axis=0, keepdims=True)

    rev = pl.BlockSpec((tb, LANES), lambda i: (nb - 1 - i, 0))
    return pl.pallas_call(
        body, name=name, grid=(nb,),
        in_specs=[rev, rev, _bcast(LANES)], out_specs=[rev, _bcast(LANES)],
        out_shape=[jax.ShapeDtypeStruct((S, LANES), F32), jax.ShapeDtypeStruct((1, LANES), F32)],
        scratch_shapes=[pltpu.VMEM((1, LANES), F32)],
        compiler_params=_params(("arbitrary",)),
    )(dc, f, b)


FOX_Q_TILE = 256
_FOX_Q0 = N_DIL_HEADS
_FOX_K0 = N_HEADS + N_DIL_HEADS
_FOX_V0 = 2 * N_HEADS + N_DIL_HEADS


def _fox_scores(q, k, cq, ck, iq, tq, S):
    nt = (((1,), (1,)), ((), ()))
    s = lax.dot_general(q, k, nt, preferred_element_type=F32) * SCALE + cq - ck
    qpos = iq * tq + lax.broadcasted_iota(jnp.int32, (tq, S), 0)
    kpos = lax.broadcasted_iota(jnp.int32, (tq, S), 1)
    return jnp.where(kpos <= qpos, s, NEG)


def _fox_fwd(z, cq, ck, name):
    S = z.shape[0]
    tq = min(FOX_Q_TILE, S)

    def body(q_ref, k_ref, v_ref, cq_ref, ck_ref, o_ref, lse_ref):
        s = _fox_scores(q_ref[...], k_ref[...], cq_ref[...], ck_ref[...], pl.program_id(1), tq, S)
        m = jnp.max(s, axis=-1, keepdims=True)
        p = jnp.exp(s - m)
        l = jnp.sum(p, axis=-1, keepdims=True)
        o = jnp.dot(p.astype(BF16), v_ref[...], preferred_element_type=F32)
        o_ref[...] = (o / l).astype(o_ref.dtype)
        lse_ref[...] = m + jnp.log(l)

    qblk = lambda c0: pl.BlockSpec((tq, HEAD_DIM), lambda h, i: (i, c0 + h))
    full = lambda c0: pl.BlockSpec((S, HEAD_DIM), lambda h, i: (0, c0 + h))
    col = pl.BlockSpec((None, tq, 1), lambda h, i: (h, i, 0))
    rowv = pl.BlockSpec((None, 1, S), lambda h, i: (h, 0, 0))
    return pl.pallas_call(
        body, name=name, grid=(N_FOX_HEADS, S // tq),
        in_specs=[qblk(_FOX_Q0), full(_FOX_K0), full(_FOX_V0), col, rowv],
        out_specs=[qblk(0), col],
        out_shape=[jax.ShapeDtypeStruct((S, FOX_WIDTH), BF16),
                   jax.ShapeDtypeStruct((N_FOX_HEADS, S, 1), F32)],
        compiler_params=_params(("parallel", "parallel")),
    )(z, z, z, cq, ck)


def _fox_bwd(z, cq, ck, lse, yb, dyb, name):
    S = z.shape[0]
    tq = min(FOX_Q_TILE, S)
    nq = S // tq
    nt = (((1,), (1,)), ((), ()))
    tn = (((0,), (0,)), ((), ()))

    def body(q_ref, k_ref, v_ref, cq_ref, ck_ref, lse_ref, o_ref, do_ref,
             dq_ref, dk_ref, dv_ref, dc_ref, dk_acc, dv_acc):
        i = pl.program_id(1)

        @pl.when(i == 0)
        def _():
            dk_acc[...] = jnp.zeros_like(dk_acc)
            dv_acc[...] = jnp.zeros_like(dv_acc)
            dc_ref[...] = jnp.zeros_like(dc_ref)

        q, k, v, do = q_ref[...], k_ref[...], v_ref[...], do_ref[...]
        s = _fox_scores(q, k, cq_ref[...], ck_ref[...], i, tq, S)
        p = jnp.exp(s - lse_ref[...])
        dp = lax.dot_general(do, v, nt, preferred_element_type=F32)
        ds = p * (dp - jnp.sum(p * dp, axis=-1, keepdims=True))
        dsb = ds.astype(BF16)
        dq_ref[...] = (jnp.dot(dsb, k, preferred_element_type=F32) * SCALE).astype(BF16)
        dk_acc[...] += lax.dot_general(dsb, q, tn, preferred_element_type=F32) * SCALE
        dv_acc[...] += lax.dot_general(p.astype(BF16), do, tn, preferred_element_type=F32)
        dc_ref[...] -= jnp.sum(ds, axis=0, keepdims=True)

        @pl.when(i == nq - 1)
        def _():
            dk_ref[...] = dk_acc[...].astype(BF16)
            dv_ref[...] = dv_acc[...].astype(BF16)

    qblk = lambda c0: pl.BlockSpec((tq, HEAD_DIM), lambda h, i: (i, c0 + h))
    full = lambda c0: pl.BlockSpec((S, HEAD_DIM), lambda h, i: (0, c0 + h))
    col = pl.BlockSpec((None, tq, 1), lambda h, i: (h, i, 0))
    rowv = pl.BlockSpec((None, 1, S), lambda h, i: (h, 0, 0))
    wide = jax.ShapeDtypeStruct((S, FOX_WIDTH), BF16)
    return pl.pallas_call(
        body, name=name, grid=(N_FOX_HEADS, nq),
        in_specs=[qblk(_FOX_Q0), full(_FOX_K0), full(_FOX_V0), col, rowv, col, qblk(0), qblk(0)],
        out_specs=[qblk(0), full(0), full(0), rowv],
        out_shape=[wide, wide, wide, jax.ShapeDtypeStruct((N_FOX_HEADS, 1, S), F32)],
        scratch_shapes=[pltpu.VMEM((S, HEAD_DIM), F32), pltpu.VMEM((S, HEAD_DIM), F32)],
        compiler_params=_params(("parallel", "arbitrary")),
    )(z, z, z, cq, ck, lse, yb, dyb)


def _sigmoid(x):
    return 1.0 / (1.0 + jnp.exp(-x))


def _merge_fwd(gates, a, bm, name):
    S, D = a.shape
    tr = _row_tile(S, D * 4)
    g1 = pl.BlockSpec((tr, D), lambda i: (i, 0))
    g2 = pl.BlockSpec((tr, D), lambda i: (i, 1))

    def fn(x1, x2, av, bv):
        return (x1.astype(F32) * av.astype(F32) + x2.astype(F32) * bv.astype(F32),)

    return _ew(fn, [(gates, g1), (gates, g2), (a, _rows(tr, D)), (bm, _rows(tr, D))],
               [((S, D), BF16, _rows(tr, D), False)], (S // tr,), name)


def _merge_bwd(dmerged, gates, a, bm, name):
    S, D = a.shape
    tr = _row_tile(S, D * 8)
    g1 = pl.BlockSpec((tr, D), lambda i: (i, 0))
    g2 = pl.BlockSpec((tr, D), lambda i: (i, 1))

    def fn(dm, x1, x2, av, bv):
        dm, x1, x2 = dm.astype(F32), x1.astype(F32), x2.astype(F32)
        dg1 = dm * av.astype(F32) * x1 * (1.0 - x1)
        dg2 = dm * bv.astype(F32) * x2 * (1.0 - x2)
        dgp = jnp.concatenate([dg1, dg2], axis=1)
        return dm * x1, dm * x2, dgp, jnp.sum(dgp, axis=0, keepdims=True)

    return _ew(fn, [(dmerged, _rows(tr, D)), (gates, g1), (gates, g2), (a, _rows(tr, D)), (bm, _rows(tr, D))],
               [((S, D), BF16, _rows(tr, D), False), ((S, D), BF16, _rows(tr, D), False),
                ((S, 2 * D), BF16, _rows(tr, 2 * D), False), ((1, 2 * D), F32, _bcast(2 * D), True)],
               (S // tr,), name)


def _ple_bwd(dh, pg, pe, name):
    S, D = dh.shape
    tr = _row_tile(S, D * 4)

    def fn(d, g, e):
        g, e = g.astype(F32), e.astype(F32)
        return d * g, d * e * g * (1.0 - g)

    spec = _rows(tr, D)
    return _ew(fn, [(dh, spec), (pg, spec), (pe, spec)],
               [((S, D), BF16, spec, False), ((S, D), BF16, spec, False)], (S // tr,), name)


def _position():
    x, y, c = lax.axis_index("x"), lax.axis_index("y"), lax.axis_index("c")
    chips = [(1 - x, y), (x, 1 - y), (1 - x, 1 - y)]
    return x, y, c, chips


def _remote(src, dst, send_sem, recv_sem, target):
    return pltpu.make_async_remote_copy(src_ref=src, dst_ref=dst, send_sem=send_sem, recv_sem=recv_sem,
                                        device_id=target, device_id_type=MESH)


HBM = pl.BlockSpec(memory_space=pltpu.HBM)
SEM = pl.BlockSpec(memory_space=pltpu.SEMAPHORE)
EFFECT = pltpu.SideEffectType.DATAFLOW_SIDE_EFFECTING


def _copies_start(plan, arrays, sem_shape, after, name):
    n = len(arrays)

    def body(*refs):
        send_sems, recv_sems, token = refs[n + 1], refs[n + 2], refs[-1]
        for send, _ in plan(refs[:n], send_sems, recv_sems):
            send.start()
        token[...] = jnp.zeros_like(token)

    outs = pl.pallas_call(
        body, name=name,
        out_shape=(pltpu.SemaphoreType.DMA(sem_shape), pltpu.SemaphoreType.DMA(sem_shape),
                   *[pltpu.HBM(a.shape, a.dtype) for a in arrays], jax.ShapeDtypeStruct((8, LANES), F32)),
        in_specs=[HBM] * n + [ANY],
        out_specs=(SEM, SEM, *[HBM] * n, pl.BlockSpec(memory_space=pltpu.VMEM)),
        input_output_aliases={a: 2 + a for a in range(n)},
        compiler_params=pltpu.CompilerParams(has_side_effects=EFFECT),
    )(*[pltpu.with_memory_space_constraint(a, pltpu.HBM) for a in arrays], after)
    return outs[0], outs[1], list(outs[2:2 + n]), outs[-1]


def _copies_wait(plan, send_sems, recv_sems, arrays, after, name):
    n = len(arrays)

    def body(*refs):
        for send, recv in plan(refs[:n], refs[n], refs[n + 1]):
            send.wait_send()
            recv.wait_recv()

    return list(pl.pallas_call(
        body, name=name,
        out_shape=[pltpu.HBM(a.shape, a.dtype) for a in arrays],
        in_specs=[HBM] * n + [SEM, SEM, ANY], out_specs=[HBM] * n,
        input_output_aliases={a: a for a in range(n)},
        compiler_params=pltpu.CompilerParams(has_side_effects=EFFECT),
    )(*arrays, send_sems, recv_sems, after))


def _gather_ici_plan(refs, send_sems, recv_sems):
    x, y, c, chips = _position()
    plan = []
    for a, ref in enumerate(refs):
        rh = ref.shape[1] // 2
        mine = ref.at[2 * x + y, pl.ds(c * rh, rh)]
        for j, (cx, cy) in enumerate(chips):
            landed = ref.at[2 * cx + cy, pl.ds(c * rh, rh)]
            plan.append((_remote(mine, mine, send_sems.at[3 * a + j], recv_sems.at[3 * a + j], (cx, cy, c)),
                         _remote(landed, landed, send_sems.at[3 * a + j], recv_sems.at[3 * a + j], (cx, cy, c))))
    return plan


def _gather_d2d_plan(refs, send_sems, recv_sems):
    x, y, c, chips = _position()
    sibling = (x, y, 1 - c)
    plan = []
    for a, ref in enumerate(refs):
        rh = ref.shape[1] // 2
        for j, (cx, cy) in enumerate(chips):
            landed = ref.at[2 * cx + cy, pl.ds(c * rh, rh)]
            theirs = ref.at[2 * cx + cy, pl.ds((1 - c) * rh, rh)]
            plan.append((_remote(landed, landed, send_sems.at[3 * a + j], recv_sems.at[3 * a + j], sibling),
                         _remote(theirs, theirs, send_sems.at[3 * a + j], recv_sems.at[3 * a + j], sibling)))
    return plan


def _scatter_ici_plan(refs, send_sems, recv_sems):
    x, y, c, chips = _position()
    n = len(refs) // 2
    plan = []
    for a in range(n):
        for j, (cx, cy) in enumerate(chips):
            cp = _remote(refs[a].at[2 * cx + cy], refs[n + a].at[j], send_sems.at[3 * a + j], recv_sems.at[3 * a + j],
                         (cx, cy, c))
            plan.append((cp, cp))
    return plan


def _place_shard(w, layer, me, name):
    _, r, cc = w.shape
    tr = _row_tile(r, cc * 4)

    def body(me_ref, w_ref, o_ref):
        o_ref[...] = w_ref[...].astype(o_ref.dtype)

    return pl.pallas_call(
        body, name=name,
        grid_spec=pltpu.PrefetchScalarGridSpec(
            num_scalar_prefetch=1, grid=(r // tr,),
            in_specs=[pl.BlockSpec((None, tr, cc), lambda i, me_ref: (layer, i, 0))],
            out_specs=pl.BlockSpec((None, tr, cc), lambda i, me_ref: (me_ref[0], i, 0))),
        out_shape=jax.ShapeDtypeStruct((N_CHIPS, r, cc), BF16),
        compiler_params=_params(("parallel",)),
    )(me, w)


def _pair_exchange(grads, name):
    n = len(grads)

    def body(*refs):
        ins, outs = refs[:n], refs[n:2 * n]
        send_sems, recv_sems = refs[2 * n:]
        x, y, c, _ = _position()
        copies = []
        for a in range(n):
            rh = ins[a].shape[1] // 2
            cp = _remote(ins[a].at[:, pl.ds((1 - c) * rh, rh)], outs[a], send_sems.at[a], recv_sems.at[a],
                         (x, y, 1 - c))
            cp.start()
            copies.append(cp)
        for cp in copies:
            cp.wait()

    return pl.pallas_call(
        body, name=name, in_specs=[ANY] * n, out_specs=[ANY] * n,
        out_shape=[jax.ShapeDtypeStruct((g.shape[0], g.shape[1] // 2, g.shape[2]), g.dtype) for g in grads],
        scratch_shapes=[pltpu.SemaphoreType.DMA((n,)), pltpu.SemaphoreType.DMA((n,))],
    )(*grads)


def _pair_sum(mine, theirs, c, name):
    nch, rh, cc = theirs.shape
    tr = _row_tile(rh, cc * 4)
    nb = rh // tr

    def body(c_ref, m_ref, t_ref, o_ref):
        o_ref[...] = (m_ref[...].astype(F32) + t_ref[...].astype(F32)).astype(o_ref.dtype)

    return pl.pallas_call(
        body, name=name,
        grid_spec=pltpu.PrefetchScalarGridSpec(
            num_scalar_prefetch=1, grid=(nch, nb),
            in_specs=[pl.BlockSpec((1, tr, cc), lambda k, i, c_ref: (k, c_ref[0] * nb + i, 0)),
                      pl.BlockSpec((1, tr, cc), lambda k, i, c_ref: (k, i, 0))],
            out_specs=pl.BlockSpec((1, tr, cc), lambda k, i, c_ref: (k, i, 0))),
        out_shape=jax.ShapeDtypeStruct(theirs.shape, BF16),
        compiler_params=_params(("parallel", "parallel")),
    )(c, mine, theirs)


def _chip_sum(own, others, me, c, total, layer, depth, name):
    _, rh, cc = own.shape
    tr = _row_tile(rh, cc * 4)
    nb = rh // tr
    chained = total is not None

    def body(me_ref, c_ref, o_ref, r_ref, *rest):
        g_ref = rest[-1]
        g_ref[...] = (o_ref[0].astype(F32) + r_ref[0].astype(F32)) + (r_ref[1].astype(F32) + r_ref[2].astype(F32))

    in_specs = [pl.BlockSpec((1, tr, cc), lambda i, me_ref, c_ref: (me_ref[0], i, 0)),
                pl.BlockSpec((3, tr, cc), lambda i, me_ref, c_ref: (0, i, 0))]
    args = [me, c, own, others]
    if chained:
        in_specs.append(ANY)
        args.append(total)
    return pl.pallas_call(
        body, name=name,
        grid_spec=pltpu.PrefetchScalarGridSpec(
            num_scalar_prefetch=2, grid=(nb,), in_specs=in_specs,
            out_specs=pl.BlockSpec((None, tr, cc), lambda i, me_ref, c_ref: (layer, c_ref[0] * nb + i, 0))),
        out_shape=jax.ShapeDtypeStruct((depth, 2 * rh, cc), F32),
        input_output_aliases={4: 0} if chained else {},
        compiler_params=_params(("parallel",)),
    )(*args)


def _half_exchange(totals, layer, name):
    n = len(totals)

    def body(*refs):
        outs = refs[n:2 * n]
        send_sems, recv_sems = refs[2 * n:]
        x, y, c, _ = _position()
        copies = []
        for a in range(n):
            rh = outs[a].shape[1] // 2
            mine = outs[a].at[layer, pl.ds(c * rh, rh)]
            cp = _remote(mine, mine, send_sems.at[a], recv_sems.at[a], (x, y, 1 - c))
            cp.start()
            copies.append(cp)
        for a, cp in enumerate(copies):
            rh = outs[a].shape[1] // 2
            theirs = outs[a].at[layer, pl.ds((1 - c) * rh, rh)]
            cp.wait_send()
            _remote(theirs, theirs, send_sems.at[a], recv_sems.at[a], (x, y, 1 - c)).wait_recv()

    return pl.pallas_call(
        body, name=name, in_specs=[ANY] * n, out_specs=[ANY] * n,
        out_shape=[jax.ShapeDtypeStruct(t.shape, t.dtype) for t in totals],
        input_output_aliases={a: a for a in range(n)},
        scratch_shapes=[pltpu.SemaphoreType.DMA((n,)), pltpu.SemaphoreType.DMA((n,))],
    )(*totals)


def _allgather_devices(v, name):
    m_per, n = v.shape

    def body(x_ref, out_ref, send_sems, recv_sems, local_sem):
        x, y, c, chips = _position()
        me, sibling = (x, y, c), (x, y, 1 - c)

        def rows(px, py, pc):
            return out_ref.at[pl.ds((4 * px + 2 * py + pc) * m_per, m_per), :]

        def copy(k, block, to, src=None):
            return _remote(rows(*block) if src is None else src, rows(*block), send_sems.at[k], recv_sems.at[k], to)

        mine = pltpu.make_async_copy(x_ref, rows(*me), local_sem)
        mine.start()
        first = [copy(0, me, sibling, src=x_ref)]
        first += [copy(1 + j, me, (*chip, c), src=x_ref) for j, chip in enumerate(chips)]
        for cp in first:
            cp.start()
        passed = [copy(4 + j, (*chip, c), sibling) for j, chip in enumerate(chips)]
        for j, chip in enumerate(chips):
            copy(1 + j, (*chip, c), me).wait_recv()
            passed[j].start()
        copy(0, sibling, me).wait_recv()
        for j, chip in enumerate(chips):
            copy(4 + j, (*chip, 1 - c), me).wait_recv()
        for cp in first + passed:
            cp.wait_send()
        mine.wait()

    vm = pl.BlockSpec(memory_space=pltpu.VMEM)
    return pl.pallas_call(
        body, name=name, in_specs=[vm], out_specs=vm,
        out_shape=jax.ShapeDtypeStruct((8 * m_per, n), v.dtype),
        scratch_shapes=[pltpu.SemaphoreType.DMA((7,)), pltpu.SemaphoreType.DMA((7,)), pltpu.SemaphoreType.DMA],
    )(v)


def _adamw_math(w, g, m, v):
    m = ADAM_B1 * m + (1.0 - ADAM_B1) * g
    v = ADAM_B2 * v + (1.0 - ADAM_B2) * (g * g)
    m_hat = m / (1.0 - ADAM_B1 ** ADAM_STEP)
    v_hat = v / (1.0 - ADAM_B2 ** ADAM_STEP)
    delta = -ADAM_LR * (m_hat / (jnp.sqrt(v_hat) + ADAM_EPS) + ADAM_WD * w)
    return delta, m, v


def _adamw(w, g, m, v, name):
    depth, r, cc = w.shape
    tr = _row_tile(r, cc * 4 * 2)
    spec = pl.BlockSpec((1, tr, cc), lambda l, i: (l, i, 0))

    def fn(wv, gv, mv, vv):
        return (gv,) + _adamw_math(wv, gv, mv, vv)

    out = (w.shape, F32, spec, False)
    return _ew(fn, [(w, spec), (g, spec), (m, spec), (v, spec)], [out] * 4, (depth, r // tr), name)


def _adamw_small(w, parts, m, v, name):
    M = w.shape[0]

    def body(w_ref, p_ref, m_ref, v_ref, g_ref, d_ref, nm_ref, nv_ref):
        g = p_ref[pl.ds(0, M), :]
        for k in range(1, 8):
            g = g + p_ref[pl.ds(k * M, M), :]
        d, nm, nv = _adamw_math(w_ref[...], g, m_ref[...], v_ref[...])
        g_ref[...] = g
        d_ref[...] = d
        nm_ref[...] = nm
        nv_ref[...] = nv

    vm = pl.BlockSpec(memory_space=pltpu.VMEM)
    return pl.pallas_call(
        body, name=name, in_specs=[vm] * 4, out_specs=[vm] * 4,
        out_shape=[jax.ShapeDtypeStruct(w.shape, F32)] * 4,
    )(w, parts, m, v)


def _layer_fwd(h0, p_l, W, small, tabs, after_mlp):
    S, D = h0.shape
    ctab, stab = tabs
    u = _rms_fwd(h0, small["g_mix"], "rms_mix")
    z = _mm(u, W["w_qkv"], mode="nn", name="mm_qkv")
    f = _mm(u, W["w_f"], mode="nn", name="mm_f", out_dtypes=(F32,))
    gates = _mm(u, W["w_gate"], mode="nn", name="mm_gate", b_chunked=True, extras=[(small["b_gate"], "row")],
                epilogue=lambda acc, b: (_sigmoid(acc + b),))
    qr, kr = _rope(z, 0, z, N_HEADS, ctab, stab, 1.0, "rope_fwd")
    qs, ks, vs = _to_strided(qr), _to_strided(kr), _to_strided(z[:, 2 * ATTN_WIDTH:2 * ATTN_WIDTH + DIL_WIDTH])
    o_s, lse_s = _sw_fwd(qs, ks, vs, "sw_fwd")
    o_g, lse_g = _from_strided(o_s, S), _from_strided(lse_s, S)
    ya = _mix_fwd(o_g, lse_g, "mix_fwd")
    a = _mm(ya, W["w_br_a"], mode="nn", name="mm_br_a", b_chunked=True)
    cum = _fox_prep(f, small["b_f"], "fox_prep")
    cq = cum[:, :N_FOX_HEADS].T[:, :, None]
    ck = cum[:, :N_FOX_HEADS].T[:, None, :]
    yb, lse_f = _fox_fwd(z, cq, ck, "fox_fwd")
    bm = _mm(yb, W["w_br_b"], mode="nn", name="mm_br_b", b_chunked=True)
    merged = _merge_fwd(gates, a, bm, "merge_fwd")
    h1 = _mm(merged, W["w_o"], mode="nn", name="mm_o", out_dtypes=(F32,), extras=[(h0, "tile")],
             epilogue=lambda acc, r: (acc + r,), tj=256)
    m = _rms_fwd(h1, small["g_mlp"], "rms_mlp")
    ra, act = _mm(m, W["w_up"], mode="nn", name="mm_up", b_chunked=True, out_dtypes=(BF16, BF16),
                  epilogue=lambda acc: (jnp.maximum(acc, 0.0), jnp.square(jnp.maximum(acc, 0.0))))
    h2 = _mm(act, W["w_down"], mode="nn", name="mm_down", out_dtypes=(F32,), extras=[(h1, "tile")],
             epilogue=lambda acc, r: (acc + r,), ti=1024)
    after_mlp(h2)
    n = _rms_fwd(h2, small["g_ple"], "rms_ple")
    pg = _mm(n, W["w_ple_gate"], mode="nn", name="mm_ple_gate", epilogue=lambda acc: (_sigmoid(acc),))
    h3, pe = _mm(p_l, W["w_ple"], mode="nn", name="mm_ple", b_chunked=True, out_dtypes=(F32, BF16),
                 extras=[(h2, "tile"), (pg, "tile")], tj=256,
                 epilogue=lambda acc, r, g: (r + g.astype(F32) * acc, acc))
    saved = dict(h0=h0, u=u, z=z, f=f, gates=gates, qs=qs, ks=ks, vs=vs, lse_s=lse_s, o_g=o_g, lse_g=lse_g,
                 ya=ya, a=a, cq=cq, ck=ck, yb=yb, lse_f=lse_f, bm=bm, merged=merged, h1=h1, m=m, ra=ra,
                 act=act, h2=h2, n=n, pg=pg, pe=pe, p_l=p_l)
    return h3, saved


def _layer_bwd(dh3, sv, W, small, tabs):
    S, D = dh3.shape
    ctab, stab = tabs
    gw, gs = {}, {}
    tn = functools.partial(_mm, mode="tn", ti=512, tj=1024)
    dpe, dpg = _ple_bwd(dh3, sv["pg"], sv["pe"], "ple_bwd")
    gw["w_ple"] = tn(sv["p_l"], dpe, name="dw_ple", out_chunks=N_CHIPS)
    gw["w_ple_gate"] = tn(sv["n"], dpg, name="dw_ple_gate").reshape(N_CHIPS, D // N_CHIPS, D)
    dn = _mm(dpg, W["w_ple_gate"], mode="nt", name="mm_dn")
    dh2, dh2b, gs["g_ple"] = _rms_bwd(sv["h2"], small["g_ple"], dn, dh3, "rms_ple_bwd")
    da = _mm(dh2b, W["w_down"], mode="nt", name="mm_dact", extras=[(sv["ra"], "tile")],
             epilogue=lambda acc, r: (acc * (2.0 * r.astype(F32)),))
    FF = da.shape[1]
    gw["w_down"] = tn(sv["act"], dh2b, name="dw_down").reshape(N_CHIPS, FF // N_CHIPS, D)
    gw["w_up"] = tn(sv["m"], da, name="dw_up", out_chunks=N_CHIPS)
    dm = _mm(da, W["w_up"], mode="nt", name="mm_dm", b_chunked=True)
    dh1, dh1b, gs["g_mlp"] = _rms_bwd(sv["h1"], small["g_mlp"], dm, dh2, "rms_mlp_bwd")
    dmerged = _mm(dh1b, W["w_o"], mode="nt", name="mm_dmerged")
    gw["w_o"] = tn(sv["merged"], dh1b, name="dw_o").reshape(N_CHIPS, D // N_CHIPS, D)
    d_a, d_b, dgp, gs["b_gate"] = _merge_bwd(dmerged, sv["gates"], sv["a"], sv["bm"], "merge_bwd")
    gw["w_gate"] = tn(sv["u"], dgp, name="dw_gate", out_chunks=N_CHIPS)
    gw["w_br_a"] = tn(sv["ya"], d_a, name="dw_br_a", out_chunks=N_CHIPS, tj=512)
    gw["w_br_b"] = tn(sv["yb"], d_b, name="dw_br_b", out_chunks=N_CHIPS, tj=512)
    dya = _mm(d_a, W["w_br_a"], mode="nt", name="mm_dya", b_chunked=True)
    dyb = _mm(d_b, W["w_br_b"], mode="nt", name="mm_dyb", b_chunked=True)
    z = sv["z"]
    dq_f, dk_f, dv_f, dck = _fox_bwd(z, sv["cq"], sv["ck"], sv["lse_f"], sv["yb"], dyb, "fox_bwd")
    dc = jnp.pad(dck[:, 0, :].T, ((0, 0), (0, LANES - N_FOX_HEADS)))
    df, dbf = _fox_prep_bwd(dc, sv["f"], small["b_f"], "fox_prep_bwd")
    gs["b_f"] = dbf[:, :N_FOX_HEADS]
    lane = jnp.arange(LANES)[None, :] < N_FOX_HEADS
    dzf = jnp.where(lane, df, 0.0).astype(BF16)
    do_g, tt_g = _mix_bwd(dya, sv["ya"], sv["o_g"], sv["lse_g"], "mix_bwd")
    do_s = _to_strided(do_g.transpose(1, 0, 2).reshape(S, DIL_WIDTH))
    tt_s = _to_strided(tt_g.transpose(1, 0, 2).reshape(S, DIL_WIDTH))
    dq_s, dk_s, dv_s = _sw_bwd(sv["qs"], sv["ks"], sv["vs"], do_s, sv["lse_s"], tt_s, "sw_bwd")
    unstride = lambda t: _from_strided(t, S).transpose(1, 0, 2).reshape(S, DIL_WIDTH)
    dq_a, dk_a = _rope(unstride(dq_s), 0, unstride(dk_s), 0, ctab, stab, -1.0, "rope_bwd")
    dz = jnp.concatenate([dq_a, dq_f, dk_a, dk_f, unstride(dv_s), dv_f], axis=1)
    g_qkv = tn(sv["u"], dz, name="dw_qkv")
    g_f = tn(sv["u"], dzf, name="dw_f", tj=128)
    cols = W["w_in_cols"]
    g_in = jnp.concatenate([g_qkv, g_f[:, :N_FOX_HEADS]], axis=1)
    gw["w_in"] = g_in.reshape(D, N_CHIPS, cols).transpose(1, 0, 2)
    du = _mm(dzf, W["w_f"], mode="nt", name="mm_du_f", out_dtypes=(F32,))
    du = _mm(dgp, W["w_gate"], mode="nt", name="mm_du_gate", b_chunked=True, out_dtypes=(F32,),
             extras=[(du, "tile")], epilogue=lambda acc, r: (acc + r,), tj=256)
    du = _mm(dz, W["w_qkv"], mode="nt", name="mm_du_qkv", extras=[(du, "tile")],
             epilogue=lambda acc, r: (acc + r,), tj=256)
    dh0, _, gs["g_mix"] = _rms_bwd(sv["h0"], small["g_mix"], du, dh1, "rms_mix_bwd")
    return dh0, gw, gs


BIG = ("w_in", "w_gate", "w_br_a", "w_br_b", "w_o", "w_up", "w_down", "w_ple", "w_ple_gate")
SMALL = ("g_mix", "b_f", "b_gate", "g_mlp", "g_ple", "g_final")
ORDER = ("g_mix", "w_in", "b_f", "w_gate", "b_gate", "w_br_a", "w_br_b", "w_o", "g_mlp", "w_up", "w_down",
         "g_ple", "w_ple", "w_ple_gate", "g_final")


def _gathered_layer_weights(full, D):
    W = {}
    for name in ("w_gate", "w_br_a", "w_br_b", "w_up", "w_ple"):
        W[name] = full[name]
    for name in ("w_o", "w_down", "w_ple_gate"):
        t = full[name]
        W[name] = t.reshape(t.shape[0] * t.shape[1], t.shape[2])
    w_in = full["w_in"]
    cols = w_in.shape[2]
    w_in = w_in.transpose(1, 0, 2).reshape(D, N_CHIPS * cols)
    W["w_qkv"] = w_in[:, :3 * ATTN_WIDTH]
    W["w_f"] = jnp.pad(w_in[:, 3 * ATTN_WIDTH:], ((0, 0), (0, LANES - N_FOX_HEADS)))
    W["w_in_cols"] = cols
    return W


def _pack_rows(vals):
    flat = jnp.concatenate([v.reshape(-1) for v in vals])
    rows = -(-flat.shape[0] // (8 * LANES)) * 8
    return jnp.pad(flat, (0, rows * LANES - flat.shape[0])).reshape(rows, LANES)


def _unpack_rows(packed, shapes):
    flat = packed.reshape(-1)
    out, pos = [], 0
    for s in shapes:
        size = 1
        for dim in s:
            size *= dim
        out.append(flat[pos:pos + size].reshape(s))
        pos += size
    return out


def _local_step(x, p, small_w, comm, loss_target):
    depth = p.shape[0]
    S, D = x.shape
    tabs = _rope_tables(S)
    h = x
    saved, weights, smalls = [], [], []
    state, _ = comm["gather_start"](0, x)
    full = comm["gather_finish"](comm["gather_mid"](state, x), x)
    for l in range(depth):
        nxt = [None]
        g_mix = small_w["g_mix"][l][None]
        if l + 1 < depth:
            nxt[0], token = comm["gather_start"](l + 1, full["w_ple"])
            g_mix = g_mix + token[0, 0]

        def after_mlp(h2):
            if nxt[0] is not None:
                nxt[0] = comm["gather_mid"](nxt[0], h2)

        W = _gathered_layer_weights(full, D)
        sm = dict(g_mix=g_mix, g_mlp=small_w["g_mlp"][l][None],
                  g_ple=small_w["g_ple"][l][None], b_gate=small_w["b_gate"][l][None],
                  b_f=jnp.pad(small_w["b_f"][l][None], ((0, 0), (0, LANES - N_FOX_HEADS))))
        h, sv = _layer_fwd(h, p[l].astype(BF16), W, sm, tabs, after_mlp)
        if nxt[0] is not None:
            full = comm["gather_finish"](nxt[0], h)
        saved.append(sv)
        weights.append(W)
        smalls.append(sm)
    dh, loss_row, dg_final = _loss_head(h, small_w["g_final"][None], loss_target, "loss_head")
    gss = [None] * depth
    pending, token = None, None
    for l in reversed(range(depth)):
        sm = smalls[l]
        if token is not None:
            sm = {**sm, "g_ple": sm["g_ple"] + token[0, 0]}
        dh, gw, gss[l] = _layer_bwd(dh, saved[l], weights[l], sm, tabs)
        if pending is not None:
            comm["reduce_end"](pending, dh)
        pending, token = comm["reduce_begin"](l, gw)
    comm["reduce_end"](pending, dh)
    return loss_row, dh, gss, dg_final


def _device_comm(w, depth, c_arr, me_arr):
    n = len(BIG)
    totals = [[None] * n]

    def gather_start(l, after):
        placed = [_place_shard(w[name], l, me_arr, "place_shard") for name in BIG]
        send, recv, bufs, token = _copies_start(_gather_ici_plan, placed, (3 * n,), after, f"gather_ici_start_{l}")
        return (l, send, recv, bufs), token

    def gather_mid(state, after):
        l, send, recv, bufs = state
        bufs = _copies_wait(_gather_ici_plan, send, recv, bufs, after, f"gather_ici_wait_{l}")
        send, recv, bufs, _ = _copies_start(_gather_d2d_plan, bufs, (3 * n,), after, f"gather_d2d_start_{l}")
        return l, send, recv, bufs

    def gather_finish(state, after):
        l, send, recv, bufs = state
        return dict(zip(BIG, _copies_wait(_gather_d2d_plan, send, recv, bufs, after, f"gather_d2d_wait_{l}")))

    def reduce_begin(l, gw):
        grads = [gw[name] for name in BIG]
        theirs = _pair_exchange(grads, "rs_pair_exchange")
        sums = [_pair_sum(g, t, c_arr, "rs_pair_sum") for g, t in zip(grads, theirs)]
        landing = [lax.empty((3,) + s.shape[1:], s.dtype) for s in sums]
        send, recv, arrays, token = _copies_start(_scatter_ici_plan, sums + landing, (3 * n,), sums[0],
                                                  f"scatter_ici_start_{l}")
        return (l, send, recv, arrays), token

    def reduce_end(state, after):
        l, send, recv, arrays = state
        arrays = _copies_wait(_scatter_ici_plan, send, recv, arrays, after, f"scatter_ici_wait_{l}")
        done = [_chip_sum(s, o, me_arr, c_arr, t, l, depth, "rs_chip_sum")
                for s, o, t in zip(arrays[:n], arrays[n:], totals[0])]
        totals[0] = _half_exchange(done, l, "rs_half_exchange")

    comm = dict(gather_start=gather_start, gather_mid=gather_mid, gather_finish=gather_finish,
                reduce_begin=reduce_begin, reduce_end=reduce_end)
    return comm, totals


def kernel(x, p, g_mix, w_in, b_f, w_gate, b_gate, w_br_a, w_br_b, w_o, g_mlp, w_up, w_down, g_ple, w_ple, w_ple_gate, g_final, loss_target, m_g_mix, m_w_in, m_b_f, m_w_gate, m_b_gate, m_w_br_a, m_w_br_b, m_w_o, m_g_mlp, m_w_up, m_w_down, m_g_ple, m_w_ple, m_w_ple_gate, m_g_final, v_g_mix, v_w_in, v_b_f, v_w_gate, v_b_gate, v_w_br_a, v_w_br_b, v_w_o, v_g_mlp, v_w_up, v_w_down, v_g_ple, v_w_ple, v_w_ple_gate, v_g_final):
    w = dict(g_mix=g_mix, w_in=w_in, b_f=b_f, w_gate=w_gate, b_gate=b_gate, w_br_a=w_br_a, w_br_b=w_br_b,
             w_o=w_o, g_mlp=g_mlp, w_up=w_up, w_down=w_down, g_ple=g_ple, w_ple=w_ple, w_ple_gate=w_ple_gate,
             g_final=g_final)
    m = dict(g_mix=m_g_mix, w_in=m_w_in, b_f=m_b_f, w_gate=m_w_gate, b_gate=m_b_gate, w_br_a=m_w_br_a,
             w_br_b=m_w_br_b, w_o=m_w_o, g_mlp=m_g_mlp, w_up=m_w_up, w_down=m_w_down, g_ple=m_g_ple,
             w_ple=m_w_ple, w_ple_gate=m_w_ple_gate, g_final=m_g_final)
    v = dict(g_mix=v_g_mix, w_in=v_w_in, b_f=v_b_f, w_gate=v_w_gate, b_gate=v_b_gate, w_br_a=v_w_br_a,
             w_br_b=v_w_br_b, w_o=v_w_o, g_mlp=v_g_mlp, w_up=v_w_up, w_down=v_w_down, g_ple=v_g_ple,
             w_ple=v_w_ple, w_ple_gate=v_w_ple_gate, g_final=v_g_final)
    depth = p.shape[0]
    cx, cy, cc = lax.axis_index("x"), lax.axis_index("y"), lax.axis_index("c")
    c_arr = jnp.reshape(cc, (1,)).astype(jnp.int32)
    me_arr = jnp.reshape(2 * cx + cy, (1,)).astype(jnp.int32)

    comm, totals = _device_comm(w, depth, c_arr, me_arr)
    loss_row, grad_x, gss, dg_final = _local_step(x[0], p[:, 0], w, comm, loss_target[0])
    big_out = {n: _adamw(w[n], t, m[n], v[n], "adamw") for n, t in zip(BIG, totals[0])}

    small_grads = [jnp.stack([gss[l][n][0] for l in range(depth)]) for n in SMALL[:-1]] + [dg_final[0]]
    shapes = [w[n].shape for n in SMALL]
    parts = _allgather_devices(_pack_rows(small_grads), "allgather_small")
    packed = _adamw_small(_pack_rows([w[n] for n in SMALL]), parts, _pack_rows([m[n] for n in SMALL]),
                          _pack_rows([v[n] for n in SMALL]), "adamw_small")
    small_out = {n: vals for n, vals in zip(SMALL, zip(*[_unpack_rows(t, shapes) for t in packed]))}

    loss = lax.psum(loss_row[0, 0], ("x", "y", "c"))
    out = {**big_out, **small_out}
    return (loss, grad_x[None], *[out[n][0] for n in ORDER], *[out[n][1] for n in ORDER],
            *[out[n][2] for n in ORDER], *[out[n][3] for n in ORDER])
```

```python
import functools

import jax
import jax.numpy as jnp
from jax import lax
from jax.experimental import pallas as pl
from jax.experimental.pallas import tpu as pltpu

F32 = jnp.float32
BF16 = jnp.bfloat16

HEAD_DIM = 128
N_HEADS = 16
N_DIL_HEADS = 12
N_FOX_HEADS = 4
HEADS_PER_DIL = 4
DILATIONS = (1, 4, 16)
BLOCK = 128
ATTN_WIDTH = N_HEADS * HEAD_DIM
DIL_WIDTH = N_DIL_HEADS * HEAD_DIM
FOX_WIDTH = N_FOX_HEADS * HEAD_DIM
ROPE_THETA = 500000.0
ROPE_HALF = 16
NORM_EPS = 1e-6
SCALE = HEAD_DIM ** -0.5
NEG = -1e30

ADAM_LR = 0.001
ADAM_B1 = 0.9
ADAM_B2 = 0.999
ADAM_EPS = 1e-08
ADAM_WD = 0.01
ADAM_STEP = 10

N_CHIPS = 4
V7X_VMEM_LIMIT_BYTES = 56 * 1024 * 1024
LANES = 128
MESH = pl.DeviceIdType.MESH
ANY = pl.BlockSpec(memory_space=pl.ANY)


def _params(sem):
    return pltpu.CompilerParams(dimension_semantics=sem, vmem_limit_bytes=V7X_VMEM_LIMIT_BYTES)


def _tile(n, pref):
    if n <= pref:
        return n
    t = (pref // LANES) * LANES
    while t > LANES and n % t:
        t -= LANES
    assert n % t == 0, (n, pref)
    return t


def _mm(a, b, *, mode, name, out_dtypes=(BF16,), epilogue=None, extras=(), b_chunked=False,
        out_chunks=0, ti=2048, tj=512, tc=2048):
    if mode == "tn":
        C, I = a.shape
    else:
        I, C = a.shape
    if b_chunked:
        nch, d0, n = b.shape
        if mode == "nn":
            assert d0 == C
            J = nch * n
        else:
            assert mode == "nt" and nch * n == C
            J = d0
    elif mode == "nt":
        J = b.shape[0]
        assert b.shape[1] == C
    else:
        assert b.shape[0] == C
        J = b.shape[1]
    ti, tc = _tile(I, ti), _tile(C, tc)
    if b_chunked and mode == "nn":
        tj = _tile(n, tj)
    elif out_chunks:
        tj = _tile(J // out_chunks, tj)
    else:
        tj = _tile(J, tj)
    if b_chunked and mode == "nt":
        tc = _tile(n, tc)
    ni, nj, nc = I // ti, J // tj, C // tc

    if mode == "tn":
        a_spec = pl.BlockSpec((tc, ti), lambda i, j, c: (c, i))
        dims = (((0,), (0,)), ((), ()))
    else:
        a_spec = pl.BlockSpec((ti, tc), lambda i, j, c: (i, c))
        dims = (((1,), (0,)), ((), ())) if mode == "nn" else (((1,), (1,)), ((), ()))
    if mode == "nt":
        if b_chunked:
            cb = n // tc
            b_spec = pl.BlockSpec((None, tj, tc), lambda i, j, c: (c // cb, j, c % cb))
        else:
            b_spec = pl.BlockSpec((tj, tc), lambda i, j, c: (j, c))
    else:
        if b_chunked:
            jb = n // tj
            b_spec = pl.BlockSpec((None, tc, tj), lambda i, j, c: (j // jb, c, j % jb))
        else:
            b_spec = pl.BlockSpec((tc, tj), lambda i, j, c: (c, j))
    extra_specs = []
    for arr, kind in extras:
        if kind == "tile":
            assert arr.shape == (I, J), (arr.shape, I, J)
            extra_specs.append(pl.BlockSpec((ti, tj), lambda i, j, c: (i, j)))
        else:
            assert arr.shape == (1, J)
            extra_specs.append(pl.BlockSpec((1, tj), lambda i, j, c: (0, j)))
    if out_chunks:
        ob = (J // out_chunks) // tj
        out_spec = pl.BlockSpec((None, ti, tj), lambda i, j, c: (j // ob, i, j % ob))
        out_shape = [jax.ShapeDtypeStruct((out_chunks, I, J // out_chunks), d) for d in out_dtypes]
    else:
        out_spec = pl.BlockSpec((ti, tj), lambda i, j, c: (i, j))
        out_shape = [jax.ShapeDtypeStruct((I, J), d) for d in out_dtypes]
    ne, no = len(extras), len(out_dtypes)
    if epilogue is None:
        epilogue = lambda acc: (acc,)

    def body(a_ref, b_ref, *rest):
        extra_refs, out_refs = rest[:ne], rest[ne:ne + no]

        def finish(acc):
            outs = epilogue(acc, *[r[...] for r in extra_refs])
            for o_ref, val in zip(out_refs, outs):
                o_ref[...] = val.astype(o_ref.dtype)

        part = lax.dot_general(a_ref[...], b_ref[...], dims, preferred_element_type=F32)
        if nc == 1:
            finish(part)
        else:
            acc_ref = rest[-1]
            k = pl.program_id(2)

            @pl.when(k == 0)
            def _():
                acc_ref[...] = part

            @pl.when(k > 0)
            def _():
                acc_ref[...] += part

            @pl.when(k == nc - 1)
            def _():
                finish(acc_ref[...])

    outs = pl.pallas_call(
        body, name=name, grid=(ni, nj, nc),
        in_specs=[a_spec, b_spec] + extra_specs,
        out_specs=[out_spec] * no, out_shape=out_shape,
        scratch_shapes=[pltpu.VMEM((ti, tj), F32)] if nc > 1 else [],
        compiler_params=_params(("parallel", "parallel", "arbitrary")),
    )(a, b, *[e[0] for e in extras])
    return outs[0] if no == 1 else tuple(outs)


def _ew(fn, ins, outs, grid, name):
    n_in = len(ins)
    has_acc = any(o[3] for o in outs)
    assert not has_acc or len(grid) == 1

    def body(*refs):
        vals = fn(*[r[...] for r in refs[:n_in]])
        for o_ref, o, val in zip(refs[n_in:], outs, vals):
            if o[3]:
                step = pl.program_id(0)

                @pl.when(step == 0)
                def _(o_ref=o_ref, val=val):
                    o_ref[...] = val

                @pl.when(step > 0)
                def _(o_ref=o_ref, val=val):
                    o_ref[...] += val
            else:
                o_ref[...] = val.astype(o_ref.dtype)

    sem = ("arbitrary",) if has_acc else ("parallel",) * len(grid)
    res = pl.pallas_call(
        body, name=name, grid=grid,
        in_specs=[i[1] for i in ins], out_specs=[o[2] for o in outs],
        out_shape=[jax.ShapeDtypeStruct(o[0], o[1]) for o in outs],
        compiler_params=_params(sem),
    )(*[i[0] for i in ins])
    return res[0] if len(outs) == 1 else tuple(res)


def _rows(tr, w):
    return pl.BlockSpec((tr, w), lambda i: (i, 0))


def _bcast(w):
    return pl.BlockSpec((1, w), lambda i: (0, 0))


def _row_tile(S, width_bytes):
    tr = 512
    while tr > 16 and tr * width_bytes > 2 * 1024 * 1024:
        tr //= 2
    return min(tr, S)


def _rms_fwd(h, g, name):
    S, D = h.shape
    tr = _row_tile(S, D * 4)

    def fn(x, gg):
        r = lax.rsqrt(jnp.mean(x * x, axis=-1, keepdims=True) + NORM_EPS)
        return (x * r * gg,)

    return _ew(fn, [(h, _rows(tr, D)), (g, _bcast(D))], [((S, D), BF16, _rows(tr, D), False)],
               (S // tr,), name)


def _rms_bwd(x, g, dy, dres, name):
    S, D = x.shape
    tr = _row_tile(S, D * 4)

    def fn(xv, gg, dyv, dr):
        r = lax.rsqrt(jnp.mean(xv * xv, axis=-1, keepdims=True) + NORM_EPS)
        dyf = dyv.astype(F32)
        gy = dyf * gg
        dx = r * gy - xv * (r * r * r) * jnp.mean(xv * gy, axis=-1, keepdims=True)
        tot = dr + dx
        dg = jnp.sum(dyf * xv * r, axis=0, keepdims=True)
        return tot, tot, dg

    return _ew(fn, [(x, _rows(tr, D)), (g, _bcast(D)), (dy, _rows(tr, D)), (dres, _rows(tr, D))],
               [((S, D), F32, _rows(tr, D), False), ((S, D), BF16, _rows(tr, D), False),
                ((1, D), F32, _bcast(D), True)], (S // tr,), name)


def _loss_head(h, g, target, name):
    S, D = h.shape
    tr = _row_tile(S, D * 4)

    def fn(xv, gg, tgt):
        r = lax.rsqrt(jnp.mean(xv * xv, axis=-1, keepdims=True) + NORM_EPS)
        y = xv * r * gg
        e = y - tgt
        loss = 0.5 * jnp.sum(jnp.mean(e * e, axis=-1, keepdims=True), axis=0, keepdims=True)
        dy = e * (1.0 / D)
        gy = dy * gg
        dx = r * gy - xv * (r * r * r) * jnp.mean(xv * gy, axis=-1, keepdims=True)
        dg = jnp.sum(dy * xv * r, axis=0, keepdims=True)
        return dx, jnp.broadcast_to(loss, (1, LANES)), dg

    return _ew(fn, [(h, _rows(tr, D)), (g, _bcast(D)), (target, _rows(tr, D))],
               [((S, D), F32, _rows(tr, D), False), ((1, LANES), F32, _bcast(LANES), True),
                ((1, D), F32, _bcast(D), True)], (S // tr,), name)


def _rope_tables(S):
    inv = ROPE_THETA ** (-jnp.arange(ROPE_HALF, dtype=F32) / ROPE_HALF)
    ang = jnp.arange(S, dtype=F32)[:, None] * inv[None, :]
    cos, sin = jnp.cos(ang), jnp.sin(ang)
    rest = HEAD_DIM - 2 * ROPE_HALF
    ctab = jnp.concatenate([cos, cos, jnp.ones((S, rest), F32)], axis=1)
    stab = jnp.concatenate([-sin, sin, jnp.zeros((S, rest), F32)], axis=1)
    return ctab, stab


def _swap_halves(x):
    lane = lax.broadcasted_iota(jnp.int32, x.shape, 1)
    return jnp.where(lane < ROPE_HALF, pltpu.roll(x, HEAD_DIM - ROPE_HALF, 1), pltpu.roll(x, ROPE_HALF, 1))


def _rope(q_src, q_col0, k_src, k_col0, ctab, stab, sign, name):
    S = q_src.shape[0]
    tr = min(512, S)

    def fn(q, k, ct, st):
        outs = []
        for v in (q, k):
            vf = v.astype(F32)
            outs.append(vf * ct + sign * _swap_halves(vf) * st)
        return tuple(outs)

    head = lambda c0: pl.BlockSpec((tr, HEAD_DIM), lambda i, h: (i, c0 + h))
    tab = pl.BlockSpec((tr, HEAD_DIM), lambda i, h: (i, 0))
    out = ((S, DIL_WIDTH), BF16, head(0), False)
    return _ew(fn, [(q_src, head(q_col0)), (k_src, head(k_col0)), (ctab, tab), (stab, tab)],
               [out, out], (S // tr, N_DIL_HEADS), name)


SW_BLOCKS_PER_STEP = 8


def _to_strided(x):
    S = x.shape[0]
    parts = []
    for g, d in enumerate(DILATIONS):
        xg = x[:, g * 512:(g + 1) * 512].reshape(S // d, d, HEADS_PER_DIL, HEAD_DIM)
        parts.append(xg.transpose(1, 2, 0, 3).reshape(-1, BLOCK, HEAD_DIM))
    return jnp.concatenate(parts, axis=0)


def _from_strided(y, S):
    per = y.shape[0] // len(DILATIONS)
    parts = []
    for g, d in enumerate(DILATIONS):
        yg = y[g * per:(g + 1) * per].reshape(d, HEADS_PER_DIL, S // d, HEAD_DIM)
        parts.append(yg.transpose(2, 0, 1, 3).reshape(S, HEADS_PER_DIL * HEAD_DIM))
    return jnp.stack(parts, axis=0)


def _seq_blocks(b0, per_group):
    g = b0 // per_group
    n0 = per_group // HEADS_PER_DIL
    return jnp.where(g == 0, n0, jnp.where(g == 1, n0 // 4, n0 // 16))


def _sw_masks():
    qi = lax.broadcasted_iota(jnp.int32, (BLOCK, BLOCK), 0)
    ki = lax.broadcasted_iota(jnp.int32, (BLOCK, BLOCK), 1)
    return qi >= ki, qi <= ki


def _sw_fwd(q, k, v, name):
    NB = q.shape[0]
    T = SW_BLOCKS_PER_STEP
    per_group = NB // len(DILATIONS)
    nt = (((1,), (1,)), ((), ()))

    def body(q_ref, k_ref, v_ref, kp_ref, vp_ref, o_ref, lse_ref):
        b0 = pl.program_id(0) * T
        nseq = _seq_blocks(b0, per_group)
        cur_mask, prev_mask = _sw_masks()
        for t in range(T):
            has_prev = ((b0 + t) & (nseq - 1)) != 0
            qt = q_ref[t]
            kp = kp_ref[0] if t == 0 else k_ref[t - 1]
            vp = vp_ref[0] if t == 0 else v_ref[t - 1]
            s_c = lax.dot_general(qt, k_ref[t], nt, preferred_element_type=F32) * SCALE
            s_p = lax.dot_general(qt, kp, nt, preferred_element_type=F32) * SCALE
            s_c = jnp.where(cur_mask, s_c, NEG)
            s_p = jnp.where(prev_mask, s_p, NEG) + jnp.where(has_prev, 0.0, NEG)
            m = jnp.maximum(jnp.max(s_c, axis=-1, keepdims=True), jnp.max(s_p, axis=-1, keepdims=True))
            p_c = jnp.exp(s_c - m)
            p_p = jnp.exp(s_p - m)
            l = jnp.sum(p_c, axis=-1, keepdims=True) + jnp.sum(p_p, axis=-1, keepdims=True)
            o = (jnp.dot(p_c.astype(BF16), v_ref[t], preferred_element_type=F32)
                 + jnp.dot(p_p.astype(BF16), vp, preferred_element_type=F32))
            o_ref[t] = (o / l).astype(o_ref.dtype)
            lse_ref[t] = jnp.broadcast_to(m + jnp.log(l), (BLOCK, HEAD_DIM))

    tile = pl.BlockSpec((T, BLOCK, HEAD_DIM), lambda i: (i, 0, 0))
    before = pl.BlockSpec((1, BLOCK, HEAD_DIM), lambda i: (jnp.maximum(i * T - 1, 0), 0, 0))
    return pl.pallas_call(
        body, name=name, grid=(NB // T,),
        in_specs=[tile, tile, tile, before, before], out_specs=[tile, tile],
        out_shape=[jax.ShapeDtypeStruct(q.shape, BF16), jax.ShapeDtypeStruct(q.shape, F32)],
        compiler_params=_params(("parallel",)),
    )(q, k, v, k, v)


def _sw_bwd(q, k, v, do, lse, tt, name):
    NB = q.shape[0]
    T = SW_BLOCKS_PER_STEP
    per_group = NB // len(DILATIONS)
    nt = (((1,), (1,)), ((), ()))
    tn = (((0,), (0,)), ((), ()))

    def body(q_ref, k_ref, v_ref, do_ref, lse_ref, tt_ref, kp_ref, vp_ref, qn_ref, don_ref, lsen_ref,
             ttn_ref, dq_ref, dk_ref, dv_ref):
        b0 = pl.program_id(0) * T
        nseq = _seq_blocks(b0, per_group)
        cur_mask, prev_mask = _sw_masks()

        def probs(qq, kk, lse_b, mask, gate):
            s = lax.dot_general(qq, kk, nt, preferred_element_type=F32) * SCALE
            return jnp.exp(jnp.where(mask, s, NEG) + gate - lse_b)

        for t in range(T):
            has_prev = jnp.where(((b0 + t) & (nseq - 1)) != 0, 0.0, NEG)
            has_next = jnp.where(((b0 + t + 1) & (nseq - 1)) != 0, 0.0, NEG)
            last = t == T - 1
            qt, kt, vt, dot = q_ref[t], k_ref[t], v_ref[t], do_ref[t]
            kp = kp_ref[0] if t == 0 else k_ref[t - 1]
            vp = vp_ref[0] if t == 0 else v_ref[t - 1]
            qn = qn_ref[0] if last else q_ref[t + 1]
            don = don_ref[0] if last else do_ref[t + 1]
            lsen = lsen_ref[0] if last else lse_ref[t + 1]
            ttn = ttn_ref[0] if last else tt_ref[t + 1]
            p_cc = probs(qt, kt, lse_ref[t], cur_mask, 0.0)
            p_cp = probs(qt, kp, lse_ref[t], prev_mask, has_prev)
            p_nc = probs(qn, kt, lsen, prev_mask, has_next)
            ds_cc = p_cc * (lax.dot_general(dot, vt, nt, preferred_element_type=F32) + tt_ref[t])
            ds_cp = p_cp * (lax.dot_general(dot, vp, nt, preferred_element_type=F32) + tt_ref[t])
            ds_nc = p_nc * (lax.dot_general(don, vt, nt, preferred_element_type=F32) + ttn)
            ds_cc, ds_cp, ds_nc = ds_cc.astype(BF16), ds_cp.astype(BF16), ds_nc.astype(BF16)
            dq = (jnp.dot(ds_cc, kt, preferred_element_type=F32)
                  + jnp.dot(ds_cp, kp, preferred_element_type=F32))
            dk = (lax.dot_general(ds_cc, qt, tn, preferred_element_type=F32)
                  + lax.dot_general(ds_nc, qn, tn, preferred_element_type=F32))
            dv = (lax.dot_general(p_cc.astype(BF16), dot, tn, preferred_element_type=F32)
                  + lax.dot_general(p_nc.astype(BF16), don, tn, preferred_element_type=F32))
            dq_ref[t] = (dq * SCALE).astype(BF16)
            dk_ref[t] = (dk * SCALE).astype(BF16)
            dv_ref[t] = dv.astype(BF16)

    tile = pl.BlockSpec((T, BLOCK, HEAD_DIM), lambda i: (i, 0, 0))
    before = pl.BlockSpec((1, BLOCK, HEAD_DIM), lambda i: (jnp.maximum(i * T - 1, 0), 0, 0))
    after = pl.BlockSpec((1, BLOCK, HEAD_DIM), lambda i: (jnp.minimum(i * T + T, NB - 1), 0, 0))
    out = jax.ShapeDtypeStruct(q.shape, BF16)
    return pl.pallas_call(
        body, name=name, grid=(NB // T,),
        in_specs=[tile] * 6 + [before, before, after, after, after, after],
        out_specs=[tile] * 3, out_shape=[out] * 3,
        compiler_params=_params(("parallel",)),
    )(q, k, v, do, lse, tt, k, v, q, do, lse, tt)


def _group_softmax(lse):
    m = jnp.max(lse, axis=0, keepdims=True)
    e = jnp.exp(lse - m)
    return e / jnp.sum(e, axis=0, keepdims=True)


def _mix_fwd(o, lse, name):
    G, S, W = o.shape
    tr = min(256, S)
    blk = pl.BlockSpec((G, tr, W), lambda i: (0, i, 0))

    def fn(ov, lv):
        return (jnp.sum(_group_softmax(lv) * ov.astype(F32), axis=0),)

    return _ew(fn, [(o, blk), (lse, blk)], [((S, W), BF16, _rows(tr, W), False)], (S // tr,), name)


def _mix_bwd(dya, ya, o, lse, name):
    G, S, W = o.shape
    tr = min(256, S)
    blk = pl.BlockSpec((G, tr, HEAD_DIM), lambda i, h: (0, i, h))
    row = pl.BlockSpec((tr, HEAD_DIM), lambda i, h: (i, h))

    def fn(dy, yv, ov, lv):
        w = _group_softmax(lv)
        dyf = dy.astype(F32)
        inner = jnp.sum(dyf * yv.astype(F32), axis=-1, keepdims=True)
        return w * dyf[None], -w * inner[None]

    return _ew(fn, [(dya, row), (ya, row), (o, blk), (lse, blk)],
               [((G, S, W), BF16, blk, False), ((G, S, W), F32, blk, False)],
               (S // tr, HEADS_PER_DIL), name)


CUM_BLOCK = 256


def _split3(x):
    hi = x.astype(BF16)
    r = x - hi.astype(F32)
    mid = r.astype(BF16)
    lo = (r - mid.astype(F32)).astype(BF16)
    return hi, mid, lo


def _tri_matmul(tri, x):
    return sum(jnp.dot(tri, part, preferred_element_type=F32) for part in _split3(x))


def _log_sigmoid(x):
    return jnp.minimum(x, 0.0) - jnp.log(1.0 + jnp.exp(-jnp.abs(x)))


def _fox_prep(f, b, name):
    S = f.shape[0]
    tb = min(CUM_BLOCK, S)

    def body(f_ref, b_ref, c_ref, carry):
        @pl.when(pl.program_id(0) == 0)
        def _():
            carry[...] = jnp.zeros_like(carry)

        ls = _log_sigmoid(f_ref[...] + b_ref[...])
        r = lax.broadcasted_iota(jnp.int32, (tb, tb), 0)
        cidx = lax.broadcasted_iota(jnp.int32, (tb, tb), 1)
        tri = jnp.where(r >= cidx, 1.0, 0.0).astype(BF16)
        c_ref[...] = _tri_matmul(tri, ls) + carry[...]
        carry[...] += jnp.sum(ls, axis=0, keepdims=True)

    return pl.pallas_call(
        body, name=name, grid=(S // tb,),
        in_specs=[_rows(tb, LANES), _bcast(LANES)], out_specs=_rows(tb, LANES),
        out_shape=jax.ShapeDtypeStruct((S, LANES), F32),
        scratch_shapes=[pltpu.VMEM((1, LANES), F32)],
        compiler_params=_params(("arbitrary",)),
    )(f, b)


def _fox_prep_bwd(dc, f, b, name):
    S = f.shape[0]
    tb = min(CUM_BLOCK, S)
    nb = S // tb

    def body(dc_ref, f_ref, b_ref, df_ref, db_ref, carry):
        @pl.when(pl.program_id(0) == 0)
        def _():
            carry[...] = jnp.zeros_like(carry)
            db_ref[...] = jnp.zeros_like(db_ref)

        r = lax.broadcasted_iota(jnp.int32, (tb, tb), 0)
        cidx = lax.broadcasted_iota(jnp.int32, (tb, tb), 1)
        tri = jnp.where(r <= cidx, 1.0, 0.0).astype(BF16)
        dcv = dc_ref[...]
        dls = _tri_matmul(tri, dcv) + carry[...]
        carry[...] += jnp.sum(dcv, axis=0, keepdims=True)
        z = f_ref[...] + b_ref[...]
        df = dls * (1.0 / (1.0 + jnp.exp(z)))
        df_ref[...] = df
        db_ref[...] += jnp.sum(df, axis=0, keepdims=True)

    rev = pl.BlockSpec((tb, LANES), lambda i: (nb - 1 - i, 0))
    return pl.pallas_call(
        body, name=name, grid=(nb,),
        in_specs=[rev, rev, _bcast(LANES)], out_specs=[rev, _bcast(LANES)],
        out_shape=[jax.ShapeDtypeStruct((S, LANES), F32), jax.ShapeDtypeStruct((1, LANES), F32)],
        scratch_shapes=[pltpu.VMEM((1, LANES), F32)],
        compiler_params=_params(("arbitrary",)),
    )(dc, f, b)


FOX_Q_TILE = 256
_FOX_Q0 = N_DIL_HEADS
_FOX_K0 = N_HEADS + N_DIL_HEADS
_FOX_V0 = 2 * N_HEADS + N_DIL_HEADS


def _fox_scores(q, k, cq, ck, iq, tq, S):
    nt = (((1,), (1,)), ((), ()))
    s = lax.dot_general(q, k, nt, preferred_element_type=F32) * SCALE + cq - ck
    qpos = iq * tq + lax.broadcasted_iota(jnp.int32, (tq, S), 0)
    kpos = lax.broadcasted_iota(jnp.int32, (tq, S), 1)
    return jnp.where(kpos <= qpos, s, NEG)


def _fox_fwd(z, cq, ck, name):
    S = z.shape[0]
    tq = min(FOX_Q_TILE, S)

    def body(q_ref, k_ref, v_ref, cq_ref, ck_ref, o_ref, lse_ref):
        s = _fox_scores(q_ref[...], k_ref[...], cq_ref[...], ck_ref[...], pl.program_id(1), tq, S)
        m = jnp.max(s, axis=-1, keepdims=True)
        p = jnp.exp(s - m)
        l = jnp.sum(p, axis=-1, keepdims=True)
        o = jnp.dot(p.astype(BF16), v_ref[...], preferred_element_type=F32)
        o_ref[...] = (o / l).astype(o_ref.dtype)
        lse_ref[...] = m + jnp.log(l)

    qblk = lambda c0: pl.BlockSpec((tq, HEAD_DIM), lambda h, i: (i, c0 + h))
    full = lambda c0: pl.BlockSpec((S, HEAD_DIM), lambda h, i: (0, c0 + h))
    col = pl.BlockSpec((None, tq, 1), lambda h, i: (h, i, 0))
    rowv = pl.BlockSpec((None, 1, S), lambda h, i: (h, 0, 0))
    return pl.pallas_call(
        body, name=name, grid=(N_FOX_HEADS, S // tq),
        in_specs=[qblk(_FOX_Q0), full(_FOX_K0), full(_FOX_V0), col, rowv],
        out_specs=[qblk(0), col],
        out_shape=[jax.ShapeDtypeStruct((S, FOX_WIDTH), BF16),
                   jax.ShapeDtypeStruct((N_FOX_HEADS, S, 1), F32)],
        compiler_params=_params(("parallel", "parallel")),
    )(z, z, z, cq, ck)


def _fox_bwd(z, cq, ck, lse, yb, dyb, name):
    S = z.shape[0]
    tq = min(FOX_Q_TILE, S)
    nq = S // tq
    nt = (((1,), (1,)), ((), ()))
    tn = (((0,), (0,)), ((), ()))

    def body(q_ref, k_ref, v_ref, cq_ref, ck_ref, lse_ref, o_ref, do_ref,
             dq_ref, dk_ref, dv_ref, dc_ref, dk_acc, dv_acc):
        i = pl.program_id(1)

        @pl.when(i == 0)
        def _():
            dk_acc[...] = jnp.zeros_like(dk_acc)
            dv_acc[...] = jnp.zeros_like(dv_acc)
            dc_ref[...] = jnp.zeros_like(dc_ref)

        q, k, v, do = q_ref[...], k_ref[...], v_ref[...], do_ref[...]
        s = _fox_scores(q, k, cq_ref[...], ck_ref[...], i, tq, S)
        p = jnp.exp(s - lse_ref[...])
        dp = lax.dot_general(do, v, nt, preferred_element_type=F32)
        ds = p * (dp - jnp.sum(p * dp, axis=-1, keepdims=True))
        dsb = ds.astype(BF16)
        dq_ref[...] = (jnp.dot(dsb, k, preferred_element_type=F32) * SCALE).astype(BF16)
        dk_acc[...] += lax.dot_general(dsb, q, tn, preferred_element_type=F32) * SCALE
        dv_acc[...] += lax.dot_general(p.astype(BF16), do, tn, preferred_element_type=F32)
        dc_ref[...] -= jnp.sum(ds, axis=0, keepdims=True)

        @pl.when(i == nq - 1)
        def _():
            dk_ref[...] = dk_acc[...].astype(BF16)
            dv_ref[...] = dv_acc[...].astype(BF16)

    qblk = lambda c0: pl.BlockSpec((tq, HEAD_DIM), lambda h, i: (i, c0 + h))
    full = lambda c0: pl.BlockSpec((S, HEAD_DIM), lambda h, i: (0, c0 + h))
    col = pl.BlockSpec((None, tq, 1), lambda h, i: (h, i, 0))
    rowv = pl.BlockSpec((None, 1, S), lambda h, i: (h, 0, 0))
    wide = jax.ShapeDtypeStruct((S, FOX_WIDTH), BF16)
    return pl.pallas_call(
        body, name=name, grid=(N_FOX_HEADS, nq),
        in_specs=[qblk(_FOX_Q0), full(_FOX_K0), full(_FOX_V0), col, rowv, col, qblk(0), qblk(0)],
        out_specs=[qblk(0), full(0), full(0), rowv],
        out_shape=[wide, wide, wide, jax.ShapeDtypeStruct((N_FOX_HEADS, 1, S), F32)],
        scratch_shapes=[pltpu.VMEM((S, HEAD_DIM), F32), pltpu.VMEM((S, HEAD_DIM), F32)],
        compiler_params=_params(("parallel", "arbitrary")),
    )(z, z, z, cq, ck, lse, yb, dyb)


def _sigmoid(x):
    return 1.0 / (1.0 + jnp.exp(-x))


def _merge_fwd(gates, a, bm, name):
    S, D = a.shape
    tr = _row_tile(S, D * 4)
    g1 = pl.BlockSpec((tr, D), lambda i: (i, 0))
    g2 = pl.BlockSpec((tr, D), lambda i: (i, 1))

    def fn(x1, x2, av, bv):
        return (x1.astype(F32) * av.astype(F32) + x2.astype(F32) * bv.astype(F32),)

    return _ew(fn, [(gates, g1), (gates, g2), (a, _rows(tr, D)), (bm, _rows(tr, D))],
               [((S, D), BF16, _rows(tr, D), False)], (S // tr,), name)


def _merge_bwd(dmerged, gates, a, bm, name):
    S, D = a.shape
    tr = _row_tile(S, D * 8)
    g1 = pl.BlockSpec((tr, D), lambda i: (i, 0))
    g2 = pl.BlockSpec((tr, D), lambda i: (i, 1))

    def fn(dm, x1, x2, av, bv):
        dm, x1, x2 = dm.astype(F32), x1.astype(F32), x2.astype(F32)
        dg1 = dm * av.astype(F32) * x1 * (1.0 - x1)
        dg2 = dm * bv.astype(F32) * x2 * (1.0 - x2)
        dgp = jnp.concatenate([dg1, dg2], axis=1)
        return dm * x1, dm * x2, dgp, jnp.sum(dgp, axis=0, keepdims=True)

    return _ew(fn, [(dmerged, _rows(tr, D)), (gates, g1), (gates, g2), (a, _rows(tr, D)), (bm, _rows(tr, D))],
               [((S, D), BF16, _rows(tr, D), False), ((S, D), BF16, _rows(tr, D), False),
                ((S, 2 * D), BF16, _rows(tr, 2 * D), False), ((1, 2 * D), F32, _bcast(2 * D), True)],
               (S // tr,), name)


def _ple_bwd(dh, pg, pe, name):
    S, D = dh.shape
    tr = _row_tile(S, D * 4)

    def fn(d, g, e):
        g, e = g.astype(F32), e.astype(F32)
        return d * g, d * e * g * (1.0 - g)

    spec = _rows(tr, D)
    return _ew(fn, [(dh, spec), (pg, spec), (pe, spec)],
               [((S, D), BF16, spec, False), ((S, D), BF16, spec, False)], (S // tr,), name)


def _position():
    x, y, c = lax.axis_index("x"), lax.axis_index("y"), lax.axis_index("c")
    chips = [(1 - x, y), (x, 1 - y), (1 - x, 1 - y)]
    return x, y, c, chips


def _remote(src, dst, send_sem, recv_sem, target):
    return pltpu.make_async_remote_copy(src_ref=src, dst_ref=dst, send_sem=send_sem, recv_sem=recv_sem,
                                        device_id=target, device_id_type=MESH)


HBM = pl.BlockSpec(memory_space=pltpu.HBM)
SEM = pl.BlockSpec(memory_space=pltpu.SEMAPHORE)
EFFECT = pltpu.SideEffectType.DATAFLOW_SIDE_EFFECTING


def _copies_start(plan, arrays, sem_shape, after, name):
    n = len(arrays)

    def body(*refs):
        send_sems, recv_sems, token = refs[n + 1], refs[n + 2], refs[-1]
        for send, _ in plan(refs[:n], send_sems, recv_sems):
            send.start()
        token[...] = jnp.zeros_like(token)

    outs = pl.pallas_call(
        body, name=name,
        out_shape=(pltpu.SemaphoreType.DMA(sem_shape), pltpu.SemaphoreType.DMA(sem_shape),
                   *[pltpu.HBM(a.shape, a.dtype) for a in arrays], jax.ShapeDtypeStruct((8, LANES), F32)),
        in_specs=[HBM] * n + [ANY],
        out_specs=(SEM, SEM, *[HBM] * n, pl.BlockSpec(memory_space=pltpu.VMEM)),
        input_output_aliases={a: 2 + a for a in range(n)},
        compiler_params=pltpu.CompilerParams(has_side_effects=EFFECT),
    )(*[pltpu.with_memory_space_constraint(a, pltpu.HBM) for a in arrays], after)
    return outs[0], outs[1], list(outs[2:2 + n]), outs[-1]


def _copies_wait(plan, send_sems, recv_sems, arrays, after, name):
    n = len(arrays)

    def body(*refs):
        for send, recv in plan(refs[:n], refs[n], refs[n + 1]):
            send.wait_send()
            recv.wait_recv()

    return list(pl.pallas_call(
        body, name=name,
        out_shape=[pltpu.HBM(a.shape, a.dtype) for a in arrays],
        in_specs=[HBM] * n + [SEM, SEM, ANY], out_specs=[HBM] * n,
        input_output_aliases={a: a for a in range(n)},
        compiler_params=pltpu.CompilerParams(has_side_effects=EFFECT),
    )(*arrays, send_sems, recv_sems, after))


def _gather_ici_plan(refs, send_sems, recv_sems):
    x, y, c, chips = _position()
    plan = []
    for a, ref in enumerate(refs):
        rh = ref.shape[1] // 2
        mine = ref.at[2 * x + y, pl.ds(c * rh, rh)]
        for j, (cx, cy) in enumerate(chips):
            landed = ref.at[2 * cx + cy, pl.ds(c * rh, rh)]
            plan.append((_remote(mine, mine, send_sems.at[3 * a + j], recv_sems.at[3 * a + j], (cx, cy, c)),
                         _remote(landed, landed, send_sems.at[3 * a + j], recv_sems.at[3 * a + j], (cx, cy, c))))
    return plan


def _gather_d2d_plan(refs, send_sems, recv_sems):
    x, y, c, chips = _position()
    sibling = (x, y, 1 - c)
    plan = []
    for a, ref in enumerate(refs):
        rh = ref.shape[1] // 2
        for j, (cx, cy) in enumerate(chips):
            landed = ref.at[2 * cx + cy, pl.ds(c * rh, rh)]
            theirs = ref.at[2 * cx + cy, pl.ds((1 - c) * rh, rh)]
            plan.append((_remote(landed, landed, send_sems.at[3 * a + j], recv_sems.at[3 * a + j], sibling),
                         _remote(theirs, theirs, send_sems.at[3 * a + j], recv_sems.at[3 * a + j], sibling)))
    return plan


def _scatter_ici_plan(refs, send_sems, recv_sems):
    x, y, c, chips = _position()
    n = len(refs) // 2
    plan = []
    for a in range(n):
        for j, (cx, cy) in enumerate(chips):
            cp = _remote(refs[a].at[2 * cx + cy], refs[n + a].at[j], send_sems.at[3 * a + j], recv_sems.at[3 * a + j],
                         (cx, cy, c))
            plan.append((cp, cp))
    return plan


def _place_shard(w, layer, me, after, name):
    _, r, cc = w.shape
    tr = _row_tile(r, cc * 4)

    def body(me_ref, w_ref, after_ref, o_ref):
        o_ref[...] = w_ref[...].astype(o_ref.dtype)

    return pl.pallas_call(
        body, name=name,
        grid_spec=pltpu.PrefetchScalarGridSpec(
            num_scalar_prefetch=1, grid=(r // tr,),
            in_specs=[pl.BlockSpec((None, tr, cc), lambda i, me_ref: (layer, i, 0)), ANY],
            out_specs=pl.BlockSpec((None, tr, cc), lambda i, me_ref: (me_ref[0], i, 0))),
        out_shape=jax.ShapeDtypeStruct((N_CHIPS, r, cc), BF16),
        compiler_params=_params(("parallel",)),
    )(me, w, after)


def _pair_plan(refs, send_sems, recv_sems):
    x, y, c, _ = _position()
    n = len(refs) // 2
    plan = []
    for a in range(n):
        rh = refs[a].shape[1] // 2
        cp = _remote(refs[a].at[:, pl.ds((1 - c) * rh, rh)], refs[n + a], send_sems.at[a], recv_sems.at[a],
                     (x, y, 1 - c))
        plan.append((cp, cp))
    return plan


def _pair_sum(mine, theirs, c, name):
    nch, rh, cc = theirs.shape
    tr = _row_tile(rh, cc * 4)
    nb = rh // tr

    def body(c_ref, m_ref, t_ref, o_ref):
        o_ref[...] = (m_ref[...].astype(F32) + t_ref[...].astype(F32)).astype(o_ref.dtype)

    return pl.pallas_call(
        body, name=name,
        grid_spec=pltpu.PrefetchScalarGridSpec(
            num_scalar_prefetch=1, grid=(nch, nb),
            in_specs=[pl.BlockSpec((1, tr, cc), lambda k, i, c_ref: (k, c_ref[0] * nb + i, 0)),
                      pl.BlockSpec((1, tr, cc), lambda k, i, c_ref: (k, i, 0))],
            out_specs=pl.BlockSpec((1, tr, cc), lambda k, i, c_ref: (k, i, 0))),
        out_shape=jax.ShapeDtypeStruct(theirs.shape, BF16),
        compiler_params=_params(("parallel", "parallel")),
    )(c, mine, theirs)


def _chip_sum(own, others, me, c, total, layer, depth, name):
    _, rh, cc = own.shape
    tr = _row_tile(rh, cc * 4)
    nb = rh // tr
    chained = total is not None

    def body(me_ref, c_ref, o_ref, r_ref, *rest):
        g_ref = rest[-1]
        g_ref[...] = (o_ref[0].astype(F32) + r_ref[0].astype(F32)) + (r_ref[1].astype(F32) + r_ref[2].astype(F32))

    in_specs = [pl.BlockSpec((1, tr, cc), lambda i, me_ref, c_ref: (me_ref[0], i, 0)),
                pl.BlockSpec((3, tr, cc), lambda i, me_ref, c_ref: (0, i, 0))]
    args = [me, c, own, others]
    if chained:
        in_specs.append(ANY)
        args.append(total)
    return pl.pallas_call(
        body, name=name,
        grid_spec=pltpu.PrefetchScalarGridSpec(
            num_scalar_prefetch=2, grid=(nb,), in_specs=in_specs,
            out_specs=pl.BlockSpec((None, tr, cc), lambda i, me_ref, c_ref: (layer, c_ref[0] * nb + i, 0))),
        out_shape=jax.ShapeDtypeStruct((depth, 2 * rh, cc), F32),
        input_output_aliases={4: 0} if chained else {},
        compiler_params=_params(("parallel",)),
    )(*args)


def _half_exchange(totals, layer, name):
    n = len(totals)

    def body(*refs):
        outs = refs[n:2 * n]
        send_sems, recv_sems = refs[2 * n:]
        x, y, c, _ = _position()
        copies = []
        for a in range(n):
            rh = outs[a].shape[1] // 2
            mine = outs[a].at[layer, pl.ds(c * rh, rh)]
            cp = _remote(mine, mine, send_sems.at[a], recv_sems.at[a], (x, y, 1 - c))
            cp.start()
            copies.append(cp)
        for a, cp in enumerate(copies):
            rh = outs[a].shape[1] // 2
            theirs = outs[a].at[layer, pl.ds((1 - c) * rh, rh)]
            cp.wait_send()
            _remote(theirs, theirs, send_sems.at[a], recv_sems.at[a], (x, y, 1 - c)).wait_recv()

    return pl.pallas_call(
        body, name=name, in_specs=[ANY] * n, out_specs=[ANY] * n,
        out_shape=[jax.ShapeDtypeStruct(t.shape, t.dtype) for t in totals],
        input_output_aliases={a: a for a in range(n)},
        scratch_shapes=[pltpu.SemaphoreType.DMA((n,)), pltpu.SemaphoreType.DMA((n,))],
    )(*totals)


def _allgather_devices(v, name):
    m_per, n = v.shape

    def body(x_ref, out_ref, send_sems, recv_sems, local_sem):
        x, y, c, chips = _position()
        me, sibling = (x, y, c), (x, y, 1 - c)

        def rows(px, py, pc):
            return out_ref.at[pl.ds((4 * px + 2 * py + pc) * m_per, m_per), :]

        def copy(k, block, to, src=None):
            return _remote(rows(*block) if src is None else src, rows(*block), send_sems.at[k], recv_sems.at[k], to)

        mine = pltpu.make_async_copy(x_ref, rows(*me), local_sem)
        mine.start()
        first = [copy(0, me, sibling, src=x_ref)]
        first += [copy(1 + j, me, (*chip, c), src=x_ref) for j, chip in enumerate(chips)]
        for cp in first:
            cp.start()
        passed = [copy(4 + j, (*chip, c), sibling) for j, chip in enumerate(chips)]
        for j, chip in enumerate(chips):
            copy(1 + j, (*chip, c), me).wait_recv()
            passed[j].start()
        copy(0, sibling, me).wait_recv()
        for j, chip in enumerate(chips):
            copy(4 + j, (*chip, 1 - c), me).wait_recv()
        for cp in first + passed:
            cp.wait_send()
        mine.wait()

    vm = pl.BlockSpec(memory_space=pltpu.VMEM)
    return pl.pallas_call(
        body, name=name, in_specs=[vm], out_specs=vm,
        out_shape=jax.ShapeDtypeStruct((8 * m_per, n), v.dtype),
        scratch_shapes=[pltpu.SemaphoreType.DMA((7,)), pltpu.SemaphoreType.DMA((7,)), pltpu.SemaphoreType.DMA],
    )(v)


def _adamw_math(w, g, m, v):
    m = ADAM_B1 * m + (1.0 - ADAM_B1) * g
    v = ADAM_B2 * v + (1.0 - ADAM_B2) * (g * g)
    m_hat = m / (1.0 - ADAM_B1 ** ADAM_STEP)
    v_hat = v / (1.0 - ADAM_B2 ** ADAM_STEP)
    delta = -ADAM_LR * (m_hat / (jnp.sqrt(v_hat) + ADAM_EPS) + ADAM_WD * w)
    return delta, m, v


def _adamw(w, g, m, v, lo, hi, prev, after, name):
    depth, r, cc = w.shape
    tr = _row_tile(r, cc * 4 * 2)
    spec = pl.BlockSpec((1, tr, cc), lambda l, i: (lo + l, i, 0))

    def body(w_ref, g_ref, m_ref, v_ref, *rest):
        outs = rest[-4:]
        gv = g_ref[...]
        for o_ref, val in zip(outs, (gv,) + _adamw_math(w_ref[...], gv, m_ref[...], v_ref[...])):
            o_ref[...] = val

    prev = list(prev) if prev is not None else []
    return tuple(pl.pallas_call(
        body, name=name, grid=(hi - lo, r // tr),
        in_specs=[spec] * 4 + [ANY] * (1 + len(prev)), out_specs=[spec] * 4,
        out_shape=[jax.ShapeDtypeStruct(w.shape, F32)] * 4,
        input_output_aliases={5 + k: k for k in range(len(prev))},
        compiler_params=_params(("parallel", "parallel")),
    )(w, g, m, v, after, *prev))


def _adamw_small(w, parts, m, v, name):
    M = w.shape[0]

    def body(w_ref, p_ref, m_ref, v_ref, g_ref, d_ref, nm_ref, nv_ref):
        g = p_ref[pl.ds(0, M), :]
        for k in range(1, 8):
            g = g + p_ref[pl.ds(k * M, M), :]
        d, nm, nv = _adamw_math(w_ref[...], g, m_ref[...], v_ref[...])
        g_ref[...] = g
        d_ref[...] = d
        nm_ref[...] = nm
        nv_ref[...] = nv

    vm = pl.BlockSpec(memory_space=pltpu.VMEM)
    return pl.pallas_call(
        body, name=name, in_specs=[vm] * 4, out_specs=[vm] * 4,
        out_shape=[jax.ShapeDtypeStruct(w.shape, F32)] * 4,
    )(w, parts, m, v)


def _layer_fwd(h0, p_l, W, small, tabs, after_mlp):
    S, D = h0.shape
    ctab, stab = tabs
    u = _rms_fwd(h0, small["g_mix"], "rms_mix")
    z = _mm(u, W["w_qkv"], mode="nn", name="mm_qkv")
    f = _mm(u, W["w_f"], mode="nn", name="mm_f", out_dtypes=(F32,))
    gates = _mm(u, W["w_gate"], mode="nn", name="mm_gate", b_chunked=True, extras=[(small["b_gate"], "row")],
                epilogue=lambda acc, b: (_sigmoid(acc + b),))
    qr, kr = _rope(z, 0, z, N_HEADS, ctab, stab, 1.0, "rope_fwd")
    qs, ks, vs = _to_strided(qr), _to_strided(kr), _to_strided(z[:, 2 * ATTN_WIDTH:2 * ATTN_WIDTH + DIL_WIDTH])
    o_s, lse_s = _sw_fwd(qs, ks, vs, "sw_fwd")
    o_g, lse_g = _from_strided(o_s, S), _from_strided(lse_s, S)
    ya = _mix_fwd(o_g, lse_g, "mix_fwd")
    a = _mm(ya, W["w_br_a"], mode="nn", name="mm_br_a", b_chunked=True)
    cum = _fox_prep(f, small["b_f"], "fox_prep")
    cq = cum[:, :N_FOX_HEADS].T[:, :, None]
    ck = cum[:, :N_FOX_HEADS].T[:, None, :]
    yb, lse_f = _fox_fwd(z, cq, ck, "fox_fwd")
    bm = _mm(yb, W["w_br_b"], mode="nn", name="mm_br_b", b_chunked=True)
    merged = _merge_fwd(gates, a, bm, "merge_fwd")
    h1 = _mm(merged, W["w_o"], mode="nn", name="mm_o", out_dtypes=(F32,), extras=[(h0, "tile")],
             epilogue=lambda acc, r: (acc + r,))
    m = _rms_fwd(h1, small["g_mlp"], "rms_mlp")
    ra, act = _mm(m, W["w_up"], mode="nn", name="mm_up", b_chunked=True, out_dtypes=(BF16, BF16),
                  epilogue=lambda acc: (jnp.maximum(acc, 0.0), jnp.square(jnp.maximum(acc, 0.0))))
    h2 = _mm(act, W["w_down"], mode="nn", name="mm_down", out_dtypes=(F32,), extras=[(h1, "tile")],
             epilogue=lambda acc, r: (acc + r,), ti=1024, tc=4096)
    after_mlp(h2)
    n = _rms_fwd(h2, small["g_ple"], "rms_ple")
    pg = _mm(n, W["w_ple_gate"], mode="nn", name="mm_ple_gate", epilogue=lambda acc: (_sigmoid(acc),))
    h3, pe = _mm(p_l, W["w_ple"], mode="nn", name="mm_ple", b_chunked=True, out_dtypes=(F32, BF16),
                 extras=[(h2, "tile"), (pg, "tile")], tj=256,
                 epilogue=lambda acc, r, g: (r + g.astype(F32) * acc, acc))
    saved = dict(h0=h0, u=u, z=z, f=f, gates=gates, qs=qs, ks=ks, vs=vs, lse_s=lse_s, o_g=o_g, lse_g=lse_g,
                 ya=ya, a=a, cq=cq, ck=ck, yb=yb, lse_f=lse_f, bm=bm, merged=merged, h1=h1, m=m, ra=ra,
                 act=act, h2=h2, n=n, pg=pg, pe=pe, p_l=p_l)
    return h3, saved


def _layer_bwd(dh3, sv, W, small, tabs, after_mlp_grad):
    S, D = dh3.shape
    ctab, stab = tabs
    gw, gs = {}, {}
    tn = functools.partial(_mm, mode="tn", ti=512, tj=1024)
    dpe, dpg = _ple_bwd(dh3, sv["pg"], sv["pe"], "ple_bwd")
    gw["w_ple"] = tn(sv["p_l"], dpe, name="dw_ple", out_chunks=N_CHIPS)
    gw["w_ple_gate"] = tn(sv["n"], dpg, name="dw_ple_gate").reshape(N_CHIPS, D // N_CHIPS, D)
    dn = _mm(dpg, W["w_ple_gate"], mode="nt", name="mm_dn")
    dh2, dh2b, gs["g_ple"] = _rms_bwd(sv["h2"], small["g_ple"], dn, dh3, "rms_ple_bwd")
    da = _mm(dh2b, W["w_down"], mode="nt", name="mm_dact", extras=[(sv["ra"], "tile")],
             epilogue=lambda acc, r: (acc * (2.0 * r.astype(F32)),))
    FF = da.shape[1]
    after_mlp_grad(da)
    gw["w_down"] = tn(sv["act"], dh2b, name="dw_down").reshape(N_CHIPS, FF // N_CHIPS, D)
    gw["w_up"] = tn(sv["m"], da, name="dw_up", out_chunks=N_CHIPS)
    dm = _mm(da, W["w_up"], mode="nt", name="mm_dm", b_chunked=True)
    dh1, dh1b, gs["g_mlp"] = _rms_bwd(sv["h1"], small["g_mlp"], dm, dh2, "rms_mlp_bwd")
    dmerged = _mm(dh1b, W["w_o"], mode="nt", name="mm_dmerged")
    gw["w_o"] = tn(sv["merged"], dh1b, name="dw_o").reshape(N_CHIPS, D // N_CHIPS, D)
    d_a, d_b, dgp, gs["b_gate"] = _merge_bwd(dmerged, sv["gates"], sv["a"], sv["bm"], "merge_bwd")
    gw["w_gate"] = tn(sv["u"], dgp, name="dw_gate", out_chunks=N_CHIPS)
    gw["w_br_a"] = tn(sv["ya"], d_a, name="dw_br_a", out_chunks=N_CHIPS, tj=512)
    gw["w_br_b"] = tn(sv["yb"], d_b, name="dw_br_b", out_chunks=N_CHIPS, tj=512)
    dya = _mm(d_a, W["w_br_a"], mode="nt", name="mm_dya", b_chunked=True)
    dyb = _mm(d_b, W["w_br_b"], mode="nt", name="mm_dyb", b_chunked=True)
    z = sv["z"]
    dq_f, dk_f, dv_f, dck = _fox_bwd(z, sv["cq"], sv["ck"], sv["lse_f"], sv["yb"], dyb, "fox_bwd")
    dc = jnp.pad(dck[:, 0, :].T, ((0, 0), (0, LANES - N_FOX_HEADS)))
    df, dbf = _fox_prep_bwd(dc, sv["f"], small["b_f"], "fox_prep_bwd")
    gs["b_f"] = dbf[:, :N_FOX_HEADS]
    lane = jnp.arange(LANES)[None, :] < N_FOX_HEADS
    dzf = jnp.where(lane, df, 0.0).astype(BF16)
    do_g, tt_g = _mix_bwd(dya, sv["ya"], sv["o_g"], sv["lse_g"], "mix_bwd")
    do_s = _to_strided(do_g.transpose(1, 0, 2).reshape(S, DIL_WIDTH))
    tt_s = _to_strided(tt_g.transpose(1, 0, 2).reshape(S, DIL_WIDTH))
    dq_s, dk_s, dv_s = _sw_bwd(sv["qs"], sv["ks"], sv["vs"], do_s, sv["lse_s"], tt_s, "sw_bwd")
    unstride = lambda t: _from_strided(t, S).transpose(1, 0, 2).reshape(S, DIL_WIDTH)
    dq_a, dk_a = _rope(unstride(dq_s), 0, unstride(dk_s), 0, ctab, stab, -1.0, "rope_bwd")
    dz = jnp.concatenate([dq_a, dq_f, dk_a, dk_f, unstride(dv_s), dv_f], axis=1)
    g_qkv = tn(sv["u"], dz, name="dw_qkv")
    g_f = tn(sv["u"], dzf, name="dw_f", tj=128)
    cols = W["w_in_cols"]
    g_in = jnp.concatenate([g_qkv, g_f[:, :N_FOX_HEADS]], axis=1)
    gw["w_in"] = jnp.stack([g_in[:, k * cols:(k + 1) * cols] for k in range(N_CHIPS)])
    du = _mm(dzf, W["w_f"], mode="nt", name="mm_du_f", out_dtypes=(F32,))
    du = _mm(dgp, W["w_gate"], mode="nt", name="mm_du_gate", b_chunked=True, out_dtypes=(F32,),
             extras=[(du, "tile")], epilogue=lambda acc, r: (acc + r,))
    du = _mm(dz, W["w_qkv"], mode="nt", name="mm_du_qkv", extras=[(du, "tile")],
             epilogue=lambda acc, r: (acc + r,))
    dh0, _, gs["g_mix"] = _rms_bwd(sv["h0"], small["g_mix"], du, dh1, "rms_mix_bwd")
    return dh0, gw, gs


BIG = ("w_in", "w_gate", "w_br_a", "w_br_b", "w_o", "w_up", "w_down", "w_ple", "w_ple_gate")
SMALL = ("g_mix", "b_f", "b_gate", "g_mlp", "g_ple", "g_final")
ORDER = ("g_mix", "w_in", "b_f", "w_gate", "b_gate", "w_br_a", "w_br_b", "w_o", "g_mlp", "w_up", "w_down",
         "g_ple", "w_ple", "w_ple_gate", "g_final")


def _gathered_layer_weights(full, D):
    W = {}
    for name in ("w_gate", "w_br_a", "w_br_b", "w_up", "w_ple"):
        W[name] = full[name]
    for name in ("w_o", "w_down", "w_ple_gate"):
        t = full[name]
        W[name] = t.reshape(t.shape[0] * t.shape[1], t.shape[2])
    w_in = full["w_in"]
    cols = w_in.shape[2]
    w_in = jnp.concatenate([w_in[k] for k in range(N_CHIPS)], axis=1)
    W["w_qkv"] = w_in[:, :3 * ATTN_WIDTH]
    W["w_f"] = jnp.pad(w_in[:, 3 * ATTN_WIDTH:], ((0, 0), (0, LANES - N_FOX_HEADS)))
    W["w_in_cols"] = cols
    return W


def _pack_rows(vals):
    flat = jnp.concatenate([v.reshape(-1) for v in vals])
    rows = -(-flat.shape[0] // (8 * LANES)) * 8
    return jnp.pad(flat, (0, rows * LANES - flat.shape[0])).reshape(rows, LANES)


def _unpack_rows(packed, shapes):
    flat = packed.reshape(-1)
    out, pos = [], 0
    for s in shapes:
        size = 1
        for dim in s:
            size *= dim
        out.append(flat[pos:pos + size].reshape(s))
        pos += size
    return out


def _local_step(x, p, small_w, comm, loss_target):
    depth = p.shape[0]
    S, D = x.shape
    tabs = _rope_tables(S)
    h = x
    saved, weights, smalls = [], [], []
    state, _ = comm["gather_start"](0, x)
    full = comm["gather_finish"](comm["gather_mid"](state, x), x)
    for l in range(depth):
        nxt = [None]
        g_mix = small_w["g_mix"][l][None]
        if l + 1 < depth:
            nxt[0], token = comm["gather_start"](l + 1, full["w_ple"])
            g_mix = g_mix + token[0, 0]

        def after_mlp(h2):
            if nxt[0] is not None:
                nxt[0] = comm["gather_mid"](nxt[0], h2)

        W = _gathered_layer_weights(full, D)
        sm = dict(g_mix=g_mix, g_mlp=small_w["g_mlp"][l][None],
                  g_ple=small_w["g_ple"][l][None], b_gate=small_w["b_gate"][l][None],
                  b_f=jnp.pad(small_w["b_f"][l][None], ((0, 0), (0, LANES - N_FOX_HEADS))))
        h, sv = _layer_fwd(h, p[l].astype(BF16), W, sm, tabs, after_mlp)
        if nxt[0] is not None:
            full = comm["gather_finish"](nxt[0], h)
        saved.append(sv)
        weights.append(W)
        smalls.append(sm)
    dh, loss_row, dg_final = _loss_head(h, small_w["g_final"][None], loss_target, "loss_head")
    gss = [None] * depth
    pending, token = [None], None
    for l in reversed(range(depth)):
        sm = smalls[l]
        if token is not None:
            sm = {**sm, "g_ple": sm["g_ple"] + token[0, 0]}

        def after_mlp_grad(da):
            if pending[0] is not None:
                pending[0] = comm["reduce_mid"](pending[0], da)

        dh, gw, gss[l] = _layer_bwd(dh, saved[l], weights[l], sm, tabs, after_mlp_grad)
        if pending[0] is not None:
            comm["reduce_end"](pending[0], dh)
        if l > 0:
            pending[0], token = comm["reduce_begin"](l, gw, dh)
    return loss_row, dh, gss, dg_final, gw


def _device_comm(w, m, v, depth, c_arr, me_arr):
    n = len(BIG)
    totals = [[None] * n]
    placed, results = [], {}

    def gather_start(l, after):
        if l == 0:
            placed.append([_place_shard(w[name], 0, me_arr, me_arr, "place_shard") for name in BIG])
        send, recv, bufs, token = _copies_start(_gather_ici_plan, placed[l], (3 * n,), after, f"gather_ici_start_{l}")
        if l == 0:
            placed.extend([_place_shard(w[name], k, me_arr, token, "place_shard") for name in BIG]
                          for k in range(1, depth))
        return (l, send, recv, bufs), token

    def gather_mid(state, after):
        l, send, recv, bufs = state
        if l == 0:
            after = placed[-1][-1]
        bufs = _copies_wait(_gather_ici_plan, send, recv, bufs, after, f"gather_ici_wait_{l}")
        send, recv, bufs, _ = _copies_start(_gather_d2d_plan, bufs, (3 * n,), after, f"gather_d2d_start_{l}")
        return l, send, recv, bufs

    def gather_finish(state, after):
        l, send, recv, bufs = state
        return dict(zip(BIG, _copies_wait(_gather_d2d_plan, send, recv, bufs, after, f"gather_d2d_wait_{l}")))

    def reduce_begin(l, gw, after):
        grads = [gw[name] for name in BIG]
        landing = [lax.empty((g.shape[0], g.shape[1] // 2, g.shape[2]), g.dtype) for g in grads]
        send, recv, arrays, token = _copies_start(_pair_plan, grads + landing, (n,), after, f"pair_start_{l}")
        return (l, send, recv, arrays), token

    def reduce_mid(state, after):
        l, send, recv, arrays = state
        arrays = _copies_wait(_pair_plan, send, recv, arrays, after, f"pair_wait_{l}")
        sums = [_pair_sum(g, t, c_arr, "rs_pair_sum") for g, t in zip(arrays[:n], arrays[n:])]
        landing = [lax.empty((3,) + s.shape[1:], s.dtype) for s in sums]
        send, recv, arrays, token = _copies_start(_scatter_ici_plan, sums + landing, (3 * n,), sums[0],
                                                  f"scatter_ici_start_{l}")
        return l, send, recv, arrays, token

    def reduce_end(state, after):
        l, send, recv, arrays, _ = state
        arrays = _copies_wait(_scatter_ici_plan, send, recv, arrays, after, f"scatter_ici_wait_{l}")
        done = [_chip_sum(s, o, me_arr, c_arr, t, l, depth, "rs_chip_sum")
                for s, o, t in zip(arrays[:n], arrays[n:], totals[0])]
        totals[0] = _half_exchange(done, l, "rs_half_exchange")

    def reduce_last(gw, after):
        upper = {}

        def adamw_upper(names, token):
            for name in names:
                if depth > 1:
                    t = totals[0][BIG.index(name)]
                    upper[name] = _adamw(w[name], t, m[name], v[name], 1, depth, None, token, "adamw_upper")
                    token = upper[name][0]
            return token

        state, token = reduce_begin(0, gw, after)
        state = reduce_mid(state, adamw_upper(("w_up",), token))
        reduce_end(state, adamw_upper([name for name in BIG if name != "w_up"], state[-1]))
        for name, t in zip(BIG, totals[0]):
            results[name] = _adamw(w[name], t, m[name], v[name], 0, 1, upper.get(name), t, "adamw_first")

    comm = dict(gather_start=gather_start, gather_mid=gather_mid, gather_finish=gather_finish,
                reduce_begin=reduce_begin, reduce_mid=reduce_mid, reduce_end=reduce_end, reduce_last=reduce_last)
    return comm, results


def kernel(x, p, g_mix, w_in, b_f, w_gate, b_gate, w_br_a, w_br_b, w_o, g_mlp, w_up, w_down, g_ple, w_ple, w_ple_gate, g_final, loss_target, m_g_mix, m_w_in, m_b_f, m_w_gate, m_b_gate, m_w_br_a, m_w_br_b, m_w_o, m_g_mlp, m_w_up, m_w_down, m_g_ple, m_w_ple, m_w_ple_gate, m_g_final, v_g_mix, v_w_in, v_b_f, v_w_gate, v_b_gate, v_w_br_a, v_w_br_b, v_w_o, v_g_mlp, v_w_up, v_w_down, v_g_ple, v_w_ple, v_w_ple_gate, v_g_final):
    w = dict(g_mix=g_mix, w_in=w_in, b_f=b_f, w_gate=w_gate, b_gate=b_gate, w_br_a=w_br_a, w_br_b=w_br_b,
             w_o=w_o, g_mlp=g_mlp, w_up=w_up, w_down=w_down, g_ple=g_ple, w_ple=w_ple, w_ple_gate=w_ple_gate,
             g_final=g_final)
    m = dict(g_mix=m_g_mix, w_in=m_w_in, b_f=m_b_f, w_gate=m_w_gate, b_gate=m_b_gate, w_br_a=m_w_br_a,
             w_br_b=m_w_br_b, w_o=m_w_o, g_mlp=m_g_mlp, w_up=m_w_up, w_down=m_w_down, g_ple=m_g_ple,
             w_ple=m_w_ple, w_ple_gate=m_w_ple_gate, g_final=m_g_final)
    v = dict(g_mix=v_g_mix, w_in=v_w_in, b_f=v_b_f, w_gate=v_w_gate, b_gate=v_b_gate, w_br_a=v_w_br_a,
             w_br_b=v_w_br_b, w_o=v_w_o, g_mlp=v_g_mlp, w_up=v_w_up, w_down=v_w_down, g_ple=v_g_ple,
             w_ple=v_w_ple, w_ple_gate=v_w_ple_gate, g_final=v_g_final)
    depth = p.shape[0]
    cx, cy, cc = lax.axis_index("x"), lax.axis_index("y"), lax.axis_index("c")
    c_arr = jnp.reshape(cc, (1,)).astype(jnp.int32)
    me_arr = jnp.reshape(2 * cx + cy, (1,)).astype(jnp.int32)

    comm, big_out = _device_comm(w, m, v, depth, c_arr, me_arr)
    loss_row, grad_x, gss, dg_final, gw0 = _local_step(x[0], p[:, 0], w, comm, loss_target[0])

    small_grads = [jnp.stack([gss[l][n][0] for l in range(depth)]) for n in SMALL[:-1]] + [dg_final[0]]
    shapes = [w[n].shape for n in SMALL]
    parts = _allgather_devices(_pack_rows(small_grads), "allgather_small")
    packed = _adamw_small(_pack_rows([w[n] for n in SMALL]), parts, _pack_rows([m[n] for n in SMALL]),
                          _pack_rows([v[n] for n in SMALL]), "adamw_small")
    small_out = {n: vals for n, vals in zip(SMALL, zip(*[_unpack_rows(t, shapes) for t in packed]))}
    comm["reduce_last"](gw0, packed[0])

    loss = lax.psum(loss_row[0, 0], ("x", "y", "c"))
    out = {**big_out, **small_out}
    return (loss, grad_x[None], *[out[n][0] for n in ORDER], *[out[n][1] for n in ORDER],
            *[out[n][2] for n in ORDER], *[out[n][3] for n in ORDER])
```

```python
import functools

import jax
import jax.numpy as jnp
from jax import lax
from jax.experimental import pallas as pl
from jax.experimental.pallas import tpu as pltpu

F32 = jnp.float32
BF16 = jnp.bfloat16

HEAD_DIM = 128
N_HEADS = 16
N_DIL_HEADS = 12
N_FOX_HEADS = 4
HEADS_PER_DIL = 4
DILATIONS = (1, 4, 16)
BLOCK = 128
ATTN_WIDTH = N_HEADS * HEAD_DIM
DIL_WIDTH = N_DIL_HEADS * HEAD_DIM
FOX_WIDTH = N_FOX_HEADS * HEAD_DIM
ROPE_THETA = 500000.0
ROPE_HALF = 16
NORM_EPS = 1e-6
SCALE = HEAD_DIM ** -0.5
NEG = -1e30

ADAM_LR = 0.001
ADAM_B1 = 0.9
ADAM_B2 = 0.999
ADAM_EPS = 1e-08
ADAM_WD = 0.01
ADAM_STEP = 10

N_CHIPS = 4
V7X_VMEM_LIMIT_BYTES = 56 * 1024 * 1024
LANES = 128
MESH = pl.DeviceIdType.MESH
ANY = pl.BlockSpec(memory_space=pl.ANY)


def _params(sem):
    return pltpu.CompilerParams(dimension_semantics=sem, vmem_limit_bytes=V7X_VMEM_LIMIT_BYTES)


def _tile(n, pref):
    if n <= pref:
        return n
    t = (pref // LANES) * LANES
    while t > LANES and n % t:
        t -= LANES
    assert n % t == 0, (n, pref)
    return t


def _mm(a, b, *, mode, name, out_dtypes=(BF16,), epilogue=None, extras=(), b_chunked=False,
        out_chunks=0, ti=2048, tj=512, tc=2048):
    if mode == "tn":
        C, I = a.shape
    else:
        I, C = a.shape
    if b_chunked:
        nch, d0, n = b.shape
        if mode == "nn":
            assert d0 == C
            J = nch * n
        else:
            assert mode == "nt" and nch * n == C
            J = d0
    elif mode == "nt":
        J = b.shape[0]
        assert b.shape[1] == C
    else:
        assert b.shape[0] == C
        J = b.shape[1]
    ti, tc = _tile(I, ti), _tile(C, tc)
    if b_chunked and mode == "nn":
        tj = _tile(n, tj)
    elif out_chunks:
        tj = _tile(J // out_chunks, tj)
    else:
        tj = _tile(J, tj)
    if b_chunked and mode == "nt":
        tc = _tile(n, tc)
    ni, nj, nc = I // ti, J // tj, C // tc

    if mode == "tn":
        a_spec = pl.BlockSpec((tc, ti), lambda i, j, c: (c, i))
        dims = (((0,), (0,)), ((), ()))
    else:
        a_spec = pl.BlockSpec((ti, tc), lambda i, j, c: (i, c))
        dims = (((1,), (0,)), ((), ())) if mode == "nn" else (((1,), (1,)), ((), ()))
    if mode == "nt":
        if b_chunked:
            cb = n // tc
            b_spec = pl.BlockSpec((None, tj, tc), lambda i, j, c: (c // cb, j, c % cb))
        else:
            b_spec = pl.BlockSpec((tj, tc), lambda i, j, c: (j, c))
    else:
        if b_chunked:
            jb = n // tj
            b_spec = pl.BlockSpec((None, tc, tj), lambda i, j, c: (j // jb, c, j % jb))
        else:
            b_spec = pl.BlockSpec((tc, tj), lambda i, j, c: (c, j))
    extra_specs = []
    for arr, kind in extras:
        if kind == "tile":
            assert arr.shape == (I, J), (arr.shape, I, J)
            extra_specs.append(pl.BlockSpec((ti, tj), lambda i, j, c: (i, j)))
        else:
            assert arr.shape == (1, J)
            extra_specs.append(pl.BlockSpec((1, tj), lambda i, j, c: (0, j)))
    if out_chunks:
        ob = (J // out_chunks) // tj
        out_spec = pl.BlockSpec((None, ti, tj), lambda i, j, c: (j // ob, i, j % ob))
        out_shape = [jax.ShapeDtypeStruct((out_chunks, I, J // out_chunks), d) for d in out_dtypes]
    else:
        out_spec = pl.BlockSpec((ti, tj), lambda i, j, c: (i, j))
        out_shape = [jax.ShapeDtypeStruct((I, J), d) for d in out_dtypes]
    ne, no = len(extras), len(out_dtypes)
    if epilogue is None:
        epilogue = lambda acc: (acc,)

    def body(a_ref, b_ref, *rest):
        extra_refs, out_refs = rest[:ne], rest[ne:ne + no]

        def finish(acc):
            outs = epilogue(acc, *[r[...] for r in extra_refs])
            for o_ref, val in zip(out_refs, outs):
                o_ref[...] = val.astype(o_ref.dtype)

        part = lax.dot_general(a_ref[...], b_ref[...], dims, preferred_element_type=F32)
        if nc == 1:
            finish(part)
        else:
            acc_ref = rest[-1]
            k = pl.program_id(2)

            @pl.when(k == 0)
            def _():
                acc_ref[...] = part

            @pl.when(k > 0)
            def _():
                acc_ref[...] += part

            @pl.when(k == nc - 1)
            def _():
                finish(acc_ref[...])

    outs = pl.pallas_call(
        body, name=name, grid=(ni, nj, nc),
        in_specs=[a_spec, b_spec] + extra_specs,
        out_specs=[out_spec] * no, out_shape=out_shape,
        scratch_shapes=[pltpu.VMEM((ti, tj), F32)] if nc > 1 else [],
        compiler_params=_params(("parallel", "parallel", "arbitrary")),
    )(a, b, *[e[0] for e in extras])
    return outs[0] if no == 1 else tuple(outs)


def _ew(fn, ins, outs, grid, name):
    n_in = len(ins)
    has_acc = any(o[3] for o in outs)
    assert not has_acc or len(grid) == 1

    def body(*refs):
        vals = fn(*[r[...] for r in refs[:n_in]])
        for o_ref, o, val in zip(refs[n_in:], outs, vals):
            if o[3]:
                step = pl.program_id(0)

                @pl.when(step == 0)
                def _(o_ref=o_ref, val=val):
                    o_ref[...] = val

                @pl.when(step > 0)
                def _(o_ref=o_ref, val=val):
                    o_ref[...] += val
            else:
                o_ref[...] = val.astype(o_ref.dtype)

    sem = ("arbitrary",) if has_acc else ("parallel",) * len(grid)
    res = pl.pallas_call(
        body, name=name, grid=grid,
        in_specs=[i[1] for i in ins], out_specs=[o[2] for o in outs],
        out_shape=[jax.ShapeDtypeStruct(o[0], o[1]) for o in outs],
        compiler_params=_params(sem),
    )(*[i[0] for i in ins])
    return res[0] if len(outs) == 1 else tuple(res)


def _rows(tr, w):
    return pl.BlockSpec((tr, w), lambda i: (i, 0))


def _bcast(w):
    return pl.BlockSpec((1, w), lambda i: (0, 0))


def _row_tile(S, width_bytes):
    tr = 512
    while tr > 16 and tr * width_bytes > 2 * 1024 * 1024:
        tr //= 2
    return min(tr, S)


def _rms_fwd(h, g, name):
    S, D = h.shape
    tr = _row_tile(S, D * 4)

    def fn(x, gg):
        r = lax.rsqrt(jnp.mean(x * x, axis=-1, keepdims=True) + NORM_EPS)
        return (x * r * gg,)

    return _ew(fn, [(h, _rows(tr, D)), (g, _bcast(D))], [((S, D), BF16, _rows(tr, D), False)],
               (S // tr,), name)


def _rms_bwd(x, g, dy, dres, name):
    S, D = x.shape
    tr = _row_tile(S, D * 4)

    def fn(xv, gg, dyv, dr):
        r = lax.rsqrt(jnp.mean(xv * xv, axis=-1, keepdims=True) + NORM_EPS)
        dyf = dyv.astype(F32)
        gy = dyf * gg
        dx = r * gy - xv * (r * r * r) * jnp.mean(xv * gy, axis=-1, keepdims=True)
        tot = dr + dx
        dg = jnp.sum(dyf * xv * r, axis=0, keepdims=True)
        return tot, tot, dg

    return _ew(fn, [(x, _rows(tr, D)), (g, _bcast(D)), (dy, _rows(tr, D)), (dres, _rows(tr, D))],
               [((S, D), F32, _rows(tr, D), False), ((S, D), BF16, _rows(tr, D), False),
                ((1, D), F32, _bcast(D), True)], (S // tr,), name)


def _loss_head(h, g, target, name):
    S, D = h.shape
    tr = _row_tile(S, D * 4)

    def fn(xv, gg, tgt):
        r = lax.rsqrt(jnp.mean(xv * xv, axis=-1, keepdims=True) + NORM_EPS)
        y = xv * r * gg
        e = y - tgt
        loss = 0.5 * jnp.sum(jnp.mean(e * e, axis=-1, keepdims=True), axis=0, keepdims=True)
        dy = e * (1.0 / D)
        gy = dy * gg
        dx = r * gy - xv * (r * r * r) * jnp.mean(xv * gy, axis=-1, keepdims=True)
        dg = jnp.sum(dy * xv * r, axis=0, keepdims=True)
        return dx, jnp.broadcast_to(loss, (1, LANES)), dg

    return _ew(fn, [(h, _rows(tr, D)), (g, _bcast(D)), (target, _rows(tr, D))],
               [((S, D), F32, _rows(tr, D), False), ((1, LANES), F32, _bcast(LANES), True),
                ((1, D), F32, _bcast(D), True)], (S // tr,), name)


def _rope_tables(S):
    inv = ROPE_THETA ** (-jnp.arange(ROPE_HALF, dtype=F32) / ROPE_HALF)
    ang = jnp.arange(S, dtype=F32)[:, None] * inv[None, :]
    cos, sin = jnp.cos(ang), jnp.sin(ang)
    rest = HEAD_DIM - 2 * ROPE_HALF
    ctab = jnp.concatenate([cos, cos, jnp.ones((S, rest), F32)], axis=1)
    stab = jnp.concatenate([-sin, sin, jnp.zeros((S, rest), F32)], axis=1)
    return ctab, stab


def _swap_halves(x):
    lane = lax.broadcasted_iota(jnp.int32, x.shape, 1)
    return jnp.where(lane < ROPE_HALF, pltpu.roll(x, HEAD_DIM - ROPE_HALF, 1), pltpu.roll(x, ROPE_HALF, 1))


def _rope(q_src, q_col0, k_src, k_col0, ctab, stab, sign, name):
    S = q_src.shape[0]
    tr = min(512, S)
    width = HEADS_PER_DIL * HEAD_DIM

    def fn(q, k, ct, st):
        outs = []
        for v in (q, k):
            heads = []
            for h in range(HEADS_PER_DIL):
                vf = v[:, h * HEAD_DIM:(h + 1) * HEAD_DIM].astype(F32)
                heads.append(vf * ct + sign * _swap_halves(vf) * st)
            outs.append(jnp.concatenate(heads, axis=1))
        return tuple(outs)

    group = lambda c0: pl.BlockSpec((tr, width), lambda i, g: (i, c0 // HEADS_PER_DIL + g))
    tab = pl.BlockSpec((tr, HEAD_DIM), lambda i, g: (i, 0))
    out = ((S, DIL_WIDTH), BF16, group(0), False)
    return _ew(fn, [(q_src, group(q_col0)), (k_src, group(k_col0)), (ctab, tab), (stab, tab)],
               [out, out], (S // tr, len(DILATIONS)), name)


SW_BLOCKS_PER_STEP = 8


def _to_strided(x):
    S = x.shape[0]
    parts = []
    for g, d in enumerate(DILATIONS):
        xg = x[:, g * 512:(g + 1) * 512].reshape(S // d, d, HEADS_PER_DIL, HEAD_DIM)
        parts.append(xg.transpose(1, 2, 0, 3).reshape(-1, BLOCK, HEAD_DIM))
    return jnp.concatenate(parts, axis=0)


def _from_strided(y, S):
    per = y.shape[0] // len(DILATIONS)
    parts = []
    for g, d in enumerate(DILATIONS):
        yg = y[g * per:(g + 1) * per].reshape(d, HEADS_PER_DIL, S // d, HEAD_DIM)
        parts.append(yg.transpose(2, 0, 1, 3).reshape(S, HEADS_PER_DIL * HEAD_DIM))
    return jnp.stack(parts, axis=0)


def _seq_blocks(b0, per_group):
    g = b0 // per_group
    n0 = per_group // HEADS_PER_DIL
    return jnp.where(g == 0, n0, jnp.where(g == 1, n0 // 4, n0 // 16))


def _sw_masks():
    qi = lax.broadcasted_iota(jnp.int32, (BLOCK, BLOCK), 0)
    ki = lax.broadcasted_iota(jnp.int32, (BLOCK, BLOCK), 1)
    return qi >= ki, qi <= ki


def _sw_fwd(q, k, v, name):
    NB = q.shape[0]
    T = SW_BLOCKS_PER_STEP
    per_group = NB // len(DILATIONS)
    nt = (((1,), (1,)), ((), ()))

    def body(q_ref, k_ref, v_ref, kp_ref, vp_ref, o_ref, lse_ref):
        b0 = pl.program_id(0) * T
        nseq = _seq_blocks(b0, per_group)
        cur_mask, prev_mask = _sw_masks()
        for t in range(T):
            has_prev = ((b0 + t) & (nseq - 1)) != 0
            qt = q_ref[t]
            kp = kp_ref[0] if t == 0 else k_ref[t - 1]
            vp = vp_ref[0] if t == 0 else v_ref[t - 1]
            s_c = lax.dot_general(qt, k_ref[t], nt, preferred_element_type=F32) * SCALE
            s_p = lax.dot_general(qt, kp, nt, preferred_element_type=F32) * SCALE
            s_c = jnp.where(cur_mask, s_c, NEG)
            s_p = jnp.where(prev_mask, s_p, NEG) + jnp.where(has_prev, 0.0, NEG)
            m = jnp.maximum(jnp.max(s_c, axis=-1, keepdims=True), jnp.max(s_p, axis=-1, keepdims=True))
            p_c = jnp.exp(s_c - m)
            p_p = jnp.exp(s_p - m)
            l = jnp.sum(p_c, axis=-1, keepdims=True) + jnp.sum(p_p, axis=-1, keepdims=True)
            o = (jnp.dot(p_c.astype(BF16), v_ref[t], preferred_element_type=F32)
                 + jnp.dot(p_p.astype(BF16), vp, preferred_element_type=F32))
            o_ref[t] = (o / l).astype(o_ref.dtype)
            lse_ref[t] = jnp.broadcast_to(m + jnp.log(l), (BLOCK, HEAD_DIM))

    tile = pl.BlockSpec((T, BLOCK, HEAD_DIM), lambda i: (i, 0, 0))
    before = pl.BlockSpec((1, BLOCK, HEAD_DIM), lambda i: (jnp.maximum(i * T - 1, 0), 0, 0))
    return pl.pallas_call(
        body, name=name, grid=(NB // T,),
        in_specs=[tile, tile, tile, before, before], out_specs=[tile, tile],
        out_shape=[jax.ShapeDtypeStruct(q.shape, BF16), jax.ShapeDtypeStruct(q.shape, F32)],
        compiler_params=_params(("parallel",)),
    )(q, k, v, k, v)


def _sw_bwd(q, k, v, do, lse, tt, name):
    NB = q.shape[0]
    T = SW_BLOCKS_PER_STEP
    per_group = NB // len(DILATIONS)
    nt = (((1,), (1,)), ((), ()))
    tn = (((0,), (0,)), ((), ()))

    def body(q_ref, k_ref, v_ref, do_ref, lse_ref, tt_ref, kp_ref, vp_ref, qn_ref, don_ref, lsen_ref,
             ttn_ref, dq_ref, dk_ref, dv_ref):
        b0 = pl.program_id(0) * T
        nseq = _seq_blocks(b0, per_group)
        cur_mask, prev_mask = _sw_masks()

        def probs(qq, kk, lse_b, mask, gate):
            s = lax.dot_general(qq, kk, nt, preferred_element_type=F32) * SCALE
            return jnp.exp(jnp.where(mask, s, NEG) + gate - lse_b)

        for t in range(T):
            has_prev = jnp.where(((b0 + t) & (nseq - 1)) != 0, 0.0, NEG)
            has_next = jnp.where(((b0 + t + 1) & (nseq - 1)) != 0, 0.0, NEG)
            last = t == T - 1
            qt, kt, vt, dot = q_ref[t], k_ref[t], v_ref[t], do_ref[t]
            kp = kp_ref[0] if t == 0 else k_ref[t - 1]
            vp = vp_ref[0] if t == 0 else v_ref[t - 1]
            qn = qn_ref[0] if last else q_ref[t + 1]
            don = don_ref[0] if last else do_ref[t + 1]
            lsen = lsen_ref[0] if last else lse_ref[t + 1]
            ttn = ttn_ref[0] if last else tt_ref[t + 1]
            p_cc = probs(qt, kt, lse_ref[t], cur_mask, 0.0)
            p_cp = probs(qt, kp, lse_ref[t], prev_mask, has_prev)
            p_nc = probs(qn, kt, lsen, prev_mask, has_next)
            ds_cc = p_cc * (lax.dot_general(dot, vt, nt, preferred_element_type=F32) + tt_ref[t])
            ds_cp = p_cp * (lax.dot_general(dot, vp, nt, preferred_element_type=F32) + tt_ref[t])
            ds_nc = p_nc * (lax.dot_general(don, vt, nt, preferred_element_type=F32) + ttn)
            ds_cc, ds_cp, ds_nc = ds_cc.astype(BF16), ds_cp.astype(BF16), ds_nc.astype(BF16)
            dq = (jnp.dot(ds_cc, kt, preferred_element_type=F32)
                  + jnp.dot(ds_cp, kp, preferred_element_type=F32))
            dk = (lax.dot_general(ds_cc, qt, tn, preferred_element_type=F32)
                  + lax.dot_general(ds_nc, qn, tn, preferred_element_type=F32))
            dv = (lax.dot_general(p_cc.astype(BF16), dot, tn, preferred_element_type=F32)
                  + lax.dot_general(p_nc.astype(BF16), don, tn, preferred_element_type=F32))
            dq_ref[t] = (dq * SCALE).astype(BF16)
            dk_ref[t] = (dk * SCALE).astype(BF16)
            dv_ref[t] = dv.astype(BF16)

    tile = pl.BlockSpec((T, BLOCK, HEAD_DIM), lambda i: (i, 0, 0))
    before = pl.BlockSpec((1, BLOCK, HEAD_DIM), lambda i: (jnp.maximum(i * T - 1, 0), 0, 0))
    after = pl.BlockSpec((1, BLOCK, HEAD_DIM), lambda i: (jnp.minimum(i * T + T, NB - 1), 0, 0))
    out = jax.ShapeDtypeStruct(q.shape, BF16)
    return pl.pallas_call(
        body, name=name, grid=(NB // T,),
        in_specs=[tile] * 6 + [before, before, after, after, after, after],
        out_specs=[tile] * 3, out_shape=[out] * 3,
        compiler_params=_params(("parallel",)),
    )(q, k, v, do, lse, tt, k, v, q, do, lse, tt)


def _group_softmax(lse):
    m = jnp.max(lse, axis=0, keepdims=True)
    e = jnp.exp(lse - m)
    return e / jnp.sum(e, axis=0, keepdims=True)


def _mix_fwd(o, lse, name):
    G, S, W = o.shape
    tr = min(256, S)
    blk = pl.BlockSpec((G, tr, W), lambda i: (0, i, 0))

    def fn(ov, lv):
        return (jnp.sum(_group_softmax(lv) * ov.astype(F32), axis=0),)

    return _ew(fn, [(o, blk), (lse, blk)], [((S, W), BF16, _rows(tr, W), False)], (S // tr,), name)


def _mix_bwd(dya, ya, o, lse, name):
    G, S, W = o.shape
    tr = min(256, S)
    blk = pl.BlockSpec((G, tr, HEAD_DIM), lambda i, h: (0, i, h))
    row = pl.BlockSpec((tr, HEAD_DIM), lambda i, h: (i, h))

    def fn(dy, yv, ov, lv):
        w = _group_softmax(lv)
        dyf = dy.astype(F32)
        inner = jnp.sum(dyf * yv.astype(F32), axis=-1, keepdims=True)
        return w * dyf[None], -w * inner[None]

    return _ew(fn, [(dya, row), (ya, row), (o, blk), (lse, blk)],
               [((G, S, W), BF16, blk, False), ((G, S, W), F32, blk, False)],
               (S // tr, HEADS_PER_DIL), name)


CUM_BLOCK = 256


def _split3(x):
    hi = x.astype(BF16)
    r = x - hi.astype(F32)
    mid = r.astype(BF16)
    lo = (r - mid.astype(F32)).astype(BF16)
    return hi, mid, lo


def _tri_matmul(tri, x):
    return sum(jnp.dot(tri, part, preferred_element_type=F32) for part in _split3(x))


def _log_sigmoid(x):
    return jnp.minimum(x, 0.0) - jnp.log(1.0 + jnp.exp(-jnp.abs(x)))


def _fox_prep(f, b, name):
    S = f.shape[0]
    tb = min(CUM_BLOCK, S)

    def body(f_ref, b_ref, c_ref, carry):
        @pl.when(pl.program_id(0) == 0)
        def _():
            carry[...] = jnp.zeros_like(carry)

        ls = _log_sigmoid(f_ref[...] + b_ref[...])
        r = lax.broadcasted_iota(jnp.int32, (tb, tb), 0)
        cidx = lax.broadcasted_iota(jnp.int32, (tb, tb), 1)
        tri = jnp.where(r >= cidx, 1.0, 0.0).astype(BF16)
        c_ref[...] = _tri_matmul(tri, ls) + carry[...]
        carry[...] += jnp.sum(ls, axis=0, keepdims=True)

    return pl.pallas_call(
        body, name=name, grid=(S // tb,),
        in_specs=[_rows(tb, LANES), _bcast(LANES)], out_specs=_rows(tb, LANES),
        out_shape=jax.ShapeDtypeStruct((S, LANES), F32),
        scratch_shapes=[pltpu.VMEM((1, LANES), F32)],
        compiler_params=_params(("arbitrary",)),
    )(f, b)


def _fox_prep_bwd(dc, f, b, name):
    S = f.shape[0]
    tb = min(CUM_BLOCK, S)
    nb = S // tb

    def body(dc_ref, f_ref, b_ref, df_ref, db_ref, carry):
        @pl.when(pl.program_id(0) == 0)
        def _():
            carry[...] = jnp.zeros_like(carry)
            db_ref[...] = jnp.zeros_like(db_ref)

        r = lax.broadcasted_iota(jnp.int32, (tb, tb), 0)
        cidx = lax.broadcasted_iota(jnp.int32, (tb, tb), 1)
        tri = jnp.where(r <= cidx, 1.0, 0.0).astype(BF16)
        dcv = dc_ref[...]
        dls = _tri_matmul(tri, dcv) + carry[...]
        carry[...] += jnp.sum(dcv, axis=0, keepdims=True)
        z = f_ref[...] + b_ref[...]
        df = dls * (1.0 / (1.0 + jnp.exp(z)))
        df_ref[...] = df
        db_ref[...] += jnp.sum(df, axis=0, keepdims=True)

    rev = pl.BlockSpec((tb, LANES), lambda i: (nb - 1 - i, 0))
    return pl.pallas_call(
        body, name=name, grid=(nb,),
        in_specs=[rev, rev, _bcast(LANES)], out_specs=[rev, _bcast(LANES)],
        out_shape=[jax.ShapeDtypeStruct((S, LANES), F32), jax.ShapeDtypeStruct((1, LANES), F32)],
        scratch_shapes=[pltpu.VMEM((1, LANES), F32)],
        compiler_params=_params(("arbitrary",)),
    )(dc, f, b)


FOX_Q_TILE = 256
_FOX_Q0 = N_DIL_HEADS
_FOX_K0 = N_HEADS + N_DIL_HEADS
_FOX_V0 = 2 * N_HEADS + N_DIL_HEADS


def _fox_scores(q, k, cq, ck, iq, tq, S):
    nt = (((1,), (1,)), ((), ()))
    s = lax.dot_general(q, k, nt, preferred_element_type=F32) * SCALE + cq - ck
    qpos = iq * tq + lax.broadcasted_iota(jnp.int32, (tq, S), 0)
    kpos = lax.broadcasted_iota(jnp.int32, (tq, S), 1)
    return jnp.where(kpos <= qpos, s, NEG)


def _fox_fwd(z, cq, ck, name):
    S = z.shape[0]
    tq = min(FOX_Q_TILE, S)

    def body(q_ref, k_ref, v_ref, cq_ref, ck_ref, o_ref, lse_ref):
        s = _fox_scores(q_ref[...], k_ref[...], cq_ref[...], ck_ref[...], pl.program_id(1), tq, S)
        m = jnp.max(s, axis=-1, keepdims=True)
        p = jnp.exp(s - m)
        l = jnp.sum(p, axis=-1, keepdims=True)
        o = jnp.dot(p.astype(BF16), v_ref[...], preferred_element_type=F32)
        o_ref[...] = (o / l).astype(o_ref.dtype)
        lse_ref[...] = m + jnp.log(l)

    qblk = lambda c0: pl.BlockSpec((tq, HEAD_DIM), lambda h, i: (i, c0 + h))
    full = lambda c0: pl.BlockSpec((S, HEAD_DIM), lambda h, i: (0, c0 + h))
    col = pl.BlockSpec((None, tq, 1), lambda h, i: (h, i, 0))
    rowv = pl.BlockSpec((None, 1, S), lambda h, i: (h, 0, 0))
    return pl.pallas_call(
        body, name=name, grid=(N_FOX_HEADS, S // tq),
        in_specs=[qblk(_FOX_Q0), full(_FOX_K0), full(_FOX_V0), col, rowv],
        out_specs=[qblk(0), col],
        out_shape=[jax.ShapeDtypeStruct((S, FOX_WIDTH), BF16),
                   jax.ShapeDtypeStruct((N_FOX_HEADS, S, 1), F32)],
        compiler_params=_params(("parallel", "parallel")),
    )(z, z, z, cq, ck)


def _fox_bwd(z, cq, ck, lse, yb, dyb, name):
    S = z.shape[0]
    tq = min(FOX_Q_TILE, S)
    nq = S // tq
    nt = (((1,), (1,)), ((), ()))
    tn = (((0,), (0,)), ((), ()))

    def body(q_ref, k_ref, v_ref, cq_ref, ck_ref, lse_ref, o_ref, do_ref,
             dq_ref, dk_ref, dv_ref, dc_ref, dk_acc, dv_acc):
        i = pl.program_id(1)

        @pl.when(i == 0)
        def _():
            dk_acc[...] = jnp.zeros_like(dk_acc)
            dv_acc[...] = jnp.zeros_like(dv_acc)
            dc_ref[...] = jnp.zeros_like(dc_ref)

        q, k, v, do = q_ref[...], k_ref[...], v_ref[...], do_ref[...]
        s = _fox_scores(q, k, cq_ref[...], ck_ref[...], i, tq, S)
        p = jnp.exp(s - lse_ref[...])
        dp = lax.dot_general(do, v, nt, preferred_element_type=F32)
        ds = p * (dp - jnp.sum(p * dp, axis=-1, keepdims=True))
        dsb = ds.astype(BF16)
        dq_ref[...] = (jnp.dot(dsb, k, preferred_element_type=F32) * SCALE).astype(BF16)
        dk_acc[...] += lax.dot_general(dsb, q, tn, preferred_element_type=F32) * SCALE
        dv_acc[...] += lax.dot_general(p.astype(BF16), do, tn, preferred_element_type=F32)
        dc_ref[...] -= jnp.sum(ds, axis=0, keepdims=True)

        @pl.when(i == nq - 1)
        def _():
            dk_ref[...] = dk_acc[...].astype(BF16)
            dv_ref[...] = dv_acc[...].astype(BF16)

    qblk = lambda c0: pl.BlockSpec((tq, HEAD_DIM), lambda h, i: (i, c0 + h))
    full = lambda c0: pl.BlockSpec((S, HEAD_DIM), lambda h, i: (0, c0 + h))
    col = pl.BlockSpec((None, tq, 1), lambda h, i: (h, i, 0))
    rowv = pl.BlockSpec((None, 1, S), lambda h, i: (h, 0, 0))
    wide = jax.ShapeDtypeStruct((S, FOX_WIDTH), BF16)
    return pl.pallas_call(
        body, name=name, grid=(N_FOX_HEADS, nq),
        in_specs=[qblk(_FOX_Q0), full(_FOX_K0), full(_FOX_V0), col, rowv, col, qblk(0), qblk(0)],
        out_specs=[qblk(0), full(0), full(0), rowv],
        out_shape=[wide, wide, wide, jax.ShapeDtypeStruct((N_FOX_HEADS, 1, S), F32)],
        scratch_shapes=[pltpu.VMEM((S, HEAD_DIM), F32), pltpu.VMEM((S, HEAD_DIM), F32)],
        compiler_params=_params(("parallel", "arbitrary")),
    )(z, z, z, cq, ck, lse, yb, dyb)


def _sigmoid(x):
    return 1.0 / (1.0 + jnp.exp(-x))


def _merge_fwd(gates, a, bm, name):
    S, D = a.shape
    tr = _row_tile(S, D * 4)
    g1 = pl.BlockSpec((tr, D), lambda i: (i, 0))
    g2 = pl.BlockSpec((tr, D), lambda i: (i, 1))

    def fn(x1, x2, av, bv):
        return (x1.astype(F32) * av.astype(F32) + x2.astype(F32) * bv.astype(F32),)

    return _ew(fn, [(gates, g1), (gates, g2), (a, _rows(tr, D)), (bm, _rows(tr, D))],
               [((S, D), BF16, _rows(tr, D), False)], (S // tr,), name)


def _merge_bwd(dmerged, gates, a, bm, name):
    S, D = a.shape
    tr = _row_tile(S, D * 8)
    g1 = pl.BlockSpec((tr, D), lambda i: (i, 0))
    g2 = pl.BlockSpec((tr, D), lambda i: (i, 1))

    def fn(dm, x1, x2, av, bv):
        dm, x1, x2 = dm.astype(F32), x1.astype(F32), x2.astype(F32)
        dg1 = dm * av.astype(F32) * x1 * (1.0 - x1)
        dg2 = dm * bv.astype(F32) * x2 * (1.0 - x2)
        dgp = jnp.concatenate([dg1, dg2], axis=1)
        return dm * x1, dm * x2, dgp, jnp.sum(dgp, axis=0, keepdims=True)

    return _ew(fn, [(dmerged, _rows(tr, D)), (gates, g1), (gates, g2), (a, _rows(tr, D)), (bm, _rows(tr, D))],
               [((S, D), BF16, _rows(tr, D), False), ((S, D), BF16, _rows(tr, D), False),
                ((S, 2 * D), BF16, _rows(tr, 2 * D), False), ((1, 2 * D), F32, _bcast(2 * D), True)],
               (S // tr,), name)


def _ple_bwd(dh, pg, pe, name):
    S, D = dh.shape
    tr = _row_tile(S, D * 4)

    def fn(d, g, e):
        g, e = g.astype(F32), e.astype(F32)
        return d * g, d * e * g * (1.0 - g)

    spec = _rows(tr, D)
    return _ew(fn, [(dh, spec), (pg, spec), (pe, spec)],
               [((S, D), BF16, spec, False), ((S, D), BF16, spec, False)], (S // tr,), name)


def _position():
    x, y, c = lax.axis_index("x"), lax.axis_index("y"), lax.axis_index("c")
    chips = [(1 - x, y), (x, 1 - y), (1 - x, 1 - y)]
    return x, y, c, chips


def _remote(src, dst, send_sem, recv_sem, target):
    return pltpu.make_async_remote_copy(src_ref=src, dst_ref=dst, send_sem=send_sem, recv_sem=recv_sem,
                                        device_id=target, device_id_type=MESH)


HBM = pl.BlockSpec(memory_space=pltpu.HBM)
SEM = pl.BlockSpec(memory_space=pltpu.SEMAPHORE)
EFFECT = pltpu.SideEffectType.DATAFLOW_SIDE_EFFECTING


def _copies_start(plan, arrays, sem_shape, after, name):
    n = len(arrays)

    def body(*refs):
        send_sems, recv_sems, token = refs[n + 1], refs[n + 2], refs[-1]
        for send, _ in plan(refs[:n], send_sems, recv_sems):
            send.start()
        token[...] = jnp.zeros_like(token)

    outs = pl.pallas_call(
        body, name=name,
        out_shape=(pltpu.SemaphoreType.DMA(sem_shape), pltpu.SemaphoreType.DMA(sem_shape),
                   *[pltpu.HBM(a.shape, a.dtype) for a in arrays], jax.ShapeDtypeStruct((8, LANES), F32)),
        in_specs=[HBM] * n + [ANY],
        out_specs=(SEM, SEM, *[HBM] * n, pl.BlockSpec(memory_space=pltpu.VMEM)),
        input_output_aliases={a: 2 + a for a in range(n)},
        compiler_params=pltpu.CompilerParams(has_side_effects=EFFECT),
    )(*[pltpu.with_memory_space_constraint(a, pltpu.HBM) for a in arrays], after)
    return outs[0], outs[1], list(outs[2:2 + n]), outs[-1]


def _copies_wait(plan, send_sems, recv_sems, arrays, after, name):
    n = len(arrays)

    def body(*refs):
        for send, recv in plan(refs[:n], refs[n], refs[n + 1]):
            send.wait_send()
            recv.wait_recv()

    return list(pl.pallas_call(
        body, name=name,
        out_shape=[pltpu.HBM(a.shape, a.dtype) for a in arrays],
        in_specs=[HBM] * n + [SEM, SEM, ANY], out_specs=[HBM] * n,
        input_output_aliases={a: a for a in range(n)},
        compiler_params=pltpu.CompilerParams(has_side_effects=EFFECT),
    )(*arrays, send_sems, recv_sems, after))


def _gather_ici_plan(refs, send_sems, recv_sems):
    x, y, c, chips = _position()
    plan = []
    for a, ref in enumerate(refs):
        rh = ref.shape[1] // 2
        mine = ref.at[2 * x + y, pl.ds(c * rh, rh)]
        for j, (cx, cy) in enumerate(chips):
            landed = ref.at[2 * cx + cy, pl.ds(c * rh, rh)]
            plan.append((_remote(mine, mine, send_sems.at[3 * a + j], recv_sems.at[3 * a + j], (cx, cy, c)),
                         _remote(landed, landed, send_sems.at[3 * a + j], recv_sems.at[3 * a + j], (cx, cy, c))))
    return plan


def _gather_d2d_plan(refs, send_sems, recv_sems):
    x, y, c, chips = _position()
    sibling = (x, y, 1 - c)
    plan = []
    for a, ref in enumerate(refs):
        rh = ref.shape[1] // 2
        for j, (cx, cy) in enumerate(chips):
            landed = ref.at[2 * cx + cy, pl.ds(c * rh, rh)]
            theirs = ref.at[2 * cx + cy, pl.ds((1 - c) * rh, rh)]
            plan.append((_remote(landed, landed, send_sems.at[3 * a + j], recv_sems.at[3 * a + j], sibling),
                         _remote(theirs, theirs, send_sems.at[3 * a + j], recv_sems.at[3 * a + j], sibling)))
    return plan


def _scatter_ici_plan(refs, send_sems, recv_sems):
    x, y, c, chips = _position()
    n = len(refs) // 2
    plan = []
    for a in range(n):
        for j, (cx, cy) in enumerate(chips):
            cp = _remote(refs[a].at[2 * cx + cy], refs[n + a].at[j], send_sems.at[3 * a + j], recv_sems.at[3 * a + j],
                         (cx, cy, c))
            plan.append((cp, cp))
    return plan


def _place_shard(w, layer, me, after, name):
    _, r, cc = w.shape
    tr = _row_tile(r, cc * 4)

    def body(me_ref, w_ref, after_ref, o_ref):
        o_ref[...] = w_ref[...].astype(o_ref.dtype)

    return pl.pallas_call(
        body, name=name,
        grid_spec=pltpu.PrefetchScalarGridSpec(
            num_scalar_prefetch=1, grid=(r // tr,),
            in_specs=[pl.BlockSpec((None, tr, cc), lambda i, me_ref: (layer, i, 0)), ANY],
            out_specs=pl.BlockSpec((None, tr, cc), lambda i, me_ref: (me_ref[0], i, 0))),
        out_shape=jax.ShapeDtypeStruct((N_CHIPS, r, cc), BF16),
        compiler_params=_params(("parallel",)),
    )(me, w, after)


def _pair_plan(refs, send_sems, recv_sems):
    x, y, c, _ = _position()
    n = len(refs) // 2
    plan = []
    for a in range(n):
        rh = refs[a].shape[1] // 2
        cp = _remote(refs[a].at[:, pl.ds((1 - c) * rh, rh)], refs[n + a], send_sems.at[a], recv_sems.at[a],
                     (x, y, 1 - c))
        plan.append((cp, cp))
    return plan


def _pair_sum(mine, theirs, c, name):
    nch, rh, cc = theirs.shape
    tr = _row_tile(rh, cc * 4)
    nb = rh // tr

    def body(c_ref, m_ref, t_ref, o_ref):
        o_ref[...] = (m_ref[...].astype(F32) + t_ref[...].astype(F32)).astype(o_ref.dtype)

    return pl.pallas_call(
        body, name=name,
        grid_spec=pltpu.PrefetchScalarGridSpec(
            num_scalar_prefetch=1, grid=(nch, nb),
            in_specs=[pl.BlockSpec((1, tr, cc), lambda k, i, c_ref: (k, c_ref[0] * nb + i, 0)),
                      pl.BlockSpec((1, tr, cc), lambda k, i, c_ref: (k, i, 0))],
            out_specs=pl.BlockSpec((1, tr, cc), lambda k, i, c_ref: (k, i, 0))),
        out_shape=jax.ShapeDtypeStruct(theirs.shape, BF16),
        compiler_params=_params(("parallel", "parallel")),
    )(c, mine, theirs)


def _chip_sum(own, others, me, c, total, layer, depth, name):
    _, rh, cc = own.shape
    tr = _row_tile(rh, cc * 4)
    nb = rh // tr
    chained = total is not None

    def body(me_ref, c_ref, o_ref, r_ref, *rest):
        g_ref = rest[-1]
        g_ref[...] = (o_ref[0].astype(F32) + r_ref[0].astype(F32)) + (r_ref[1].astype(F32) + r_ref[2].astype(F32))

    in_specs = [pl.BlockSpec((1, tr, cc), lambda i, me_ref, c_ref: (me_ref[0], i, 0)),
                pl.BlockSpec((3, tr, cc), lambda i, me_ref, c_ref: (0, i, 0))]
    args = [me, c, own, others]
    if chained:
        in_specs.append(ANY)
        args.append(total)
    return pl.pallas_call(
        body, name=name,
        grid_spec=pltpu.PrefetchScalarGridSpec(
            num_scalar_prefetch=2, grid=(nb,), in_specs=in_specs,
            out_specs=pl.BlockSpec((None, tr, cc), lambda i, me_ref, c_ref: (layer, c_ref[0] * nb + i, 0))),
        out_shape=jax.ShapeDtypeStruct((depth, 2 * rh, cc), F32),
        input_output_aliases={4: 0} if chained else {},
        compiler_params=_params(("parallel",)),
    )(*args)


def _half_exchange(totals, layer, name):
    n = len(totals)

    def body(*refs):
        outs = refs[n:2 * n]
        send_sems, recv_sems = refs[2 * n:]
        x, y, c, _ = _position()
        copies = []
        for a in range(n):
            rh = outs[a].shape[1] // 2
            mine = outs[a].at[layer, pl.ds(c * rh, rh)]
            cp = _remote(mine, mine, send_sems.at[a], recv_sems.at[a], (x, y, 1 - c))
            cp.start()
            copies.append(cp)
        for a, cp in enumerate(copies):
            rh = outs[a].shape[1] // 2
            theirs = outs[a].at[layer, pl.ds((1 - c) * rh, rh)]
            cp.wait_send()
            _remote(theirs, theirs, send_sems.at[a], recv_sems.at[a], (x, y, 1 - c)).wait_recv()

    return pl.pallas_call(
        body, name=name, in_specs=[ANY] * n, out_specs=[ANY] * n,
        out_shape=[jax.ShapeDtypeStruct(t.shape, t.dtype) for t in totals],
        input_output_aliases={a: a for a in range(n)},
        scratch_shapes=[pltpu.SemaphoreType.DMA((n,)), pltpu.SemaphoreType.DMA((n,))],
    )(*totals)


def _allgather_devices(v, name):
    m_per, n = v.shape

    def body(x_ref, out_ref, send_sems, recv_sems, local_sem):
        x, y, c, chips = _position()
        me, sibling = (x, y, c), (x, y, 1 - c)

        def rows(px, py, pc):
            return out_ref.at[pl.ds((4 * px + 2 * py + pc) * m_per, m_per), :]

        def copy(k, block, to, src=None):
            return _remote(rows(*block) if src is None else src, rows(*block), send_sems.at[k], recv_sems.at[k], to)

        mine = pltpu.make_async_copy(x_ref, rows(*me), local_sem)
        mine.start()
        first = [copy(0, me, sibling, src=x_ref)]
        first += [copy(1 + j, me, (*chip, c), src=x_ref) for j, chip in enumerate(chips)]
        for cp in first:
            cp.start()
        passed = [copy(4 + j, (*chip, c), sibling) for j, chip in enumerate(chips)]
        for j, chip in enumerate(chips):
            copy(1 + j, (*chip, c), me).wait_recv()
            passed[j].start()
        copy(0, sibling, me).wait_recv()
        for j, chip in enumerate(chips):
            copy(4 + j, (*chip, 1 - c), me).wait_recv()
        for cp in first + passed:
            cp.wait_send()
        mine.wait()

    vm = pl.BlockSpec(memory_space=pltpu.VMEM)
    return pl.pallas_call(
        body, name=name, in_specs=[vm], out_specs=vm,
        out_shape=jax.ShapeDtypeStruct((8 * m_per, n), v.dtype),
        scratch_shapes=[pltpu.SemaphoreType.DMA((7,)), pltpu.SemaphoreType.DMA((7,)), pltpu.SemaphoreType.DMA],
    )(v)


def _adamw_math(w, g, m, v):
    m = ADAM_B1 * m + (1.0 - ADAM_B1) * g
    v = ADAM_B2 * v + (1.0 - ADAM_B2) * (g * g)
    m_hat = m / (1.0 - ADAM_B1 ** ADAM_STEP)
    v_hat = v / (1.0 - ADAM_B2 ** ADAM_STEP)
    delta = -ADAM_LR * (m_hat / (jnp.sqrt(v_hat) + ADAM_EPS) + ADAM_WD * w)
    return delta, m, v


def _adamw(w, g, m, v, lo, hi, prev, after, name):
    depth, r, cc = w.shape
    tr = _row_tile(r, cc * 4 * 2)
    spec = pl.BlockSpec((1, tr, cc), lambda l, i: (lo + l, i, 0))

    def body(w_ref, g_ref, m_ref, v_ref, *rest):
        outs = rest[-4:]
        gv = g_ref[...]
        for o_ref, val in zip(outs, (gv,) + _adamw_math(w_ref[...], gv, m_ref[...], v_ref[...])):
            o_ref[...] = val

    prev = list(prev) if prev is not None else []
    return tuple(pl.pallas_call(
        body, name=name, grid=(hi - lo, r // tr),
        in_specs=[spec] * 4 + [ANY] * (1 + len(prev)), out_specs=[spec] * 4,
        out_shape=[jax.ShapeDtypeStruct(w.shape, F32)] * 4,
        input_output_aliases={5 + k: k for k in range(len(prev))},
        compiler_params=_params(("parallel", "parallel")),
    )(w, g, m, v, after, *prev))


def _adamw_small(w, parts, m, v, name):
    M = w.shape[0]

    def body(w_ref, p_ref, m_ref, v_ref, g_ref, d_ref, nm_ref, nv_ref):
        g = p_ref[pl.ds(0, M), :]
        for k in range(1, 8):
            g = g + p_ref[pl.ds(k * M, M), :]
        d, nm, nv = _adamw_math(w_ref[...], g, m_ref[...], v_ref[...])
        g_ref[...] = g
        d_ref[...] = d
        nm_ref[...] = nm
        nv_ref[...] = nv

    vm = pl.BlockSpec(memory_space=pltpu.VMEM)
    return pl.pallas_call(
        body, name=name, in_specs=[vm] * 4, out_specs=[vm] * 4,
        out_shape=[jax.ShapeDtypeStruct(w.shape, F32)] * 4,
    )(w, parts, m, v)


def _layer_fwd(h0, p_l, W, small, tabs, after_mlp):
    S, D = h0.shape
    ctab, stab = tabs
    u = _rms_fwd(h0, small["g_mix"], "rms_mix")
    z = _mm(u, W["w_qkv"], mode="nn", name="mm_qkv")
    f = _mm(u, W["w_f"], mode="nn", name="mm_f", out_dtypes=(F32,))
    gates = _mm(u, W["w_gate"], mode="nn", name="mm_gate", b_chunked=True, extras=[(small["b_gate"], "row")],
                epilogue=lambda acc, b: (_sigmoid(acc + b),))
    qr, kr = _rope(z, 0, z, N_HEADS, ctab, stab, 1.0, "rope_fwd")
    qs, ks, vs = _to_strided(qr), _to_strided(kr), _to_strided(z[:, 2 * ATTN_WIDTH:2 * ATTN_WIDTH + DIL_WIDTH])
    o_s, lse_s = _sw_fwd(qs, ks, vs, "sw_fwd")
    o_g, lse_g = _from_strided(o_s, S), _from_strided(lse_s, S)
    ya = _mix_fwd(o_g, lse_g, "mix_fwd")
    a = _mm(ya, W["w_br_a"], mode="nn", name="mm_br_a", b_chunked=True)
    cum = _fox_prep(f, small["b_f"], "fox_prep")
    cq = cum[:, :N_FOX_HEADS].T[:, :, None]
    ck = cum[:, :N_FOX_HEADS].T[:, None, :]
    yb, lse_f = _fox_fwd(z, cq, ck, "fox_fwd")
    bm = _mm(yb, W["w_br_b"], mode="nn", name="mm_br_b", b_chunked=True)
    merged = _merge_fwd(gates, a, bm, "merge_fwd")
    h1 = _mm(merged, W["w_o"], mode="nn", name="mm_o", out_dtypes=(F32,), extras=[(h0, "tile")],
             epilogue=lambda acc, r: (acc + r,))
    m = _rms_fwd(h1, small["g_mlp"], "rms_mlp")
    ra, act = _mm(m, W["w_up"], mode="nn", name="mm_up", b_chunked=True, out_dtypes=(BF16, BF16),
                  epilogue=lambda acc: (jnp.maximum(acc, 0.0), jnp.square(jnp.maximum(acc, 0.0))))
    h2 = _mm(act, W["w_down"], mode="nn", name="mm_down", out_dtypes=(F32,), extras=[(h1, "tile")],
             epilogue=lambda acc, r: (acc + r,), ti=1024, tc=4096)
    token = after_mlp(h2)
    n = _rms_fwd(h2, small["g_ple"] if token is None else small["g_ple"] + token[0, 0], "rms_ple")
    pg = _mm(n, W["w_ple_gate"], mode="nn", name="mm_ple_gate", epilogue=lambda acc: (_sigmoid(acc),))
    h3, pe = _mm(p_l, W["w_ple"], mode="nn", name="mm_ple", b_chunked=True, out_dtypes=(F32, BF16),
                 extras=[(h2, "tile"), (pg, "tile")], tj=256,
                 epilogue=lambda acc, r, g: (r + g.astype(F32) * acc, acc))
    saved = dict(h0=h0, u=u, z=z, f=f, gates=gates, qs=qs, ks=ks, vs=vs, lse_s=lse_s, o_g=o_g, lse_g=lse_g,
                 ya=ya, a=a, cq=cq, ck=ck, yb=yb, lse_f=lse_f, bm=bm, merged=merged, h1=h1, m=m, ra=ra,
                 act=act, h2=h2, n=n, pg=pg, pe=pe, p_l=p_l)
    return h3, saved


def _after(hooks, name, small_value, *args):
    token = hooks[name](*args) if name in hooks else None
    return small_value if token is None else small_value + token[0, 0]


def _layer_bwd(dh3, sv, W, small, tabs, hooks):
    S, D = dh3.shape
    ctab, stab = tabs
    gw, gs = {}, {}
    tn = functools.partial(_mm, mode="tn", ti=512, tj=1024)
    dpe, dpg = _ple_bwd(dh3, sv["pg"], sv["pe"], "ple_bwd")
    gw["w_ple"] = tn(sv["p_l"], dpe, name="dw_ple", out_chunks=N_CHIPS)
    gw["w_ple_gate"] = tn(sv["n"], dpg, name="dw_ple_gate").reshape(N_CHIPS, D // N_CHIPS, D)
    dn = _mm(dpg, W["w_ple_gate"], mode="nt", name="mm_dn")
    dh2, dh2b, gs["g_ple"] = _rms_bwd(sv["h2"], small["g_ple"], dn, dh3, "rms_ple_bwd")
    da = _mm(dh2b, W["w_down"], mode="nt", name="mm_dact", extras=[(sv["ra"], "tile")],
             epilogue=lambda acc, r: (acc * (2.0 * r.astype(F32)),))
    FF = da.shape[1]
    g_mlp = _after(hooks, "mlp_grad", small["g_mlp"], da)
    gw["w_down"] = tn(sv["act"], dh2b, name="dw_down").reshape(N_CHIPS, FF // N_CHIPS, D)
    gw["w_up"] = tn(sv["m"], da, name="dw_up", out_chunks=N_CHIPS)
    g_mlp = _after(hooks, "mlp_weights", g_mlp, dict(gw))
    dm = _mm(da, W["w_up"], mode="nt", name="mm_dm", b_chunked=True)
    dh1, dh1b, gs["g_mlp"] = _rms_bwd(sv["h1"], g_mlp, dm, dh2, "rms_mlp_bwd")
    dmerged = _mm(dh1b, W["w_o"], mode="nt", name="mm_dmerged")
    b_f = _after(hooks, "merge_grad", small["b_f"], dmerged)
    gw["w_o"] = tn(sv["merged"], dh1b, name="dw_o").reshape(N_CHIPS, D // N_CHIPS, D)
    d_a, d_b, dgp, gs["b_gate"] = _merge_bwd(dmerged, sv["gates"], sv["a"], sv["bm"], "merge_bwd")
    gw["w_gate"] = tn(sv["u"], dgp, name="dw_gate", out_chunks=N_CHIPS)
    gw["w_br_a"] = tn(sv["ya"], d_a, name="dw_br_a", out_chunks=N_CHIPS, tj=512)
    gw["w_br_b"] = tn(sv["yb"], d_b, name="dw_br_b", out_chunks=N_CHIPS, tj=512)
    dya = _mm(d_a, W["w_br_a"], mode="nt", name="mm_dya", b_chunked=True)
    dyb = _mm(d_b, W["w_br_b"], mode="nt", name="mm_dyb", b_chunked=True)
    z = sv["z"]
    dq_f, dk_f, dv_f, dck = _fox_bwd(z, sv["cq"], sv["ck"], sv["lse_f"], sv["yb"], dyb, "fox_bwd")
    dc = jnp.pad(dck[:, 0, :].T, ((0, 0), (0, LANES - N_FOX_HEADS)))
    df, dbf = _fox_prep_bwd(dc, sv["f"], b_f, "fox_prep_bwd")
    gs["b_f"] = dbf[:, :N_FOX_HEADS]
    lane = jnp.arange(LANES)[None, :] < N_FOX_HEADS
    dzf = jnp.where(lane, df, 0.0).astype(BF16)
    do_g, tt_g = _mix_bwd(dya, sv["ya"], sv["o_g"], sv["lse_g"], "mix_bwd")
    do_s = _to_strided(do_g.transpose(1, 0, 2).reshape(S, DIL_WIDTH))
    tt_s = _to_strided(tt_g.transpose(1, 0, 2).reshape(S, DIL_WIDTH))
    dq_s, dk_s, dv_s = _sw_bwd(sv["qs"], sv["ks"], sv["vs"], do_s, sv["lse_s"], tt_s, "sw_bwd")
    unstride = lambda t: _from_strided(t, S).transpose(1, 0, 2).reshape(S, DIL_WIDTH)
    dq_a, dk_a = _rope(unstride(dq_s), 0, unstride(dk_s), 0, ctab, stab, -1.0, "rope_bwd")
    dz = jnp.concatenate([dq_a, dq_f, dk_a, dk_f, unstride(dv_s), dv_f], axis=1)
    g_qkv = tn(sv["u"], dz, name="dw_qkv")
    g_f = tn(sv["u"], dzf, name="dw_f", tj=128)
    cols = W["w_in_cols"]
    g_in = jnp.concatenate([g_qkv, g_f[:, :N_FOX_HEADS]], axis=1)
    gw["w_in"] = jnp.stack([g_in[:, k * cols:(k + 1) * cols] for k in range(N_CHIPS)])
    du = _mm(dzf, W["w_f"], mode="nt", name="mm_du_f", out_dtypes=(F32,))
    du = _mm(dgp, W["w_gate"], mode="nt", name="mm_du_gate", b_chunked=True, out_dtypes=(F32,),
             extras=[(du, "tile")], epilogue=lambda acc, r: (acc + r,))
    du = _mm(dz, W["w_qkv"], mode="nt", name="mm_du_qkv", extras=[(du, "tile")],
             epilogue=lambda acc, r: (acc + r,))
    dh0, _, gs["g_mix"] = _rms_bwd(sv["h0"], small["g_mix"], du, dh1, "rms_mix_bwd")
    return dh0, gw, gs


BIG = ("w_in", "w_gate", "w_br_a", "w_br_b", "w_o", "w_up", "w_down", "w_ple", "w_ple_gate")
EARLY = ("w_ple", "w_ple_gate", "w_down", "w_up")
SMALL = ("g_mix", "b_f", "b_gate", "g_mlp", "g_ple", "g_final")
ORDER = ("g_mix", "w_in", "b_f", "w_gate", "b_gate", "w_br_a", "w_br_b", "w_o", "g_mlp", "w_up", "w_down",
         "g_ple", "w_ple", "w_ple_gate", "g_final")


def _gathered_layer_weights(full, D):
    W = {}
    for name in ("w_gate", "w_br_a", "w_br_b", "w_up", "w_ple"):
        W[name] = full[name]
    for name in ("w_o", "w_down", "w_ple_gate"):
        t = full[name]
        W[name] = t.reshape(t.shape[0] * t.shape[1], t.shape[2])
    w_in = full["w_in"]
    cols = w_in.shape[2]
    w_in = jnp.concatenate([w_in[k] for k in range(N_CHIPS)], axis=1)
    W["w_qkv"] = w_in[:, :3 * ATTN_WIDTH]
    W["w_f"] = jnp.pad(w_in[:, 3 * ATTN_WIDTH:], ((0, 0), (0, LANES - N_FOX_HEADS)))
    W["w_in_cols"] = cols
    return W


def _pack_rows(vals):
    flat = jnp.concatenate([v.reshape(-1) for v in vals])
    rows = -(-flat.shape[0] // (8 * LANES)) * 8
    return jnp.pad(flat, (0, rows * LANES - flat.shape[0])).reshape(rows, LANES)


def _unpack_rows(packed, shapes):
    flat = packed.reshape(-1)
    out, pos = [], 0
    for s in shapes:
        size = 1
        for dim in s:
            size *= dim
        out.append(flat[pos:pos + size].reshape(s))
        pos += size
    return out


def _local_step(x, p, small_w, comm, loss_target):
    depth = p.shape[0]
    S, D = x.shape
    tabs = _rope_tables(S)
    h = x
    saved, weights, smalls = [], [], []
    state, _ = comm["gather_start"](0, x)
    full = comm["gather_finish"](comm["gather_mid"](state, x)[0], x)
    for l in range(depth):
        nxt = [None]
        g_mix = small_w["g_mix"][l][None]
        if l + 1 < depth:
            nxt[0], token = comm["gather_start"](l + 1, full["w_ple"])
            g_mix = g_mix + token[0, 0]

        def after_mlp(h2):
            if nxt[0] is None:
                return None
            nxt[0], token = comm["gather_mid"](nxt[0], h2)
            return token

        W = _gathered_layer_weights(full, D)
        sm = dict(g_mix=g_mix, g_mlp=small_w["g_mlp"][l][None],
                  g_ple=small_w["g_ple"][l][None], b_gate=small_w["b_gate"][l][None],
                  b_f=jnp.pad(small_w["b_f"][l][None], ((0, 0), (0, LANES - N_FOX_HEADS))))
        h, sv = _layer_fwd(h, p[l].astype(BF16), W, sm, tabs, after_mlp)
        if nxt[0] is not None:
            full = comm["gather_finish"](nxt[0], h)
        saved.append(sv)
        weights.append(W)
        smalls.append(sm)
    dh, loss_row, dg_final = _loss_head(h, small_w["g_final"][None], loss_target, "loss_head")
    gss = [None] * depth
    pending, token = [None], None
    for l in reversed(range(depth)):
        sm = smalls[l]
        if token is not None:
            sm = {**sm, "g_ple": sm["g_ple"] + token[0, 0]}

        def after_mlp_grad(da):
            if pending[0] is None:
                return None
            pending[0] = comm["reduce_mid"](pending[0], da)
            return pending[0][-1]

        hooks = dict(mlp_grad=after_mlp_grad)
        if l == 0 and "reduce_early_begin" in comm:
            hooks.update(mlp_weights=comm["reduce_early_begin"], merge_grad=comm["reduce_early_mid"])
        dh, gw, gss[l] = _layer_bwd(dh, saved[l], weights[l], sm, tabs, hooks)
        if pending[0] is not None:
            comm["reduce_end"](pending[0], dh)
        if l > 0:
            pending[0], token = comm["reduce_begin"](l, gw, dh)
    return loss_row, dh, gss, dg_final, gw


def _device_comm(w, m, v, depth, c_arr, me_arr):
    n = len(BIG)
    totals = {name: None for name in BIG}
    placed, results = [], {}

    def gather_start(l, after):
        if l == 0:
            placed.append([_place_shard(w[name], 0, me_arr, me_arr, "place_shard") for name in BIG])
        send, recv, bufs, token = _copies_start(_gather_ici_plan, placed[l], (3 * n,), after, f"gather_ici_start_{l}")
        if l == 0:
            placed.extend([_place_shard(w[name], k, me_arr, token, "place_shard") for name in BIG]
                          for k in range(1, depth))
        return (l, send, recv, bufs), token

    def gather_mid(state, after):
        l, send, recv, bufs = state
        if l == 0:
            after = placed[-1][-1]
        bufs = _copies_wait(_gather_ici_plan, send, recv, bufs, after, f"gather_ici_wait_{l}")
        send, recv, bufs, token = _copies_start(_gather_d2d_plan, bufs, (3 * n,), after, f"gather_d2d_start_{l}")
        return (l, send, recv, bufs), token

    def gather_finish(state, after):
        l, send, recv, bufs = state
        return dict(zip(BIG, _copies_wait(_gather_d2d_plan, send, recv, bufs, after, f"gather_d2d_wait_{l}")))

    def reduce_begin(l, gw, after, names=BIG, tag=""):
        grads = [gw[name] for name in names]
        landing = [lax.empty((g.shape[0], g.shape[1] // 2, g.shape[2]), g.dtype) for g in grads]
        send, recv, arrays, token = _copies_start(_pair_plan, grads + landing, (len(names),), after,
                                                  f"pair_start_{l}{tag}")
        return (l, tag, names, send, recv, arrays), token

    def reduce_mid(state, after):
        l, tag, names, send, recv, arrays = state
        k = len(names)
        arrays = _copies_wait(_pair_plan, send, recv, arrays, after, f"pair_wait_{l}{tag}")
        sums = [_pair_sum(g, t, c_arr, "rs_pair_sum") for g, t in zip(arrays[:k], arrays[k:])]
        landing = [lax.empty((3,) + s.shape[1:], s.dtype) for s in sums]
        send, recv, arrays, token = _copies_start(_scatter_ici_plan, sums + landing, (3 * k,), sums[0],
                                                  f"scatter_ici_start_{l}{tag}")
        return l, tag, names, send, recv, arrays, token

    def reduce_end(state, after):
        l, tag, names, send, recv, arrays, _ = state
        k = len(names)
        arrays = _copies_wait(_scatter_ici_plan, send, recv, arrays, after, f"scatter_ici_wait_{l}{tag}")
        done = [_chip_sum(s, o, me_arr, c_arr, totals[name], l, depth, "rs_chip_sum")
                for name, s, o in zip(names, arrays[:k], arrays[k:])]
        totals.update(zip(names, _half_exchange(done, l, "rs_half_exchange")))

    early = [None]

    def reduce_early_begin(gw):
        early[0], token = reduce_begin(0, gw, gw[EARLY[-1]], EARLY, "a")
        return token

    def reduce_early_mid(after):
        early[0] = reduce_mid(early[0], after)
        return early[0][-1]

    def reduce_last(gw, after):
        upper = {}

        def adamw_upper(names, token):
            for name in names:
                if depth > 1:
                    upper[name] = _adamw(w[name], totals[name], m[name], v[name], 1, depth, None, token,
                                         "adamw_upper")
                    token = upper[name][0]
            return token

        def adamw_first(names, token):
            for name in names:
                results[name] = _adamw(w[name], totals[name], m[name], v[name], 0, 1, upper.get(name), token,
                                       "adamw_first")
                token = results[name][0]
            return token

        late = BIG if early[0] is None else tuple(name for name in BIG if name not in EARLY)
        state, token = reduce_begin(0, gw, after, late, "b")
        state = reduce_mid(state, adamw_upper(("w_up",), token))
        token = adamw_upper([name for name in BIG if name != "w_up"], state[-1])
        if early[0] is not None:
            reduce_end(early[0], token)
            token = adamw_first(EARLY, totals[EARLY[0]])
        reduce_end(state, token)
        adamw_first(late, totals[late[0]])

    comm = dict(gather_start=gather_start, gather_mid=gather_mid, gather_finish=gather_finish,
                reduce_begin=reduce_begin, reduce_mid=reduce_mid, reduce_end=reduce_end, reduce_last=reduce_last,
                reduce_early_begin=reduce_early_begin, reduce_early_mid=reduce_early_mid)
    return comm, results


def kernel(x, p, g_mix, w_in, b_f, w_gate, b_gate, w_br_a, w_br_b, w_o, g_mlp, w_up, w_down, g_ple, w_ple, w_ple_gate, g_final, loss_target, m_g_mix, m_w_in, m_b_f, m_w_gate, m_b_gate, m_w_br_a, m_w_br_b, m_w_o, m_g_mlp, m_w_up, m_w_down, m_g_ple, m_w_ple, m_w_ple_gate, m_g_final, v_g_mix, v_w_in, v_b_f, v_w_gate, v_b_gate, v_w_br_a, v_w_br_b, v_w_o, v_g_mlp, v_w_up, v_w_down, v_g_ple, v_w_ple, v_w_ple_gate, v_g_final):
    w = dict(g_mix=g_mix, w_in=w_in, b_f=b_f, w_gate=w_gate, b_gate=b_gate, w_br_a=w_br_a, w_br_b=w_br_b,
             w_o=w_o, g_mlp=g_mlp, w_up=w_up, w_down=w_down, g_ple=g_ple, w_ple=w_ple, w_ple_gate=w_ple_gate,
             g_final=g_final)
    m = dict(g_mix=m_g_mix, w_in=m_w_in, b_f=m_b_f, w_gate=m_w_gate, b_gate=m_b_gate, w_br_a=m_w_br_a,
             w_br_b=m_w_br_b, w_o=m_w_o, g_mlp=m_g_mlp, w_up=m_w_up, w_down=m_w_down, g_ple=m_g_ple,
             w_ple=m_w_ple, w_ple_gate=m_w_ple_gate, g_final=m_g_final)
    v = dict(g_mix=v_g_mix, w_in=v_w_in, b_f=v_b_f, w_gate=v_w_gate, b_gate=v_b_gate, w_br_a=v_w_br_a,
             w_br_b=v_w_br_b, w_o=v_w_o, g_mlp=v_g_mlp, w_up=v_w_up, w_down=v_w_down, g_ple=v_g_ple,
             w_ple=v_w_ple, w_ple_gate=v_w_ple_gate, g_final=v_g_final)
    depth = p.shape[0]
    cx, cy, cc = lax.axis_index("x"), lax.axis_index("y"), lax.axis_index("c")
    c_arr = jnp.reshape(cc, (1,)).astype(jnp.int32)
    me_arr = jnp.reshape(2 * cx + cy, (1,)).astype(jnp.int32)

    comm, big_out = _device_comm(w, m, v, depth, c_arr, me_arr)
    loss_row, grad_x, gss, dg_final, gw0 = _local_step(x[0], p[:, 0], w, comm, loss_target[0])

    small_grads = [jnp.stack([gss[l][n][0] for l in range(depth)]) for n in SMALL[:-1]] + [dg_final[0]]
    shapes = [w[n].shape for n in SMALL]
    parts = _allgather_devices(_pack_rows(small_grads), "allgather_small")
    packed = _adamw_small(_pack_rows([w[n] for n in SMALL]), parts, _pack_rows([m[n] for n in SMALL]),
                          _pack_rows([v[n] for n in SMALL]), "adamw_small")
    small_out = {n: vals for n, vals in zip(SMALL, zip(*[_unpack_rows(t, shapes) for t in packed]))}
    comm["reduce_last"](gw0, packed[0])

    loss = lax.psum(loss_row[0, 0], ("x", "y", "c"))
    out = {**big_out, **small_out}
    return (loss, grad_x[None], *[out[n][0] for n in ORDER], *[out[n][1] for n in ORDER],
            *[out[n][2] for n in ORDER], *[out[n][3] for n in ORDER])
```

```python
import functools

import jax
import jax.numpy as jnp
from jax import lax
from jax.experimental import pallas as pl
from jax.experimental.pallas import tpu as pltpu

F32 = jnp.float32
BF16 = jnp.bfloat16

HEAD_DIM = 128
N_HEADS = 16
N_DIL_HEADS = 12
N_FOX_HEADS = 4
HEADS_PER_DIL = 4
DILATIONS = (1, 4, 16)
BLOCK = 128
ATTN_WIDTH = N_HEADS * HEAD_DIM
DIL_WIDTH = N_DIL_HEADS * HEAD_DIM
FOX_WIDTH = N_FOX_HEADS * HEAD_DIM
ROPE_THETA = 500000.0
ROPE_HALF = 16
NORM_EPS = 1e-6
SCALE = HEAD_DIM ** -0.5
NEG = -1e30

ADAM_LR = 0.001
ADAM_B1 = 0.9
ADAM_B2 = 0.999
ADAM_EPS = 1e-08
ADAM_WD = 0.01
ADAM_STEP = 10

N_CHIPS = 4
V7X_VMEM_LIMIT_BYTES = 56 * 1024 * 1024
LANES = 128
MESH = pl.DeviceIdType.MESH
ANY = pl.BlockSpec(memory_space=pl.ANY)


def _params(sem):
    return pltpu.CompilerParams(dimension_semantics=sem, vmem_limit_bytes=V7X_VMEM_LIMIT_BYTES)


def _tile(n, pref):
    if n <= pref:
        return n
    t = (pref // LANES) * LANES
    while t > LANES and n % t:
        t -= LANES
    assert n % t == 0, (n, pref)
    return t


def _mm(a, b, *, mode, name, out_dtypes=(BF16,), epilogue=None, extras=(), b_chunked=False,
        out_chunks=0, ti=2048, tj=512, tc=2048):
    if mode == "tn":
        C, I = a.shape
    else:
        I, C = a.shape
    if b_chunked:
        nch, d0, n = b.shape
        if mode == "nn":
            assert d0 == C
            J = nch * n
        else:
            assert mode == "nt" and nch * n == C
            J = d0
    elif mode == "nt":
        J = b.shape[0]
        assert b.shape[1] == C
    else:
        assert b.shape[0] == C
        J = b.shape[1]
    ti, tc = _tile(I, ti), _tile(C, tc)
    if b_chunked and mode == "nn":
        tj = _tile(n, tj)
    elif out_chunks:
        tj = _tile(J // out_chunks, tj)
    else:
        tj = _tile(J, tj)
    if b_chunked and mode == "nt":
        tc = _tile(n, tc)
    ni, nj, nc = I // ti, J // tj, C // tc

    if mode == "tn":
        a_spec = pl.BlockSpec((tc, ti), lambda i, j, c: (c, i))
        dims = (((0,), (0,)), ((), ()))
    else:
        a_spec = pl.BlockSpec((ti, tc), lambda i, j, c: (i, c))
        dims = (((1,), (0,)), ((), ())) if mode == "nn" else (((1,), (1,)), ((), ()))
    if mode == "nt":
        if b_chunked:
            cb = n // tc
            b_spec = pl.BlockSpec((None, tj, tc), lambda i, j, c: (c // cb, j, c % cb))
        else:
            b_spec = pl.BlockSpec((tj, tc), lambda i, j, c: (j, c))
    else:
        if b_chunked:
            jb = n // tj
            b_spec = pl.BlockSpec((None, tc, tj), lambda i, j, c: (j // jb, c, j % jb))
        else:
            b_spec = pl.BlockSpec((tc, tj), lambda i, j, c: (c, j))
    extra_specs = []
    for arr, kind in extras:
        if kind == "tile":
            assert arr.shape == (I, J), (arr.shape, I, J)
            extra_specs.append(pl.BlockSpec((ti, tj), lambda i, j, c: (i, j)))
        else:
            assert arr.shape == (1, J)
            extra_specs.append(pl.BlockSpec((1, tj), lambda i, j, c: (0, j)))
    if out_chunks:
        ob = (J // out_chunks) // tj
        out_spec = pl.BlockSpec((None, ti, tj), lambda i, j, c: (j // ob, i, j % ob))
        out_shape = [jax.ShapeDtypeStruct((out_chunks, I, J // out_chunks), d) for d in out_dtypes]
    else:
        out_spec = pl.BlockSpec((ti, tj), lambda i, j, c: (i, j))
        out_shape = [jax.ShapeDtypeStruct((I, J), d) for d in out_dtypes]
    ne, no = len(extras), len(out_dtypes)
    if epilogue is None:
        epilogue = lambda acc: (acc,)

    def body(a_ref, b_ref, *rest):
        extra_refs, out_refs = rest[:ne], rest[ne:ne + no]

        def finish(acc):
            outs = epilogue(acc, *[r[...] for r in extra_refs])
            for o_ref, val in zip(out_refs, outs):
                o_ref[...] = val.astype(o_ref.dtype)

        part = lax.dot_general(a_ref[...], b_ref[...], dims, preferred_element_type=F32)
        if nc == 1:
            finish(part)
        else:
            acc_ref = rest[-1]
            k = pl.program_id(2)

            @pl.when(k == 0)
            def _():
                acc_ref[...] = part

            @pl.when(k > 0)
            def _():
                acc_ref[...] += part

            @pl.when(k == nc - 1)
            def _():
                finish(acc_ref[...])

    outs = pl.pallas_call(
        body, name=name, grid=(ni, nj, nc),
        in_specs=[a_spec, b_spec] + extra_specs,
        out_specs=[out_spec] * no, out_shape=out_shape,
        scratch_shapes=[pltpu.VMEM((ti, tj), F32)] if nc > 1 else [],
        compiler_params=_params(("parallel", "parallel", "arbitrary")),
    )(a, b, *[e[0] for e in extras])
    return outs[0] if no == 1 else tuple(outs)


def _ew(fn, ins, outs, grid, name):
    n_in = len(ins)
    has_acc = any(o[3] for o in outs)
    assert not has_acc or len(grid) == 1

    def body(*refs):
        vals = fn(*[r[...] for r in refs[:n_in]])
        for o_ref, o, val in zip(refs[n_in:], outs, vals):
            if o[3]:
                step = pl.program_id(0)

                @pl.when(step == 0)
                def _(o_ref=o_ref, val=val):
                    o_ref[...] = val

                @pl.when(step > 0)
                def _(o_ref=o_ref, val=val):
                    o_ref[...] += val
            else:
                o_ref[...] = val.astype(o_ref.dtype)

    sem = ("arbitrary",) if has_acc else ("parallel",) * len(grid)
    res = pl.pallas_call(
        body, name=name, grid=grid,
        in_specs=[i[1] for i in ins], out_specs=[o[2] for o in outs],
        out_shape=[jax.ShapeDtypeStruct(o[0], o[1]) for o in outs],
        compiler_params=_params(sem),
    )(*[i[0] for i in ins])
    return res[0] if len(outs) == 1 else tuple(res)


def _rows(tr, w):
    return pl.BlockSpec((tr, w), lambda i: (i, 0))


def _bcast(w):
    return pl.BlockSpec((1, w), lambda i: (0, 0))


def _row_tile(S, width_bytes):
    tr = 512
    while tr > 16 and tr * width_bytes > 2 * 1024 * 1024:
        tr //= 2
    return min(tr, S)


def _rms_fwd(h, g, name):
    S, D = h.shape
    tr = _row_tile(S, D * 4)

    def fn(x, gg):
        r = lax.rsqrt(jnp.mean(x * x, axis=-1, keepdims=True) + NORM_EPS)
        return (x * r * gg,)

    return _ew(fn, [(h, _rows(tr, D)), (g, _bcast(D))], [((S, D), BF16, _rows(tr, D), False)],
               (S // tr,), name)


def _rms_bwd(x, g, dy, dres, name):
    S, D = x.shape
    tr = _row_tile(S, D * 4)

    def fn(xv, gg, dyv, dr):
        r = lax.rsqrt(jnp.mean(xv * xv, axis=-1, keepdims=True) + NORM_EPS)
        dyf = dyv.astype(F32)
        gy = dyf * gg
        dx = r * gy - xv * (r * r * r) * jnp.mean(xv * gy, axis=-1, keepdims=True)
        tot = dr + dx
        dg = jnp.sum(dyf * xv * r, axis=0, keepdims=True)
        return tot, tot, dg

    return _ew(fn, [(x, _rows(tr, D)), (g, _bcast(D)), (dy, _rows(tr, D)), (dres, _rows(tr, D))],
               [((S, D), F32, _rows(tr, D), False), ((S, D), BF16, _rows(tr, D), False),
                ((1, D), F32, _bcast(D), True)], (S // tr,), name)


def _loss_head(h, g, target, name):
    S, D = h.shape
    tr = _row_tile(S, D * 4)

    def fn(xv, gg, tgt):
        r = lax.rsqrt(jnp.mean(xv * xv, axis=-1, keepdims=True) + NORM_EPS)
        y = xv * r * gg
        e = y - tgt
        loss = 0.5 * jnp.sum(jnp.mean(e * e, axis=-1, keepdims=True), axis=0, keepdims=True)
        dy = e * (1.0 / D)
        gy = dy * gg
        dx = r * gy - xv * (r * r * r) * jnp.mean(xv * gy, axis=-1, keepdims=True)
        dg = jnp.sum(dy * xv * r, axis=0, keepdims=True)
        return dx, jnp.broadcast_to(loss, (1, LANES)), dg

    return _ew(fn, [(h, _rows(tr, D)), (g, _bcast(D)), (target, _rows(tr, D))],
               [((S, D), F32, _rows(tr, D), False), ((1, LANES), F32, _bcast(LANES), True),
                ((1, D), F32, _bcast(D), True)], (S // tr,), name)


def _rope_tables(S):
    inv = ROPE_THETA ** (-jnp.arange(ROPE_HALF, dtype=F32) / ROPE_HALF)
    ang = jnp.arange(S, dtype=F32)[:, None] * inv[None, :]
    cos, sin = jnp.cos(ang), jnp.sin(ang)
    rest = HEAD_DIM - 2 * ROPE_HALF
    ctab = jnp.concatenate([cos, cos, jnp.ones((S, rest), F32)], axis=1)
    stab = jnp.concatenate([-sin, sin, jnp.zeros((S, rest), F32)], axis=1)
    return ctab, stab


def _swap_halves(x):
    lane = lax.broadcasted_iota(jnp.int32, x.shape, 1)
    return jnp.where(lane < ROPE_HALF, pltpu.roll(x, HEAD_DIM - ROPE_HALF, 1), pltpu.roll(x, ROPE_HALF, 1))


def _rope(q_src, q_col0, k_src, k_col0, ctab, stab, sign, name):
    S = q_src.shape[0]
    tr = min(512, S)
    width = HEADS_PER_DIL * HEAD_DIM

    def fn(q, k, ct, st):
        outs = []
        for v in (q, k):
            heads = []
            for h in range(HEADS_PER_DIL):
                vf = v[:, h * HEAD_DIM:(h + 1) * HEAD_DIM].astype(F32)
                heads.append(vf * ct + sign * _swap_halves(vf) * st)
            outs.append(jnp.concatenate(heads, axis=1))
        return tuple(outs)

    group = lambda c0: pl.BlockSpec((tr, width), lambda i, g: (i, c0 // HEADS_PER_DIL + g))
    tab = pl.BlockSpec((tr, HEAD_DIM), lambda i, g: (i, 0))
    out = ((S, DIL_WIDTH), BF16, group(0), False)
    return _ew(fn, [(q_src, group(q_col0)), (k_src, group(k_col0)), (ctab, tab), (stab, tab)],
               [out, out], (S // tr, len(DILATIONS)), name)


SW_BLOCKS_PER_STEP = 8


def _to_strided(x):
    S = x.shape[0]
    parts = []
    for g, d in enumerate(DILATIONS):
        xg = x[:, g * 512:(g + 1) * 512].reshape(S // d, d, HEADS_PER_DIL, HEAD_DIM)
        parts.append(xg.transpose(1, 2, 0, 3).reshape(-1, BLOCK, HEAD_DIM))
    return jnp.concatenate(parts, axis=0)


def _from_strided(y, S):
    per = y.shape[0] // len(DILATIONS)
    parts = []
    for g, d in enumerate(DILATIONS):
        yg = y[g * per:(g + 1) * per].reshape(d, HEADS_PER_DIL, S // d, HEAD_DIM)
        parts.append(yg.transpose(2, 0, 1, 3).reshape(S, HEADS_PER_DIL * HEAD_DIM))
    return jnp.stack(parts, axis=0)


def _seq_blocks(b0, per_group):
    g = b0 // per_group
    n0 = per_group // HEADS_PER_DIL
    return jnp.where(g == 0, n0, jnp.where(g == 1, n0 // 4, n0 // 16))


def _sw_masks():
    qi = lax.broadcasted_iota(jnp.int32, (BLOCK, BLOCK), 0)
    ki = lax.broadcasted_iota(jnp.int32, (BLOCK, BLOCK), 1)
    return qi >= ki, qi <= ki


def _sw_fwd(q, k, v, name):
    NB = q.shape[0]
    T = SW_BLOCKS_PER_STEP
    per_group = NB // len(DILATIONS)
    nt = (((1,), (1,)), ((), ()))

    def body(q_ref, k_ref, v_ref, kp_ref, vp_ref, o_ref, lse_ref):
        b0 = pl.program_id(0) * T
        nseq = _seq_blocks(b0, per_group)
        cur_mask, prev_mask = _sw_masks()
        for t in range(T):
            has_prev = ((b0 + t) & (nseq - 1)) != 0
            qt = q_ref[t]
            kp = kp_ref[0] if t == 0 else k_ref[t - 1]
            vp = vp_ref[0] if t == 0 else v_ref[t - 1]
            s_c = lax.dot_general(qt, k_ref[t], nt, preferred_element_type=F32) * SCALE
            s_p = lax.dot_general(qt, kp, nt, preferred_element_type=F32) * SCALE
            s_c = jnp.where(cur_mask, s_c, NEG)
            s_p = jnp.where(prev_mask, s_p, NEG) + jnp.where(has_prev, 0.0, NEG)
            m = jnp.maximum(jnp.max(s_c, axis=-1, keepdims=True), jnp.max(s_p, axis=-1, keepdims=True))
            p_c = jnp.exp(s_c - m)
            p_p = jnp.exp(s_p - m)
            l = jnp.sum(p_c, axis=-1, keepdims=True) + jnp.sum(p_p, axis=-1, keepdims=True)
            o = (jnp.dot(p_c.astype(BF16), v_ref[t], preferred_element_type=F32)
                 + jnp.dot(p_p.astype(BF16), vp, preferred_element_type=F32))
            o_ref[t] = (o / l).astype(o_ref.dtype)
            lse_ref[t] = jnp.broadcast_to(m + jnp.log(l), (BLOCK, HEAD_DIM))

    tile = pl.BlockSpec((T, BLOCK, HEAD_DIM), lambda i: (i, 0, 0))
    before = pl.BlockSpec((1, BLOCK, HEAD_DIM), lambda i: (jnp.maximum(i * T - 1, 0), 0, 0))
    return pl.pallas_call(
        body, name=name, grid=(NB // T,),
        in_specs=[tile, tile, tile, before, before], out_specs=[tile, tile],
        out_shape=[jax.ShapeDtypeStruct(q.shape, BF16), jax.ShapeDtypeStruct(q.shape, F32)],
        compiler_params=_params(("parallel",)),
    )(q, k, v, k, v)


def _sw_bwd(q, k, v, do, lse, tt, name):
    NB = q.shape[0]
    T = SW_BLOCKS_PER_STEP
    per_group = NB // len(DILATIONS)
    nt = (((1,), (1,)), ((), ()))
    tn = (((0,), (0,)), ((), ()))

    def body(q_ref, k_ref, v_ref, do_ref, lse_ref, tt_ref, kp_ref, vp_ref, qn_ref, don_ref, lsen_ref,
             ttn_ref, dq_ref, dk_ref, dv_ref):
        b0 = pl.program_id(0) * T
        nseq = _seq_blocks(b0, per_group)
        cur_mask, prev_mask = _sw_masks()

        def probs(qq, kk, lse_b, mask, gate):
            s = lax.dot_general(qq, kk, nt, preferred_element_type=F32) * SCALE
            return jnp.exp(jnp.where(mask, s, NEG) + gate - lse_b)

        for t in range(T):
            has_prev = jnp.where(((b0 + t) & (nseq - 1)) != 0, 0.0, NEG)
            has_next = jnp.where(((b0 + t + 1) & (nseq - 1)) != 0, 0.0, NEG)
            last = t == T - 1
            qt, kt, vt, dot = q_ref[t], k_ref[t], v_ref[t], do_ref[t]
            kp = kp_ref[0] if t == 0 else k_ref[t - 1]
            vp = vp_ref[0] if t == 0 else v_ref[t - 1]
            qn = qn_ref[0] if last else q_ref[t + 1]
            don = don_ref[0] if last else do_ref[t + 1]
            lsen = lsen_ref[0] if last else lse_ref[t + 1]
            ttn = ttn_ref[0] if last else tt_ref[t + 1]
            p_cc = probs(qt, kt, lse_ref[t], cur_mask, 0.0)
            p_cp = probs(qt, kp, lse_ref[t], prev_mask, has_prev)
            p_nc = probs(qn, kt, lsen, prev_mask, has_next)
            ds_cc = p_cc * (lax.dot_general(dot, vt, nt, preferred_element_type=F32) + tt_ref[t])
            ds_cp = p_cp * (lax.dot_general(dot, vp, nt, preferred_element_type=F32) + tt_ref[t])
            ds_nc = p_nc * (lax.dot_general(don, vt, nt, preferred_element_type=F32) + ttn)
            ds_cc, ds_cp, ds_nc = ds_cc.astype(BF16), ds_cp.astype(BF16), ds_nc.astype(BF16)
            dq = (jnp.dot(ds_cc, kt, preferred_element_type=F32)
                  + jnp.dot(ds_cp, kp, preferred_element_type=F32))
            dk = (lax.dot_general(ds_cc, qt, tn, preferred_element_type=F32)
                  + lax.dot_general(ds_nc, qn, tn, preferred_element_type=F32))
            dv = (lax.dot_general(p_cc.astype(BF16), dot, tn, preferred_element_type=F32)
                  + lax.dot_general(p_nc.astype(BF16), don, tn, preferred_element_type=F32))
            dq_ref[t] = (dq * SCALE).astype(BF16)
            dk_ref[t] = (dk * SCALE).astype(BF16)
            dv_ref[t] = dv.astype(BF16)

    tile = pl.BlockSpec((T, BLOCK, HEAD_DIM), lambda i: (i, 0, 0))
    before = pl.BlockSpec((1, BLOCK, HEAD_DIM), lambda i: (jnp.maximum(i * T - 1, 0), 0, 0))
    after = pl.BlockSpec((1, BLOCK, HEAD_DIM), lambda i: (jnp.minimum(i * T + T, NB - 1), 0, 0))
    out = jax.ShapeDtypeStruct(q.shape, BF16)
    return pl.pallas_call(
        body, name=name, grid=(NB // T,),
        in_specs=[tile] * 6 + [before, before, after, after, after, after],
        out_specs=[tile] * 3, out_shape=[out] * 3,
        compiler_params=_params(("parallel",)),
    )(q, k, v, do, lse, tt, k, v, q, do, lse, tt)


def _group_softmax(lse):
    m = jnp.max(lse, axis=0, keepdims=True)
    e = jnp.exp(lse - m)
    return e / jnp.sum(e, axis=0, keepdims=True)


def _mix_fwd(o, lse, name):
    G, S, W = o.shape
    tr = min(256, S)
    blk = pl.BlockSpec((G, tr, W), lambda i: (0, i, 0))

    def fn(ov, lv):
        return (jnp.sum(_group_softmax(lv) * ov.astype(F32), axis=0),)

    return _ew(fn, [(o, blk), (lse, blk)], [((S, W), BF16, _rows(tr, W), False)], (S // tr,), name)


def _mix_bwd(dya, ya, o, lse, name):
    G, S, W = o.shape
    tr = min(256, S)
    blk = pl.BlockSpec((G, tr, HEAD_DIM), lambda i, h: (0, i, h))
    row = pl.BlockSpec((tr, HEAD_DIM), lambda i, h: (i, h))

    def fn(dy, yv, ov, lv):
        w = _group_softmax(lv)
        dyf = dy.astype(F32)
        inner = jnp.sum(dyf * yv.astype(F32), axis=-1, keepdims=True)
        return w * dyf[None], -w * inner[None]

    return _ew(fn, [(dya, row), (ya, row), (o, blk), (lse, blk)],
               [((G, S, W), BF16, blk, False), ((G, S, W), F32, blk, False)],
               (S // tr, HEADS_PER_DIL), name)


CUM_BLOCK = 256


def _split3(x):
    hi = x.astype(BF16)
    r = x - hi.astype(F32)
    mid = r.astype(BF16)
    lo = (r - mid.astype(F32)).astype(BF16)
    return hi, mid, lo


def _tri_matmul(tri, x):
    return sum(jnp.dot(tri, part, preferred_element_type=F32) for part in _split3(x))


def _log_sigmoid(x):
    return jnp.minimum(x, 0.0) - jnp.log(1.0 + jnp.exp(-jnp.abs(x)))


def _fox_prep(f, b, name):
    S = f.shape[0]
    tb = min(CUM_BLOCK, S)

    def body(f_ref, b_ref, c_ref, carry):
        @pl.when(pl.program_id(0) == 0)
        def _():
            carry[...] = jnp.zeros_like(carry)

        ls = _log_sigmoid(f_ref[...] + b_ref[...])
        r = lax.broadcasted_iota(jnp.int32, (tb, tb), 0)
        cidx = lax.broadcasted_iota(jnp.int32, (tb, tb), 1)
        tri = jnp.where(r >= cidx, 1.0, 0.0).astype(BF16)
        c_ref[...] = _tri_matmul(tri, ls) + carry[...]
        carry[...] += jnp.sum(ls, axis=0, keepdims=True)

    return pl.pallas_call(
        body, name=name, grid=(S // tb,),
        in_specs=[_rows(tb, LANES), _bcast(LANES)], out_specs=_rows(tb, LANES),
        out_shape=jax.ShapeDtypeStruct((S, LANES), F32),
        scratch_shapes=[pltpu.VMEM((1, LANES), F32)],
        compiler_params=_params(("arbitrary",)),
    )(f, b)


def _fox_prep_bwd(dc, f, b, name):
    S = f.shape[0]
    tb = min(CUM_BLOCK, S)
    nb = S // tb

    def body(dc_ref, f_ref, b_ref, df_ref, db_ref, carry):
        @pl.when(pl.program_id(0) == 0)
        def _():
            carry[...] = jnp.zeros_like(carry)
            db_ref[...] = jnp.zeros_like(db_ref)

        r = lax.broadcasted_iota(jnp.int32, (tb, tb), 0)
        cidx = lax.broadcasted_iota(jnp.int32, (tb, tb), 1)
        tri = jnp.where(r <= cidx, 1.0, 0.0).astype(BF16)
        dcv = dc_ref[...]
        dls = _tri_matmul(tri, dcv) + carry[...]
        carry[...] += jnp.sum(dcv, axis=0, keepdims=True)
        z = f_ref[...] + b_ref[...]
        df = dls * (1.0 / (1.0 + jnp.exp(z)))
        df_ref[...] = df
        db_ref[...] += jnp.sum(df, axis=0, keepdims=True)

    rev = pl.BlockSpec((tb, LANES), lambda i: (nb - 1 - i, 0))
    return pl.pallas_call(
        body, name=name, grid=(nb,),
        in_specs=[rev, rev, _bcast(LANES)], out_specs=[rev, _bcast(LANES)],
        out_shape=[jax.ShapeDtypeStruct((S, LANES), F32), jax.ShapeDtypeStruct((1, LANES), F32)],
        scratch_shapes=[pltpu.VMEM((1, LANES), F32)],
        compiler_params=_params(("arbitrary",)),
    )(dc, f, b)


FOX_Q_TILE = 256
_FOX_Q0 = N_DIL_HEADS
_FOX_K0 = N_HEADS + N_DIL_HEADS
_FOX_V0 = 2 * N_HEADS + N_DIL_HEADS


def _fox_scores(q, k, cq, ck, q0):
    nt = (((1,), (1,)), ((), ()))
    s = lax.dot_general(q, k, nt, preferred_element_type=F32) * SCALE + cq - ck
    qpos = q0 + lax.broadcasted_iota(jnp.int32, s.shape, 0)
    kpos = lax.broadcasted_iota(jnp.int32, s.shape, 1)
    return jnp.where(kpos <= qpos, s, NEG)


def _per_query_tile(nq, tq, fn):
    step = pl.program_id(1)
    for n in range(nq):
        @pl.when(step == n)
        def _(n=n):
            fn(n, pl.ds(0, (n + 1) * tq))


def _fox_fwd(z, cq, ck, name):
    S = z.shape[0]
    tq = min(FOX_Q_TILE, S)

    def body(q_ref, k_ref, v_ref, cq_ref, ck_ref, o_ref, lse_ref):
        def tile(n, keys):
            s = _fox_scores(q_ref[...], k_ref[keys, :], cq_ref[...], ck_ref[:, keys], n * tq)
            m = jnp.max(s, axis=-1, keepdims=True)
            p = jnp.exp(s - m)
            l = jnp.sum(p, axis=-1, keepdims=True)
            o = jnp.dot(p.astype(BF16), v_ref[keys, :], preferred_element_type=F32)
            o_ref[...] = (o / l).astype(o_ref.dtype)
            lse_ref[...] = m + jnp.log(l)

        _per_query_tile(S // tq, tq, tile)

    qblk = lambda c0: pl.BlockSpec((tq, HEAD_DIM), lambda h, i: (i, c0 + h))
    full = lambda c0: pl.BlockSpec((S, HEAD_DIM), lambda h, i: (0, c0 + h))
    col = pl.BlockSpec((None, tq, 1), lambda h, i: (h, i, 0))
    rowv = pl.BlockSpec((None, 1, S), lambda h, i: (h, 0, 0))
    return pl.pallas_call(
        body, name=name, grid=(N_FOX_HEADS, S // tq),
        in_specs=[qblk(_FOX_Q0), full(_FOX_K0), full(_FOX_V0), col, rowv],
        out_specs=[qblk(0), col],
        out_shape=[jax.ShapeDtypeStruct((S, FOX_WIDTH), BF16),
                   jax.ShapeDtypeStruct((N_FOX_HEADS, S, 1), F32)],
        compiler_params=_params(("parallel", "parallel")),
    )(z, z, z, cq, ck)


def _fox_bwd(z, cq, ck, lse, yb, dyb, name):
    S = z.shape[0]
    tq = min(FOX_Q_TILE, S)
    nq = S // tq
    nt = (((1,), (1,)), ((), ()))
    tn = (((0,), (0,)), ((), ()))

    def body(q_ref, k_ref, v_ref, cq_ref, ck_ref, lse_ref, o_ref, do_ref,
             dq_ref, dk_ref, dv_ref, dc_ref, dk_acc, dv_acc):
        i = pl.program_id(1)

        @pl.when(i == 0)
        def _():
            dk_acc[...] = jnp.zeros_like(dk_acc)
            dv_acc[...] = jnp.zeros_like(dv_acc)
            dc_ref[...] = jnp.zeros_like(dc_ref)

        def tile(n, keys):
            q, k, v, do = q_ref[...], k_ref[keys, :], v_ref[keys, :], do_ref[...]
            s = _fox_scores(q, k, cq_ref[...], ck_ref[:, keys], n * tq)
            p = jnp.exp(s - lse_ref[...])
            dp = lax.dot_general(do, v, nt, preferred_element_type=F32)
            ds = p * (dp - jnp.sum(p * dp, axis=-1, keepdims=True))
            dsb = ds.astype(BF16)
            dq_ref[...] = (jnp.dot(dsb, k, preferred_element_type=F32) * SCALE).astype(BF16)
            dk_acc[keys, :] += lax.dot_general(dsb, q, tn, preferred_element_type=F32) * SCALE
            dv_acc[keys, :] += lax.dot_general(p.astype(BF16), do, tn, preferred_element_type=F32)
            dc_ref[:, keys] -= jnp.sum(ds, axis=0, keepdims=True)

        _per_query_tile(nq, tq, tile)

        @pl.when(i == nq - 1)
        def _():
            dk_ref[...] = dk_acc[...].astype(BF16)
            dv_ref[...] = dv_acc[...].astype(BF16)

    qblk = lambda c0: pl.BlockSpec((tq, HEAD_DIM), lambda h, i: (i, c0 + h))
    full = lambda c0: pl.BlockSpec((S, HEAD_DIM), lambda h, i: (0, c0 + h))
    col = pl.BlockSpec((None, tq, 1), lambda h, i: (h, i, 0))
    rowv = pl.BlockSpec((None, 1, S), lambda h, i: (h, 0, 0))
    wide = jax.ShapeDtypeStruct((S, FOX_WIDTH), BF16)
    return pl.pallas_call(
        body, name=name, grid=(N_FOX_HEADS, nq),
        in_specs=[qblk(_FOX_Q0), full(_FOX_K0), full(_FOX_V0), col, rowv, col, qblk(0), qblk(0)],
        out_specs=[qblk(0), full(0), full(0), rowv],
        out_shape=[wide, wide, wide, jax.ShapeDtypeStruct((N_FOX_HEADS, 1, S), F32)],
        scratch_shapes=[pltpu.VMEM((S, HEAD_DIM), F32), pltpu.VMEM((S, HEAD_DIM), F32)],
        compiler_params=_params(("parallel", "arbitrary")),
    )(z, z, z, cq, ck, lse, yb, dyb)


def _sigmoid(x):
    return 1.0 / (1.0 + jnp.exp(-x))


def _merge_fwd(gates, a, bm, name):
    S, D = a.shape
    tr = _row_tile(S, D * 4)
    g1 = pl.BlockSpec((tr, D), lambda i: (i, 0))
    g2 = pl.BlockSpec((tr, D), lambda i: (i, 1))

    def fn(x1, x2, av, bv):
        return (x1.astype(F32) * av.astype(F32) + x2.astype(F32) * bv.astype(F32),)

    return _ew(fn, [(gates, g1), (gates, g2), (a, _rows(tr, D)), (bm, _rows(tr, D))],
               [((S, D), BF16, _rows(tr, D), False)], (S // tr,), name)


def _merge_bwd(dmerged, gates, a, bm, name):
    S, D = a.shape
    tr = _row_tile(S, D * 8)
    g1 = pl.BlockSpec((tr, D), lambda i: (i, 0))
    g2 = pl.BlockSpec((tr, D), lambda i: (i, 1))

    def fn(dm, x1, x2, av, bv):
        dm, x1, x2 = dm.astype(F32), x1.astype(F32), x2.astype(F32)
        dg1 = dm * av.astype(F32) * x1 * (1.0 - x1)
        dg2 = dm * bv.astype(F32) * x2 * (1.0 - x2)
        dgp = jnp.concatenate([dg1, dg2], axis=1)
        return dm * x1, dm * x2, dgp, jnp.sum(dgp, axis=0, keepdims=True)

    return _ew(fn, [(dmerged, _rows(tr, D)), (gates, g1), (gates, g2), (a, _rows(tr, D)), (bm, _rows(tr, D))],
               [((S, D), BF16, _rows(tr, D), False), ((S, D), BF16, _rows(tr, D), False),
                ((S, 2 * D), BF16, _rows(tr, 2 * D), False), ((1, 2 * D), F32, _bcast(2 * D), True)],
               (S // tr,), name)


def _ple_bwd(dh, pg, pe, name):
    S, D = dh.shape
    tr = _row_tile(S, D * 4)

    def fn(d, g, e):
        g, e = g.astype(F32), e.astype(F32)
        return d * g, d * e * g * (1.0 - g)

    spec = _rows(tr, D)
    return _ew(fn, [(dh, spec), (pg, spec), (pe, spec)],
               [((S, D), BF16, spec, False), ((S, D), BF16, spec, False)], (S // tr,), name)


def _position():
    x, y, c = lax.axis_index("x"), lax.axis_index("y"), lax.axis_index("c")
    chips = [(1 - x, y), (x, 1 - y), (1 - x, 1 - y)]
    return x, y, c, chips


def _remote(src, dst, send_sem, recv_sem, target):
    return pltpu.make_async_remote_copy(src_ref=src, dst_ref=dst, send_sem=send_sem, recv_sem=recv_sem,
                                        device_id=target, device_id_type=MESH)


HBM = pl.BlockSpec(memory_space=pltpu.HBM)
SEM = pl.BlockSpec(memory_space=pltpu.SEMAPHORE)
EFFECT = pltpu.SideEffectType.DATAFLOW_SIDE_EFFECTING


def _copies_start(plan, arrays, sem_shape, after, name):
    n = len(arrays)

    def body(*refs):
        send_sems, recv_sems, token = refs[n + 1], refs[n + 2], refs[-1]
        for send, _ in plan(refs[:n], send_sems, recv_sems):
            send.start()
        token[...] = jnp.zeros_like(token)

    outs = pl.pallas_call(
        body, name=name,
        out_shape=(pltpu.SemaphoreType.DMA(sem_shape), pltpu.SemaphoreType.DMA(sem_shape),
                   *[pltpu.HBM(a.shape, a.dtype) for a in arrays], jax.ShapeDtypeStruct((8, LANES), F32)),
        in_specs=[HBM] * n + [ANY],
        out_specs=(SEM, SEM, *[HBM] * n, pl.BlockSpec(memory_space=pltpu.VMEM)),
        input_output_aliases={a: 2 + a for a in range(n)},
        compiler_params=pltpu.CompilerParams(has_side_effects=EFFECT),
    )(*[pltpu.with_memory_space_constraint(a, pltpu.HBM) for a in arrays], after)
    return outs[0], outs[1], list(outs[2:2 + n]), outs[-1]


def _copies_wait(plan, send_sems, recv_sems, arrays, after, name):
    n = len(arrays)

    def body(*refs):
        for send, recv in plan(refs[:n], refs[n], refs[n + 1]):
            send.wait_send()
            recv.wait_recv()

    return list(pl.pallas_call(
        body, name=name,
        out_shape=[pltpu.HBM(a.shape, a.dtype) for a in arrays],
        in_specs=[HBM] * n + [SEM, SEM, ANY], out_specs=[HBM] * n,
        input_output_aliases={a: a for a in range(n)},
        compiler_params=pltpu.CompilerParams(has_side_effects=EFFECT),
    )(*arrays, send_sems, recv_sems, after))


def _gather_ici_plan(refs, send_sems, recv_sems):
    x, y, c, chips = _position()
    plan = []
    for a, ref in enumerate(refs):
        rh = ref.shape[1] // 2
        mine = ref.at[2 * x + y, pl.ds(c * rh, rh)]
        for j, (cx, cy) in enumerate(chips):
            landed = ref.at[2 * cx + cy, pl.ds(c * rh, rh)]
            plan.append((_remote(mine, mine, send_sems.at[3 * a + j], recv_sems.at[3 * a + j], (cx, cy, c)),
                         _remote(landed, landed, send_sems.at[3 * a + j], recv_sems.at[3 * a + j], (cx, cy, c))))
    return plan


def _gather_d2d_plan(refs, send_sems, recv_sems):
    x, y, c, chips = _position()
    sibling = (x, y, 1 - c)
    plan = []
    for a, ref in enumerate(refs):
        rh = ref.shape[1] // 2
        for j, (cx, cy) in enumerate(chips):
            landed = ref.at[2 * cx + cy, pl.ds(c * rh, rh)]
            theirs = ref.at[2 * cx + cy, pl.ds((1 - c) * rh, rh)]
            plan.append((_remote(landed, landed, send_sems.at[3 * a + j], recv_sems.at[3 * a + j], sibling),
                         _remote(theirs, theirs, send_sems.at[3 * a + j], recv_sems.at[3 * a + j], sibling)))
    return plan


def _scatter_ici_plan(refs, send_sems, recv_sems):
    x, y, c, chips = _position()
    n = len(refs) // 2
    plan = []
    for a in range(n):
        for j, (cx, cy) in enumerate(chips):
            cp = _remote(refs[a].at[2 * cx + cy], refs[n + a].at[j], send_sems.at[3 * a + j], recv_sems.at[3 * a + j],
                         (cx, cy, c))
            plan.append((cp, cp))
    return plan


def _place_shard(w, layer, me, after, name):
    _, r, cc = w.shape
    tr = _row_tile(r, cc * 4)

    def body(me_ref, w_ref, after_ref, o_ref):
        o_ref[...] = w_ref[...].astype(o_ref.dtype)

    return pl.pallas_call(
        body, name=name,
        grid_spec=pltpu.PrefetchScalarGridSpec(
            num_scalar_prefetch=1, grid=(r // tr,),
            in_specs=[pl.BlockSpec((None, tr, cc), lambda i, me_ref: (layer, i, 0)), ANY],
            out_specs=pl.BlockSpec((None, tr, cc), lambda i, me_ref: (me_ref[0], i, 0))),
        out_shape=jax.ShapeDtypeStruct((N_CHIPS, r, cc), BF16),
        compiler_params=_params(("parallel",)),
    )(me, w, after)


def _pair_plan(refs, send_sems, recv_sems):
    x, y, c, _ = _position()
    n = len(refs) // 2
    plan = []
    for a in range(n):
        rh = refs[a].shape[1] // 2
        cp = _remote(refs[a].at[:, pl.ds((1 - c) * rh, rh)], refs[n + a], send_sems.at[a], recv_sems.at[a],
                     (x, y, 1 - c))
        plan.append((cp, cp))
    return plan


def _pair_sum(mine, theirs, c, name):
    nch, rh, cc = theirs.shape
    tr = _row_tile(rh, cc * 4)
    nb = rh // tr

    def body(c_ref, m_ref, t_ref, o_ref):
        o_ref[...] = (m_ref[...].astype(F32) + t_ref[...].astype(F32)).astype(o_ref.dtype)

    return pl.pallas_call(
        body, name=name,
        grid_spec=pltpu.PrefetchScalarGridSpec(
            num_scalar_prefetch=1, grid=(nch, nb),
            in_specs=[pl.BlockSpec((1, tr, cc), lambda k, i, c_ref: (k, c_ref[0] * nb + i, 0)),
                      pl.BlockSpec((1, tr, cc), lambda k, i, c_ref: (k, i, 0))],
            out_specs=pl.BlockSpec((1, tr, cc), lambda k, i, c_ref: (k, i, 0))),
        out_shape=jax.ShapeDtypeStruct(theirs.shape, BF16),
        compiler_params=_params(("parallel", "parallel")),
    )(c, mine, theirs)


def _chip_sum(own, others, me, c, total, layer, depth, name):
    _, rh, cc = own.shape
    tr = _row_tile(rh, cc * 4)
    nb = rh // tr
    chained = total is not None

    def body(me_ref, c_ref, o_ref, r_ref, *rest):
        g_ref = rest[-1]
        g_ref[...] = (o_ref[0].astype(F32) + r_ref[0].astype(F32)) + (r_ref[1].astype(F32) + r_ref[2].astype(F32))

    in_specs = [pl.BlockSpec((1, tr, cc), lambda i, me_ref, c_ref: (me_ref[0], i, 0)),
                pl.BlockSpec((3, tr, cc), lambda i, me_ref, c_ref: (0, i, 0))]
    args = [me, c, own, others]
    if chained:
        in_specs.append(ANY)
        args.append(total)
    return pl.pallas_call(
        body, name=name,
        grid_spec=pltpu.PrefetchScalarGridSpec(
            num_scalar_prefetch=2, grid=(nb,), in_specs=in_specs,
            out_specs=pl.BlockSpec((None, tr, cc), lambda i, me_ref, c_ref: (layer, c_ref[0] * nb + i, 0))),
        out_shape=jax.ShapeDtypeStruct((depth, 2 * rh, cc), F32),
        input_output_aliases={4: 0} if chained else {},
        compiler_params=_params(("parallel",)),
    )(*args)


def _half_exchange(totals, layer, name):
    n = len(totals)

    def body(*refs):
        outs = refs[n:2 * n]
        send_sems, recv_sems = refs[2 * n:]
        x, y, c, _ = _position()
        copies = []
        for a in range(n):
            rh = outs[a].shape[1] // 2
            mine = outs[a].at[layer, pl.ds(c * rh, rh)]
            cp = _remote(mine, mine, send_sems.at[a], recv_sems.at[a], (x, y, 1 - c))
            cp.start()
            copies.append(cp)
        for a, cp in enumerate(copies):
            rh = outs[a].shape[1] // 2
            theirs = outs[a].at[layer, pl.ds((1 - c) * rh, rh)]
            cp.wait_send()
            _remote(theirs, theirs, send_sems.at[a], recv_sems.at[a], (x, y, 1 - c)).wait_recv()

    return pl.pallas_call(
        body, name=name, in_specs=[ANY] * n, out_specs=[ANY] * n,
        out_shape=[jax.ShapeDtypeStruct(t.shape, t.dtype) for t in totals],
        input_output_aliases={a: a for a in range(n)},
        scratch_shapes=[pltpu.SemaphoreType.DMA((n,)), pltpu.SemaphoreType.DMA((n,))],
    )(*totals)


def _allgather_devices(v, name):
    m_per, n = v.shape

    def body(x_ref, out_ref, send_sems, recv_sems, local_sem):
        x, y, c, chips = _position()
        me, sibling = (x, y, c), (x, y, 1 - c)

        def rows(px, py, pc):
            return out_ref.at[pl.ds((4 * px + 2 * py + pc) * m_per, m_per), :]

        def copy(k, block, to, src=None):
            return _remote(rows(*block) if src is None else src, rows(*block), send_sems.at[k], recv_sems.at[k], to)

        mine = pltpu.make_async_copy(x_ref, rows(*me), local_sem)
        mine.start()
        first = [copy(0, me, sibling, src=x_ref)]
        first += [copy(1 + j, me, (*chip, c), src=x_ref) for j, chip in enumerate(chips)]
        for cp in first:
            cp.start()
        passed = [copy(4 + j, (*chip, c), sibling) for j, chip in enumerate(chips)]
        for j, chip in enumerate(chips):
            copy(1 + j, (*chip, c), me).wait_recv()
            passed[j].start()
        copy(0, sibling, me).wait_recv()
        for j, chip in enumerate(chips):
            copy(4 + j, (*chip, 1 - c), me).wait_recv()
        for cp in first + passed:
            cp.wait_send()
        mine.wait()

    vm = pl.BlockSpec(memory_space=pltpu.VMEM)
    return pl.pallas_call(
        body, name=name, in_specs=[vm], out_specs=vm,
        out_shape=jax.ShapeDtypeStruct((8 * m_per, n), v.dtype),
        scratch_shapes=[pltpu.SemaphoreType.DMA((7,)), pltpu.SemaphoreType.DMA((7,)), pltpu.SemaphoreType.DMA],
    )(v)


def _adamw_math(w, g, m, v):
    m = ADAM_B1 * m + (1.0 - ADAM_B1) * g
    v = ADAM_B2 * v + (1.0 - ADAM_B2) * (g * g)
    m_hat = m / (1.0 - ADAM_B1 ** ADAM_STEP)
    v_hat = v / (1.0 - ADAM_B2 ** ADAM_STEP)
    delta = -ADAM_LR * (m_hat / (jnp.sqrt(v_hat) + ADAM_EPS) + ADAM_WD * w)
    return delta, m, v


def _adamw(w, g, m, v, lo, hi, prev, after, name):
    depth, r, cc = w.shape
    tr = _row_tile(r, cc * 4 * 2)
    spec = pl.BlockSpec((1, tr, cc), lambda l, i: (lo + l, i, 0))

    def body(w_ref, g_ref, m_ref, v_ref, *rest):
        outs = rest[-4:]
        gv = g_ref[...]
        for o_ref, val in zip(outs, (gv,) + _adamw_math(w_ref[...], gv, m_ref[...], v_ref[...])):
            o_ref[...] = val

    prev = list(prev) if prev is not None else []
    return tuple(pl.pallas_call(
        body, name=name, grid=(hi - lo, r // tr),
        in_specs=[spec] * 4 + [ANY] * (1 + len(prev)), out_specs=[spec] * 4,
        out_shape=[jax.ShapeDtypeStruct(w.shape, F32)] * 4,
        input_output_aliases={5 + k: k for k in range(len(prev))},
        compiler_params=_params(("parallel", "parallel")),
    )(w, g, m, v, after, *prev))


def _adamw_small(w, parts, m, v, name):
    M = w.shape[0]

    def body(w_ref, p_ref, m_ref, v_ref, g_ref, d_ref, nm_ref, nv_ref):
        g = p_ref[pl.ds(0, M), :]
        for k in range(1, 8):
            g = g + p_ref[pl.ds(k * M, M), :]
        d, nm, nv = _adamw_math(w_ref[...], g, m_ref[...], v_ref[...])
        g_ref[...] = g
        d_ref[...] = d
        nm_ref[...] = nm
        nv_ref[...] = nv

    vm = pl.BlockSpec(memory_space=pltpu.VMEM)
    return pl.pallas_call(
        body, name=name, in_specs=[vm] * 4, out_specs=[vm] * 4,
        out_shape=[jax.ShapeDtypeStruct(w.shape, F32)] * 4,
    )(w, parts, m, v)


def _layer_fwd(h0, p_l, W, small, tabs, after_mlp):
    S, D = h0.shape
    ctab, stab = tabs
    u = _rms_fwd(h0, small["g_mix"], "rms_mix")
    z = _mm(u, W["w_qkv"], mode="nn", name="mm_qkv")
    f = _mm(u, W["w_f"], mode="nn", name="mm_f", out_dtypes=(F32,))
    gates = _mm(u, W["w_gate"], mode="nn", name="mm_gate", b_chunked=True, extras=[(small["b_gate"], "row")],
                epilogue=lambda acc, b: (_sigmoid(acc + b),))
    qr, kr = _rope(z, 0, z, N_HEADS, ctab, stab, 1.0, "rope_fwd")
    qs, ks, vs = _to_strided(qr), _to_strided(kr), _to_strided(z[:, 2 * ATTN_WIDTH:2 * ATTN_WIDTH + DIL_WIDTH])
    o_s, lse_s = _sw_fwd(qs, ks, vs, "sw_fwd")
    o_g, lse_g = _from_strided(o_s, S), _from_strided(lse_s, S)
    ya = _mix_fwd(o_g, lse_g, "mix_fwd")
    a = _mm(ya, W["w_br_a"], mode="nn", name="mm_br_a", b_chunked=True)
    cum = _fox_prep(f, small["b_f"], "fox_prep")
    cq = cum[:, :N_FOX_HEADS].T[:, :, None]
    ck = cum[:, :N_FOX_HEADS].T[:, None, :]
    yb, lse_f = _fox_fwd(z, cq, ck, "fox_fwd")
    bm = _mm(yb, W["w_br_b"], mode="nn", name="mm_br_b", b_chunked=True)
    merged = _merge_fwd(gates, a, bm, "merge_fwd")
    h1 = _mm(merged, W["w_o"], mode="nn", name="mm_o", out_dtypes=(F32,), extras=[(h0, "tile")],
             epilogue=lambda acc, r: (acc + r,), tj=512)
    m = _rms_fwd(h1, small["g_mlp"], "rms_mlp")
    ra, act = _mm(m, W["w_up"], mode="nn", name="mm_up", b_chunked=True, out_dtypes=(BF16, BF16),
                  epilogue=lambda acc: (jnp.maximum(acc, 0.0), jnp.square(jnp.maximum(acc, 0.0))))
    h2 = _mm(act, W["w_down"], mode="nn", name="mm_down", out_dtypes=(F32,), extras=[(h1, "tile")],
             epilogue=lambda acc, r: (acc + r,), ti=1024, tj=512, tc=4096)
    token = after_mlp(h2)
    n = _rms_fwd(h2, small["g_ple"] if token is None else small["g_ple"] + token[0, 0], "rms_ple")
    pg = _mm(n, W["w_ple_gate"], mode="nn", name="mm_ple_gate", epilogue=lambda acc: (_sigmoid(acc),))
    h3, pe = _mm(p_l, W["w_ple"], mode="nn", name="mm_ple", b_chunked=True, out_dtypes=(F32, BF16),
                 extras=[(h2, "tile"), (pg, "tile")], tj=256,
                 epilogue=lambda acc, r, g: (r + g.astype(F32) * acc, acc))
    saved = dict(h0=h0, u=u, z=z, f=f, gates=gates, qs=qs, ks=ks, vs=vs, lse_s=lse_s, o_g=o_g, lse_g=lse_g,
                 ya=ya, a=a, cq=cq, ck=ck, yb=yb, lse_f=lse_f, bm=bm, merged=merged, h1=h1, m=m, ra=ra,
                 act=act, h2=h2, n=n, pg=pg, pe=pe, p_l=p_l)
    return h3, saved


def _after(hooks, name, small_value, *args):
    token = hooks[name](*args) if name in hooks else None
    return small_value if token is None else small_value + token[0, 0]


def _layer_bwd(dh3, sv, W, small, tabs, hooks):
    S, D = dh3.shape
    ctab, stab = tabs
    gw, gs = {}, {}
    tn = functools.partial(_mm, mode="tn", ti=1024, tj=2048)
    dpe, dpg = _ple_bwd(dh3, sv["pg"], sv["pe"], "ple_bwd")
    gw["w_ple"] = tn(sv["p_l"], dpe, name="dw_ple", out_chunks=N_CHIPS)
    gw["w_ple_gate"] = tn(sv["n"], dpg, name="dw_ple_gate").reshape(N_CHIPS, D // N_CHIPS, D)
    dn = _mm(dpg, W["w_ple_gate"], mode="nt", name="mm_dn")
    dh2, dh2b, gs["g_ple"] = _rms_bwd(sv["h2"], small["g_ple"], dn, dh3, "rms_ple_bwd")
    da = _mm(dh2b, W["w_down"], mode="nt", name="mm_dact", extras=[(sv["ra"], "tile")],
             epilogue=lambda acc, r: (acc * (2.0 * r.astype(F32)),))
    FF = da.shape[1]
    g_mlp = _after(hooks, "mlp_grad", small["g_mlp"], da)
    gw["w_down"] = tn(sv["act"], dh2b, name="dw_down").reshape(N_CHIPS, FF // N_CHIPS, D)
    gw["w_up"] = tn(sv["m"], da, name="dw_up", out_chunks=N_CHIPS)
    g_mlp = _after(hooks, "mlp_weights", g_mlp, dict(gw))
    dm = _mm(da, W["w_up"], mode="nt", name="mm_dm", b_chunked=True)
    dh1, dh1b, gs["g_mlp"] = _rms_bwd(sv["h1"], g_mlp, dm, dh2, "rms_mlp_bwd")
    dmerged = _mm(dh1b, W["w_o"], mode="nt", name="mm_dmerged")
    b_f = _after(hooks, "merge_grad", small["b_f"], dmerged)
    gw["w_o"] = tn(sv["merged"], dh1b, name="dw_o").reshape(N_CHIPS, D // N_CHIPS, D)
    d_a, d_b, dgp, gs["b_gate"] = _merge_bwd(dmerged, sv["gates"], sv["a"], sv["bm"], "merge_bwd")
    gw["w_gate"] = tn(sv["u"], dgp, name="dw_gate", out_chunks=N_CHIPS)
    gw["w_br_a"] = tn(sv["ya"], d_a, name="dw_br_a", out_chunks=N_CHIPS, tj=512)
    gw["w_br_b"] = tn(sv["yb"], d_b, name="dw_br_b", out_chunks=N_CHIPS, tj=512)
    dya = _mm(d_a, W["w_br_a"], mode="nt", name="mm_dya", b_chunked=True)
    dyb = _mm(d_b, W["w_br_b"], mode="nt", name="mm_dyb", b_chunked=True)
    z = sv["z"]
    dq_f, dk_f, dv_f, dck = _fox_bwd(z, sv["cq"], sv["ck"], sv["lse_f"], sv["yb"], dyb, "fox_bwd")
    dc = jnp.pad(dck[:, 0, :].T, ((0, 0), (0, LANES - N_FOX_HEADS)))
    df, dbf = _fox_prep_bwd(dc, sv["f"], b_f, "fox_prep_bwd")
    gs["b_f"] = dbf[:, :N_FOX_HEADS]
    lane = jnp.arange(LANES)[None, :] < N_FOX_HEADS
    dzf = jnp.where(lane, df, 0.0).astype(BF16)
    do_g, tt_g = _mix_bwd(dya, sv["ya"], sv["o_g"], sv["lse_g"], "mix_bwd")
    do_s = _to_strided(do_g.transpose(1, 0, 2).reshape(S, DIL_WIDTH))
    tt_s = _to_strided(tt_g.transpose(1, 0, 2).reshape(S, DIL_WIDTH))
    dq_s, dk_s, dv_s = _sw_bwd(sv["qs"], sv["ks"], sv["vs"], do_s, sv["lse_s"], tt_s, "sw_bwd")
    unstride = lambda t: _from_strided(t, S).transpose(1, 0, 2).reshape(S, DIL_WIDTH)
    dq_a, dk_a = _rope(unstride(dq_s), 0, unstride(dk_s), 0, ctab, stab, -1.0, "rope_bwd")
    dz = jnp.concatenate([dq_a, dq_f, dk_a, dk_f, unstride(dv_s), dv_f], axis=1)
    g_qkv = tn(sv["u"], dz, name="dw_qkv")
    g_f = tn(sv["u"], dzf, name="dw_f", tj=128)
    cols = W["w_in_cols"]
    g_in = jnp.concatenate([g_qkv, g_f[:, :N_FOX_HEADS]], axis=1)
    gw["w_in"] = jnp.stack([g_in[:, k * cols:(k + 1) * cols] for k in range(N_CHIPS)])
    du = _mm(dzf, W["w_f"], mode="nt", name="mm_du_f", out_dtypes=(F32,))
    du = _mm(dgp, W["w_gate"], mode="nt", name="mm_du_gate", b_chunked=True, out_dtypes=(F32,),
             extras=[(du, "tile")], epilogue=lambda acc, r: (acc + r,), tj=512)
    du = _mm(dz, W["w_qkv"], mode="nt", name="mm_du_qkv", extras=[(du, "tile")],
             epilogue=lambda acc, r: (acc + r,), tj=512)
    dh0, _, gs["g_mix"] = _rms_bwd(sv["h0"], small["g_mix"], du, dh1, "rms_mix_bwd")
    return dh0, gw, gs


BIG = ("w_in", "w_gate", "w_br_a", "w_br_b", "w_o", "w_up", "w_down", "w_ple", "w_ple_gate")
EARLY = ("w_ple", "w_ple_gate", "w_down", "w_up")
SMALL = ("g_mix", "b_f", "b_gate", "g_mlp", "g_ple", "g_final")
ORDER = ("g_mix", "w_in", "b_f", "w_gate", "b_gate", "w_br_a", "w_br_b", "w_o", "g_mlp", "w_up", "w_down",
         "g_ple", "w_ple", "w_ple_gate", "g_final")


def _gathered_layer_weights(full, D):
    W = {}
    for name in ("w_gate", "w_br_a", "w_br_b", "w_up", "w_ple"):
        W[name] = full[name]
    for name in ("w_o", "w_down", "w_ple_gate"):
        t = full[name]
        W[name] = t.reshape(t.shape[0] * t.shape[1], t.shape[2])
    w_in = full["w_in"]
    cols = w_in.shape[2]
    w_in = jnp.concatenate([w_in[k] for k in range(N_CHIPS)], axis=1)
    W["w_qkv"] = w_in[:, :3 * ATTN_WIDTH]
    W["w_f"] = jnp.pad(w_in[:, 3 * ATTN_WIDTH:], ((0, 0), (0, LANES - N_FOX_HEADS)))
    W["w_in_cols"] = cols
    return W


def _pack_rows(vals):
    flat = jnp.concatenate([v.reshape(-1) for v in vals])
    rows = -(-flat.shape[0] // (8 * LANES)) * 8
    return jnp.pad(flat, (0, rows * LANES - flat.shape[0])).reshape(rows, LANES)


def _unpack_rows(packed, shapes):
    flat = packed.reshape(-1)
    out, pos = [], 0
    for s in shapes:
        size = 1
        for dim in s:
            size *= dim
        out.append(flat[pos:pos + size].reshape(s))
        pos += size
    return out


def _local_step(x, p, small_w, comm, loss_target):
    depth = p.shape[0]
    S, D = x.shape
    tabs = _rope_tables(S)
    h = x
    saved, weights, smalls = [], [], []
    state, _ = comm["gather_start"](0, x)
    full = comm["gather_finish"](comm["gather_mid"](state, x)[0], x)
    for l in range(depth):
        nxt = [None]
        g_mix = small_w["g_mix"][l][None]
        if l + 1 < depth:
            nxt[0], token = comm["gather_start"](l + 1, full["w_ple"])
            g_mix = g_mix + token[0, 0]

        def after_mlp(h2):
            if nxt[0] is None:
                return None
            nxt[0], token = comm["gather_mid"](nxt[0], h2)
            return token

        W = _gathered_layer_weights(full, D)
        sm = dict(g_mix=g_mix, g_mlp=small_w["g_mlp"][l][None],
                  g_ple=small_w["g_ple"][l][None], b_gate=small_w["b_gate"][l][None],
                  b_f=jnp.pad(small_w["b_f"][l][None], ((0, 0), (0, LANES - N_FOX_HEADS))))
        h, sv = _layer_fwd(h, p[l].astype(BF16), W, sm, tabs, after_mlp)
        if nxt[0] is not None:
            full = comm["gather_finish"](nxt[0], h)
        saved.append(sv)
        weights.append(W)
        smalls.append(sm)
    dh, loss_row, dg_final = _loss_head(h, small_w["g_final"][None], loss_target, "loss_head")
    gss = [None] * depth
    pending, token = [None], None
    for l in reversed(range(depth)):
        sm = smalls[l]
        if token is not None:
            sm = {**sm, "g_ple": sm["g_ple"] + token[0, 0]}

        def after_mlp_grad(da):
            if pending[0] is None:
                return None
            pending[0] = comm["reduce_mid"](pending[0], da)
            return pending[0][-1]

        hooks = dict(mlp_grad=after_mlp_grad)
        if l == 0 and "reduce_early_begin" in comm:
            hooks.update(mlp_weights=comm["reduce_early_begin"], merge_grad=comm["reduce_early_mid"])
        dh, gw, gss[l] = _layer_bwd(dh, saved[l], weights[l], sm, tabs, hooks)
        if pending[0] is not None:
            comm["reduce_end"](pending[0], dh)
        if l > 0:
            pending[0], token = comm["reduce_begin"](l, gw, dh)
    return loss_row, dh, gss, dg_final, gw


def _device_comm(w, m, v, depth, c_arr, me_arr):
    n = len(BIG)
    totals = {name: None for name in BIG}
    placed, results = [], {}

    def gather_start(l, after):
        if l == 0:
            placed.append([_place_shard(w[name], 0, me_arr, me_arr, "place_shard") for name in BIG])
        send, recv, bufs, token = _copies_start(_gather_ici_plan, placed[l], (3 * n,), after, f"gather_ici_start_{l}")
        if l == 0:
            placed.extend([_place_shard(w[name], k, me_arr, token, "place_shard") for name in BIG]
                          for k in range(1, depth))
        return (l, send, recv, bufs), token

    def gather_mid(state, after):
        l, send, recv, bufs = state
        if l == 0:
            after = placed[-1][-1]
        bufs = _copies_wait(_gather_ici_plan, send, recv, bufs, after, f"gather_ici_wait_{l}")
        send, recv, bufs, token = _copies_start(_gather_d2d_plan, bufs, (3 * n,), after, f"gather_d2d_start_{l}")
        return (l, send, recv, bufs), token

    def gather_finish(state, after):
        l, send, recv, bufs = state
        return dict(zip(BIG, _copies_wait(_gather_d2d_plan, send, recv, bufs, after, f"gather_d2d_wait_{l}")))

    def reduce_begin(l, gw, after, names=BIG, tag=""):
        grads = [gw[name] for name in names]
        landing = [lax.empty((g.shape[0], g.shape[1] // 2, g.shape[2]), g.dtype) for g in grads]
        send, recv, arrays, token = _copies_start(_pair_plan, grads + landing, (len(names),), after,
                                                  f"pair_start_{l}{tag}")
        return (l, tag, names, send, recv, arrays), token

    def reduce_mid(state, after):
        l, tag, names, send, recv, arrays = state
        k = len(names)
        arrays = _copies_wait(_pair_plan, send, recv, arrays, after, f"pair_wait_{l}{tag}")
        sums = [_pair_sum(g, t, c_arr, "rs_pair_sum") for g, t in zip(arrays[:k], arrays[k:])]
        landing = [lax.empty((3,) + s.shape[1:], s.dtype) for s in sums]
        send, recv, arrays, token = _copies_start(_scatter_ici_plan, sums + landing, (3 * k,), sums[0],
                                                  f"scatter_ici_start_{l}{tag}")
        return l, tag, names, send, recv, arrays, token

    def reduce_end(state, after):
        l, tag, names, send, recv, arrays, _ = state
        k = len(names)
        arrays = _copies_wait(_scatter_ici_plan, send, recv, arrays, after, f"scatter_ici_wait_{l}{tag}")
        done = [_chip_sum(s, o, me_arr, c_arr, totals[name], l, depth, "rs_chip_sum")
                for name, s, o in zip(names, arrays[:k], arrays[k:])]
        totals.update(zip(names, _half_exchange(done, l, "rs_half_exchange")))

    early = [None]

    def reduce_early_begin(gw):
        early[0], token = reduce_begin(0, gw, gw[EARLY[-1]], EARLY, "a")
        return token

    def reduce_early_mid(after):
        early[0] = reduce_mid(early[0], after)
        return early[0][-1]

    def reduce_last(gw, after):
        upper = {}

        def adamw_upper(names, token):
            for name in names:
                if depth > 1:
                    upper[name] = _adamw(w[name], totals[name], m[name], v[name], 1, depth, None, token,
                                         "adamw_upper")
                    token = upper[name][0]
            return token

        def adamw_first(names, token):
            for name in names:
                results[name] = _adamw(w[name], totals[name], m[name], v[name], 0, 1, upper.get(name), token,
                                       "adamw_first")
                token = results[name][0]
            return token

        late = BIG if early[0] is None else tuple(name for name in BIG if name not in EARLY)
        state, token = reduce_begin(0, gw, after, late, "b")
        state = reduce_mid(state, adamw_upper(("w_up",), token))
        token = adamw_upper([name for name in BIG if name != "w_up"], state[-1])
        if early[0] is not None:
            reduce_end(early[0], token)
            token = adamw_first(EARLY, totals[EARLY[0]])
        reduce_end(state, token)
        adamw_first(late, totals[late[0]])

    comm = dict(gather_start=gather_start, gather_mid=gather_mid, gather_finish=gather_finish,
                reduce_begin=reduce_begin, reduce_mid=reduce_mid, reduce_end=reduce_end, reduce_last=reduce_last,
                reduce_early_begin=reduce_early_begin, reduce_early_mid=reduce_early_mid)
    return comm, results


def kernel(x, p, g_mix, w_in, b_f, w_gate, b_gate, w_br_a, w_br_b, w_o, g_mlp, w_up, w_down, g_ple, w_ple, w_ple_gate, g_final, loss_target, m_g_mix, m_w_in, m_b_f, m_w_gate, m_b_gate, m_w_br_a, m_w_br_b, m_w_o, m_g_mlp, m_w_up, m_w_down, m_g_ple, m_w_ple, m_w_ple_gate, m_g_final, v_g_mix, v_w_in, v_b_f, v_w_gate, v_b_gate, v_w_br_a, v_w_br_b, v_w_o, v_g_mlp, v_w_up, v_w_down, v_g_ple, v_w_ple, v_w_ple_gate, v_g_final):
    w = dict(g_mix=g_mix, w_in=w_in, b_f=b_f, w_gate=w_gate, b_gate=b_gate, w_br_a=w_br_a, w_br_b=w_br_b,
             w_o=w_o, g_mlp=g_mlp, w_up=w_up, w_down=w_down, g_ple=g_ple, w_ple=w_ple, w_ple_gate=w_ple_gate,
             g_final=g_final)
    m = dict(g_mix=m_g_mix, w_in=m_w_in, b_f=m_b_f, w_gate=m_w_gate, b_gate=m_b_gate, w_br_a=m_w_br_a,
             w_br_b=m_w_br_b, w_o=m_w_o, g_mlp=m_g_mlp, w_up=m_w_up, w_down=m_w_down, g_ple=m_g_ple,
             w_ple=m_w_ple, w_ple_gate=m_w_ple_gate, g_final=m_g_final)
    v = dict(g_mix=v_g_mix, w_in=v_w_in, b_f=v_b_f, w_gate=v_w_gate, b_gate=v_b_gate, w_br_a=v_w_br_a,
             w_br_b=v_w_br_b, w_o=v_w_o, g_mlp=v_g_mlp, w_up=v_w_up, w_down=v_w_down, g_ple=v_g_ple,
             w_ple=v_w_ple, w_ple_gate=v_w_ple_gate, g_final=v_g_final)
    depth = p.shape[0]
    cx, cy, cc = lax.axis_index("x"), lax.axis_index("y"), lax.axis_index("c")
    c_arr = jnp.reshape(cc, (1,)).astype(jnp.int32)
    me_arr = jnp.reshape(2 * cx + cy, (1,)).astype(jnp.int32)

    comm, big_out = _device_comm(w, m, v, depth, c_arr, me_arr)
    loss_row, grad_x, gss, dg_final, gw0 = _local_step(x[0], p[:, 0], w, comm, loss_target[0])

    small_grads = [jnp.stack([gss[l][n][0] for l in range(depth)]) for n in SMALL[:-1]] + [dg_final[0]]
    shapes = [w[n].shape for n in SMALL]
    parts = _allgather_devices(_pack_rows(small_grads), "allgather_small")
    packed = _adamw_small(_pack_rows([w[n] for n in SMALL]), parts, _pack_rows([m[n] for n in SMALL]),
                          _pack_rows([v[n] for n in SMALL]), "adamw_small")
    small_out = {n: vals for n, vals in zip(SMALL, zip(*[_unpack_rows(t, shapes) for t in packed]))}
    comm["reduce_last"](gw0, packed[0])

    loss = lax.psum(loss_row[0, 0], ("x", "y", "c"))
    out = {**big_out, **small_out}
    return (loss, grad_x[None], *[out[n][0] for n in ORDER], *[out[n][1] for n in ORDER],
            *[out[n][2] for n in ORDER], *[out[n][3] for n in ORDER])
```

```python
import functools

import jax
import jax.numpy as jnp
from jax import lax
from jax.experimental import pallas as pl
from jax.experimental.pallas import tpu as pltpu

F32 = jnp.float32
BF16 = jnp.bfloat16

HEAD_DIM = 128
N_HEADS = 16
N_DIL_HEADS = 12
N_FOX_HEADS = 4
HEADS_PER_DIL = 4
DILATIONS = (1, 4, 16)
BLOCK = 128
ATTN_WIDTH = N_HEADS * HEAD_DIM
DIL_WIDTH = N_DIL_HEADS * HEAD_DIM
FOX_WIDTH = N_FOX_HEADS * HEAD_DIM
ROPE_THETA = 500000.0
ROPE_HALF = 16
NORM_EPS = 1e-6
SCALE = HEAD_DIM ** -0.5
NEG = -1e30

ADAM_LR = 0.001
ADAM_B1 = 0.9
ADAM_B2 = 0.999
ADAM_EPS = 1e-08
ADAM_WD = 0.01
ADAM_STEP = 10

N_CHIPS = 4
V7X_VMEM_LIMIT_BYTES = 56 * 1024 * 1024
LANES = 128
MESH = pl.DeviceIdType.MESH
ANY = pl.BlockSpec(memory_space=pl.ANY)


def _params(sem):
    return pltpu.CompilerParams(dimension_semantics=sem, vmem_limit_bytes=V7X_VMEM_LIMIT_BYTES)


def _tile(n, pref):
    if n <= pref:
        return n
    t = (pref // LANES) * LANES
    while t > LANES and n % t:
        t -= LANES
    assert n % t == 0, (n, pref)
    return t


def _mm(a, b, *, mode, name, out_dtypes=(BF16,), epilogue=None, extras=(), b_chunked=False,
        out_chunks=0, ti=2048, tj=512, tc=2048):
    if mode == "tn":
        C, I = a.shape
    else:
        I, C = a.shape
    if b_chunked:
        nch, d0, n = b.shape
        if mode == "nn":
            assert d0 == C
            J = nch * n
        else:
            assert mode == "nt" and nch * n == C
            J = d0
    elif mode == "nt":
        J = b.shape[0]
        assert b.shape[1] == C
    else:
        assert b.shape[0] == C
        J = b.shape[1]
    ti, tc = _tile(I, ti), _tile(C, tc)
    if b_chunked and mode == "nn":
        tj = _tile(n, tj)
    elif out_chunks:
        tj = _tile(J // out_chunks, tj)
    else:
        tj = _tile(J, tj)
    if b_chunked and mode == "nt":
        tc = _tile(n, tc)
    ni, nj, nc = I // ti, J // tj, C // tc

    if mode == "tn":
        a_spec = pl.BlockSpec((tc, ti), lambda i, j, c: (c, i))
        dims = (((0,), (0,)), ((), ()))
    else:
        a_spec = pl.BlockSpec((ti, tc), lambda i, j, c: (i, c))
        dims = (((1,), (0,)), ((), ())) if mode == "nn" else (((1,), (1,)), ((), ()))
    if mode == "nt":
        if b_chunked:
            cb = n // tc
            b_spec = pl.BlockSpec((None, tj, tc), lambda i, j, c: (c // cb, j, c % cb))
        else:
            b_spec = pl.BlockSpec((tj, tc), lambda i, j, c: (j, c))
    else:
        if b_chunked:
            jb = n // tj
            b_spec = pl.BlockSpec((None, tc, tj), lambda i, j, c: (j // jb, c, j % jb))
        else:
            b_spec = pl.BlockSpec((tc, tj), lambda i, j, c: (c, j))
    extra_specs = []
    for arr, kind in extras:
        if kind == "tile":
            assert arr.shape == (I, J), (arr.shape, I, J)
            extra_specs.append(pl.BlockSpec((ti, tj), lambda i, j, c: (i, j)))
        else:
            assert arr.shape == (1, J)
            extra_specs.append(pl.BlockSpec((1, tj), lambda i, j, c: (0, j)))
    if out_chunks:
        ob = (J // out_chunks) // tj
        out_spec = pl.BlockSpec((None, ti, tj), lambda i, j, c: (j // ob, i, j % ob))
        out_shape = [jax.ShapeDtypeStruct((out_chunks, I, J // out_chunks), d) for d in out_dtypes]
    else:
        out_spec = pl.BlockSpec((ti, tj), lambda i, j, c: (i, j))
        out_shape = [jax.ShapeDtypeStruct((I, J), d) for d in out_dtypes]
    ne, no = len(extras), len(out_dtypes)
    if epilogue is None:
        epilogue = lambda acc: (acc,)

    def body(a_ref, b_ref, *rest):
        extra_refs, out_refs = rest[:ne], rest[ne:ne + no]

        def finish(acc):
            outs = epilogue(acc, *[r[...] for r in extra_refs])
            for o_ref, val in zip(out_refs, outs):
                o_ref[...] = val.astype(o_ref.dtype)

        part = lax.dot_general(a_ref[...], b_ref[...], dims, preferred_element_type=F32)
        if nc == 1:
            finish(part)
        else:
            acc_ref = rest[-1]
            k = pl.program_id(2)

            @pl.when(k == 0)
            def _():
                acc_ref[...] = part

            @pl.when(k > 0)
            def _():
                acc_ref[...] += part

            @pl.when(k == nc - 1)
            def _():
                finish(acc_ref[...])

    outs = pl.pallas_call(
        body, name=name, grid=(ni, nj, nc),
        in_specs=[a_spec, b_spec] + extra_specs,
        out_specs=[out_spec] * no, out_shape=out_shape,
        scratch_shapes=[pltpu.VMEM((ti, tj), F32)] if nc > 1 else [],
        compiler_params=_params(("parallel", "parallel", "arbitrary")),
    )(a, b, *[e[0] for e in extras])
    return outs[0] if no == 1 else tuple(outs)


def _ew(fn, ins, outs, grid, name):
    n_in = len(ins)
    has_acc = any(o[3] for o in outs)
    assert not has_acc or len(grid) == 1

    def body(*refs):
        vals = fn(*[r[...] for r in refs[:n_in]])
        for o_ref, o, val in zip(refs[n_in:], outs, vals):
            if o[3]:
                step = pl.program_id(0)

                @pl.when(step == 0)
                def _(o_ref=o_ref, val=val):
                    o_ref[...] = val

                @pl.when(step > 0)
                def _(o_ref=o_ref, val=val):
                    o_ref[...] += val
            else:
                o_ref[...] = val.astype(o_ref.dtype)

    sem = ("arbitrary",) if has_acc else ("parallel",) * len(grid)
    res = pl.pallas_call(
        body, name=name, grid=grid,
        in_specs=[i[1] for i in ins], out_specs=[o[2] for o in outs],
        out_shape=[jax.ShapeDtypeStruct(o[0], o[1]) for o in outs],
        compiler_params=_params(sem),
    )(*[i[0] for i in ins])
    return res[0] if len(outs) == 1 else tuple(res)


def _rows(tr, w):
    return pl.BlockSpec((tr, w), lambda i: (i, 0))


def _bcast(w):
    return pl.BlockSpec((1, w), lambda i: (0, 0))


ROW_BLOCK_BYTES = 4 * 1024 * 1024


def _row_tile(S, width_bytes):
    tr = 512
    while tr > 16 and tr * width_bytes > ROW_BLOCK_BYTES:
        tr //= 2
    return min(tr, S)


def _rms_fwd(h, g, name):
    S, D = h.shape
    tr = _row_tile(S, D * 4)

    def fn(x, gg):
        r = lax.rsqrt(jnp.mean(x * x, axis=-1, keepdims=True) + NORM_EPS)
        return (x * r * gg,)

    return _ew(fn, [(h, _rows(tr, D)), (g, _bcast(D))], [((S, D), BF16, _rows(tr, D), False)],
               (S // tr,), name)


def _rms_bwd(x, g, dy, dres, name):
    S, D = x.shape
    tr = _row_tile(S, D * 4)

    def fn(xv, gg, dyv, dr):
        r = lax.rsqrt(jnp.mean(xv * xv, axis=-1, keepdims=True) + NORM_EPS)
        dyf = dyv.astype(F32)
        gy = dyf * gg
        dx = r * gy - xv * (r * r * r) * jnp.mean(xv * gy, axis=-1, keepdims=True)
        tot = dr + dx
        dg = jnp.sum(dyf * xv * r, axis=0, keepdims=True)
        return tot, tot, dg

    return _ew(fn, [(x, _rows(tr, D)), (g, _bcast(D)), (dy, _rows(tr, D)), (dres, _rows(tr, D))],
               [((S, D), F32, _rows(tr, D), False), ((S, D), BF16, _rows(tr, D), False),
                ((1, D), F32, _bcast(D), True)], (S // tr,), name)


def _loss_head(h, g, target, name):
    S, D = h.shape
    tr = _row_tile(S, D * 4)

    def fn(xv, gg, tgt):
        r = lax.rsqrt(jnp.mean(xv * xv, axis=-1, keepdims=True) + NORM_EPS)
        y = xv * r * gg
        e = y - tgt
        loss = 0.5 * jnp.sum(jnp.mean(e * e, axis=-1, keepdims=True), axis=0, keepdims=True)
        dy = e * (1.0 / D)
        gy = dy * gg
        dx = r * gy - xv * (r * r * r) * jnp.mean(xv * gy, axis=-1, keepdims=True)
        dg = jnp.sum(dy * xv * r, axis=0, keepdims=True)
        return dx, jnp.broadcast_to(loss, (1, LANES)), dg

    return _ew(fn, [(h, _rows(tr, D)), (g, _bcast(D)), (target, _rows(tr, D))],
               [((S, D), F32, _rows(tr, D), False), ((1, LANES), F32, _bcast(LANES), True),
                ((1, D), F32, _bcast(D), True)], (S // tr,), name)


def _rope_tables(S):
    inv = ROPE_THETA ** (-jnp.arange(ROPE_HALF, dtype=F32) / ROPE_HALF)
    ang = jnp.arange(S, dtype=F32)[:, None] * inv[None, :]
    cos, sin = jnp.cos(ang), jnp.sin(ang)
    rest = HEAD_DIM - 2 * ROPE_HALF
    ctab = jnp.concatenate([cos, cos, jnp.ones((S, rest), F32)], axis=1)
    stab = jnp.concatenate([-sin, sin, jnp.zeros((S, rest), F32)], axis=1)
    return ctab, stab


def _swap_halves(x):
    lane = lax.broadcasted_iota(jnp.int32, x.shape, 1)
    return jnp.where(lane < ROPE_HALF, pltpu.roll(x, HEAD_DIM - ROPE_HALF, 1), pltpu.roll(x, ROPE_HALF, 1))


def _rope(q_src, q_col0, k_src, k_col0, ctab, stab, sign, name):
    S = q_src.shape[0]
    tr = min(512, S)
    width = HEADS_PER_DIL * HEAD_DIM

    def fn(q, k, ct, st):
        outs = []
        for v in (q, k):
            heads = []
            for h in range(HEADS_PER_DIL):
                vf = v[:, h * HEAD_DIM:(h + 1) * HEAD_DIM].astype(F32)
                heads.append(vf * ct + sign * _swap_halves(vf) * st)
            outs.append(jnp.concatenate(heads, axis=1))
        return tuple(outs)

    group = lambda c0: pl.BlockSpec((tr, width), lambda i, g: (i, c0 // HEADS_PER_DIL + g))
    tab = pl.BlockSpec((tr, HEAD_DIM), lambda i, g: (i, 0))
    out = ((S, DIL_WIDTH), BF16, group(0), False)
    return _ew(fn, [(q_src, group(q_col0)), (k_src, group(k_col0)), (ctab, tab), (stab, tab)],
               [out, out], (S // tr, len(DILATIONS)), name)


SW_BLOCKS_PER_STEP = 16


def _to_strided(x):
    S = x.shape[0]
    parts = []
    for g, d in enumerate(DILATIONS):
        xg = x[:, g * 512:(g + 1) * 512].reshape(S // d, d, HEADS_PER_DIL, HEAD_DIM)
        parts.append(xg.transpose(1, 2, 0, 3).reshape(-1, BLOCK, HEAD_DIM))
    return jnp.concatenate(parts, axis=0)


def _from_strided(y, S):
    per = y.shape[0] // len(DILATIONS)
    parts = []
    for g, d in enumerate(DILATIONS):
        yg = y[g * per:(g + 1) * per].reshape(d, HEADS_PER_DIL, S // d, HEAD_DIM)
        parts.append(yg.transpose(2, 0, 1, 3).reshape(S, HEADS_PER_DIL * HEAD_DIM))
    return jnp.stack(parts, axis=0)


def _seq_blocks(b0, per_group):
    g = b0 // per_group
    n0 = per_group // HEADS_PER_DIL
    return jnp.where(g == 0, n0, jnp.where(g == 1, n0 // 4, n0 // 16))


def _sw_masks():
    qi = lax.broadcasted_iota(jnp.int32, (BLOCK, BLOCK), 0)
    ki = lax.broadcasted_iota(jnp.int32, (BLOCK, BLOCK), 1)
    return qi >= ki, qi <= ki


def _sw_fwd(q, k, v, name):
    NB = q.shape[0]
    T = SW_BLOCKS_PER_STEP
    per_group = NB // len(DILATIONS)
    nt = (((1,), (1,)), ((), ()))

    def body(q_ref, k_ref, v_ref, kp_ref, vp_ref, o_ref, lse_ref):
        b0 = pl.program_id(0) * T
        nseq = _seq_blocks(b0, per_group)
        cur_mask, prev_mask = _sw_masks()
        for t in range(T):
            has_prev = ((b0 + t) & (nseq - 1)) != 0
            qt = q_ref[t]
            kp = kp_ref[0] if t == 0 else k_ref[t - 1]
            vp = vp_ref[0] if t == 0 else v_ref[t - 1]
            s_c = lax.dot_general(qt, k_ref[t], nt, preferred_element_type=F32) * SCALE
            s_p = lax.dot_general(qt, kp, nt, preferred_element_type=F32) * SCALE
            s_c = jnp.where(cur_mask, s_c, NEG)
            s_p = jnp.where(prev_mask, s_p, NEG) + jnp.where(has_prev, 0.0, NEG)
            m = jnp.maximum(jnp.max(s_c, axis=-1, keepdims=True), jnp.max(s_p, axis=-1, keepdims=True))
            p_c = jnp.exp(s_c - m)
            p_p = jnp.exp(s_p - m)
            l = jnp.sum(p_c, axis=-1, keepdims=True) + jnp.sum(p_p, axis=-1, keepdims=True)
            o = (jnp.dot(p_c.astype(BF16), v_ref[t], preferred_element_type=F32)
                 + jnp.dot(p_p.astype(BF16), vp, preferred_element_type=F32))
            o_ref[t] = (o / l).astype(o_ref.dtype)
            lse_ref[t] = jnp.broadcast_to(m + jnp.log(l), (BLOCK, HEAD_DIM))

    tile = pl.BlockSpec((T, BLOCK, HEAD_DIM), lambda i: (i, 0, 0))
    before = pl.BlockSpec((1, BLOCK, HEAD_DIM), lambda i: (jnp.maximum(i * T - 1, 0), 0, 0))
    return pl.pallas_call(
        body, name=name, grid=(NB // T,),
        in_specs=[tile, tile, tile, before, before], out_specs=[tile, tile],
        out_shape=[jax.ShapeDtypeStruct(q.shape, BF16), jax.ShapeDtypeStruct(q.shape, F32)],
        compiler_params=_params(("parallel",)),
    )(q, k, v, k, v)


def _sw_bwd(q, k, v, do, lse, tt, name):
    NB = q.shape[0]
    T = SW_BLOCKS_PER_STEP
    per_group = NB // len(DILATIONS)
    nt = (((1,), (1,)), ((), ()))
    tn = (((0,), (0,)), ((), ()))

    def body(q_ref, k_ref, v_ref, do_ref, lse_ref, tt_ref, kp_ref, vp_ref, qn_ref, don_ref, lsen_ref,
             ttn_ref, dq_ref, dk_ref, dv_ref):
        b0 = pl.program_id(0) * T
        nseq = _seq_blocks(b0, per_group)
        cur_mask, prev_mask = _sw_masks()

        def probs(qq, kk, lse_b, mask, gate):
            s = lax.dot_general(qq, kk, nt, preferred_element_type=F32) * SCALE
            return jnp.exp(jnp.where(mask, s, NEG) + gate - lse_b)

        for t in range(T):
            has_prev = jnp.where(((b0 + t) & (nseq - 1)) != 0, 0.0, NEG)
            has_next = jnp.where(((b0 + t + 1) & (nseq - 1)) != 0, 0.0, NEG)
            last = t == T - 1
            qt, kt, vt, dot = q_ref[t], k_ref[t], v_ref[t], do_ref[t]
            kp = kp_ref[0] if t == 0 else k_ref[t - 1]
            vp = vp_ref[0] if t == 0 else v_ref[t - 1]
            qn = qn_ref[0] if last else q_ref[t + 1]
            don = don_ref[0] if last else do_ref[t + 1]
            lsen = lsen_ref[0] if last else lse_ref[t + 1]
            ttn = ttn_ref[0] if last else tt_ref[t + 1]
            p_cc = probs(qt, kt, lse_ref[t], cur_mask, 0.0)
            p_cp = probs(qt, kp, lse_ref[t], prev_mask, has_prev)
            p_nc = probs(qn, kt, lsen, prev_mask, has_next)
            ds_cc = p_cc * (lax.dot_general(dot, vt, nt, preferred_element_type=F32) + tt_ref[t])
            ds_cp = p_cp * (lax.dot_general(dot, vp, nt, preferred_element_type=F32) + tt_ref[t])
            ds_nc = p_nc * (lax.dot_general(don, vt, nt, preferred_element_type=F32) + ttn)
            ds_cc, ds_cp, ds_nc = ds_cc.astype(BF16), ds_cp.astype(BF16), ds_nc.astype(BF16)
            dq = (jnp.dot(ds_cc, kt, preferred_element_type=F32)
                  + jnp.dot(ds_cp, kp, preferred_element_type=F32))
            dk = (lax.dot_general(ds_cc, qt, tn, preferred_element_type=F32)
                  + lax.dot_general(ds_nc, qn, tn, preferred_element_type=F32))
            dv = (lax.dot_general(p_cc.astype(BF16), dot, tn, preferred_element_type=F32)
                  + lax.dot_general(p_nc.astype(BF16), don, tn, preferred_element_type=F32))
            dq_ref[t] = (dq * SCALE).astype(BF16)
            dk_ref[t] = (dk * SCALE).astype(BF16)
            dv_ref[t] = dv.astype(BF16)

    tile = pl.BlockSpec((T, BLOCK, HEAD_DIM), lambda i: (i, 0, 0))
    before = pl.BlockSpec((1, BLOCK, HEAD_DIM), lambda i: (jnp.maximum(i * T - 1, 0), 0, 0))
    after = pl.BlockSpec((1, BLOCK, HEAD_DIM), lambda i: (jnp.minimum(i * T + T, NB - 1), 0, 0))
    out = jax.ShapeDtypeStruct(q.shape, BF16)
    return pl.pallas_call(
        body, name=name, grid=(NB // T,),
        in_specs=[tile] * 6 + [before, before, after, after, after, after],
        out_specs=[tile] * 3, out_shape=[out] * 3,
        compiler_params=_params(("parallel",)),
    )(q, k, v, do, lse, tt, k, v, q, do, lse, tt)


def _group_softmax(lse):
    m = jnp.max(lse, axis=0, keepdims=True)
    e = jnp.exp(lse - m)
    return e / jnp.sum(e, axis=0, keepdims=True)


def _mix_fwd(o, lse, name):
    G, S, W = o.shape
    tr = min(256, S)
    blk = pl.BlockSpec((G, tr, W), lambda i: (0, i, 0))

    def fn(ov, lv):
        return (jnp.sum(_group_softmax(lv) * ov.astype(F32), axis=0),)

    return _ew(fn, [(o, blk), (lse, blk)], [((S, W), BF16, _rows(tr, W), False)], (S // tr,), name)


def _mix_bwd(dya, ya, o, lse, name):
    G, S, W = o.shape
    tr = min(256, S)
    blk = pl.BlockSpec((G, tr, HEAD_DIM), lambda i, h: (0, i, h))
    row = pl.BlockSpec((tr, HEAD_DIM), lambda i, h: (i, h))

    def fn(dy, yv, ov, lv):
        w = _group_softmax(lv)
        dyf = dy.astype(F32)
        inner = jnp.sum(dyf * yv.astype(F32), axis=-1, keepdims=True)
        return w * dyf[None], -w * inner[None]

    return _ew(fn, [(dya, row), (ya, row), (o, blk), (lse, blk)],
               [((G, S, W), BF16, blk, False), ((G, S, W), F32, blk, False)],
               (S // tr, HEADS_PER_DIL), name)


CUM_BLOCK = 256


def _split3(x):
    hi = x.astype(BF16)
    r = x - hi.astype(F32)
    mid = r.astype(BF16)
    lo = (r - mid.astype(F32)).astype(BF16)
    return hi, mid, lo


def _tri_matmul(tri, x):
    return sum(jnp.dot(tri, part, preferred_element_type=F32) for part in _split3(x))


def _log_sigmoid(x):
    return jnp.minimum(x, 0.0) - jnp.log(1.0 + jnp.exp(-jnp.abs(x)))


def _fox_prep(f, b, name):
    S = f.shape[0]
    tb = min(CUM_BLOCK, S)

    def body(f_ref, b_ref, c_ref, carry):
        @pl.when(pl.program_id(0) == 0)
        def _():
            carry[...] = jnp.zeros_like(carry)

        ls = _log_sigmoid(f_ref[...] + b_ref[...])
        r = lax.broadcasted_iota(jnp.int32, (tb, tb), 0)
        cidx = lax.broadcasted_iota(jnp.int32, (tb, tb), 1)
        tri = jnp.where(r >= cidx, 1.0, 0.0).astype(BF16)
        c_ref[...] = _tri_matmul(tri, ls) + carry[...]
        carry[...] += jnp.sum(ls, axis=0, keepdims=True)

    return pl.pallas_call(
        body, name=name, grid=(S // tb,),
        in_specs=[_rows(tb, LANES), _bcast(LANES)], out_specs=_rows(tb, LANES),
        out_shape=jax.ShapeDtypeStruct((S, LANES), F32),
        scratch_shapes=[pltpu.VMEM((1, LANES), F32)],
        compiler_params=_params(("arbitrary",)),
    )(f, b)


def _fox_prep_bwd(dc, f, b, name):
    S = f.shape[0]
    tb = min(CUM_BLOCK, S)
    nb = S // tb

    def body(dc_ref, f_ref, b_ref, df_ref, db_ref, carry):
        @pl.when(pl.program_id(0) == 0)
        def _():
            carry[...] = jnp.zeros_like(carry)
            db_ref[...] = jnp.zeros_like(db_ref)

        r = lax.broadcasted_iota(jnp.int32, (tb, tb), 0)
        cidx = lax.broadcasted_iota(jnp.int32, (tb, tb), 1)
        tri = jnp.where(r <= cidx, 1.0, 0.0).astype(BF16)
        dcv = dc_ref[...]
        dls = _tri_matmul(tri, dcv) + carry[...]
        carry[...] += jnp.sum(dcv, axis=0, keepdims=True)
        z = f_ref[...] + b_ref[...]
        df = dls * (1.0 / (1.0 + jnp.exp(z)))
        df_ref[...] = df
        db_ref[...] += jnp.sum(df, axis=0, keepdims=True)

    rev = pl.BlockSpec((tb, LANES), lambda i: (nb - 1 - i, 0))
    return pl.pallas_call(
        body, name=name, grid=(nb,),
        in_specs=[rev, rev, _bcast(LANES)], out_specs=[rev, _bcast(LANES)],
        out_shape=[jax.ShapeDtypeStruct((S, LANES), F32), jax.ShapeDtypeStruct((1, LANES), F32)],
        scratch_shapes=[pltpu.VMEM((1, LANES), F32)],
        compiler_params=_params(("arbitrary",)),
    )(dc, f, b)


FOX_Q_TILE = 256
_FOX_Q0 = N_DIL_HEADS
_FOX_K0 = N_HEADS + N_DIL_HEADS
_FOX_V0 = 2 * N_HEADS + N_DIL_HEADS


def _fox_scores(q, k, cq, ck, q0):
    nt = (((1,), (1,)), ((), ()))
    s = lax.dot_general(q, k, nt, preferred_element_type=F32) * SCALE + cq - ck
    qpos = q0 + lax.broadcasted_iota(jnp.int32, s.shape, 0)
    kpos = lax.broadcasted_iota(jnp.int32, s.shape, 1)
    return jnp.where(kpos <= qpos, s, NEG)


def _per_query_tile(nq, tq, fn):
    step = pl.program_id(1)
    for n in range(nq):
        @pl.when(step == n)
        def _(n=n):
            fn(n, pl.ds(0, (n + 1) * tq))


def _fox_fwd(z, cq, ck, name):
    S = z.shape[0]
    tq = min(FOX_Q_TILE, S)

    def body(q_ref, k_ref, v_ref, cq_ref, ck_ref, o_ref, lse_ref):
        def tile(n, keys):
            s = _fox_scores(q_ref[...], k_ref[keys, :], cq_ref[...], ck_ref[:, keys], n * tq)
            m = jnp.max(s, axis=-1, keepdims=True)
            p = jnp.exp(s - m)
            l = jnp.sum(p, axis=-1, keepdims=True)
            o = jnp.dot(p.astype(BF16), v_ref[keys, :], preferred_element_type=F32)
            o_ref[...] = (o / l).astype(o_ref.dtype)
            lse_ref[...] = m + jnp.log(l)

        _per_query_tile(S // tq, tq, tile)

    qblk = lambda c0: pl.BlockSpec((tq, HEAD_DIM), lambda h, i: (i, c0 + h))
    full = lambda c0: pl.BlockSpec((S, HEAD_DIM), lambda h, i: (0, c0 + h))
    col = pl.BlockSpec((None, tq, 1), lambda h, i: (h, i, 0))
    rowv = pl.BlockSpec((None, 1, S), lambda h, i: (h, 0, 0))
    return pl.pallas_call(
        body, name=name, grid=(N_FOX_HEADS, S // tq),
        in_specs=[qblk(_FOX_Q0), full(_FOX_K0), full(_FOX_V0), col, rowv],
        out_specs=[qblk(0), col],
        out_shape=[jax.ShapeDtypeStruct((S, FOX_WIDTH), BF16),
                   jax.ShapeDtypeStruct((N_FOX_HEADS, S, 1), F32)],
        compiler_params=_params(("parallel", "parallel")),
    )(z, z, z, cq, ck)


def _fox_bwd(z, cq, ck, lse, yb, dyb, name):
    S = z.shape[0]
    tq = min(FOX_Q_TILE, S)
    nq = S // tq
    nt = (((1,), (1,)), ((), ()))
    tn = (((0,), (0,)), ((), ()))

    def body(q_ref, k_ref, v_ref, cq_ref, ck_ref, lse_ref, o_ref, do_ref,
             dq_ref, dk_ref, dv_ref, dc_ref, dk_acc, dv_acc):
        i = pl.program_id(1)

        @pl.when(i == 0)
        def _():
            dk_acc[...] = jnp.zeros_like(dk_acc)
            dv_acc[...] = jnp.zeros_like(dv_acc)
            dc_ref[...] = jnp.zeros_like(dc_ref)

        def tile(n, keys):
            q, k, v, do = q_ref[...], k_ref[keys, :], v_ref[keys, :], do_ref[...]
            s = _fox_scores(q, k, cq_ref[...], ck_ref[:, keys], n * tq)
            p = jnp.exp(s - lse_ref[...])
            dp = lax.dot_general(do, v, nt, preferred_element_type=F32)
            ds = p * (dp - jnp.sum(p * dp, axis=-1, keepdims=True))
            dsb = ds.astype(BF16)
            dq_ref[...] = (jnp.dot(dsb, k, preferred_element_type=F32) * SCALE).astype(BF16)
            dk_acc[keys, :] += lax.dot_general(dsb, q, tn, preferred_element_type=F32) * SCALE
            dv_acc[keys, :] += lax.dot_general(p.astype(BF16), do, tn, preferred_element_type=F32)
            dc_ref[:, keys] -= jnp.sum(ds, axis=0, keepdims=True)

        _per_query_tile(nq, tq, tile)

        @pl.when(i == nq - 1)
        def _():
            dk_ref[...] = dk_acc[...].astype(BF16)
            dv_ref[...] = dv_acc[...].astype(BF16)

    qblk = lambda c0: pl.BlockSpec((tq, HEAD_DIM), lambda h, i: (i, c0 + h))
    full = lambda c0: pl.BlockSpec((S, HEAD_DIM), lambda h, i: (0, c0 + h))
    col = pl.BlockSpec((None, tq, 1), lambda h, i: (h, i, 0))
    rowv = pl.BlockSpec((None, 1, S), lambda h, i: (h, 0, 0))
    wide = jax.ShapeDtypeStruct((S, FOX_WIDTH), BF16)
    return pl.pallas_call(
        body, name=name, grid=(N_FOX_HEADS, nq),
        in_specs=[qblk(_FOX_Q0), full(_FOX_K0), full(_FOX_V0), col, rowv, col, qblk(0), qblk(0)],
        out_specs=[qblk(0), full(0), full(0), rowv],
        out_shape=[wide, wide, wide, jax.ShapeDtypeStruct((N_FOX_HEADS, 1, S), F32)],
        scratch_shapes=[pltpu.VMEM((S, HEAD_DIM), F32), pltpu.VMEM((S, HEAD_DIM), F32)],
        compiler_params=_params(("parallel", "arbitrary")),
    )(z, z, z, cq, ck, lse, yb, dyb)


def _sigmoid(x):
    return 1.0 / (1.0 + jnp.exp(-x))


def _merge_fwd(gates, a, bm, name):
    S, D = a.shape
    tr = _row_tile(S, D * 4)
    g1 = pl.BlockSpec((tr, D), lambda i: (i, 0))
    g2 = pl.BlockSpec((tr, D), lambda i: (i, 1))

    def fn(x1, x2, av, bv):
        return (x1.astype(F32) * av.astype(F32) + x2.astype(F32) * bv.astype(F32),)

    return _ew(fn, [(gates, g1), (gates, g2), (a, _rows(tr, D)), (bm, _rows(tr, D))],
               [((S, D), BF16, _rows(tr, D), False)], (S // tr,), name)


def _merge_bwd(dmerged, gates, a, bm, name):
    S, D = a.shape
    tr = _row_tile(S, D * 8)
    g1 = pl.BlockSpec((tr, D), lambda i: (i, 0))
    g2 = pl.BlockSpec((tr, D), lambda i: (i, 1))

    def fn(dm, x1, x2, av, bv):
        dm, x1, x2 = dm.astype(F32), x1.astype(F32), x2.astype(F32)
        dg1 = dm * av.astype(F32) * x1 * (1.0 - x1)
        dg2 = dm * bv.astype(F32) * x2 * (1.0 - x2)
        dgp = jnp.concatenate([dg1, dg2], axis=1)
        return dm * x1, dm * x2, dgp, jnp.sum(dgp, axis=0, keepdims=True)

    return _ew(fn, [(dmerged, _rows(tr, D)), (gates, g1), (gates, g2), (a, _rows(tr, D)), (bm, _rows(tr, D))],
               [((S, D), BF16, _rows(tr, D), False), ((S, D), BF16, _rows(tr, D), False),
                ((S, 2 * D), BF16, _rows(tr, 2 * D), False), ((1, 2 * D), F32, _bcast(2 * D), True)],
               (S // tr,), name)


def _ple_bwd(dh, pg, pe, name):
    S, D = dh.shape
    tr = _row_tile(S, D * 4)

    def fn(d, g, e):
        g, e = g.astype(F32), e.astype(F32)
        return d * g, d * e * g * (1.0 - g)

    spec = _rows(tr, D)
    return _ew(fn, [(dh, spec), (pg, spec), (pe, spec)],
               [((S, D), BF16, spec, False), ((S, D), BF16, spec, False)], (S // tr,), name)


def _position():
    x, y, c = lax.axis_index("x"), lax.axis_index("y"), lax.axis_index("c")
    chips = [(1 - x, y), (x, 1 - y), (1 - x, 1 - y)]
    return x, y, c, chips


def _remote(src, dst, send_sem, recv_sem, target):
    return pltpu.make_async_remote_copy(src_ref=src, dst_ref=dst, send_sem=send_sem, recv_sem=recv_sem,
                                        device_id=target, device_id_type=MESH)


HBM = pl.BlockSpec(memory_space=pltpu.HBM)
SEM = pl.BlockSpec(memory_space=pltpu.SEMAPHORE)
EFFECT = pltpu.SideEffectType.DATAFLOW_SIDE_EFFECTING


def _copies_start(plan, arrays, sem_shape, after, name):
    n = len(arrays)

    def body(*refs):
        send_sems, recv_sems, token = refs[n + 1], refs[n + 2], refs[-1]
        for send, _ in plan(refs[:n], send_sems, recv_sems):
            send.start()
        token[...] = jnp.zeros_like(token)

    outs = pl.pallas_call(
        body, name=name,
        out_shape=(pltpu.SemaphoreType.DMA(sem_shape), pltpu.SemaphoreType.DMA(sem_shape),
                   *[pltpu.HBM(a.shape, a.dtype) for a in arrays], jax.ShapeDtypeStruct((8, LANES), F32)),
        in_specs=[HBM] * n + [ANY],
        out_specs=(SEM, SEM, *[HBM] * n, pl.BlockSpec(memory_space=pltpu.VMEM)),
        input_output_aliases={a: 2 + a for a in range(n)},
        compiler_params=pltpu.CompilerParams(has_side_effects=EFFECT),
    )(*[pltpu.with_memory_space_constraint(a, pltpu.HBM) for a in arrays], after)
    return outs[0], outs[1], list(outs[2:2 + n]), outs[-1]


def _copies_wait(plan, send_sems, recv_sems, arrays, after, name):
    n = len(arrays)

    def body(*refs):
        for send, recv in plan(refs[:n], refs[n], refs[n + 1]):
            send.wait_send()
            recv.wait_recv()

    return list(pl.pallas_call(
        body, name=name,
        out_shape=[pltpu.HBM(a.shape, a.dtype) for a in arrays],
        in_specs=[HBM] * n + [SEM, SEM, ANY], out_specs=[HBM] * n,
        input_output_aliases={a: a for a in range(n)},
        compiler_params=pltpu.CompilerParams(has_side_effects=EFFECT),
    )(*arrays, send_sems, recv_sems, after))


def _gather_ici_plan(refs, send_sems, recv_sems):
    x, y, c, chips = _position()
    plan = []
    for a, ref in enumerate(refs):
        rh = ref.shape[1] // 2
        mine = ref.at[2 * x + y, pl.ds(c * rh, rh)]
        for j, (cx, cy) in enumerate(chips):
            landed = ref.at[2 * cx + cy, pl.ds(c * rh, rh)]
            plan.append((_remote(mine, mine, send_sems.at[3 * a + j], recv_sems.at[3 * a + j], (cx, cy, c)),
                         _remote(landed, landed, send_sems.at[3 * a + j], recv_sems.at[3 * a + j], (cx, cy, c))))
    return plan


def _gather_d2d_plan(refs, send_sems, recv_sems):
    x, y, c, chips = _position()
    sibling = (x, y, 1 - c)
    plan = []
    for a, ref in enumerate(refs):
        rh = ref.shape[1] // 2
        for j, (cx, cy) in enumerate(chips):
            landed = ref.at[2 * cx + cy, pl.ds(c * rh, rh)]
            theirs = ref.at[2 * cx + cy, pl.ds((1 - c) * rh, rh)]
            plan.append((_remote(landed, landed, send_sems.at[3 * a + j], recv_sems.at[3 * a + j], sibling),
                         _remote(theirs, theirs, send_sems.at[3 * a + j], recv_sems.at[3 * a + j], sibling)))
    return plan


def _scatter_ici_plan(refs, send_sems, recv_sems):
    x, y, c, chips = _position()
    n = len(refs) // 2
    plan = []
    for a in range(n):
        for j, (cx, cy) in enumerate(chips):
            cp = _remote(refs[a].at[2 * cx + cy], refs[n + a].at[j], send_sems.at[3 * a + j], recv_sems.at[3 * a + j],
                         (cx, cy, c))
            plan.append((cp, cp))
    return plan


def _place_shard(w, layer, me, after, name):
    _, r, cc = w.shape
    tr = _row_tile(r, cc * 4)

    def body(me_ref, w_ref, after_ref, o_ref):
        o_ref[...] = w_ref[...].astype(o_ref.dtype)

    return pl.pallas_call(
        body, name=name,
        grid_spec=pltpu.PrefetchScalarGridSpec(
            num_scalar_prefetch=1, grid=(r // tr,),
            in_specs=[pl.BlockSpec((None, tr, cc), lambda i, me_ref: (layer, i, 0)), ANY],
            out_specs=pl.BlockSpec((None, tr, cc), lambda i, me_ref: (me_ref[0], i, 0))),
        out_shape=jax.ShapeDtypeStruct((N_CHIPS, r, cc), BF16),
        compiler_params=_params(("parallel",)),
    )(me, w, after)


def _pair_plan(refs, send_sems, recv_sems):
    x, y, c, _ = _position()
    n = len(refs) // 2
    plan = []
    for a in range(n):
        rh = refs[a].shape[1] // 2
        cp = _remote(refs[a].at[:, pl.ds((1 - c) * rh, rh)], refs[n + a], send_sems.at[a], recv_sems.at[a],
                     (x, y, 1 - c))
        plan.append((cp, cp))
    return plan


def _pair_sum(mine, theirs, c, name):
    nch, rh, cc = theirs.shape
    tr = _row_tile(rh, cc * 4)
    nb = rh // tr

    def body(c_ref, m_ref, t_ref, o_ref):
        o_ref[...] = (m_ref[...].astype(F32) + t_ref[...].astype(F32)).astype(o_ref.dtype)

    return pl.pallas_call(
        body, name=name,
        grid_spec=pltpu.PrefetchScalarGridSpec(
            num_scalar_prefetch=1, grid=(nch, nb),
            in_specs=[pl.BlockSpec((1, tr, cc), lambda k, i, c_ref: (k, c_ref[0] * nb + i, 0)),
                      pl.BlockSpec((1, tr, cc), lambda k, i, c_ref: (k, i, 0))],
            out_specs=pl.BlockSpec((1, tr, cc), lambda k, i, c_ref: (k, i, 0))),
        out_shape=jax.ShapeDtypeStruct(theirs.shape, BF16),
        compiler_params=_params(("parallel", "parallel")),
    )(c, mine, theirs)


def _chip_sum(own, others, me, c, total, layer, depth, name):
    _, rh, cc = own.shape
    tr = _row_tile(rh, cc * 4)
    nb = rh // tr
    chained = total is not None

    def body(me_ref, c_ref, o_ref, r_ref, *rest):
        g_ref = rest[-1]
        g_ref[...] = (o_ref[0].astype(F32) + r_ref[0].astype(F32)) + (r_ref[1].astype(F32) + r_ref[2].astype(F32))

    in_specs = [pl.BlockSpec((1, tr, cc), lambda i, me_ref, c_ref: (me_ref[0], i, 0)),
                pl.BlockSpec((3, tr, cc), lambda i, me_ref, c_ref: (0, i, 0))]
    args = [me, c, own, others]
    if chained:
        in_specs.append(ANY)
        args.append(total)
    return pl.pallas_call(
        body, name=name,
        grid_spec=pltpu.PrefetchScalarGridSpec(
            num_scalar_prefetch=2, grid=(nb,), in_specs=in_specs,
            out_specs=pl.BlockSpec((None, tr, cc), lambda i, me_ref, c_ref: (layer, c_ref[0] * nb + i, 0))),
        out_shape=jax.ShapeDtypeStruct((depth, 2 * rh, cc), F32),
        input_output_aliases={4: 0} if chained else {},
        compiler_params=_params(("parallel",)),
    )(*args)


def _half_exchange(totals, layer, name):
    n = len(totals)

    def body(*refs):
        outs = refs[n:2 * n]
        send_sems, recv_sems = refs[2 * n:]
        x, y, c, _ = _position()
        copies = []
        for a in range(n):
            rh = outs[a].shape[1] // 2
            mine = outs[a].at[layer, pl.ds(c * rh, rh)]
            cp = _remote(mine, mine, send_sems.at[a], recv_sems.at[a], (x, y, 1 - c))
            cp.start()
            copies.append(cp)
        for a, cp in enumerate(copies):
            rh = outs[a].shape[1] // 2
            theirs = outs[a].at[layer, pl.ds((1 - c) * rh, rh)]
            cp.wait_send()
            _remote(theirs, theirs, send_sems.at[a], recv_sems.at[a], (x, y, 1 - c)).wait_recv()

    return pl.pallas_call(
        body, name=name, in_specs=[ANY] * n, out_specs=[ANY] * n,
        out_shape=[jax.ShapeDtypeStruct(t.shape, t.dtype) for t in totals],
        input_output_aliases={a: a for a in range(n)},
        scratch_shapes=[pltpu.SemaphoreType.DMA((n,)), pltpu.SemaphoreType.DMA((n,))],
    )(*totals)


def _allgather_devices(v, name):
    m_per, n = v.shape

    def body(x_ref, out_ref, send_sems, recv_sems, local_sem):
        x, y, c, chips = _position()
        me, sibling = (x, y, c), (x, y, 1 - c)

        def rows(px, py, pc):
            return out_ref.at[pl.ds((4 * px + 2 * py + pc) * m_per, m_per), :]

        def copy(k, block, to, src=None):
            return _remote(rows(*block) if src is None else src, rows(*block), send_sems.at[k], recv_sems.at[k], to)

        mine = pltpu.make_async_copy(x_ref, rows(*me), local_sem)
        mine.start()
        first = [copy(0, me, sibling, src=x_ref)]
        first += [copy(1 + j, me, (*chip, c), src=x_ref) for j, chip in enumerate(chips)]
        for cp in first:
            cp.start()
        passed = [copy(4 + j, (*chip, c), sibling) for j, chip in enumerate(chips)]
        for j, chip in enumerate(chips):
            copy(1 + j, (*chip, c), me).wait_recv()
            passed[j].start()
        copy(0, sibling, me).wait_recv()
        for j, chip in enumerate(chips):
            copy(4 + j, (*chip, 1 - c), me).wait_recv()
        for cp in first + passed:
            cp.wait_send()
        mine.wait()

    vm = pl.BlockSpec(memory_space=pltpu.VMEM)
    return pl.pallas_call(
        body, name=name, in_specs=[vm], out_specs=vm,
        out_shape=jax.ShapeDtypeStruct((8 * m_per, n), v.dtype),
        scratch_shapes=[pltpu.SemaphoreType.DMA((7,)), pltpu.SemaphoreType.DMA((7,)), pltpu.SemaphoreType.DMA],
    )(v)


def _adamw_math(w, g, m, v):
    m = ADAM_B1 * m + (1.0 - ADAM_B1) * g
    v = ADAM_B2 * v + (1.0 - ADAM_B2) * (g * g)
    m_hat = m / (1.0 - ADAM_B1 ** ADAM_STEP)
    v_hat = v / (1.0 - ADAM_B2 ** ADAM_STEP)
    delta = -ADAM_LR * (m_hat / (jnp.sqrt(v_hat) + ADAM_EPS) + ADAM_WD * w)
    return delta, m, v


def _adamw(w, g, m, v, lo, hi, prev, after, name):
    depth, r, cc = w.shape
    tr = _row_tile(r, cc * 4 * 2)
    spec = pl.BlockSpec((1, tr, cc), lambda l, i: (lo + l, i, 0))

    def body(w_ref, g_ref, m_ref, v_ref, *rest):
        outs = rest[-4:]
        gv = g_ref[...]
        for o_ref, val in zip(outs, (gv,) + _adamw_math(w_ref[...], gv, m_ref[...], v_ref[...])):
            o_ref[...] = val

    prev = list(prev) if prev is not None else []
    return tuple(pl.pallas_call(
        body, name=name, grid=(hi - lo, r // tr),
        in_specs=[spec] * 4 + [ANY] * (1 + len(prev)), out_specs=[spec] * 4,
        out_shape=[jax.ShapeDtypeStruct(w.shape, F32)] * 4,
        input_output_aliases={5 + k: k for k in range(len(prev))},
        compiler_params=_params(("parallel", "parallel")),
    )(w, g, m, v, after, *prev))


def _adamw_small(w, parts, m, v, name):
    M = w.shape[0]

    def body(w_ref, p_ref, m_ref, v_ref, g_ref, d_ref, nm_ref, nv_ref):
        g = p_ref[pl.ds(0, M), :]
        for k in range(1, 8):
            g = g + p_ref[pl.ds(k * M, M), :]
        d, nm, nv = _adamw_math(w_ref[...], g, m_ref[...], v_ref[...])
        g_ref[...] = g
        d_ref[...] = d
        nm_ref[...] = nm
        nv_ref[...] = nv

    vm = pl.BlockSpec(memory_space=pltpu.VMEM)
    return pl.pallas_call(
        body, name=name, in_specs=[vm] * 4, out_specs=[vm] * 4,
        out_shape=[jax.ShapeDtypeStruct(w.shape, F32)] * 4,
    )(w, parts, m, v)


def _layer_fwd(h0, p_l, W, small, tabs, after_mlp):
    S, D = h0.shape
    ctab, stab = tabs
    u = _rms_fwd(h0, small["g_mix"], "rms_mix")
    z = _mm(u, W["w_qkv"], mode="nn", name="mm_qkv")
    f = _mm(u, W["w_f"], mode="nn", name="mm_f", out_dtypes=(F32,))
    gates = _mm(u, W["w_gate"], mode="nn", name="mm_gate", b_chunked=True, extras=[(small["b_gate"], "row")],
                epilogue=lambda acc, b: (_sigmoid(acc + b),))
    qr, kr = _rope(z, 0, z, N_HEADS, ctab, stab, 1.0, "rope_fwd")
    qs, ks, vs = _to_strided(qr), _to_strided(kr), _to_strided(z[:, 2 * ATTN_WIDTH:2 * ATTN_WIDTH + DIL_WIDTH])
    o_s, lse_s = _sw_fwd(qs, ks, vs, "sw_fwd")
    o_g, lse_g = _from_strided(o_s, S), _from_strided(lse_s, S)
    ya = _mix_fwd(o_g, lse_g, "mix_fwd")
    a = _mm(ya, W["w_br_a"], mode="nn", name="mm_br_a", b_chunked=True)
    cum = _fox_prep(f, small["b_f"], "fox_prep")
    cq = cum[:, :N_FOX_HEADS].T[:, :, None]
    ck = cum[:, :N_FOX_HEADS].T[:, None, :]
    yb, lse_f = _fox_fwd(z, cq, ck, "fox_fwd")
    bm = _mm(yb, W["w_br_b"], mode="nn", name="mm_br_b", b_chunked=True)
    merged = _merge_fwd(gates, a, bm, "merge_fwd")
    h1 = _mm(merged, W["w_o"], mode="nn", name="mm_o", out_dtypes=(F32,), extras=[(h0, "tile")],
             epilogue=lambda acc, r: (acc + r,), tj=512)
    m = _rms_fwd(h1, small["g_mlp"], "rms_mlp")
    ra, act = _mm(m, W["w_up"], mode="nn", name="mm_up", b_chunked=True, out_dtypes=(BF16, BF16),
                  epilogue=lambda acc: (jnp.maximum(acc, 0.0), jnp.square(jnp.maximum(acc, 0.0))))
    h2 = _mm(act, W["w_down"], mode="nn", name="mm_down", out_dtypes=(F32,), extras=[(h1, "tile")],
             epilogue=lambda acc, r: (acc + r,), ti=1024, tj=512, tc=4096)
    token = after_mlp(h2)
    n = _rms_fwd(h2, small["g_ple"] if token is None else small["g_ple"] + token[0, 0], "rms_ple")
    pg = _mm(n, W["w_ple_gate"], mode="nn", name="mm_ple_gate", epilogue=lambda acc: (_sigmoid(acc),))
    h3, pe = _mm(p_l, W["w_ple"], mode="nn", name="mm_ple", b_chunked=True, out_dtypes=(F32, BF16),
                 extras=[(h2, "tile"), (pg, "tile")], tj=256,
                 epilogue=lambda acc, r, g: (r + g.astype(F32) * acc, acc))
    saved = dict(h0=h0, u=u, z=z, f=f, gates=gates, qs=qs, ks=ks, vs=vs, lse_s=lse_s, o_g=o_g, lse_g=lse_g,
                 ya=ya, a=a, cq=cq, ck=ck, yb=yb, lse_f=lse_f, bm=bm, merged=merged, h1=h1, m=m, ra=ra,
                 act=act, h2=h2, n=n, pg=pg, pe=pe, p_l=p_l)
    return h3, saved


def _after(hooks, name, small_value, *args):
    token = hooks[name](*args) if name in hooks else None
    return small_value if token is None else small_value + token[0, 0]


def _layer_bwd(dh3, sv, W, small, tabs, hooks):
    S, D = dh3.shape
    ctab, stab = tabs
    gw, gs = {}, {}
    tn = functools.partial(_mm, mode="tn", ti=1024, tj=2048)
    dpe, dpg = _ple_bwd(dh3, sv["pg"], sv["pe"], "ple_bwd")
    gw["w_ple"] = tn(sv["p_l"], dpe, name="dw_ple", out_chunks=N_CHIPS)
    gw["w_ple_gate"] = tn(sv["n"], dpg, name="dw_ple_gate").reshape(N_CHIPS, D // N_CHIPS, D)
    dn = _mm(dpg, W["w_ple_gate"], mode="nt", name="mm_dn")
    dh2, dh2b, gs["g_ple"] = _rms_bwd(sv["h2"], small["g_ple"], dn, dh3, "rms_ple_bwd")
    da = _mm(dh2b, W["w_down"], mode="nt", name="mm_dact", extras=[(sv["ra"], "tile")],
             epilogue=lambda acc, r: (acc * (2.0 * r.astype(F32)),))
    FF = da.shape[1]
    g_mlp = _after(hooks, "mlp_grad", small["g_mlp"], da)
    gw["w_down"] = tn(sv["act"], dh2b, name="dw_down").reshape(N_CHIPS, FF // N_CHIPS, D)
    gw["w_up"] = tn(sv["m"], da, name="dw_up", out_chunks=N_CHIPS)
    g_mlp = _after(hooks, "mlp_weights", g_mlp, dict(gw))
    dm = _mm(da, W["w_up"], mode="nt", name="mm_dm", b_chunked=True)
    dh1, dh1b, gs["g_mlp"] = _rms_bwd(sv["h1"], g_mlp, dm, dh2, "rms_mlp_bwd")
    dmerged = _mm(dh1b, W["w_o"], mode="nt", name="mm_dmerged")
    b_f = _after(hooks, "merge_grad", small["b_f"], dmerged)
    gw["w_o"] = tn(sv["merged"], dh1b, name="dw_o").reshape(N_CHIPS, D // N_CHIPS, D)
    d_a, d_b, dgp, gs["b_gate"] = _merge_bwd(dmerged, sv["gates"], sv["a"], sv["bm"], "merge_bwd")
    gw["w_gate"] = tn(sv["u"], dgp, name="dw_gate", out_chunks=N_CHIPS)
    gw["w_br_a"] = tn(sv["ya"], d_a, name="dw_br_a", out_chunks=N_CHIPS, tj=512)
    gw["w_br_b"] = tn(sv["yb"], d_b, name="dw_br_b", out_chunks=N_CHIPS, tj=512)
    dya = _mm(d_a, W["w_br_a"], mode="nt", name="mm_dya", b_chunked=True)
    dyb = _mm(d_b, W["w_br_b"], mode="nt", name="mm_dyb", b_chunked=True)
    z = sv["z"]
    dq_f, dk_f, dv_f, dck = _fox_bwd(z, sv["cq"], sv["ck"], sv["lse_f"], sv["yb"], dyb, "fox_bwd")
    dc = jnp.pad(dck[:, 0, :].T, ((0, 0), (0, LANES - N_FOX_HEADS)))
    df, dbf = _fox_prep_bwd(dc, sv["f"], b_f, "fox_prep_bwd")
    gs["b_f"] = dbf[:, :N_FOX_HEADS]
    lane = jnp.arange(LANES)[None, :] < N_FOX_HEADS
    dzf = jnp.where(lane, df, 0.0).astype(BF16)
    do_g, tt_g = _mix_bwd(dya, sv["ya"], sv["o_g"], sv["lse_g"], "mix_bwd")
    do_s = _to_strided(do_g.transpose(1, 0, 2).reshape(S, DIL_WIDTH))
    tt_s = _to_strided(tt_g.transpose(1, 0, 2).reshape(S, DIL_WIDTH))
    dq_s, dk_s, dv_s = _sw_bwd(sv["qs"], sv["ks"], sv["vs"], do_s, sv["lse_s"], tt_s, "sw_bwd")
    unstride = lambda t: _from_strided(t, S).transpose(1, 0, 2).reshape(S, DIL_WIDTH)
    dq_a, dk_a = _rope(unstride(dq_s), 0, unstride(dk_s), 0, ctab, stab, -1.0, "rope_bwd")
    dz = jnp.concatenate([dq_a, dq_f, dk_a, dk_f, unstride(dv_s), dv_f], axis=1)
    g_qkv = tn(sv["u"], dz, name="dw_qkv")
    g_f = tn(sv["u"], dzf, name="dw_f", tj=128)
    cols = W["w_in_cols"]
    g_in = jnp.concatenate([g_qkv, g_f[:, :N_FOX_HEADS]], axis=1)
    gw["w_in"] = jnp.stack([g_in[:, k * cols:(k + 1) * cols] for k in range(N_CHIPS)])
    du = _mm(dzf, W["w_f"], mode="nt", name="mm_du_f", out_dtypes=(F32,))
    du = _mm(dgp, W["w_gate"], mode="nt", name="mm_du_gate", b_chunked=True, out_dtypes=(F32,),
             extras=[(du, "tile")], epilogue=lambda acc, r: (acc + r,), tj=512)
    du = _mm(dz, W["w_qkv"], mode="nt", name="mm_du_qkv", extras=[(du, "tile")],
             epilogue=lambda acc, r: (acc + r,), tj=512)
    dh0, _, gs["g_mix"] = _rms_bwd(sv["h0"], small["g_mix"], du, dh1, "rms_mix_bwd")
    return dh0, gw, gs


BIG = ("w_in", "w_gate", "w_br_a", "w_br_b", "w_o", "w_up", "w_down", "w_ple", "w_ple_gate")
EARLY = ("w_ple", "w_ple_gate", "w_down", "w_up")
SMALL = ("g_mix", "b_f", "b_gate", "g_mlp", "g_ple", "g_final")
ORDER = ("g_mix", "w_in", "b_f", "w_gate", "b_gate", "w_br_a", "w_br_b", "w_o", "g_mlp", "w_up", "w_down",
         "g_ple", "w_ple", "w_ple_gate", "g_final")


def _gathered_layer_weights(full, D):
    W = {}
    for name in ("w_gate", "w_br_a", "w_br_b", "w_up", "w_ple"):
        W[name] = full[name]
    for name in ("w_o", "w_down", "w_ple_gate"):
        t = full[name]
        W[name] = t.reshape(t.shape[0] * t.shape[1], t.shape[2])
    w_in = full["w_in"]
    cols = w_in.shape[2]
    w_in = jnp.concatenate([w_in[k] for k in range(N_CHIPS)], axis=1)
    W["w_qkv"] = w_in[:, :3 * ATTN_WIDTH]
    W["w_f"] = jnp.pad(w_in[:, 3 * ATTN_WIDTH:], ((0, 0), (0, LANES - N_FOX_HEADS)))
    W["w_in_cols"] = cols
    return W


def _pack_rows(vals):
    flat = jnp.concatenate([v.reshape(-1) for v in vals])
    rows = -(-flat.shape[0] // (8 * LANES)) * 8
    return jnp.pad(flat, (0, rows * LANES - flat.shape[0])).reshape(rows, LANES)


def _unpack_rows(packed, shapes):
    flat = packed.reshape(-1)
    out, pos = [], 0
    for s in shapes:
        size = 1
        for dim in s:
            size *= dim
        out.append(flat[pos:pos + size].reshape(s))
        pos += size
    return out


def _local_step(x, p, small_w, comm, loss_target):
    depth = p.shape[0]
    S, D = x.shape
    tabs = _rope_tables(S)
    h = x
    saved, weights, smalls = [], [], []
    state, _ = comm["gather_start"](0, x)
    full = comm["gather_finish"](comm["gather_mid"](state, x)[0], x)
    for l in range(depth):
        nxt = [None]
        g_mix = small_w["g_mix"][l][None]
        if l + 1 < depth:
            nxt[0], token = comm["gather_start"](l + 1, full["w_ple"])
            g_mix = g_mix + token[0, 0]

        def after_mlp(h2):
            if nxt[0] is None:
                return None
            nxt[0], token = comm["gather_mid"](nxt[0], h2)
            return token

        W = _gathered_layer_weights(full, D)
        sm = dict(g_mix=g_mix, g_mlp=small_w["g_mlp"][l][None],
                  g_ple=small_w["g_ple"][l][None], b_gate=small_w["b_gate"][l][None],
                  b_f=jnp.pad(small_w["b_f"][l][None], ((0, 0), (0, LANES - N_FOX_HEADS))))
        h, sv = _layer_fwd(h, p[l].astype(BF16), W, sm, tabs, after_mlp)
        if nxt[0] is not None:
            full = comm["gather_finish"](nxt[0], h)
        saved.append(sv)
        weights.append(W)
        smalls.append(sm)
    dh, loss_row, dg_final = _loss_head(h, small_w["g_final"][None], loss_target, "loss_head")
    gss = [None] * depth
    pending, token = [None], None
    for l in reversed(range(depth)):
        sm = smalls[l]
        if token is not None:
            sm = {**sm, "g_ple": sm["g_ple"] + token[0, 0]}

        def after_mlp_grad(da):
            if pending[0] is None:
                return None
            pending[0] = comm["reduce_mid"](pending[0], da)
            return pending[0][-1]

        hooks = dict(mlp_grad=after_mlp_grad)
        if l == 0 and "reduce_early_begin" in comm:
            hooks.update(mlp_weights=comm["reduce_early_begin"], merge_grad=comm["reduce_early_mid"])
        dh, gw, gss[l] = _layer_bwd(dh, saved[l], weights[l], sm, tabs, hooks)
        if pending[0] is not None:
            comm["reduce_end"](pending[0], dh)
        if l > 0:
            pending[0], token = comm["reduce_begin"](l, gw, dh)
    return loss_row, dh, gss, dg_final, gw


def _device_comm(w, m, v, depth, c_arr, me_arr):
    n = len(BIG)
    totals = {name: None for name in BIG}
    placed, results = [], {}

    def gather_start(l, after):
        if l == 0:
            placed.append([_place_shard(w[name], 0, me_arr, me_arr, "place_shard") for name in BIG])
        send, recv, bufs, token = _copies_start(_gather_ici_plan, placed[l], (3 * n,), after, f"gather_ici_start_{l}")
        if l == 0:
            placed.extend([_place_shard(w[name], k, me_arr, token, "place_shard") for name in BIG]
                          for k in range(1, depth))
        return (l, send, recv, bufs), token

    def gather_mid(state, after):
        l, send, recv, bufs = state
        if l == 0:
            after = placed[-1][-1]
        bufs = _copies_wait(_gather_ici_plan, send, recv, bufs, after, f"gather_ici_wait_{l}")
        send, recv, bufs, token = _copies_start(_gather_d2d_plan, bufs, (3 * n,), after, f"gather_d2d_start_{l}")
        return (l, send, recv, bufs), token

    def gather_finish(state, after):
        l, send, recv, bufs = state
        return dict(zip(BIG, _copies_wait(_gather_d2d_plan, send, recv, bufs, after, f"gather_d2d_wait_{l}")))

    def reduce_begin(l, gw, after, names=BIG, tag=""):
        grads = [gw[name] for name in names]
        landing = [lax.empty((g.shape[0], g.shape[1] // 2, g.shape[2]), g.dtype) for g in grads]
        send, recv, arrays, token = _copies_start(_pair_plan, grads + landing, (len(names),), after,
                                                  f"pair_start_{l}{tag}")
        return (l, tag, names, send, recv, arrays), token

    def reduce_mid(state, after):
        l, tag, names, send, recv, arrays = state
        k = len(names)
        arrays = _copies_wait(_pair_plan, send, recv, arrays, after, f"pair_wait_{l}{tag}")
        sums = [_pair_sum(g, t, c_arr, "rs_pair_sum") for g, t in zip(arrays[:k], arrays[k:])]
        landing = [lax.empty((3,) + s.shape[1:], s.dtype) for s in sums]
        send, recv, arrays, token = _copies_start(_scatter_ici_plan, sums + landing, (3 * k,), sums[0],
                                                  f"scatter_ici_start_{l}{tag}")
        return l, tag, names, send, recv, arrays, token

    def reduce_end(state, after):
        l, tag, names, send, recv, arrays, _ = state
        k = len(names)
        arrays = _copies_wait(_scatter_ici_plan, send, recv, arrays, after, f"scatter_ici_wait_{l}{tag}")
        done = [_chip_sum(s, o, me_arr, c_arr, totals[name], l, depth, "rs_chip_sum")
                for name, s, o in zip(names, arrays[:k], arrays[k:])]
        totals.update(zip(names, _half_exchange(done, l, "rs_half_exchange")))

    early = [None]

    def reduce_early_begin(gw):
        early[0], token = reduce_begin(0, gw, gw[EARLY[-1]], EARLY, "a")
        return token

    def reduce_early_mid(after):
        early[0] = reduce_mid(early[0], after)
        return early[0][-1]

    def reduce_last(gw, after):
        upper = {}

        def adamw_upper(names, token):
            for name in names:
                if depth > 1:
                    upper[name] = _adamw(w[name], totals[name], m[name], v[name], 1, depth, None, token,
                                         "adamw_upper")
                    token = upper[name][0]
            return token

        def adamw_first(names, token):
            for name in names:
                results[name] = _adamw(w[name], totals[name], m[name], v[name], 0, 1, upper.get(name), token,
                                       "adamw_first")
                token = results[name][0]
            return token

        late = BIG if early[0] is None else tuple(name for name in BIG if name not in EARLY)
        state, token = reduce_begin(0, gw, after, late, "b")
        state = reduce_mid(state, adamw_upper(("w_up",), token))
        token = adamw_upper([name for name in BIG if name != "w_up"], state[-1])
        if early[0] is not None:
            reduce_end(early[0], token)
            token = adamw_first(EARLY, totals[EARLY[0]])
        reduce_end(state, token)
        adamw_first(late, totals[late[0]])

    comm = dict(gather_start=gather_start, gather_mid=gather_mid, gather_finish=gather_finish,
                reduce_begin=reduce_begin, reduce_mid=reduce_mid, reduce_end=reduce_end, reduce_last=reduce_last,
                reduce_early_begin=reduce_early_begin, reduce_early_mid=reduce_early_mid)
    return comm, results


def kernel(x, p, g_mix, w_in, b_f, w_gate, b_gate, w_br_a, w_br_b, w_o, g_mlp, w_up, w_down, g_ple, w_ple, w_ple_gate, g_final, loss_target, m_g_mix, m_w_in, m_b_f, m_w_gate, m_b_gate, m_w_br_a, m_w_br_b, m_w_o, m_g_mlp, m_w_up, m_w_down, m_g_ple, m_w_ple, m_w_ple_gate, m_g_final, v_g_mix, v_w_in, v_b_f, v_w_gate, v_b_gate, v_w_br_a, v_w_br_b, v_w_o, v_g_mlp, v_w_up, v_w_down, v_g_ple, v_w_ple, v_w_ple_gate, v_g_final):
    w = dict(g_mix=g_mix, w_in=w_in, b_f=b_f, w_gate=w_gate, b_gate=b_gate, w_br_a=w_br_a, w_br_b=w_br_b,
             w_o=w_o, g_mlp=g_mlp, w_up=w_up, w_down=w_down, g_ple=g_ple, w_ple=w_ple, w_ple_gate=w_ple_gate,
             g_final=g_final)
    m = dict(g_mix=m_g_mix, w_in=m_w_in, b_f=m_b_f, w_gate=m_w_gate, b_gate=m_b_gate, w_br_a=m_w_br_a,
             w_br_b=m_w_br_b, w_o=m_w_o, g_mlp=m_g_mlp, w_up=m_w_up, w_down=m_w_down, g_ple=m_g_ple,
             w_ple=m_w_ple, w_ple_gate=m_w_ple_gate, g_final=m_g_final)
    v = dict(g_mix=v_g_mix, w_in=v_w_in, b_f=v_b_f, w_gate=v_w_gate, b_gate=v_b_gate, w_br_a=v_w_br_a,
             w_br_b=v_w_br_b, w_o=v_w_o, g_mlp=v_g_mlp, w_up=v_w_up, w_down=v_w_down, g_ple=v_g_ple,
             w_ple=v_w_ple, w_ple_gate=v_w_ple_gate, g_final=v_g_final)
    depth = p.shape[0]
    cx, cy, cc = lax.axis_index("x"), lax.axis_index("y"), lax.axis_index("c")
    c_arr = jnp.reshape(cc, (1,)).astype(jnp.int32)
    me_arr = jnp.reshape(2 * cx + cy, (1,)).astype(jnp.int32)

    comm, big_out = _device_comm(w, m, v, depth, c_arr, me_arr)
    loss_row, grad_x, gss, dg_final, gw0 = _local_step(x[0], p[:, 0], w, comm, loss_target[0])

    small_grads = [jnp.stack([gss[l][n][0] for l in range(depth)]) for n in SMALL[:-1]] + [dg_final[0]]
    shapes = [w[n].shape for n in SMALL]
    parts = _allgather_devices(_pack_rows(small_grads), "allgather_small")
    packed = _adamw_small(_pack_rows([w[n] for n in SMALL]), parts, _pack_rows([m[n] for n in SMALL]),
                          _pack_rows([v[n] for n in SMALL]), "adamw_small")
    small_out = {n: vals for n, vals in zip(SMALL, zip(*[_unpack_rows(t, shapes) for t in packed]))}
    comm["reduce_last"](gw0, packed[0])

    loss = lax.psum(loss_row[0, 0], ("x", "y", "c"))
    out = {**big_out, **small_out}
    return (loss, grad_x[None], *[out[n][0] for n in ORDER], *[out[n][1] for n in ORDER],
            *[out[n][2] for n in ORDER], *[out[n][3] for n in ORDER])
```

```python
import functools

import jax
import jax.numpy as jnp
from jax import lax
from jax.experimental import pallas as pl
from jax.experimental.pallas import tpu as pltpu

F32 = jnp.float32
BF16 = jnp.bfloat16

HEAD_DIM = 128
N_HEADS = 16
N_DIL_HEADS = 12
N_FOX_HEADS = 4
HEADS_PER_DIL = 4
DILATIONS = (1, 4, 16)
BLOCK = 128
ATTN_WIDTH = N_HEADS * HEAD_DIM
DIL_WIDTH = N_DIL_HEADS * HEAD_DIM
FOX_WIDTH = N_FOX_HEADS * HEAD_DIM
ROPE_THETA = 500000.0
ROPE_HALF = 16
NORM_EPS = 1e-6
SCALE = HEAD_DIM ** -0.5
NEG = -1e30

ADAM_LR = 0.001
ADAM_B1 = 0.9
ADAM_B2 = 0.999
ADAM_EPS = 1e-08
ADAM_WD = 0.01
ADAM_STEP = 10

N_CHIPS = 4
V7X_VMEM_LIMIT_BYTES = 56 * 1024 * 1024
LANES = 128
MESH = pl.DeviceIdType.MESH
ANY = pl.BlockSpec(memory_space=pl.ANY)


def _params(sem):
    return pltpu.CompilerParams(dimension_semantics=sem, vmem_limit_bytes=V7X_VMEM_LIMIT_BYTES)


def _tile(n, pref):
    if n <= pref:
        return n
    t = (pref // LANES) * LANES
    while t > LANES and n % t:
        t -= LANES
    assert n % t == 0, (n, pref)
    return t


def _mm(a, b, *, mode, name, out_dtypes=(BF16,), epilogue=None, extras=(), b_chunked=False,
        out_chunks=0, b_cols=None, ti=2048, tj=512, tc=2048):
    if mode == "tn":
        C, I = a.shape
    else:
        I, C = a.shape
    if b_chunked:
        nch, d0, n = b.shape
        if mode == "nn":
            assert d0 == C
            J = nch * n
        else:
            assert mode == "nt" and nch * n == C
            J = d0
    elif mode == "nt":
        J = b.shape[0]
        assert (b.shape[1] if b_cols is None else b_cols) == C
    else:
        assert b.shape[0] == C
        J = b.shape[1] if b_cols is None else b_cols
    ti, tc = _tile(I, ti), _tile(C, tc)
    if b_chunked and mode == "nn":
        tj = _tile(n, tj)
    elif out_chunks:
        tj = _tile(J // out_chunks, tj)
    else:
        tj = _tile(J, tj)
    if b_chunked and mode == "nt":
        tc = _tile(n, tc)
    ni, nj, nc = I // ti, J // tj, C // tc

    if mode == "tn":
        a_spec = pl.BlockSpec((tc, ti), lambda i, j, c: (c, i))
        dims = (((0,), (0,)), ((), ()))
    else:
        a_spec = pl.BlockSpec((ti, tc), lambda i, j, c: (i, c))
        dims = (((1,), (0,)), ((), ())) if mode == "nn" else (((1,), (1,)), ((), ()))
    if mode == "nt":
        if b_chunked:
            cb = n // tc
            b_spec = pl.BlockSpec((None, tj, tc), lambda i, j, c: (c // cb, j, c % cb))
        else:
            b_spec = pl.BlockSpec((tj, tc), lambda i, j, c: (j, c))
    else:
        if b_chunked:
            jb = n // tj
            b_spec = pl.BlockSpec((None, tc, tj), lambda i, j, c: (j // jb, c, j % jb))
        else:
            b_spec = pl.BlockSpec((tc, tj), lambda i, j, c: (c, j))
    extra_specs = []
    for arr, kind in extras:
        if kind == "tile":
            assert arr.shape == (I, J), (arr.shape, I, J)
            extra_specs.append(pl.BlockSpec((ti, tj), lambda i, j, c: (i, j)))
        else:
            assert arr.shape == (1, J)
            extra_specs.append(pl.BlockSpec((1, tj), lambda i, j, c: (0, j)))
    if out_chunks:
        ob = (J // out_chunks) // tj
        out_spec = pl.BlockSpec((None, ti, tj), lambda i, j, c: (j // ob, i, j % ob))
        out_shape = [jax.ShapeDtypeStruct((out_chunks, I, J // out_chunks), d) for d in out_dtypes]
    else:
        out_spec = pl.BlockSpec((ti, tj), lambda i, j, c: (i, j))
        out_shape = [jax.ShapeDtypeStruct((I, J), d) for d in out_dtypes]
    ne, no = len(extras), len(out_dtypes)
    if epilogue is None:
        epilogue = lambda acc: (acc,)

    def body(a_ref, b_ref, *rest):
        extra_refs, out_refs = rest[:ne], rest[ne:ne + no]

        def finish(acc):
            outs = epilogue(acc, *[r[...] for r in extra_refs])
            for o_ref, val in zip(out_refs, outs):
                o_ref[...] = val.astype(o_ref.dtype)

        part = lax.dot_general(a_ref[...], b_ref[...], dims, preferred_element_type=F32)
        if nc == 1:
            finish(part)
        else:
            acc_ref = rest[-1]
            k = pl.program_id(2)

            @pl.when(k == 0)
            def _():
                acc_ref[...] = part

            @pl.when(k > 0)
            def _():
                acc_ref[...] += part

            @pl.when(k == nc - 1)
            def _():
                finish(acc_ref[...])

    outs = pl.pallas_call(
        body, name=name, grid=(ni, nj, nc),
        in_specs=[a_spec, b_spec] + extra_specs,
        out_specs=[out_spec] * no, out_shape=out_shape,
        scratch_shapes=[pltpu.VMEM((ti, tj), F32)] if nc > 1 else [],
        compiler_params=_params(("parallel", "parallel", "arbitrary")),
    )(a, b, *[e[0] for e in extras])
    return outs[0] if no == 1 else tuple(outs)


def _ew(fn, ins, outs, grid, name):
    n_in = len(ins)
    has_acc = any(o[3] for o in outs)
    assert not has_acc or len(grid) == 1

    def body(*refs):
        vals = fn(*[r[...] for r in refs[:n_in]])
        for o_ref, o, val in zip(refs[n_in:], outs, vals):
            if o[3]:
                step = pl.program_id(0)

                @pl.when(step == 0)
                def _(o_ref=o_ref, val=val):
                    o_ref[...] = val

                @pl.when(step > 0)
                def _(o_ref=o_ref, val=val):
                    o_ref[...] += val
            else:
                o_ref[...] = val.astype(o_ref.dtype)

    sem = ("arbitrary",) if has_acc else ("parallel",) * len(grid)
    res = pl.pallas_call(
        body, name=name, grid=grid,
        in_specs=[i[1] for i in ins], out_specs=[o[2] for o in outs],
        out_shape=[jax.ShapeDtypeStruct(o[0], o[1]) for o in outs],
        compiler_params=_params(sem),
    )(*[i[0] for i in ins])
    return res[0] if len(outs) == 1 else tuple(res)


def _rows(tr, w):
    return pl.BlockSpec((tr, w), lambda i: (i, 0))


def _bcast(w):
    return pl.BlockSpec((1, w), lambda i: (0, 0))


ROW_BLOCK_BYTES = 4 * 1024 * 1024


def _row_tile(S, width_bytes):
    tr = 512
    while tr > 16 and tr * width_bytes > ROW_BLOCK_BYTES:
        tr //= 2
    return min(tr, S)


def _rms_fwd(h, g, name):
    S, D = h.shape
    tr = _row_tile(S, D * 4)

    def fn(x, gg):
        r = lax.rsqrt(jnp.mean(x * x, axis=-1, keepdims=True) + NORM_EPS)
        return (x * r * gg,)

    return _ew(fn, [(h, _rows(tr, D)), (g, _bcast(D))], [((S, D), BF16, _rows(tr, D), False)],
               (S // tr,), name)


def _rms_bwd(x, g, dy, dres, name):
    S, D = x.shape
    tr = _row_tile(S, D * 4)

    def fn(xv, gg, dyv, dr):
        r = lax.rsqrt(jnp.mean(xv * xv, axis=-1, keepdims=True) + NORM_EPS)
        dyf = dyv.astype(F32)
        gy = dyf * gg
        dx = r * gy - xv * (r * r * r) * jnp.mean(xv * gy, axis=-1, keepdims=True)
        tot = dr + dx
        dg = jnp.sum(dyf * xv * r, axis=0, keepdims=True)
        return tot, tot, dg

    return _ew(fn, [(x, _rows(tr, D)), (g, _bcast(D)), (dy, _rows(tr, D)), (dres, _rows(tr, D))],
               [((S, D), F32, _rows(tr, D), False), ((S, D), BF16, _rows(tr, D), False),
                ((1, D), F32, _bcast(D), True)], (S // tr,), name)


def _loss_head(h, g, target, name):
    S, D = h.shape
    tr = _row_tile(S, D * 4)

    def fn(xv, gg, tgt):
        r = lax.rsqrt(jnp.mean(xv * xv, axis=-1, keepdims=True) + NORM_EPS)
        y = xv * r * gg
        e = y - tgt
        loss = 0.5 * jnp.sum(jnp.mean(e * e, axis=-1, keepdims=True), axis=0, keepdims=True)
        dy = e * (1.0 / D)
        gy = dy * gg
        dx = r * gy - xv * (r * r * r) * jnp.mean(xv * gy, axis=-1, keepdims=True)
        dg = jnp.sum(dy * xv * r, axis=0, keepdims=True)
        return dx, jnp.broadcast_to(loss, (1, LANES)), dg

    return _ew(fn, [(h, _rows(tr, D)), (g, _bcast(D)), (target, _rows(tr, D))],
               [((S, D), F32, _rows(tr, D), False), ((1, LANES), F32, _bcast(LANES), True),
                ((1, D), F32, _bcast(D), True)], (S // tr,), name)


def _rope_tables(S):
    inv = ROPE_THETA ** (-jnp.arange(ROPE_HALF, dtype=F32) / ROPE_HALF)
    ang = jnp.arange(S, dtype=F32)[:, None] * inv[None, :]
    cos, sin = jnp.cos(ang), jnp.sin(ang)
    rest = HEAD_DIM - 2 * ROPE_HALF
    ctab = jnp.concatenate([cos, cos, jnp.ones((S, rest), F32)], axis=1)
    stab = jnp.concatenate([-sin, sin, jnp.zeros((S, rest), F32)], axis=1)
    return ctab, stab


def _swap_halves(x):
    lane = lax.broadcasted_iota(jnp.int32, x.shape, 1)
    return jnp.where(lane < ROPE_HALF, pltpu.roll(x, HEAD_DIM - ROPE_HALF, 1), pltpu.roll(x, ROPE_HALF, 1))


def _rope(q_src, q_col0, k_src, k_col0, ctab, stab, sign, name):
    S = q_src.shape[0]
    tr = min(512, S)
    width = HEADS_PER_DIL * HEAD_DIM

    def fn(q, k, ct, st):
        outs = []
        for v in (q, k):
            heads = []
            for h in range(HEADS_PER_DIL):
                vf = v[:, h * HEAD_DIM:(h + 1) * HEAD_DIM].astype(F32)
                heads.append(vf * ct + sign * _swap_halves(vf) * st)
            outs.append(jnp.concatenate(heads, axis=1))
        return tuple(outs)

    group = lambda c0: pl.BlockSpec((tr, width), lambda i, g: (i, c0 // HEADS_PER_DIL + g))
    tab = pl.BlockSpec((tr, HEAD_DIM), lambda i, g: (i, 0))
    out = ((S, DIL_WIDTH), BF16, group(0), False)
    return _ew(fn, [(q_src, group(q_col0)), (k_src, group(k_col0)), (ctab, tab), (stab, tab)],
               [out, out], (S // tr, len(DILATIONS)), name)


SW_BLOCKS_PER_STEP = 16


def _to_strided(x):
    S = x.shape[0]
    parts = []
    for g, d in enumerate(DILATIONS):
        xg = x[:, g * 512:(g + 1) * 512].reshape(S // d, d, HEADS_PER_DIL, HEAD_DIM)
        parts.append(xg.transpose(1, 2, 0, 3).reshape(-1, BLOCK, HEAD_DIM))
    return jnp.concatenate(parts, axis=0)


def _from_strided(y, S):
    per = y.shape[0] // len(DILATIONS)
    parts = []
    for g, d in enumerate(DILATIONS):
        yg = y[g * per:(g + 1) * per].reshape(d, HEADS_PER_DIL, S // d, HEAD_DIM)
        parts.append(yg.transpose(2, 0, 1, 3).reshape(S, HEADS_PER_DIL * HEAD_DIM))
    return jnp.stack(parts, axis=0)


def _seq_blocks(b0, per_group):
    g = b0 // per_group
    n0 = per_group // HEADS_PER_DIL
    return jnp.where(g == 0, n0, jnp.where(g == 1, n0 // 4, n0 // 16))


def _sw_masks():
    qi = lax.broadcasted_iota(jnp.int32, (BLOCK, BLOCK), 0)
    ki = lax.broadcasted_iota(jnp.int32, (BLOCK, BLOCK), 1)
    return qi >= ki, qi <= ki


def _sw_fwd(q, k, v, name):
    NB = q.shape[0]
    T = SW_BLOCKS_PER_STEP
    per_group = NB // len(DILATIONS)
    nt = (((1,), (1,)), ((), ()))

    def body(q_ref, k_ref, v_ref, kp_ref, vp_ref, o_ref, lse_ref):
        b0 = pl.program_id(0) * T
        nseq = _seq_blocks(b0, per_group)
        cur_mask, prev_mask = _sw_masks()
        for t in range(T):
            has_prev = ((b0 + t) & (nseq - 1)) != 0
            qt = q_ref[t]
            kp = kp_ref[0] if t == 0 else k_ref[t - 1]
            vp = vp_ref[0] if t == 0 else v_ref[t - 1]
            s_c = lax.dot_general(qt, k_ref[t], nt, preferred_element_type=F32) * SCALE
            s_p = lax.dot_general(qt, kp, nt, preferred_element_type=F32) * SCALE
            s_c = jnp.where(cur_mask, s_c, NEG)
            s_p = jnp.where(prev_mask, s_p, NEG) + jnp.where(has_prev, 0.0, NEG)
            m = jnp.maximum(jnp.max(s_c, axis=-1, keepdims=True), jnp.max(s_p, axis=-1, keepdims=True))
            p_c = jnp.exp(s_c - m)
            p_p = jnp.exp(s_p - m)
            l = jnp.sum(p_c, axis=-1, keepdims=True) + jnp.sum(p_p, axis=-1, keepdims=True)
            o = (jnp.dot(p_c.astype(BF16), v_ref[t], preferred_element_type=F32)
                 + jnp.dot(p_p.astype(BF16), vp, preferred_element_type=F32))
            o_ref[t] = (o / l).astype(o_ref.dtype)
            lse_ref[t] = jnp.broadcast_to(m + jnp.log(l), (BLOCK, HEAD_DIM))

    tile = pl.BlockSpec((T, BLOCK, HEAD_DIM), lambda i: (i, 0, 0))
    before = pl.BlockSpec((1, BLOCK, HEAD_DIM), lambda i: (jnp.maximum(i * T - 1, 0), 0, 0))
    return pl.pallas_call(
        body, name=name, grid=(NB // T,),
        in_specs=[tile, tile, tile, before, before], out_specs=[tile, tile],
        out_shape=[jax.ShapeDtypeStruct(q.shape, BF16), jax.ShapeDtypeStruct(q.shape, F32)],
        compiler_params=_params(("parallel",)),
    )(q, k, v, k, v)


def _sw_bwd(q, k, v, do, lse, tt, name):
    NB = q.shape[0]
    T = SW_BLOCKS_PER_STEP
    per_group = NB // len(DILATIONS)
    nt = (((1,), (1,)), ((), ()))
    tn = (((0,), (0,)), ((), ()))

    def body(q_ref, k_ref, v_ref, do_ref, lse_ref, tt_ref, kp_ref, vp_ref, qn_ref, don_ref, lsen_ref,
             ttn_ref, dq_ref, dk_ref, dv_ref):
        b0 = pl.program_id(0) * T
        nseq = _seq_blocks(b0, per_group)
        cur_mask, prev_mask = _sw_masks()

        def probs(qq, kk, lse_b, mask, gate):
            s = lax.dot_general(qq, kk, nt, preferred_element_type=F32) * SCALE
            return jnp.exp(jnp.where(mask, s, NEG) + gate - lse_b)

        for t in range(T):
            has_prev = jnp.where(((b0 + t) & (nseq - 1)) != 0, 0.0, NEG)
            has_next = jnp.where(((b0 + t + 1) & (nseq - 1)) != 0, 0.0, NEG)
            last = t == T - 1
            qt, kt, vt, dot = q_ref[t], k_ref[t], v_ref[t], do_ref[t]
            kp = kp_ref[0] if t == 0 else k_ref[t - 1]
            vp = vp_ref[0] if t == 0 else v_ref[t - 1]
            qn = qn_ref[0] if last else q_ref[t + 1]
            don = don_ref[0] if last else do_ref[t + 1]
            lsen = lsen_ref[0] if last else lse_ref[t + 1]
            ttn = ttn_ref[0] if last else tt_ref[t + 1]
            p_cc = probs(qt, kt, lse_ref[t], cur_mask, 0.0)
            p_cp = probs(qt, kp, lse_ref[t], prev_mask, has_prev)
            p_nc = probs(qn, kt, lsen, prev_mask, has_next)
            ds_cc = p_cc * (lax.dot_general(dot, vt, nt, preferred_element_type=F32) + tt_ref[t])
            ds_cp = p_cp * (lax.dot_general(dot, vp, nt, preferred_element_type=F32) + tt_ref[t])
            ds_nc = p_nc * (lax.dot_general(don, vt, nt, preferred_element_type=F32) + ttn)
            ds_cc, ds_cp, ds_nc = ds_cc.astype(BF16), ds_cp.astype(BF16), ds_nc.astype(BF16)
            dq = (jnp.dot(ds_cc, kt, preferred_element_type=F32)
                  + jnp.dot(ds_cp, kp, preferred_element_type=F32))
            dk = (lax.dot_general(ds_cc, qt, tn, preferred_element_type=F32)
                  + lax.dot_general(ds_nc, qn, tn, preferred_element_type=F32))
            dv = (lax.dot_general(p_cc.astype(BF16), dot, tn, preferred_element_type=F32)
                  + lax.dot_general(p_nc.astype(BF16), don, tn, preferred_element_type=F32))
            dq_ref[t] = (dq * SCALE).astype(BF16)
            dk_ref[t] = (dk * SCALE).astype(BF16)
            dv_ref[t] = dv.astype(BF16)

    tile = pl.BlockSpec((T, BLOCK, HEAD_DIM), lambda i: (i, 0, 0))
    before = pl.BlockSpec((1, BLOCK, HEAD_DIM), lambda i: (jnp.maximum(i * T - 1, 0), 0, 0))
    after = pl.BlockSpec((1, BLOCK, HEAD_DIM), lambda i: (jnp.minimum(i * T + T, NB - 1), 0, 0))
    out = jax.ShapeDtypeStruct(q.shape, BF16)
    return pl.pallas_call(
        body, name=name, grid=(NB // T,),
        in_specs=[tile] * 6 + [before, before, after, after, after, after],
        out_specs=[tile] * 3, out_shape=[out] * 3,
        compiler_params=_params(("parallel",)),
    )(q, k, v, do, lse, tt, k, v, q, do, lse, tt)


def _group_softmax(lse):
    m = jnp.max(lse, axis=0, keepdims=True)
    e = jnp.exp(lse - m)
    return e / jnp.sum(e, axis=0, keepdims=True)


def _mix_fwd(o, lse, name):
    G, S, W = o.shape
    tr = min(256, S)
    blk = pl.BlockSpec((G, tr, W), lambda i: (0, i, 0))

    def fn(ov, lv):
        return (jnp.sum(_group_softmax(lv) * ov.astype(F32), axis=0),)

    return _ew(fn, [(o, blk), (lse, blk)], [((S, W), BF16, _rows(tr, W), False)], (S // tr,), name)


def _mix_bwd(dya, ya, o, lse, name):
    G, S, W = o.shape
    tr = min(256, S)
    blk = pl.BlockSpec((G, tr, HEAD_DIM), lambda i, h: (0, i, h))
    row = pl.BlockSpec((tr, HEAD_DIM), lambda i, h: (i, h))

    def fn(dy, yv, ov, lv):
        w = _group_softmax(lv)
        dyf = dy.astype(F32)
        inner = jnp.sum(dyf * yv.astype(F32), axis=-1, keepdims=True)
        return w * dyf[None], -w * inner[None]

    return _ew(fn, [(dya, row), (ya, row), (o, blk), (lse, blk)],
               [((G, S, W), BF16, blk, False), ((G, S, W), F32, blk, False)],
               (S // tr, HEADS_PER_DIL), name)


CUM_BLOCK = 256


def _split3(x):
    hi = x.astype(BF16)
    r = x - hi.astype(F32)
    mid = r.astype(BF16)
    lo = (r - mid.astype(F32)).astype(BF16)
    return hi, mid, lo


def _tri_matmul(tri, x):
    return sum(jnp.dot(tri, part, preferred_element_type=F32) for part in _split3(x))


def _log_sigmoid(x):
    return jnp.minimum(x, 0.0) - jnp.log(1.0 + jnp.exp(-jnp.abs(x)))


def _fox_prep(f, b, name):
    S = f.shape[0]
    tb = min(CUM_BLOCK, S)

    def body(f_ref, b_ref, c_ref, carry):
        @pl.when(pl.program_id(0) == 0)
        def _():
            carry[...] = jnp.zeros_like(carry)

        ls = _log_sigmoid(f_ref[...] + b_ref[...])
        r = lax.broadcasted_iota(jnp.int32, (tb, tb), 0)
        cidx = lax.broadcasted_iota(jnp.int32, (tb, tb), 1)
        tri = jnp.where(r >= cidx, 1.0, 0.0).astype(BF16)
        c_ref[...] = _tri_matmul(tri, ls) + carry[...]
        carry[...] += jnp.sum(ls, axis=0, keepdims=True)

    return pl.pallas_call(
        body, name=name, grid=(S // tb,),
        in_specs=[_rows(tb, LANES), _bcast(LANES)], out_specs=_rows(tb, LANES),
        out_shape=jax.ShapeDtypeStruct((S, LANES), F32),
        scratch_shapes=[pltpu.VMEM((1, LANES), F32)],
        compiler_params=_params(("arbitrary",)),
    )(f, b)


def _fox_prep_bwd(dc, f, b, name):
    S = f.shape[0]
    tb = min(CUM_BLOCK, S)
    nb = S // tb

    def body(dc_ref, f_ref, b_ref, df_ref, db_ref, carry):
        @pl.when(pl.program_id(0) == 0)
        def _():
            carry[...] = jnp.zeros_like(carry)
            db_ref[...] = jnp.zeros_like(db_ref)

        r = lax.broadcasted_iota(jnp.int32, (tb, tb), 0)
        cidx = lax.broadcasted_iota(jnp.int32, (tb, tb), 1)
        tri = jnp.where(r <= cidx, 1.0, 0.0).astype(BF16)
        dcv = dc_ref[...]
        dls = _tri_matmul(tri, dcv) + carry[...]
        carry[...] += jnp.sum(dcv, axis=0, keepdims=True)
        z = f_ref[...] + b_ref[...]
        df = dls * (1.0 / (1.0 + jnp.exp(z)))
        df_ref[...] = df
        db_ref[...] += jnp.sum(df, axis=0, keepdims=True)

    rev = pl.BlockSpec((tb, LANES), lambda i: (nb - 1 - i, 0))
    return pl.pallas_call(
        body, name=name, grid=(nb,),
        in_specs=[rev, rev, _bcast(LANES)], out_specs=[rev, _bcast(LANES)],
        out_shape=[jax.ShapeDtypeStruct((S, LANES), F32), jax.ShapeDtypeStruct((1, LANES), F32)],
        scratch_shapes=[pltpu.VMEM((1, LANES), F32)],
        compiler_params=_params(("arbitrary",)),
    )(dc, f, b)


FOX_Q_TILE = 256
_FOX_Q0 = N_DIL_HEADS
_FOX_K0 = N_HEADS + N_DIL_HEADS
_FOX_V0 = 2 * N_HEADS + N_DIL_HEADS


def _fox_scores(q, k, cq, ck, q0):
    nt = (((1,), (1,)), ((), ()))
    s = lax.dot_general(q, k, nt, preferred_element_type=F32) * SCALE + cq - ck
    qpos = q0 + lax.broadcasted_iota(jnp.int32, s.shape, 0)
    kpos = lax.broadcasted_iota(jnp.int32, s.shape, 1)
    return jnp.where(kpos <= qpos, s, NEG)


def _per_query_tile(nq, tq, fn):
    step = pl.program_id(1)
    for n in range(nq):
        @pl.when(step == n)
        def _(n=n):
            fn(n, pl.ds(0, (n + 1) * tq))


def _fox_fwd(z, cq, ck, name):
    S = z.shape[0]
    tq = min(FOX_Q_TILE, S)

    def body(q_ref, k_ref, v_ref, cq_ref, ck_ref, o_ref, lse_ref):
        def tile(n, keys):
            s = _fox_scores(q_ref[...], k_ref[keys, :], cq_ref[...], ck_ref[:, keys], n * tq)
            m = jnp.max(s, axis=-1, keepdims=True)
            p = jnp.exp(s - m)
            l = jnp.sum(p, axis=-1, keepdims=True)
            o = jnp.dot(p.astype(BF16), v_ref[keys, :], preferred_element_type=F32)
            o_ref[...] = (o / l).astype(o_ref.dtype)
            lse_ref[...] = m + jnp.log(l)

        _per_query_tile(S // tq, tq, tile)

    qblk = lambda c0: pl.BlockSpec((tq, HEAD_DIM), lambda h, i: (i, c0 + h))
    full = lambda c0: pl.BlockSpec((S, HEAD_DIM), lambda h, i: (0, c0 + h))
    col = pl.BlockSpec((None, tq, 1), lambda h, i: (h, i, 0))
    rowv = pl.BlockSpec((None, 1, S), lambda h, i: (h, 0, 0))
    return pl.pallas_call(
        body, name=name, grid=(N_FOX_HEADS, S // tq),
        in_specs=[qblk(_FOX_Q0), full(_FOX_K0), full(_FOX_V0), col, rowv],
        out_specs=[qblk(0), col],
        out_shape=[jax.ShapeDtypeStruct((S, FOX_WIDTH), BF16),
                   jax.ShapeDtypeStruct((N_FOX_HEADS, S, 1), F32)],
        compiler_params=_params(("parallel", "parallel")),
    )(z, z, z, cq, ck)


def _fox_bwd(z, cq, ck, lse, yb, dyb, name):
    S = z.shape[0]
    tq = min(FOX_Q_TILE, S)
    nq = S // tq
    nt = (((1,), (1,)), ((), ()))
    tn = (((0,), (0,)), ((), ()))

    def body(q_ref, k_ref, v_ref, cq_ref, ck_ref, lse_ref, o_ref, do_ref,
             dq_ref, dk_ref, dv_ref, dc_ref, dk_acc, dv_acc):
        i = pl.program_id(1)

        @pl.when(i == 0)
        def _():
            dk_acc[...] = jnp.zeros_like(dk_acc)
            dv_acc[...] = jnp.zeros_like(dv_acc)
            dc_ref[...] = jnp.zeros_like(dc_ref)

        def tile(n, keys):
            q, k, v, do = q_ref[...], k_ref[keys, :], v_ref[keys, :], do_ref[...]
            s = _fox_scores(q, k, cq_ref[...], ck_ref[:, keys], n * tq)
            p = jnp.exp(s - lse_ref[...])
            dp = lax.dot_general(do, v, nt, preferred_element_type=F32)
            ds = p * (dp - jnp.sum(p * dp, axis=-1, keepdims=True))
            dsb = ds.astype(BF16)
            dq_ref[...] = (jnp.dot(dsb, k, preferred_element_type=F32) * SCALE).astype(BF16)
            dk_acc[keys, :] += lax.dot_general(dsb, q, tn, preferred_element_type=F32) * SCALE
            dv_acc[keys, :] += lax.dot_general(p.astype(BF16), do, tn, preferred_element_type=F32)
            dc_ref[:, keys] -= jnp.sum(ds, axis=0, keepdims=True)

        _per_query_tile(nq, tq, tile)

        @pl.when(i == nq - 1)
        def _():
            dk_ref[...] = dk_acc[...].astype(BF16)
            dv_ref[...] = dv_acc[...].astype(BF16)

    qblk = lambda c0: pl.BlockSpec((tq, HEAD_DIM), lambda h, i: (i, c0 + h))
    full = lambda c0: pl.BlockSpec((S, HEAD_DIM), lambda h, i: (0, c0 + h))
    col = pl.BlockSpec((None, tq, 1), lambda h, i: (h, i, 0))
    rowv = pl.BlockSpec((None, 1, S), lambda h, i: (h, 0, 0))
    wide = jax.ShapeDtypeStruct((S, FOX_WIDTH), BF16)
    return pl.pallas_call(
        body, name=name, grid=(N_FOX_HEADS, nq),
        in_specs=[qblk(_FOX_Q0), full(_FOX_K0), full(_FOX_V0), col, rowv, col, qblk(0), qblk(0)],
        out_specs=[qblk(0), full(0), full(0), rowv],
        out_shape=[wide, wide, wide, jax.ShapeDtypeStruct((N_FOX_HEADS, 1, S), F32)],
        scratch_shapes=[pltpu.VMEM((S, HEAD_DIM), F32), pltpu.VMEM((S, HEAD_DIM), F32)],
        compiler_params=_params(("parallel", "arbitrary")),
    )(z, z, z, cq, ck, lse, yb, dyb)


def _sigmoid(x):
    return 1.0 / (1.0 + jnp.exp(-x))


def _merge_fwd(gates, a, bm, name):
    S, D = a.shape
    tr = _row_tile(S, D * 4)
    g1 = pl.BlockSpec((tr, D), lambda i: (i, 0))
    g2 = pl.BlockSpec((tr, D), lambda i: (i, 1))

    def fn(x1, x2, av, bv):
        return (x1.astype(F32) * av.astype(F32) + x2.astype(F32) * bv.astype(F32),)

    return _ew(fn, [(gates, g1), (gates, g2), (a, _rows(tr, D)), (bm, _rows(tr, D))],
               [((S, D), BF16, _rows(tr, D), False)], (S // tr,), name)


def _merge_bwd(dmerged, gates, a, bm, name):
    S, D = a.shape
    tr = _row_tile(S, D * 8)
    g1 = pl.BlockSpec((tr, D), lambda i: (i, 0))
    g2 = pl.BlockSpec((tr, D), lambda i: (i, 1))

    def fn(dm, x1, x2, av, bv):
        dm, x1, x2 = dm.astype(F32), x1.astype(F32), x2.astype(F32)
        dg1 = dm * av.astype(F32) * x1 * (1.0 - x1)
        dg2 = dm * bv.astype(F32) * x2 * (1.0 - x2)
        dgp = jnp.concatenate([dg1, dg2], axis=1)
        return dm * x1, dm * x2, dgp, jnp.sum(dgp, axis=0, keepdims=True)

    return _ew(fn, [(dmerged, _rows(tr, D)), (gates, g1), (gates, g2), (a, _rows(tr, D)), (bm, _rows(tr, D))],
               [((S, D), BF16, _rows(tr, D), False), ((S, D), BF16, _rows(tr, D), False),
                ((S, 2 * D), BF16, _rows(tr, 2 * D), False), ((1, 2 * D), F32, _bcast(2 * D), True)],
               (S // tr,), name)


def _ple_bwd(dh, pg, pe, name):
    S, D = dh.shape
    tr = _row_tile(S, D * 4)

    def fn(d, g, e):
        g, e = g.astype(F32), e.astype(F32)
        return d * g, d * e * g * (1.0 - g)

    spec = _rows(tr, D)
    return _ew(fn, [(dh, spec), (pg, spec), (pe, spec)],
               [((S, D), BF16, spec, False), ((S, D), BF16, spec, False)], (S // tr,), name)


def _position():
    x, y, c = lax.axis_index("x"), lax.axis_index("y"), lax.axis_index("c")
    chips = [(1 - x, y), (x, 1 - y), (1 - x, 1 - y)]
    return x, y, c, chips


def _remote(src, dst, send_sem, recv_sem, target):
    return pltpu.make_async_remote_copy(src_ref=src, dst_ref=dst, send_sem=send_sem, recv_sem=recv_sem,
                                        device_id=target, device_id_type=MESH)


HBM = pl.BlockSpec(memory_space=pltpu.HBM)
SEM = pl.BlockSpec(memory_space=pltpu.SEMAPHORE)
EFFECT = pltpu.SideEffectType.DATAFLOW_SIDE_EFFECTING


def _copies_start(plan, arrays, sem_shape, after, name):
    n = len(arrays)

    def body(*refs):
        send_sems, recv_sems, token = refs[n + 1], refs[n + 2], refs[-1]
        for send, _ in plan(refs[:n], send_sems, recv_sems):
            send.start()
        token[...] = jnp.zeros_like(token)

    outs = pl.pallas_call(
        body, name=name,
        out_shape=(pltpu.SemaphoreType.DMA(sem_shape), pltpu.SemaphoreType.DMA(sem_shape),
                   *[pltpu.HBM(a.shape, a.dtype) for a in arrays], jax.ShapeDtypeStruct((8, LANES), F32)),
        in_specs=[HBM] * n + [ANY],
        out_specs=(SEM, SEM, *[HBM] * n, pl.BlockSpec(memory_space=pltpu.VMEM)),
        input_output_aliases={a: 2 + a for a in range(n)},
        compiler_params=pltpu.CompilerParams(has_side_effects=EFFECT),
    )(*[pltpu.with_memory_space_constraint(a, pltpu.HBM) for a in arrays], after)
    return outs[0], outs[1], list(outs[2:2 + n]), outs[-1]


def _copies_wait(plan, send_sems, recv_sems, arrays, after, name):
    n = len(arrays)

    def body(*refs):
        for send, recv in plan(refs[:n], refs[n], refs[n + 1]):
            send.wait_send()
            recv.wait_recv()

    return list(pl.pallas_call(
        body, name=name,
        out_shape=[pltpu.HBM(a.shape, a.dtype) for a in arrays],
        in_specs=[HBM] * n + [SEM, SEM, ANY], out_specs=[HBM] * n,
        input_output_aliases={a: a for a in range(n)},
        compiler_params=pltpu.CompilerParams(has_side_effects=EFFECT),
    )(*arrays, send_sems, recv_sems, after))


def _gather_ici_plan(refs, send_sems, recv_sems):
    x, y, c, chips = _position()
    plan = []
    for a, ref in enumerate(refs):
        rh = ref.shape[1] // 2
        mine = ref.at[2 * x + y, pl.ds(c * rh, rh)]
        for j, (cx, cy) in enumerate(chips):
            landed = ref.at[2 * cx + cy, pl.ds(c * rh, rh)]
            plan.append((_remote(mine, mine, send_sems.at[3 * a + j], recv_sems.at[3 * a + j], (cx, cy, c)),
                         _remote(landed, landed, send_sems.at[3 * a + j], recv_sems.at[3 * a + j], (cx, cy, c))))
    return plan


def _gather_d2d_plan(refs, send_sems, recv_sems):
    x, y, c, chips = _position()
    sibling = (x, y, 1 - c)
    plan = []
    for a, ref in enumerate(refs):
        rh = ref.shape[1] // 2
        for j, (cx, cy) in enumerate(chips):
            landed = ref.at[2 * cx + cy, pl.ds(c * rh, rh)]
            theirs = ref.at[2 * cx + cy, pl.ds((1 - c) * rh, rh)]
            plan.append((_remote(landed, landed, send_sems.at[3 * a + j], recv_sems.at[3 * a + j], sibling),
                         _remote(theirs, theirs, send_sems.at[3 * a + j], recv_sems.at[3 * a + j], sibling)))
    return plan


def _scatter_ici_plan(refs, send_sems, recv_sems):
    x, y, c, chips = _position()
    n = len(refs) // 2
    plan = []
    for a in range(n):
        for j, (cx, cy) in enumerate(chips):
            cp = _remote(refs[a].at[2 * cx + cy], refs[n + a].at[j], send_sems.at[3 * a + j], recv_sems.at[3 * a + j],
                         (cx, cy, c))
            plan.append((cp, cp))
    return plan


def _place_shard(w, layer, me, after, name):
    _, r, cc = w.shape
    tr = _row_tile(r, cc * 4)

    def body(me_ref, w_ref, after_ref, o_ref):
        o_ref[...] = w_ref[...].astype(o_ref.dtype)

    return pl.pallas_call(
        body, name=name,
        grid_spec=pltpu.PrefetchScalarGridSpec(
            num_scalar_prefetch=1, grid=(r // tr,),
            in_specs=[pl.BlockSpec((None, tr, cc), lambda i, me_ref: (layer, i, 0)), ANY],
            out_specs=pl.BlockSpec((None, tr, cc), lambda i, me_ref: (me_ref[0], i, 0))),
        out_shape=jax.ShapeDtypeStruct((N_CHIPS, r, cc), BF16),
        compiler_params=_params(("parallel",)),
    )(me, w, after)


def _pair_plan(refs, send_sems, recv_sems):
    x, y, c, _ = _position()
    n = len(refs) // 2
    plan = []
    for a in range(n):
        rh = refs[a].shape[1] // 2
        cp = _remote(refs[a].at[:, pl.ds((1 - c) * rh, rh)], refs[n + a], send_sems.at[a], recv_sems.at[a],
                     (x, y, 1 - c))
        plan.append((cp, cp))
    return plan


def _pair_sum(mine, theirs, c, name):
    nch, rh, cc = theirs.shape
    tr = _row_tile(rh, cc * 4)
    nb = rh // tr

    def body(c_ref, m_ref, t_ref, o_ref):
        o_ref[...] = (m_ref[...].astype(F32) + t_ref[...].astype(F32)).astype(o_ref.dtype)

    return pl.pallas_call(
        body, name=name,
        grid_spec=pltpu.PrefetchScalarGridSpec(
            num_scalar_prefetch=1, grid=(nch, nb),
            in_specs=[pl.BlockSpec((1, tr, cc), lambda k, i, c_ref: (k, c_ref[0] * nb + i, 0)),
                      pl.BlockSpec((1, tr, cc), lambda k, i, c_ref: (k, i, 0))],
            out_specs=pl.BlockSpec((1, tr, cc), lambda k, i, c_ref: (k, i, 0))),
        out_shape=jax.ShapeDtypeStruct(theirs.shape, BF16),
        compiler_params=_params(("parallel", "parallel")),
    )(c, mine, theirs)


def _chip_sum(own, others, me, c, total, layer, depth, name):
    _, rh, cc = own.shape
    tr = _row_tile(rh, cc * 4)
    nb = rh // tr
    chained = total is not None

    def body(me_ref, c_ref, o_ref, r_ref, *rest):
        g_ref = rest[-1]
        g_ref[...] = (o_ref[0].astype(F32) + r_ref[0].astype(F32)) + (r_ref[1].astype(F32) + r_ref[2].astype(F32))

    in_specs = [pl.BlockSpec((1, tr, cc), lambda i, me_ref, c_ref: (me_ref[0], i, 0)),
                pl.BlockSpec((3, tr, cc), lambda i, me_ref, c_ref: (0, i, 0))]
    args = [me, c, own, others]
    if chained:
        in_specs.append(ANY)
        args.append(total)
    return pl.pallas_call(
        body, name=name,
        grid_spec=pltpu.PrefetchScalarGridSpec(
            num_scalar_prefetch=2, grid=(nb,), in_specs=in_specs,
            out_specs=pl.BlockSpec((None, tr, cc), lambda i, me_ref, c_ref: (layer, c_ref[0] * nb + i, 0))),
        out_shape=jax.ShapeDtypeStruct((depth, 2 * rh, cc), F32),
        input_output_aliases={4: 0} if chained else {},
        compiler_params=_params(("parallel",)),
    )(*args)


def _half_exchange(totals, layer, name):
    n = len(totals)

    def body(*refs):
        outs = refs[n:2 * n]
        send_sems, recv_sems = refs[2 * n:]
        x, y, c, _ = _position()
        copies = []
        for a in range(n):
            rh = outs[a].shape[1] // 2
            mine = outs[a].at[layer, pl.ds(c * rh, rh)]
            cp = _remote(mine, mine, send_sems.at[a], recv_sems.at[a], (x, y, 1 - c))
            cp.start()
            copies.append(cp)
        for a, cp in enumerate(copies):
            rh = outs[a].shape[1] // 2
            theirs = outs[a].at[layer, pl.ds((1 - c) * rh, rh)]
            cp.wait_send()
            _remote(theirs, theirs, send_sems.at[a], recv_sems.at[a], (x, y, 1 - c)).wait_recv()

    return pl.pallas_call(
        body, name=name, in_specs=[ANY] * n, out_specs=[ANY] * n,
        out_shape=[jax.ShapeDtypeStruct(t.shape, t.dtype) for t in totals],
        input_output_aliases={a: a for a in range(n)},
        scratch_shapes=[pltpu.SemaphoreType.DMA((n,)), pltpu.SemaphoreType.DMA((n,))],
    )(*totals)


def _allgather_devices(v, name):
    m_per, n = v.shape

    def body(x_ref, out_ref, send_sems, recv_sems, local_sem):
        x, y, c, chips = _position()
        me, sibling = (x, y, c), (x, y, 1 - c)

        def rows(px, py, pc):
            return out_ref.at[pl.ds((4 * px + 2 * py + pc) * m_per, m_per), :]

        def copy(k, block, to, src=None):
            return _remote(rows(*block) if src is None else src, rows(*block), send_sems.at[k], recv_sems.at[k], to)

        mine = pltpu.make_async_copy(x_ref, rows(*me), local_sem)
        mine.start()
        first = [copy(0, me, sibling, src=x_ref)]
        first += [copy(1 + j, me, (*chip, c), src=x_ref) for j, chip in enumerate(chips)]
        for cp in first:
            cp.start()
        passed = [copy(4 + j, (*chip, c), sibling) for j, chip in enumerate(chips)]
        for j, chip in enumerate(chips):
            copy(1 + j, (*chip, c), me).wait_recv()
            passed[j].start()
        copy(0, sibling, me).wait_recv()
        for j, chip in enumerate(chips):
            copy(4 + j, (*chip, 1 - c), me).wait_recv()
        for cp in first + passed:
            cp.wait_send()
        mine.wait()

    vm = pl.BlockSpec(memory_space=pltpu.VMEM)
    return pl.pallas_call(
        body, name=name, in_specs=[vm], out_specs=vm,
        out_shape=jax.ShapeDtypeStruct((8 * m_per, n), v.dtype),
        scratch_shapes=[pltpu.SemaphoreType.DMA((7,)), pltpu.SemaphoreType.DMA((7,)), pltpu.SemaphoreType.DMA],
    )(v)


def _adamw_math(w, g, m, v):
    m = ADAM_B1 * m + (1.0 - ADAM_B1) * g
    v = ADAM_B2 * v + (1.0 - ADAM_B2) * (g * g)
    m_hat = m / (1.0 - ADAM_B1 ** ADAM_STEP)
    v_hat = v / (1.0 - ADAM_B2 ** ADAM_STEP)
    delta = -ADAM_LR * (m_hat / (jnp.sqrt(v_hat) + ADAM_EPS) + ADAM_WD * w)
    return delta, m, v


def _adamw(w, g, m, v, lo, hi, prev, after, name):
    depth, r, cc = w.shape
    tr = _row_tile(r, cc * 4 * 2)
    spec = pl.BlockSpec((1, tr, cc), lambda l, i: (lo + l, i, 0))

    def body(w_ref, g_ref, m_ref, v_ref, *rest):
        outs = rest[-4:]
        gv = g_ref[...]
        for o_ref, val in zip(outs, (gv,) + _adamw_math(w_ref[...], gv, m_ref[...], v_ref[...])):
            o_ref[...] = val

    prev = list(prev) if prev is not None else []
    return tuple(pl.pallas_call(
        body, name=name, grid=(hi - lo, r // tr),
        in_specs=[spec] * 4 + [ANY] * (1 + len(prev)), out_specs=[spec] * 4,
        out_shape=[jax.ShapeDtypeStruct(w.shape, F32)] * 4,
        input_output_aliases={5 + k: k for k in range(len(prev))},
        compiler_params=_params(("parallel", "parallel")),
    )(w, g, m, v, after, *prev))


def _adamw_small(w, parts, m, v, name):
    M = w.shape[0]

    def body(w_ref, p_ref, m_ref, v_ref, g_ref, d_ref, nm_ref, nv_ref):
        g = p_ref[pl.ds(0, M), :]
        for k in range(1, 8):
            g = g + p_ref[pl.ds(k * M, M), :]
        d, nm, nv = _adamw_math(w_ref[...], g, m_ref[...], v_ref[...])
        g_ref[...] = g
        d_ref[...] = d
        nm_ref[...] = nm
        nv_ref[...] = nv

    vm = pl.BlockSpec(memory_space=pltpu.VMEM)
    return pl.pallas_call(
        body, name=name, in_specs=[vm] * 4, out_specs=[vm] * 4,
        out_shape=[jax.ShapeDtypeStruct(w.shape, F32)] * 4,
    )(w, parts, m, v)


def _layer_fwd(h0, p_l, W, small, tabs, after_mlp):
    S, D = h0.shape
    ctab, stab = tabs
    u = _rms_fwd(h0, small["g_mix"], "rms_mix")
    z = _mm(u, W["w_qkv"], mode="nn", name="mm_qkv", b_cols=3 * ATTN_WIDTH)
    f = _mm(u, W["w_f"], mode="nn", name="mm_f", out_dtypes=(F32,))
    gates = _mm(u, W["w_gate"], mode="nn", name="mm_gate", b_chunked=True, extras=[(small["b_gate"], "row")],
                epilogue=lambda acc, b: (_sigmoid(acc + b),))
    qr, kr = _rope(z, 0, z, N_HEADS, ctab, stab, 1.0, "rope_fwd")
    qs, ks, vs = _to_strided(qr), _to_strided(kr), _to_strided(z[:, 2 * ATTN_WIDTH:2 * ATTN_WIDTH + DIL_WIDTH])
    o_s, lse_s = _sw_fwd(qs, ks, vs, "sw_fwd")
    o_g, lse_g = _from_strided(o_s, S), _from_strided(lse_s, S)
    ya = _mix_fwd(o_g, lse_g, "mix_fwd")
    a = _mm(ya, W["w_br_a"], mode="nn", name="mm_br_a", b_chunked=True)
    cum = _fox_prep(f, small["b_f"], "fox_prep")
    cq = cum[:, :N_FOX_HEADS].T[:, :, None]
    ck = cum[:, :N_FOX_HEADS].T[:, None, :]
    yb, lse_f = _fox_fwd(z, cq, ck, "fox_fwd")
    bm = _mm(yb, W["w_br_b"], mode="nn", name="mm_br_b", b_chunked=True)
    merged = _merge_fwd(gates, a, bm, "merge_fwd")
    h1 = _mm(merged, W["w_o"], mode="nn", name="mm_o", out_dtypes=(F32,), extras=[(h0, "tile")],
             epilogue=lambda acc, r: (acc + r,), tj=512)
    m = _rms_fwd(h1, small["g_mlp"], "rms_mlp")
    ra, act = _mm(m, W["w_up"], mode="nn", name="mm_up", b_chunked=True, out_dtypes=(BF16, BF16),
                  epilogue=lambda acc: (jnp.maximum(acc, 0.0), jnp.square(jnp.maximum(acc, 0.0))))
    h2 = _mm(act, W["w_down"], mode="nn", name="mm_down", out_dtypes=(F32,), extras=[(h1, "tile")],
             epilogue=lambda acc, r: (acc + r,), ti=1024, tj=512, tc=4096)
    token = after_mlp(h2)
    n = _rms_fwd(h2, small["g_ple"] if token is None else small["g_ple"] + token[0, 0], "rms_ple")
    pg = _mm(n, W["w_ple_gate"], mode="nn", name="mm_ple_gate", epilogue=lambda acc: (_sigmoid(acc),))
    h3, pe = _mm(p_l, W["w_ple"], mode="nn", name="mm_ple", b_chunked=True, out_dtypes=(F32, BF16),
                 extras=[(h2, "tile"), (pg, "tile")], tj=256,
                 epilogue=lambda acc, r, g: (r + g.astype(F32) * acc, acc))
    saved = dict(h0=h0, u=u, z=z, f=f, gates=gates, qs=qs, ks=ks, vs=vs, lse_s=lse_s, o_g=o_g, lse_g=lse_g,
                 ya=ya, a=a, cq=cq, ck=ck, yb=yb, lse_f=lse_f, bm=bm, merged=merged, h1=h1, m=m, ra=ra,
                 act=act, h2=h2, n=n, pg=pg, pe=pe, p_l=p_l)
    return h3, saved


def _after(hooks, name, small_value, *args):
    token = hooks[name](*args) if name in hooks else None
    return small_value if token is None else small_value + token[0, 0]


def _layer_bwd(dh3, sv, W, small, tabs, hooks):
    S, D = dh3.shape
    ctab, stab = tabs
    gw, gs = {}, {}
    tn = functools.partial(_mm, mode="tn", ti=1024, tj=2048)
    dpe, dpg = _ple_bwd(dh3, sv["pg"], sv["pe"], "ple_bwd")
    gw["w_ple"] = tn(sv["p_l"], dpe, name="dw_ple", out_chunks=N_CHIPS)
    gw["w_ple_gate"] = tn(sv["n"], dpg, name="dw_ple_gate").reshape(N_CHIPS, D // N_CHIPS, D)
    dn = _mm(dpg, W["w_ple_gate"], mode="nt", name="mm_dn")
    dh2, dh2b, gs["g_ple"] = _rms_bwd(sv["h2"], small["g_ple"], dn, dh3, "rms_ple_bwd")
    da = _mm(dh2b, W["w_down"], mode="nt", name="mm_dact", extras=[(sv["ra"], "tile")],
             epilogue=lambda acc, r: (acc * (2.0 * r.astype(F32)),))
    FF = da.shape[1]
    g_mlp = _after(hooks, "mlp_grad", small["g_mlp"], da)
    gw["w_down"] = tn(sv["act"], dh2b, name="dw_down").reshape(N_CHIPS, FF // N_CHIPS, D)
    gw["w_up"] = tn(sv["m"], da, name="dw_up", out_chunks=N_CHIPS)
    g_mlp = _after(hooks, "mlp_weights", g_mlp, dict(gw))
    dm = _mm(da, W["w_up"], mode="nt", name="mm_dm", b_chunked=True)
    dh1, dh1b, gs["g_mlp"] = _rms_bwd(sv["h1"], g_mlp, dm, dh2, "rms_mlp_bwd")
    dmerged = _mm(dh1b, W["w_o"], mode="nt", name="mm_dmerged")
    b_f = _after(hooks, "merge_grad", small["b_f"], dmerged)
    gw["w_o"] = tn(sv["merged"], dh1b, name="dw_o").reshape(N_CHIPS, D // N_CHIPS, D)
    d_a, d_b, dgp, gs["b_gate"] = _merge_bwd(dmerged, sv["gates"], sv["a"], sv["bm"], "merge_bwd")
    gw["w_gate"] = tn(sv["u"], dgp, name="dw_gate", out_chunks=N_CHIPS)
    gw["w_br_a"] = tn(sv["ya"], d_a, name="dw_br_a", out_chunks=N_CHIPS, tj=512)
    gw["w_br_b"] = tn(sv["yb"], d_b, name="dw_br_b", out_chunks=N_CHIPS, tj=512)
    dya = _mm(d_a, W["w_br_a"], mode="nt", name="mm_dya", b_chunked=True)
    dyb = _mm(d_b, W["w_br_b"], mode="nt", name="mm_dyb", b_chunked=True)
    z = sv["z"]
    dq_f, dk_f, dv_f, dck = _fox_bwd(z, sv["cq"], sv["ck"], sv["lse_f"], sv["yb"], dyb, "fox_bwd")
    dc = jnp.pad(dck[:, 0, :].T, ((0, 0), (0, LANES - N_FOX_HEADS)))
    df, dbf = _fox_prep_bwd(dc, sv["f"], b_f, "fox_prep_bwd")
    gs["b_f"] = dbf[:, :N_FOX_HEADS]
    lane = jnp.arange(LANES)[None, :] < N_FOX_HEADS
    dzf = jnp.where(lane, df, 0.0).astype(BF16)
    do_g, tt_g = _mix_bwd(dya, sv["ya"], sv["o_g"], sv["lse_g"], "mix_bwd")
    do_s = _to_strided(do_g.transpose(1, 0, 2).reshape(S, DIL_WIDTH))
    tt_s = _to_strided(tt_g.transpose(1, 0, 2).reshape(S, DIL_WIDTH))
    dq_s, dk_s, dv_s = _sw_bwd(sv["qs"], sv["ks"], sv["vs"], do_s, sv["lse_s"], tt_s, "sw_bwd")
    unstride = lambda t: _from_strided(t, S).transpose(1, 0, 2).reshape(S, DIL_WIDTH)
    dq_a, dk_a = _rope(unstride(dq_s), 0, unstride(dk_s), 0, ctab, stab, -1.0, "rope_bwd")
    dz = jnp.concatenate([dq_a, dq_f, dk_a, dk_f, unstride(dv_s), dv_f], axis=1)
    g_qkv = tn(sv["u"], dz, name="dw_qkv")
    g_f = tn(sv["u"], dzf, name="dw_f", tj=128)
    cols = W["w_in_cols"]
    g_in = jnp.concatenate([g_qkv, g_f[:, :N_FOX_HEADS]], axis=1)
    gw["w_in"] = jnp.stack([g_in[:, k * cols:(k + 1) * cols] for k in range(N_CHIPS)])
    du = _mm(dzf, W["w_f"], mode="nt", name="mm_du_f", out_dtypes=(F32,))
    du = _mm(dgp, W["w_gate"], mode="nt", name="mm_du_gate", b_chunked=True, out_dtypes=(F32,),
             extras=[(du, "tile")], epilogue=lambda acc, r: (acc + r,), tj=512)
    du = _mm(dz, W["w_qkv"], mode="nt", name="mm_du_qkv", b_cols=3 * ATTN_WIDTH, extras=[(du, "tile")],
             epilogue=lambda acc, r: (acc + r,), tj=512)
    dh0, _, gs["g_mix"] = _rms_bwd(sv["h0"], small["g_mix"], du, dh1, "rms_mix_bwd")
    return dh0, gw, gs


BIG = ("w_in", "w_gate", "w_br_a", "w_br_b", "w_o", "w_up", "w_down", "w_ple", "w_ple_gate")
EARLY = ("w_ple", "w_ple_gate", "w_down", "w_up")
SMALL = ("g_mix", "b_f", "b_gate", "g_mlp", "g_ple", "g_final")
ORDER = ("g_mix", "w_in", "b_f", "w_gate", "b_gate", "w_br_a", "w_br_b", "w_o", "g_mlp", "w_up", "w_down",
         "g_ple", "w_ple", "w_ple_gate", "g_final")


def _gathered_layer_weights(full, D):
    W = {}
    for name in ("w_gate", "w_br_a", "w_br_b", "w_up", "w_ple"):
        W[name] = full[name]
    for name in ("w_o", "w_down", "w_ple_gate"):
        t = full[name]
        W[name] = t.reshape(t.shape[0] * t.shape[1], t.shape[2])
    w_in = full["w_in"]
    cols = w_in.shape[2]
    w_in = jnp.concatenate([w_in[k] for k in range(N_CHIPS)], axis=1)
    W["w_qkv"] = w_in
    W["w_f"] = jnp.pad(w_in[:, 3 * ATTN_WIDTH:], ((0, 0), (0, LANES - N_FOX_HEADS)))
    W["w_in_cols"] = cols
    return W


def _pack_rows(vals):
    flat = jnp.concatenate([v.reshape(-1) for v in vals])
    rows = -(-flat.shape[0] // (8 * LANES)) * 8
    return jnp.pad(flat, (0, rows * LANES - flat.shape[0])).reshape(rows, LANES)


def _unpack_rows(packed, shapes):
    flat = packed.reshape(-1)
    out, pos = [], 0
    for s in shapes:
        size = 1
        for dim in s:
            size *= dim
        out.append(flat[pos:pos + size].reshape(s))
        pos += size
    return out


def _local_step(x, p, small_w, comm, loss_target):
    depth = p.shape[0]
    S, D = x.shape
    tabs = _rope_tables(S)
    h = x
    saved, weights, smalls = [], [], []
    state, _ = comm["gather_start"](0, x)
    full = comm["gather_finish"](comm["gather_mid"](state, x)[0], x)
    for l in range(depth):
        nxt = [None]
        g_mix = small_w["g_mix"][l][None]
        if l + 1 < depth:
            nxt[0], token = comm["gather_start"](l + 1, full["w_ple"])
            g_mix = g_mix + token[0, 0]

        def after_mlp(h2):
            if nxt[0] is None:
                return None
            nxt[0], token = comm["gather_mid"](nxt[0], h2)
            return token

        W = _gathered_layer_weights(full, D)
        sm = dict(g_mix=g_mix, g_mlp=small_w["g_mlp"][l][None],
                  g_ple=small_w["g_ple"][l][None], b_gate=small_w["b_gate"][l][None],
                  b_f=jnp.pad(small_w["b_f"][l][None], ((0, 0), (0, LANES - N_FOX_HEADS))))
        h, sv = _layer_fwd(h, p[l].astype(BF16), W, sm, tabs, after_mlp)
        if nxt[0] is not None:
            full = comm["gather_finish"](nxt[0], h)
        saved.append(sv)
        weights.append(W)
        smalls.append(sm)
    dh, loss_row, dg_final = _loss_head(h, small_w["g_final"][None], loss_target, "loss_head")
    gss = [None] * depth
    pending, token = [None], None
    for l in reversed(range(depth)):
        sm = smalls[l]
        if token is not None:
            sm = {**sm, "g_ple": sm["g_ple"] + token[0, 0]}

        def after_mlp_grad(da):
            if pending[0] is None:
                return None
            pending[0] = comm["reduce_mid"](pending[0], da)
            return pending[0][-1]

        hooks = dict(mlp_grad=after_mlp_grad)
        if l == 0 and "reduce_early_begin" in comm:
            hooks.update(mlp_weights=comm["reduce_early_begin"], merge_grad=comm["reduce_early_mid"])
        dh, gw, gss[l] = _layer_bwd(dh, saved[l], weights[l], sm, tabs, hooks)
        if pending[0] is not None:
            comm["reduce_end"](pending[0], dh)
        if l > 0:
            pending[0], token = comm["reduce_begin"](l, gw, dh)
    return loss_row, dh, gss, dg_final, gw


def _device_comm(w, m, v, depth, c_arr, me_arr):
    n = len(BIG)
    totals = {name: None for name in BIG}
    placed, results = [], {}

    def gather_start(l, after):
        if l == 0:
            placed.append([_place_shard(w[name], 0, me_arr, me_arr, "place_shard") for name in BIG])
        send, recv, bufs, token = _copies_start(_gather_ici_plan, placed[l], (3 * n,), after, f"gather_ici_start_{l}")
        if l == 0:
            placed.extend([_place_shard(w[name], k, me_arr, token, "place_shard") for name in BIG]
                          for k in range(1, depth))
        return (l, send, recv, bufs), token

    def gather_mid(state, after):
        l, send, recv, bufs = state
        if l == 0:
            after = placed[-1][-1]
        bufs = _copies_wait(_gather_ici_plan, send, recv, bufs, after, f"gather_ici_wait_{l}")
        send, recv, bufs, token = _copies_start(_gather_d2d_plan, bufs, (3 * n,), after, f"gather_d2d_start_{l}")
        return (l, send, recv, bufs), token

    def gather_finish(state, after):
        l, send, recv, bufs = state
        return dict(zip(BIG, _copies_wait(_gather_d2d_plan, send, recv, bufs, after, f"gather_d2d_wait_{l}")))

    def reduce_begin(l, gw, after, names=BIG, tag=""):
        grads = [gw[name] for name in names]
        landing = [lax.empty((g.shape[0], g.shape[1] // 2, g.shape[2]), g.dtype) for g in grads]
        send, recv, arrays, token = _copies_start(_pair_plan, grads + landing, (len(names),), after,
                                                  f"pair_start_{l}{tag}")
        return (l, tag, names, send, recv, arrays), token

    def reduce_mid(state, after):
        l, tag, names, send, recv, arrays = state
        k = len(names)
        arrays = _copies_wait(_pair_plan, send, recv, arrays, after, f"pair_wait_{l}{tag}")
        sums = [_pair_sum(g, t, c_arr, "rs_pair_sum") for g, t in zip(arrays[:k], arrays[k:])]
        landing = [lax.empty((3,) + s.shape[1:], s.dtype) for s in sums]
        send, recv, arrays, token = _copies_start(_scatter_ici_plan, sums + landing, (3 * k,), sums[0],
                                                  f"scatter_ici_start_{l}{tag}")
        return l, tag, names, send, recv, arrays, token

    def reduce_end(state, after):
        l, tag, names, send, recv, arrays, _ = state
        k = len(names)
        arrays = _copies_wait(_scatter_ici_plan, send, recv, arrays, after, f"scatter_ici_wait_{l}{tag}")
        done = [_chip_sum(s, o, me_arr, c_arr, totals[name], l, depth, "rs_chip_sum")
                for name, s, o in zip(names, arrays[:k], arrays[k:])]
        totals.update(zip(names, _half_exchange(done, l, "rs_half_exchange")))

    early = [None]

    def reduce_early_begin(gw):
        early[0], token = reduce_begin(0, gw, gw[EARLY[-1]], EARLY, "a")
        return token

    def reduce_early_mid(after):
        early[0] = reduce_mid(early[0], after)
        return early[0][-1]

    def reduce_last(gw, after):
        upper = {}

        def adamw_upper(names, token):
            for name in names:
                if depth > 1:
                    upper[name] = _adamw(w[name], totals[name], m[name], v[name], 1, depth, None, token,
                                         "adamw_upper")
                    token = upper[name][0]
            return token

        def adamw_first(names, token):
            for name in names:
                results[name] = _adamw(w[name], totals[name], m[name], v[name], 0, 1, upper.get(name), token,
                                       "adamw_first")
                token = results[name][0]
            return token

        late = BIG if early[0] is None else tuple(name for name in BIG if name not in EARLY)
        state, token = reduce_begin(0, gw, after, late, "b")
        state = reduce_mid(state, adamw_upper(("w_up",), token))
        token = adamw_upper([name for name in BIG if name != "w_up"], state[-1])
        if early[0] is not None:
            reduce_end(early[0], token)
            token = adamw_first(EARLY, totals[EARLY[0]])
        reduce_end(state, token)
        adamw_first(late, totals[late[0]])

    comm = dict(gather_start=gather_start, gather_mid=gather_mid, gather_finish=gather_finish,
                reduce_begin=reduce_begin, reduce_mid=reduce_mid, reduce_end=reduce_end, reduce_last=reduce_last,
                reduce_early_begin=reduce_early_begin, reduce_early_mid=reduce_early_mid)
    return comm, results


def kernel(x, p, g_mix, w_in, b_f, w_gate, b_gate, w_br_a, w_br_b, w_o, g_mlp, w_up, w_down, g_ple, w_ple, w_ple_gate, g_final, loss_target, m_g_mix, m_w_in, m_b_f, m_w_gate, m_b_gate, m_w_br_a, m_w_br_b, m_w_o, m_g_mlp, m_w_up, m_w_down, m_g_ple, m_w_ple, m_w_ple_gate, m_g_final, v_g_mix, v_w_in, v_b_f, v_w_gate, v_b_gate, v_w_br_a, v_w_br_b, v_w_o, v_g_mlp, v_w_up, v_w_down, v_g_ple, v_w_ple, v_w_ple_gate, v_g_final):
    w = dict(g_mix=g_mix, w_in=w_in, b_f=b_f, w_gate=w_gate, b_gate=b_gate, w_br_a=w_br_a, w_br_b=w_br_b,
             w_o=w_o, g_mlp=g_mlp, w_up=w_up, w_down=w_down, g_ple=g_ple, w_ple=w_ple, w_ple_gate=w_ple_gate,
             g_final=g_final)
    m = dict(g_mix=m_g_mix, w_in=m_w_in, b_f=m_b_f, w_gate=m_w_gate, b_gate=m_b_gate, w_br_a=m_w_br_a,
             w_br_b=m_w_br_b, w_o=m_w_o, g_mlp=m_g_mlp, w_up=m_w_up, w_down=m_w_down, g_ple=m_g_ple,
             w_ple=m_w_ple, w_ple_gate=m_w_ple_gate, g_final=m_g_final)
    v = dict(g_mix=v_g_mix, w_in=v_w_in, b_f=v_b_f, w_gate=v_w_gate, b_gate=v_b_gate, w_br_a=v_w_br_a,
             w_br_b=v_w_br_b, w_o=v_w_o, g_mlp=v_g_mlp, w_up=v_w_up, w_down=v_w_down, g_ple=v_g_ple,
             w_ple=v_w_ple, w_ple_gate=v_w_ple_gate, g_final=v_g_final)
    depth = p.shape[0]
    cx, cy, cc = lax.axis_index("x"), lax.axis_index("y"), lax.axis_index("c")
    c_arr = jnp.reshape(cc, (1,)).astype(jnp.int32)
    me_arr = jnp.reshape(2 * cx + cy, (1,)).astype(jnp.int32)

    comm, big_out = _device_comm(w, m, v, depth, c_arr, me_arr)
    loss_row, grad_x, gss, dg_final, gw0 = _local_step(x[0], p[:, 0], w, comm, loss_target[0])

    small_grads = [jnp.stack([gss[l][n][0] for l in range(depth)]) for n in SMALL[:-1]] + [dg_final[0]]
    shapes = [w[n].shape for n in SMALL]
    parts = _allgather_devices(_pack_rows(small_grads), "allgather_small")
    packed = _adamw_small(_pack_rows([w[n] for n in SMALL]), parts, _pack_rows([m[n] for n in SMALL]),
                          _pack_rows([v[n] for n in SMALL]), "adamw_small")
    small_out = {n: vals for n, vals in zip(SMALL, zip(*[_unpack_rows(t, shapes) for t in packed]))}
    comm["reduce_last"](gw0, packed[0])

    loss = lax.psum(loss_row[0, 0], ("x", "y", "c"))
    out = {**big_out, **small_out}
    return (loss, grad_x[None], *[out[n][0] for n in ORDER], *[out[n][1] for n in ORDER],
            *[out[n][2] for n in ORDER], *[out[n][3] for n in ORDER])
```

```python
import functools

import jax
import jax.numpy as jnp
from jax import lax
from jax.experimental import pallas as pl
from jax.experimental.pallas import tpu as pltpu

F32 = jnp.float32
BF16 = jnp.bfloat16

HEAD_DIM = 128
N_HEADS = 16
N_DIL_HEADS = 12
N_FOX_HEADS = 4
HEADS_PER_DIL = 4
DILATIONS = (1, 4, 16)
BLOCK = 128
ATTN_WIDTH = N_HEADS * HEAD_DIM
DIL_WIDTH = N_DIL_HEADS * HEAD_DIM
FOX_WIDTH = N_FOX_HEADS * HEAD_DIM
ROPE_THETA = 500000.0
ROPE_HALF = 16
NORM_EPS = 1e-6
SCALE = HEAD_DIM ** -0.5
NEG = -1e30

ADAM_LR = 0.001
ADAM_B1 = 0.9
ADAM_B2 = 0.999
ADAM_EPS = 1e-08
ADAM_WD = 0.01
ADAM_STEP = 10

N_CHIPS = 4
V7X_VMEM_LIMIT_BYTES = 56 * 1024 * 1024
LANES = 128
MESH = pl.DeviceIdType.MESH
ANY = pl.BlockSpec(memory_space=pl.ANY)


def _params(sem):
    return pltpu.CompilerParams(dimension_semantics=sem, vmem_limit_bytes=V7X_VMEM_LIMIT_BYTES)


def _tile(n, pref):
    if n <= pref:
        return n
    t = (pref // LANES) * LANES
    while t > LANES and n % t:
        t -= LANES
    assert n % t == 0, (n, pref)
    return t


def _mm(a, b, *, mode, name, out_dtypes=(BF16,), epilogue=None, extras=(), b_chunked=False,
        out_chunks=0, b_cols=None, ti=2048, tj=512, tc=2048):
    if mode == "tn":
        C, I = a.shape
    else:
        I, C = a.shape
    if b_chunked:
        nch, d0, n = b.shape
        if mode == "nn":
            assert d0 == C
            J = nch * n
        else:
            assert mode == "nt" and nch * n == C
            J = d0
    elif mode == "nt":
        J = b.shape[0]
        assert (b.shape[1] if b_cols is None else b_cols) == C
    else:
        assert b.shape[0] == C
        J = b.shape[1] if b_cols is None else b_cols
    ti, tc = _tile(I, ti), _tile(C, tc)
    if b_chunked and mode == "nn":
        tj = _tile(n, tj)
    elif out_chunks:
        tj = _tile(J // out_chunks, tj)
    else:
        tj = _tile(J, tj)
    if b_chunked and mode == "nt":
        tc = _tile(n, tc)
    ni, nj, nc = I // ti, J // tj, C // tc

    if mode == "tn":
        a_spec = pl.BlockSpec((tc, ti), lambda i, j, c: (c, i))
        dims = (((0,), (0,)), ((), ()))
    else:
        a_spec = pl.BlockSpec((ti, tc), lambda i, j, c: (i, c))
        dims = (((1,), (0,)), ((), ())) if mode == "nn" else (((1,), (1,)), ((), ()))
    if mode == "nt":
        if b_chunked:
            cb = n // tc
            b_spec = pl.BlockSpec((None, tj, tc), lambda i, j, c: (c // cb, j, c % cb))
        else:
            b_spec = pl.BlockSpec((tj, tc), lambda i, j, c: (j, c))
    else:
        if b_chunked:
            jb = n // tj
            b_spec = pl.BlockSpec((None, tc, tj), lambda i, j, c: (j // jb, c, j % jb))
        else:
            b_spec = pl.BlockSpec((tc, tj), lambda i, j, c: (c, j))
    extra_specs = []
    for arr, kind in extras:
        if kind == "tile":
            assert arr.shape == (I, J), (arr.shape, I, J)
            extra_specs.append(pl.BlockSpec((ti, tj), lambda i, j, c: (i, j)))
        else:
            assert arr.shape == (1, J)
            extra_specs.append(pl.BlockSpec((1, tj), lambda i, j, c: (0, j)))
    if out_chunks:
        ob = (J // out_chunks) // tj
        out_spec = pl.BlockSpec((None, ti, tj), lambda i, j, c: (j // ob, i, j % ob))
        out_shape = [jax.ShapeDtypeStruct((out_chunks, I, J // out_chunks), d) for d in out_dtypes]
    else:
        out_spec = pl.BlockSpec((ti, tj), lambda i, j, c: (i, j))
        out_shape = [jax.ShapeDtypeStruct((I, J), d) for d in out_dtypes]
    ne, no = len(extras), len(out_dtypes)
    if epilogue is None:
        epilogue = lambda acc: (acc,)

    def body(a_ref, b_ref, *rest):
        extra_refs, out_refs = rest[:ne], rest[ne:ne + no]

        def finish(acc):
            outs = epilogue(acc, *[r[...] for r in extra_refs])
            for o_ref, val in zip(out_refs, outs):
                o_ref[...] = val.astype(o_ref.dtype)

        part = lax.dot_general(a_ref[...], b_ref[...], dims, preferred_element_type=F32)
        if nc == 1:
            finish(part)
        else:
            acc_ref = rest[-1]
            k = pl.program_id(2)

            @pl.when(k == 0)
            def _():
                acc_ref[...] = part

            @pl.when(k > 0)
            def _():
                acc_ref[...] += part

            @pl.when(k == nc - 1)
            def _():
                finish(acc_ref[...])

    outs = pl.pallas_call(
        body, name=name, grid=(ni, nj, nc),
        in_specs=[a_spec, b_spec] + extra_specs,
        out_specs=[out_spec] * no, out_shape=out_shape,
        scratch_shapes=[pltpu.VMEM((ti, tj), F32)] if nc > 1 else [],
        compiler_params=_params(("parallel", "parallel", "arbitrary")),
    )(a, b, *[e[0] for e in extras])
    return outs[0] if no == 1 else tuple(outs)


def _ew(fn, ins, outs, grid, name):
    n_in = len(ins)
    has_acc = any(o[3] for o in outs)
    assert not has_acc or len(grid) == 1

    def body(*refs):
        vals = fn(*[r[...] for r in refs[:n_in]])
        for o_ref, o, val in zip(refs[n_in:], outs, vals):
            if o[3]:
                step = pl.program_id(0)

                @pl.when(step == 0)
                def _(o_ref=o_ref, val=val):
                    o_ref[...] = val

                @pl.when(step > 0)
                def _(o_ref=o_ref, val=val):
                    o_ref[...] += val
            else:
                o_ref[...] = val.astype(o_ref.dtype)

    sem = ("arbitrary",) if has_acc else ("parallel",) * len(grid)
    res = pl.pallas_call(
        body, name=name, grid=grid,
        in_specs=[i[1] for i in ins], out_specs=[o[2] for o in outs],
        out_shape=[jax.ShapeDtypeStruct(o[0], o[1]) for o in outs],
        compiler_params=_params(sem),
    )(*[i[0] for i in ins])
    return res[0] if len(outs) == 1 else tuple(res)


def _rows(tr, w):
    return pl.BlockSpec((tr, w), lambda i: (i, 0))


def _bcast(w):
    return pl.BlockSpec((1, w), lambda i: (0, 0))


ROW_BLOCK_BYTES = 4 * 1024 * 1024


def _row_tile(S, width_bytes):
    tr = 512
    while tr > 16 and tr * width_bytes > ROW_BLOCK_BYTES:
        tr //= 2
    return min(tr, S)


def _rms_fwd(h, g, name):
    S, D = h.shape
    tr = _row_tile(S, D * 4)

    def fn(x, gg):
        r = lax.rsqrt(jnp.mean(x * x, axis=-1, keepdims=True) + NORM_EPS)
        return (x * r * gg,)

    return _ew(fn, [(h, _rows(tr, D)), (g, _bcast(D))], [((S, D), BF16, _rows(tr, D), False)],
               (S // tr,), name)


def _rms_bwd(x, g, dy, dres, name):
    S, D = x.shape
    tr = _row_tile(S, D * 4)

    def fn(xv, gg, dyv, dr):
        r = lax.rsqrt(jnp.mean(xv * xv, axis=-1, keepdims=True) + NORM_EPS)
        dyf = dyv.astype(F32)
        gy = dyf * gg
        dx = r * gy - xv * (r * r * r) * jnp.mean(xv * gy, axis=-1, keepdims=True)
        tot = dr + dx
        dg = jnp.sum(dyf * xv * r, axis=0, keepdims=True)
        return tot, tot, dg

    return _ew(fn, [(x, _rows(tr, D)), (g, _bcast(D)), (dy, _rows(tr, D)), (dres, _rows(tr, D))],
               [((S, D), F32, _rows(tr, D), False), ((S, D), BF16, _rows(tr, D), False),
                ((1, D), F32, _bcast(D), True)], (S // tr,), name)


def _loss_head(h, g, target, name):
    S, D = h.shape
    tr = _row_tile(S, D * 4)

    def fn(xv, gg, tgt):
        r = lax.rsqrt(jnp.mean(xv * xv, axis=-1, keepdims=True) + NORM_EPS)
        y = xv * r * gg
        e = y - tgt
        loss = 0.5 * jnp.sum(jnp.mean(e * e, axis=-1, keepdims=True), axis=0, keepdims=True)
        dy = e * (1.0 / D)
        gy = dy * gg
        dx = r * gy - xv * (r * r * r) * jnp.mean(xv * gy, axis=-1, keepdims=True)
        dg = jnp.sum(dy * xv * r, axis=0, keepdims=True)
        return dx, jnp.broadcast_to(loss, (1, LANES)), dg

    return _ew(fn, [(h, _rows(tr, D)), (g, _bcast(D)), (target, _rows(tr, D))],
               [((S, D), F32, _rows(tr, D), False), ((1, LANES), F32, _bcast(LANES), True),
                ((1, D), F32, _bcast(D), True)], (S // tr,), name)


def _rope_tables(S):
    inv = ROPE_THETA ** (-jnp.arange(ROPE_HALF, dtype=F32) / ROPE_HALF)
    ang = jnp.arange(S, dtype=F32)[:, None] * inv[None, :]
    cos, sin = jnp.cos(ang), jnp.sin(ang)
    rest = HEAD_DIM - 2 * ROPE_HALF
    ctab = jnp.concatenate([cos, cos, jnp.ones((S, rest), F32)], axis=1)
    stab = jnp.concatenate([-sin, sin, jnp.zeros((S, rest), F32)], axis=1)
    return ctab, stab


def _swap_halves(x):
    lane = lax.broadcasted_iota(jnp.int32, x.shape, 1)
    return jnp.where(lane < ROPE_HALF, pltpu.roll(x, HEAD_DIM - ROPE_HALF, 1), pltpu.roll(x, ROPE_HALF, 1))


def _rope(q_src, q_col0, k_src, k_col0, ctab, stab, sign, name):
    S = q_src.shape[0]
    tr = min(512, S)
    width = HEADS_PER_DIL * HEAD_DIM

    def fn(q, k, ct, st):
        outs = []
        for v in (q, k):
            heads = []
            for h in range(HEADS_PER_DIL):
                vf = v[:, h * HEAD_DIM:(h + 1) * HEAD_DIM].astype(F32)
                heads.append(vf * ct + sign * _swap_halves(vf) * st)
            outs.append(jnp.concatenate(heads, axis=1))
        return tuple(outs)

    group = lambda c0: pl.BlockSpec((tr, width), lambda i, g: (i, c0 // HEADS_PER_DIL + g))
    tab = pl.BlockSpec((tr, HEAD_DIM), lambda i, g: (i, 0))
    out = ((S, DIL_WIDTH), BF16, group(0), False)
    return _ew(fn, [(q_src, group(q_col0)), (k_src, group(k_col0)), (ctab, tab), (stab, tab)],
               [out, out], (S // tr, len(DILATIONS)), name)


SW_BLOCKS_PER_STEP = 16


def _to_strided(x):
    S = x.shape[0]
    parts = []
    for g, d in enumerate(DILATIONS):
        xg = x[:, g * 512:(g + 1) * 512].reshape(S // d, d, HEADS_PER_DIL, HEAD_DIM)
        parts.append(xg.transpose(1, 2, 0, 3).reshape(-1, BLOCK, HEAD_DIM))
    return jnp.concatenate(parts, axis=0)


def _from_strided(y, S):
    per = y.shape[0] // len(DILATIONS)
    parts = []
    for g, d in enumerate(DILATIONS):
        yg = y[g * per:(g + 1) * per].reshape(d, HEADS_PER_DIL, S // d, HEAD_DIM)
        parts.append(yg.transpose(2, 0, 1, 3).reshape(S, HEADS_PER_DIL * HEAD_DIM))
    return jnp.stack(parts, axis=0)


def _seq_blocks(b0, per_group):
    g = b0 // per_group
    n0 = per_group // HEADS_PER_DIL
    return jnp.where(g == 0, n0, jnp.where(g == 1, n0 // 4, n0 // 16))


def _sw_masks():
    qi = lax.broadcasted_iota(jnp.int32, (BLOCK, BLOCK), 0)
    ki = lax.broadcasted_iota(jnp.int32, (BLOCK, BLOCK), 1)
    return qi >= ki, qi <= ki


def _sw_fwd(q, k, v, name):
    NB = q.shape[0]
    T = SW_BLOCKS_PER_STEP
    per_group = NB // len(DILATIONS)
    nt = (((1,), (1,)), ((), ()))

    def body(q_ref, k_ref, v_ref, kp_ref, vp_ref, o_ref, lse_ref):
        b0 = pl.program_id(0) * T
        nseq = _seq_blocks(b0, per_group)
        cur_mask, prev_mask = _sw_masks()
        for t in range(T):
            has_prev = ((b0 + t) & (nseq - 1)) != 0
            qt = q_ref[t]
            kp = kp_ref[0] if t == 0 else k_ref[t - 1]
            vp = vp_ref[0] if t == 0 else v_ref[t - 1]
            s_c = lax.dot_general(qt, k_ref[t], nt, preferred_element_type=F32) * SCALE
            s_p = lax.dot_general(qt, kp, nt, preferred_element_type=F32) * SCALE
            s_c = jnp.where(cur_mask, s_c, NEG)
            s_p = jnp.where(prev_mask, s_p, NEG) + jnp.where(has_prev, 0.0, NEG)
            m = jnp.maximum(jnp.max(s_c, axis=-1, keepdims=True), jnp.max(s_p, axis=-1, keepdims=True))
            p_c = jnp.exp(s_c - m)
            p_p = jnp.exp(s_p - m)
            l = jnp.sum(p_c, axis=-1, keepdims=True) + jnp.sum(p_p, axis=-1, keepdims=True)
            o = (jnp.dot(p_c.astype(BF16), v_ref[t], preferred_element_type=F32)
                 + jnp.dot(p_p.astype(BF16), vp, preferred_element_type=F32))
            o_ref[t] = (o / l).astype(o_ref.dtype)
            lse_ref[t] = jnp.broadcast_to(m + jnp.log(l), (BLOCK, HEAD_DIM))

    tile = pl.BlockSpec((T, BLOCK, HEAD_DIM), lambda i: (i, 0, 0))
    before = pl.BlockSpec((1, BLOCK, HEAD_DIM), lambda i: (jnp.maximum(i * T - 1, 0), 0, 0))
    return pl.pallas_call(
        body, name=name, grid=(NB // T,),
        in_specs=[tile, tile, tile, before, before], out_specs=[tile, tile],
        out_shape=[jax.ShapeDtypeStruct(q.shape, BF16), jax.ShapeDtypeStruct(q.shape, F32)],
        compiler_params=_params(("parallel",)),
    )(q, k, v, k, v)


def _sw_bwd(q, k, v, do, lse, tt, name):
    NB = q.shape[0]
    T = SW_BLOCKS_PER_STEP
    per_group = NB // len(DILATIONS)
    nt = (((1,), (1,)), ((), ()))
    tn = (((0,), (0,)), ((), ()))

    def body(q_ref, k_ref, v_ref, do_ref, lse_ref, tt_ref, kp_ref, vp_ref, qn_ref, don_ref, lsen_ref,
             ttn_ref, dq_ref, dk_ref, dv_ref):
        b0 = pl.program_id(0) * T
        nseq = _seq_blocks(b0, per_group)
        cur_mask, prev_mask = _sw_masks()

        def probs(qq, kk, lse_b, mask, gate):
            s = lax.dot_general(qq, kk, nt, preferred_element_type=F32) * SCALE
            return jnp.exp(jnp.where(mask, s, NEG) + gate - lse_b)

        for t in range(T):
            has_prev = jnp.where(((b0 + t) & (nseq - 1)) != 0, 0.0, NEG)
            has_next = jnp.where(((b0 + t + 1) & (nseq - 1)) != 0, 0.0, NEG)
            last = t == T - 1
            qt, kt, vt, dot = q_ref[t], k_ref[t], v_ref[t], do_ref[t]
            kp = kp_ref[0] if t == 0 else k_ref[t - 1]
            vp = vp_ref[0] if t == 0 else v_ref[t - 1]
            qn = qn_ref[0] if last else q_ref[t + 1]
            don = don_ref[0] if last else do_ref[t + 1]
            lsen = lsen_ref[0] if last else lse_ref[t + 1]
            ttn = ttn_ref[0] if last else tt_ref[t + 1]
            p_cc = probs(qt, kt, lse_ref[t], cur_mask, 0.0)
            p_cp = probs(qt, kp, lse_ref[t], prev_mask, has_prev)
            p_nc = probs(qn, kt, lsen, prev_mask, has_next)
            ds_cc = p_cc * (lax.dot_general(dot, vt, nt, preferred_element_type=F32) + tt_ref[t])
            ds_cp = p_cp * (lax.dot_general(dot, vp, nt, preferred_element_type=F32) + tt_ref[t])
            ds_nc = p_nc * (lax.dot_general(don, vt, nt, preferred_element_type=F32) + ttn)
            ds_cc, ds_cp, ds_nc = ds_cc.astype(BF16), ds_cp.astype(BF16), ds_nc.astype(BF16)
            dq = (jnp.dot(ds_cc, kt, preferred_element_type=F32)
                  + jnp.dot(ds_cp, kp, preferred_element_type=F32))
            dk = (lax.dot_general(ds_cc, qt, tn, preferred_element_type=F32)
                  + lax.dot_general(ds_nc, qn, tn, preferred_element_type=F32))
            dv = (lax.dot_general(p_cc.astype(BF16), dot, tn, preferred_element_type=F32)
                  + lax.dot_general(p_nc.astype(BF16), don, tn, preferred_element_type=F32))
            dq_ref[t] = (dq * SCALE).astype(BF16)
            dk_ref[t] = (dk * SCALE).astype(BF16)
            dv_ref[t] = dv.astype(BF16)

    tile = pl.BlockSpec((T, BLOCK, HEAD_DIM), lambda i: (i, 0, 0))
    before = pl.BlockSpec((1, BLOCK, HEAD_DIM), lambda i: (jnp.maximum(i * T - 1, 0), 0, 0))
    after = pl.BlockSpec((1, BLOCK, HEAD_DIM), lambda i: (jnp.minimum(i * T + T, NB - 1), 0, 0))
    out = jax.ShapeDtypeStruct(q.shape, BF16)
    return pl.pallas_call(
        body, name=name, grid=(NB // T,),
        in_specs=[tile] * 6 + [before, before, after, after, after, after],
        out_specs=[tile] * 3, out_shape=[out] * 3,
        compiler_params=_params(("parallel",)),
    )(q, k, v, do, lse, tt, k, v, q, do, lse, tt)


def _group_softmax(lse):
    m = jnp.max(lse, axis=0, keepdims=True)
    e = jnp.exp(lse - m)
    return e / jnp.sum(e, axis=0, keepdims=True)


def _mix_fwd(o, lse, name):
    G, S, W = o.shape
    tr = min(256, S)
    blk = pl.BlockSpec((G, tr, W), lambda i: (0, i, 0))

    def fn(ov, lv):
        return (jnp.sum(_group_softmax(lv) * ov.astype(F32), axis=0),)

    return _ew(fn, [(o, blk), (lse, blk)], [((S, W), BF16, _rows(tr, W), False)], (S // tr,), name)


def _mix_bwd(dya, ya, o, lse, name):
    G, S, W = o.shape
    tr = min(256, S)
    blk = pl.BlockSpec((G, tr, HEAD_DIM), lambda i, h: (0, i, h))
    row = pl.BlockSpec((tr, HEAD_DIM), lambda i, h: (i, h))

    def fn(dy, yv, ov, lv):
        w = _group_softmax(lv)
        dyf = dy.astype(F32)
        inner = jnp.sum(dyf * yv.astype(F32), axis=-1, keepdims=True)
        return w * dyf[None], -w * inner[None]

    return _ew(fn, [(dya, row), (ya, row), (o, blk), (lse, blk)],
               [((G, S, W), BF16, blk, False), ((G, S, W), F32, blk, False)],
               (S // tr, HEADS_PER_DIL), name)


CUM_BLOCK = 256


def _split3(x):
    hi = x.astype(BF16)
    r = x - hi.astype(F32)
    mid = r.astype(BF16)
    lo = (r - mid.astype(F32)).astype(BF16)
    return hi, mid, lo


def _tri_matmul(tri, x):
    return sum(jnp.dot(tri, part, preferred_element_type=F32) for part in _split3(x))


def _log_sigmoid(x):
    return jnp.minimum(x, 0.0) - jnp.log(1.0 + jnp.exp(-jnp.abs(x)))


def _fox_prep(f, b, name):
    S = f.shape[0]
    tb = min(CUM_BLOCK, S)

    def body(f_ref, b_ref, c_ref, carry):
        @pl.when(pl.program_id(0) == 0)
        def _():
            carry[...] = jnp.zeros_like(carry)

        ls = _log_sigmoid(f_ref[...] + b_ref[...])
        r = lax.broadcasted_iota(jnp.int32, (tb, tb), 0)
        cidx = lax.broadcasted_iota(jnp.int32, (tb, tb), 1)
        tri = jnp.where(r >= cidx, 1.0, 0.0).astype(BF16)
        c_ref[...] = _tri_matmul(tri, ls) + carry[...]
        carry[...] += jnp.sum(ls, axis=0, keepdims=True)

    return pl.pallas_call(
        body, name=name, grid=(S // tb,),
        in_specs=[_rows(tb, LANES), _bcast(LANES)], out_specs=_rows(tb, LANES),
        out_shape=jax.ShapeDtypeStruct((S, LANES), F32),
        scratch_shapes=[pltpu.VMEM((1, LANES), F32)],
        compiler_params=_params(("arbitrary",)),
    )(f, b)


def _fox_prep_bwd(dc, f, b, name):
    S = f.shape[0]
    tb = min(CUM_BLOCK, S)
    nb = S // tb

    def body(dc_ref, f_ref, b_ref, df_ref, db_ref, carry):
        @pl.when(pl.program_id(0) == 0)
        def _():
            carry[...] = jnp.zeros_like(carry)
            db_ref[...] = jnp.zeros_like(db_ref)

        r = lax.broadcasted_iota(jnp.int32, (tb, tb), 0)
        cidx = lax.broadcasted_iota(jnp.int32, (tb, tb), 1)
        tri = jnp.where(r <= cidx, 1.0, 0.0).astype(BF16)
        dcv = dc_ref[...]
        dls = _tri_matmul(tri, dcv) + carry[...]
        carry[...] += jnp.sum(dcv, axis=0, keepdims=True)
        z = f_ref[...] + b_ref[...]
        df = dls * (1.0 / (1.0 + jnp.exp(z)))
        df_ref[...] = df
        db_ref[...] += jnp.sum(df, axis=0, keepdims=True)

    rev = pl.BlockSpec((tb, LANES), lambda i: (nb - 1 - i, 0))
    return pl.pallas_call(
        body, name=name, grid=(nb,),
        in_specs=[rev, rev, _bcast(LANES)], out_specs=[rev, _bcast(LANES)],
        out_shape=[jax.ShapeDtypeStruct((S, LANES), F32), jax.ShapeDtypeStruct((1, LANES), F32)],
        scratch_shapes=[pltpu.VMEM((1, LANES), F32)],
        compiler_params=_params(("arbitrary",)),
    )(dc, f, b)


FOX_Q_TILE = 512
_FOX_Q0 = N_DIL_HEADS
_FOX_K0 = N_HEADS + N_DIL_HEADS
_FOX_V0 = 2 * N_HEADS + N_DIL_HEADS


def _fox_scores(q, k, cq, ck, q0):
    nt = (((1,), (1,)), ((), ()))
    s = lax.dot_general(q, k, nt, preferred_element_type=F32) * SCALE + cq - ck
    qpos = q0 + lax.broadcasted_iota(jnp.int32, s.shape, 0)
    kpos = lax.broadcasted_iota(jnp.int32, s.shape, 1)
    return jnp.where(kpos <= qpos, s, NEG)


def _per_query_tile(nq, tq, fn):
    step = pl.program_id(1)
    for n in range(nq):
        @pl.when(step == n)
        def _(n=n):
            fn(n, pl.ds(0, (n + 1) * tq))


def _fox_fwd(z, cq, ck, name):
    S = z.shape[0]
    tq = min(FOX_Q_TILE, S)

    def body(q_ref, k_ref, v_ref, cq_ref, ck_ref, o_ref, lse_ref):
        def tile(n, keys):
            s = _fox_scores(q_ref[...], k_ref[keys, :], cq_ref[...], ck_ref[:, keys], n * tq)
            m = jnp.max(s, axis=-1, keepdims=True)
            p = jnp.exp(s - m)
            l = jnp.sum(p, axis=-1, keepdims=True)
            o = jnp.dot(p.astype(BF16), v_ref[keys, :], preferred_element_type=F32)
            o_ref[...] = (o / l).astype(o_ref.dtype)
            lse_ref[...] = m + jnp.log(l)

        _per_query_tile(S // tq, tq, tile)

    qblk = lambda c0: pl.BlockSpec((tq, HEAD_DIM), lambda h, i: (i, c0 + h))
    full = lambda c0: pl.BlockSpec((S, HEAD_DIM), lambda h, i: (0, c0 + h))
    col = pl.BlockSpec((None, tq, 1), lambda h, i: (h, i, 0))
    rowv = pl.BlockSpec((None, 1, S), lambda h, i: (h, 0, 0))
    return pl.pallas_call(
        body, name=name, grid=(N_FOX_HEADS, S // tq),
        in_specs=[qblk(_FOX_Q0), full(_FOX_K0), full(_FOX_V0), col, rowv],
        out_specs=[qblk(0), col],
        out_shape=[jax.ShapeDtypeStruct((S, FOX_WIDTH), BF16),
                   jax.ShapeDtypeStruct((N_FOX_HEADS, S, 1), F32)],
        compiler_params=_params(("parallel", "parallel")),
    )(z, z, z, cq, ck)


def _fox_bwd(z, cq, ck, lse, yb, dyb, name):
    S = z.shape[0]
    tq = min(FOX_Q_TILE, S)
    nq = S // tq
    nt = (((1,), (1,)), ((), ()))
    tn = (((0,), (0,)), ((), ()))

    def body(q_ref, k_ref, v_ref, cq_ref, ck_ref, lse_ref, o_ref, do_ref,
             dq_ref, dk_ref, dv_ref, dc_ref, dk_acc, dv_acc):
        i = pl.program_id(1)

        @pl.when(i == 0)
        def _():
            dk_acc[...] = jnp.zeros_like(dk_acc)
            dv_acc[...] = jnp.zeros_like(dv_acc)
            dc_ref[...] = jnp.zeros_like(dc_ref)

        def tile(n, keys):
            q, k, v, do = q_ref[...], k_ref[keys, :], v_ref[keys, :], do_ref[...]
            s = _fox_scores(q, k, cq_ref[...], ck_ref[:, keys], n * tq)
            p = jnp.exp(s - lse_ref[...])
            dp = lax.dot_general(do, v, nt, preferred_element_type=F32)
            ds = p * (dp - jnp.sum(p * dp, axis=-1, keepdims=True))
            dsb = ds.astype(BF16)
            dq_ref[...] = (jnp.dot(dsb, k, preferred_element_type=F32) * SCALE).astype(BF16)
            dk_acc[keys, :] += lax.dot_general(dsb, q, tn, preferred_element_type=F32) * SCALE
            dv_acc[keys, :] += lax.dot_general(p.astype(BF16), do, tn, preferred_element_type=F32)
            dc_ref[:, keys] -= jnp.sum(ds, axis=0, keepdims=True)

        _per_query_tile(nq, tq, tile)

        @pl.when(i == nq - 1)
        def _():
            dk_ref[...] = dk_acc[...].astype(BF16)
            dv_ref[...] = dv_acc[...].astype(BF16)

    qblk = lambda c0: pl.BlockSpec((tq, HEAD_DIM), lambda h, i: (i, c0 + h))
    full = lambda c0: pl.BlockSpec((S, HEAD_DIM), lambda h, i: (0, c0 + h))
    col = pl.BlockSpec((None, tq, 1), lambda h, i: (h, i, 0))
    rowv = pl.BlockSpec((None, 1, S), lambda h, i: (h, 0, 0))
    wide = jax.ShapeDtypeStruct((S, FOX_WIDTH), BF16)
    return pl.pallas_call(
        body, name=name, grid=(N_FOX_HEADS, nq),
        in_specs=[qblk(_FOX_Q0), full(_FOX_K0), full(_FOX_V0), col, rowv, col, qblk(0), qblk(0)],
        out_specs=[qblk(0), full(0), full(0), rowv],
        out_shape=[wide, wide, wide, jax.ShapeDtypeStruct((N_FOX_HEADS, 1, S), F32)],
        scratch_shapes=[pltpu.VMEM((S, HEAD_DIM), F32), pltpu.VMEM((S, HEAD_DIM), F32)],
        compiler_params=_params(("parallel", "arbitrary")),
    )(z, z, z, cq, ck, lse, yb, dyb)


def _sigmoid(x):
    return 1.0 / (1.0 + jnp.exp(-x))


def _merge_fwd(gates, a, bm, name):
    S, D = a.shape
    tr = _row_tile(S, D * 4)
    g1 = pl.BlockSpec((tr, D), lambda i: (i, 0))
    g2 = pl.BlockSpec((tr, D), lambda i: (i, 1))

    def fn(x1, x2, av, bv):
        return (x1.astype(F32) * av.astype(F32) + x2.astype(F32) * bv.astype(F32),)

    return _ew(fn, [(gates, g1), (gates, g2), (a, _rows(tr, D)), (bm, _rows(tr, D))],
               [((S, D), BF16, _rows(tr, D), False)], (S // tr,), name)


def _merge_bwd(dmerged, gates, a, bm, name):
    S, D = a.shape
    tr = _row_tile(S, D * 8)
    g1 = pl.BlockSpec((tr, D), lambda i: (i, 0))
    g2 = pl.BlockSpec((tr, D), lambda i: (i, 1))

    def fn(dm, x1, x2, av, bv):
        dm, x1, x2 = dm.astype(F32), x1.astype(F32), x2.astype(F32)
        dg1 = dm * av.astype(F32) * x1 * (1.0 - x1)
        dg2 = dm * bv.astype(F32) * x2 * (1.0 - x2)
        dgp = jnp.concatenate([dg1, dg2], axis=1)
        return dm * x1, dm * x2, dgp, jnp.sum(dgp, axis=0, keepdims=True)

    return _ew(fn, [(dmerged, _rows(tr, D)), (gates, g1), (gates, g2), (a, _rows(tr, D)), (bm, _rows(tr, D))],
               [((S, D), BF16, _rows(tr, D), False), ((S, D), BF16, _rows(tr, D), False),
                ((S, 2 * D), BF16, _rows(tr, 2 * D), False), ((1, 2 * D), F32, _bcast(2 * D), True)],
               (S // tr,), name)


def _ple_bwd(dh, pg, pe, name):
    S, D = dh.shape
    tr = _row_tile(S, D * 4)

    def fn(d, g, e):
        g, e = g.astype(F32), e.astype(F32)
        return d * g, d * e * g * (1.0 - g)

    spec = _rows(tr, D)
    return _ew(fn, [(dh, spec), (pg, spec), (pe, spec)],
               [((S, D), BF16, spec, False), ((S, D), BF16, spec, False)], (S // tr,), name)


def _position():
    x, y, c = lax.axis_index("x"), lax.axis_index("y"), lax.axis_index("c")
    chips = [(1 - x, y), (x, 1 - y), (1 - x, 1 - y)]
    return x, y, c, chips


def _remote(src, dst, send_sem, recv_sem, target):
    return pltpu.make_async_remote_copy(src_ref=src, dst_ref=dst, send_sem=send_sem, recv_sem=recv_sem,
                                        device_id=target, device_id_type=MESH)


HBM = pl.BlockSpec(memory_space=pltpu.HBM)
SEM = pl.BlockSpec(memory_space=pltpu.SEMAPHORE)
EFFECT = pltpu.SideEffectType.DATAFLOW_SIDE_EFFECTING


def _copies_start(plan, arrays, sem_shape, after, name):
    n = len(arrays)

    def body(*refs):
        send_sems, recv_sems, token = refs[n + 1], refs[n + 2], refs[-1]
        for send, _ in plan(refs[:n], send_sems, recv_sems):
            send.start()
        token[...] = jnp.zeros_like(token)

    outs = pl.pallas_call(
        body, name=name,
        out_shape=(pltpu.SemaphoreType.DMA(sem_shape), pltpu.SemaphoreType.DMA(sem_shape),
                   *[pltpu.HBM(a.shape, a.dtype) for a in arrays], jax.ShapeDtypeStruct((8, LANES), F32)),
        in_specs=[HBM] * n + [ANY],
        out_specs=(SEM, SEM, *[HBM] * n, pl.BlockSpec(memory_space=pltpu.VMEM)),
        input_output_aliases={a: 2 + a for a in range(n)},
        compiler_params=pltpu.CompilerParams(has_side_effects=EFFECT),
    )(*[pltpu.with_memory_space_constraint(a, pltpu.HBM) for a in arrays], after)
    return outs[0], outs[1], list(outs[2:2 + n]), outs[-1]


def _copies_wait(plan, send_sems, recv_sems, arrays, after, name):
    n = len(arrays)

    def body(*refs):
        for send, recv in plan(refs[:n], refs[n], refs[n + 1]):
            send.wait_send()
            recv.wait_recv()

    return list(pl.pallas_call(
        body, name=name,
        out_shape=[pltpu.HBM(a.shape, a.dtype) for a in arrays],
        in_specs=[HBM] * n + [SEM, SEM, ANY], out_specs=[HBM] * n,
        input_output_aliases={a: a for a in range(n)},
        compiler_params=pltpu.CompilerParams(has_side_effects=EFFECT),
    )(*arrays, send_sems, recv_sems, after))


def _gather_ici_plan(refs, send_sems, recv_sems):
    x, y, c, chips = _position()
    plan = []
    for a, ref in enumerate(refs):
        rh = ref.shape[1] // 2
        mine = ref.at[2 * x + y, pl.ds(c * rh, rh)]
        for j, (cx, cy) in enumerate(chips):
            landed = ref.at[2 * cx + cy, pl.ds(c * rh, rh)]
            plan.append((_remote(mine, mine, send_sems.at[3 * a + j], recv_sems.at[3 * a + j], (cx, cy, c)),
                         _remote(landed, landed, send_sems.at[3 * a + j], recv_sems.at[3 * a + j], (cx, cy, c))))
    return plan


def _gather_d2d_plan(refs, send_sems, recv_sems):
    x, y, c, chips = _position()
    sibling = (x, y, 1 - c)
    plan = []
    for a, ref in enumerate(refs):
        rh = ref.shape[1] // 2
        for j, (cx, cy) in enumerate(chips):
            landed = ref.at[2 * cx + cy, pl.ds(c * rh, rh)]
            theirs = ref.at[2 * cx + cy, pl.ds((1 - c) * rh, rh)]
            plan.append((_remote(landed, landed, send_sems.at[3 * a + j], recv_sems.at[3 * a + j], sibling),
                         _remote(theirs, theirs, send_sems.at[3 * a + j], recv_sems.at[3 * a + j], sibling)))
    return plan


def _scatter_ici_plan(refs, send_sems, recv_sems):
    x, y, c, chips = _position()
    n = len(refs) // 2
    plan = []
    for a in range(n):
        for j, (cx, cy) in enumerate(chips):
            cp = _remote(refs[a].at[2 * cx + cy], refs[n + a].at[j], send_sems.at[3 * a + j], recv_sems.at[3 * a + j],
                         (cx, cy, c))
            plan.append((cp, cp))
    return plan


def _place_shard(w, layer, me, after, name):
    _, r, cc = w.shape
    tr = _row_tile(r, cc * 4)

    def body(me_ref, w_ref, after_ref, o_ref):
        o_ref[...] = w_ref[...].astype(o_ref.dtype)

    return pl.pallas_call(
        body, name=name,
        grid_spec=pltpu.PrefetchScalarGridSpec(
            num_scalar_prefetch=1, grid=(r // tr,),
            in_specs=[pl.BlockSpec((None, tr, cc), lambda i, me_ref: (layer, i, 0)), ANY],
            out_specs=pl.BlockSpec((None, tr, cc), lambda i, me_ref: (me_ref[0], i, 0))),
        out_shape=jax.ShapeDtypeStruct((N_CHIPS, r, cc), BF16),
        compiler_params=_params(("parallel",)),
    )(me, w, after)


def _pair_plan(refs, send_sems, recv_sems):
    x, y, c, _ = _position()
    n = len(refs) // 2
    plan = []
    for a in range(n):
        rh = refs[a].shape[1] // 2
        cp = _remote(refs[a].at[:, pl.ds((1 - c) * rh, rh)], refs[n + a], send_sems.at[a], recv_sems.at[a],
                     (x, y, 1 - c))
        plan.append((cp, cp))
    return plan


def _pair_sum(mine, theirs, c, name):
    nch, rh, cc = theirs.shape
    tr = _row_tile(rh, cc * 4)
    nb = rh // tr

    def body(c_ref, m_ref, t_ref, o_ref):
        o_ref[...] = (m_ref[...].astype(F32) + t_ref[...].astype(F32)).astype(o_ref.dtype)

    return pl.pallas_call(
        body, name=name,
        grid_spec=pltpu.PrefetchScalarGridSpec(
            num_scalar_prefetch=1, grid=(nch, nb),
            in_specs=[pl.BlockSpec((1, tr, cc), lambda k, i, c_ref: (k, c_ref[0] * nb + i, 0)),
                      pl.BlockSpec((1, tr, cc), lambda k, i, c_ref: (k, i, 0))],
            out_specs=pl.BlockSpec((1, tr, cc), lambda k, i, c_ref: (k, i, 0))),
        out_shape=jax.ShapeDtypeStruct(theirs.shape, BF16),
        compiler_params=_params(("parallel", "parallel")),
    )(c, mine, theirs)


def _chip_sum(own, others, me, c, total, layer, depth, name):
    _, rh, cc = own.shape
    tr = _row_tile(rh, cc * 4)
    nb = rh // tr
    chained = total is not None

    def body(me_ref, c_ref, o_ref, r_ref, *rest):
        g_ref = rest[-1]
        g_ref[...] = (o_ref[0].astype(F32) + r_ref[0].astype(F32)) + (r_ref[1].astype(F32) + r_ref[2].astype(F32))

    in_specs = [pl.BlockSpec((1, tr, cc), lambda i, me_ref, c_ref: (me_ref[0], i, 0)),
                pl.BlockSpec((3, tr, cc), lambda i, me_ref, c_ref: (0, i, 0))]
    args = [me, c, own, others]
    if chained:
        in_specs.append(ANY)
        args.append(total)
    return pl.pallas_call(
        body, name=name,
        grid_spec=pltpu.PrefetchScalarGridSpec(
            num_scalar_prefetch=2, grid=(nb,), in_specs=in_specs,
            out_specs=pl.BlockSpec((None, tr, cc), lambda i, me_ref, c_ref: (layer, c_ref[0] * nb + i, 0))),
        out_shape=jax.ShapeDtypeStruct((depth, 2 * rh, cc), F32),
        input_output_aliases={4: 0} if chained else {},
        compiler_params=_params(("parallel",)),
    )(*args)


def _half_exchange(totals, layer, name):
    n = len(totals)

    def body(*refs):
        outs = refs[n:2 * n]
        send_sems, recv_sems = refs[2 * n:]
        x, y, c, _ = _position()
        copies = []
        for a in range(n):
            rh = outs[a].shape[1] // 2
            mine = outs[a].at[layer, pl.ds(c * rh, rh)]
            cp = _remote(mine, mine, send_sems.at[a], recv_sems.at[a], (x, y, 1 - c))
            cp.start()
            copies.append(cp)
        for a, cp in enumerate(copies):
            rh = outs[a].shape[1] // 2
            theirs = outs[a].at[layer, pl.ds((1 - c) * rh, rh)]
            cp.wait_send()
            _remote(theirs, theirs, send_sems.at[a], recv_sems.at[a], (x, y, 1 - c)).wait_recv()

    return pl.pallas_call(
        body, name=name, in_specs=[ANY] * n, out_specs=[ANY] * n,
        out_shape=[jax.ShapeDtypeStruct(t.shape, t.dtype) for t in totals],
        input_output_aliases={a: a for a in range(n)},
        scratch_shapes=[pltpu.SemaphoreType.DMA((n,)), pltpu.SemaphoreType.DMA((n,))],
    )(*totals)


def _allgather_devices(v, name):
    m_per, n = v.shape

    def body(x_ref, out_ref, send_sems, recv_sems, local_sem):
        x, y, c, chips = _position()
        me, sibling = (x, y, c), (x, y, 1 - c)

        def rows(px, py, pc):
            return out_ref.at[pl.ds((4 * px + 2 * py + pc) * m_per, m_per), :]

        def copy(k, block, to, src=None):
            return _remote(rows(*block) if src is None else src, rows(*block), send_sems.at[k], recv_sems.at[k], to)

        mine = pltpu.make_async_copy(x_ref, rows(*me), local_sem)
        mine.start()
        first = [copy(0, me, sibling, src=x_ref)]
        first += [copy(1 + j, me, (*chip, c), src=x_ref) for j, chip in enumerate(chips)]
        for cp in first:
            cp.start()
        passed = [copy(4 + j, (*chip, c), sibling) for j, chip in enumerate(chips)]
        for j, chip in enumerate(chips):
            copy(1 + j, (*chip, c), me).wait_recv()
            passed[j].start()
        copy(0, sibling, me).wait_recv()
        for j, chip in enumerate(chips):
            copy(4 + j, (*chip, 1 - c), me).wait_recv()
        for cp in first + passed:
            cp.wait_send()
        mine.wait()

    vm = pl.BlockSpec(memory_space=pltpu.VMEM)
    return pl.pallas_call(
        body, name=name, in_specs=[vm], out_specs=vm,
        out_shape=jax.ShapeDtypeStruct((8 * m_per, n), v.dtype),
        scratch_shapes=[pltpu.SemaphoreType.DMA((7,)), pltpu.SemaphoreType.DMA((7,)), pltpu.SemaphoreType.DMA],
    )(v)


def _adamw_math(w, g, m, v):
    m = ADAM_B1 * m + (1.0 - ADAM_B1) * g
    v = ADAM_B2 * v + (1.0 - ADAM_B2) * (g * g)
    m_hat = m / (1.0 - ADAM_B1 ** ADAM_STEP)
    v_hat = v / (1.0 - ADAM_B2 ** ADAM_STEP)
    delta = -ADAM_LR * (m_hat / (jnp.sqrt(v_hat) + ADAM_EPS) + ADAM_WD * w)
    return delta, m, v


def _adamw(w, g, m, v, lo, hi, prev, after, name):
    depth, r, cc = w.shape
    tr = _row_tile(r, cc * 4 * 2)
    spec = pl.BlockSpec((1, tr, cc), lambda l, i: (lo + l, i, 0))

    def body(w_ref, g_ref, m_ref, v_ref, *rest):
        outs = rest[-4:]
        gv = g_ref[...]
        for o_ref, val in zip(outs, (gv,) + _adamw_math(w_ref[...], gv, m_ref[...], v_ref[...])):
            o_ref[...] = val

    prev = list(prev) if prev is not None else []
    return tuple(pl.pallas_call(
        body, name=name, grid=(hi - lo, r // tr),
        in_specs=[spec] * 4 + [ANY] * (1 + len(prev)), out_specs=[spec] * 4,
        out_shape=[jax.ShapeDtypeStruct(w.shape, F32)] * 4,
        input_output_aliases={5 + k: k for k in range(len(prev))},
        compiler_params=_params(("parallel", "parallel")),
    )(w, g, m, v, after, *prev))


def _adamw_small(w, parts, m, v, name):
    M = w.shape[0]

    def body(w_ref, p_ref, m_ref, v_ref, g_ref, d_ref, nm_ref, nv_ref):
        g = p_ref[pl.ds(0, M), :]
        for k in range(1, 8):
            g = g + p_ref[pl.ds(k * M, M), :]
        d, nm, nv = _adamw_math(w_ref[...], g, m_ref[...], v_ref[...])
        g_ref[...] = g
        d_ref[...] = d
        nm_ref[...] = nm
        nv_ref[...] = nv

    vm = pl.BlockSpec(memory_space=pltpu.VMEM)
    return pl.pallas_call(
        body, name=name, in_specs=[vm] * 4, out_specs=[vm] * 4,
        out_shape=[jax.ShapeDtypeStruct(w.shape, F32)] * 4,
    )(w, parts, m, v)


def _layer_fwd(h0, p_l, W, small, tabs, after_mlp):
    S, D = h0.shape
    ctab, stab = tabs
    u = _rms_fwd(h0, small["g_mix"], "rms_mix")
    z = _mm(u, W["w_qkv"], mode="nn", name="mm_qkv", b_cols=3 * ATTN_WIDTH)
    f = _mm(u, W["w_f"], mode="nn", name="mm_f", out_dtypes=(F32,))
    gates = _mm(u, W["w_gate"], mode="nn", name="mm_gate", b_chunked=True, extras=[(small["b_gate"], "row")],
                epilogue=lambda acc, b: (_sigmoid(acc + b),))
    qr, kr = _rope(z, 0, z, N_HEADS, ctab, stab, 1.0, "rope_fwd")
    qs, ks, vs = _to_strided(qr), _to_strided(kr), _to_strided(z[:, 2 * ATTN_WIDTH:2 * ATTN_WIDTH + DIL_WIDTH])
    o_s, lse_s = _sw_fwd(qs, ks, vs, "sw_fwd")
    o_g, lse_g = _from_strided(o_s, S), _from_strided(lse_s, S)
    ya = _mix_fwd(o_g, lse_g, "mix_fwd")
    a = _mm(ya, W["w_br_a"], mode="nn", name="mm_br_a", b_chunked=True)
    cum = _fox_prep(f, small["b_f"], "fox_prep")
    cq = cum[:, :N_FOX_HEADS].T[:, :, None]
    ck = cum[:, :N_FOX_HEADS].T[:, None, :]
    yb, lse_f = _fox_fwd(z, cq, ck, "fox_fwd")
    bm = _mm(yb, W["w_br_b"], mode="nn", name="mm_br_b", b_chunked=True)
    merged = _merge_fwd(gates, a, bm, "merge_fwd")
    h1 = _mm(merged, W["w_o"], mode="nn", name="mm_o", out_dtypes=(F32,), extras=[(h0, "tile")],
             epilogue=lambda acc, r: (acc + r,), tj=512)
    m = _rms_fwd(h1, small["g_mlp"], "rms_mlp")
    ra, act = _mm(m, W["w_up"], mode="nn", name="mm_up", b_chunked=True, out_dtypes=(BF16, BF16),
                  epilogue=lambda acc: (jnp.maximum(acc, 0.0), jnp.square(jnp.maximum(acc, 0.0))))
    h2 = _mm(act, W["w_down"], mode="nn", name="mm_down", out_dtypes=(F32,), extras=[(h1, "tile")],
             epilogue=lambda acc, r: (acc + r,), ti=1024, tj=512, tc=4096)
    token = after_mlp(h2)
    n = _rms_fwd(h2, small["g_ple"] if token is None else small["g_ple"] + token[0, 0], "rms_ple")
    pg = _mm(n, W["w_ple_gate"], mode="nn", name="mm_ple_gate", epilogue=lambda acc: (_sigmoid(acc),))
    h3, pe = _mm(p_l, W["w_ple"], mode="nn", name="mm_ple", b_chunked=True, out_dtypes=(F32, BF16),
                 extras=[(h2, "tile"), (pg, "tile")], tj=256,
                 epilogue=lambda acc, r, g: (r + g.astype(F32) * acc, acc))
    saved = dict(h0=h0, u=u, z=z, f=f, gates=gates, qs=qs, ks=ks, vs=vs, lse_s=lse_s, o_g=o_g, lse_g=lse_g,
                 ya=ya, a=a, cq=cq, ck=ck, yb=yb, lse_f=lse_f, bm=bm, merged=merged, h1=h1, m=m, ra=ra,
                 act=act, h2=h2, n=n, pg=pg, pe=pe, p_l=p_l)
    return h3, saved


def _after(hooks, name, small_value, *args):
    token = hooks[name](*args) if name in hooks else None
    return small_value if token is None else small_value + token[0, 0]


def _layer_bwd(dh3, sv, W, small, tabs, hooks):
    S, D = dh3.shape
    ctab, stab = tabs
    gw, gs = {}, {}
    tn = functools.partial(_mm, mode="tn", ti=1024, tj=2048)
    dpe, dpg = _ple_bwd(dh3, sv["pg"], sv["pe"], "ple_bwd")
    gw["w_ple"] = tn(sv["p_l"], dpe, name="dw_ple", out_chunks=N_CHIPS)
    gw["w_ple_gate"] = tn(sv["n"], dpg, name="dw_ple_gate").reshape(N_CHIPS, D // N_CHIPS, D)
    dn = _mm(dpg, W["w_ple_gate"], mode="nt", name="mm_dn")
    dh2, dh2b, gs["g_ple"] = _rms_bwd(sv["h2"], small["g_ple"], dn, dh3, "rms_ple_bwd")
    da = _mm(dh2b, W["w_down"], mode="nt", name="mm_dact", extras=[(sv["ra"], "tile")],
             epilogue=lambda acc, r: (acc * (2.0 * r.astype(F32)),))
    FF = da.shape[1]
    g_mlp = _after(hooks, "mlp_grad", small["g_mlp"], da)
    gw["w_down"] = tn(sv["act"], dh2b, name="dw_down").reshape(N_CHIPS, FF // N_CHIPS, D)
    gw["w_up"] = tn(sv["m"], da, name="dw_up", out_chunks=N_CHIPS)
    g_mlp = _after(hooks, "mlp_weights", g_mlp, dict(gw))
    dm = _mm(da, W["w_up"], mode="nt", name="mm_dm", b_chunked=True)
    dh1, dh1b, gs["g_mlp"] = _rms_bwd(sv["h1"], g_mlp, dm, dh2, "rms_mlp_bwd")
    dmerged = _mm(dh1b, W["w_o"], mode="nt", name="mm_dmerged")
    b_f = _after(hooks, "merge_grad", small["b_f"], dmerged)
    gw["w_o"] = tn(sv["merged"], dh1b, name="dw_o").reshape(N_CHIPS, D // N_CHIPS, D)
    d_a, d_b, dgp, gs["b_gate"] = _merge_bwd(dmerged, sv["gates"], sv["a"], sv["bm"], "merge_bwd")
    gw["w_gate"] = tn(sv["u"], dgp, name="dw_gate", out_chunks=N_CHIPS)
    gw["w_br_a"] = tn(sv["ya"], d_a, name="dw_br_a", out_chunks=N_CHIPS, tj=512)
    gw["w_br_b"] = tn(sv["yb"], d_b, name="dw_br_b", out_chunks=N_CHIPS, tj=512)
    dya = _mm(d_a, W["w_br_a"], mode="nt", name="mm_dya", b_chunked=True)
    dyb = _mm(d_b, W["w_br_b"], mode="nt", name="mm_dyb", b_chunked=True)
    z = sv["z"]
    dq_f, dk_f, dv_f, dck = _fox_bwd(z, sv["cq"], sv["ck"], sv["lse_f"], sv["yb"], dyb, "fox_bwd")
    dc = jnp.pad(dck[:, 0, :].T, ((0, 0), (0, LANES - N_FOX_HEADS)))
    df, dbf = _fox_prep_bwd(dc, sv["f"], b_f, "fox_prep_bwd")
    gs["b_f"] = dbf[:, :N_FOX_HEADS]
    lane = jnp.arange(LANES)[None, :] < N_FOX_HEADS
    dzf = jnp.where(lane, df, 0.0).astype(BF16)
    do_g, tt_g = _mix_bwd(dya, sv["ya"], sv["o_g"], sv["lse_g"], "mix_bwd")
    do_s = _to_strided(do_g.transpose(1, 0, 2).reshape(S, DIL_WIDTH))
    tt_s = _to_strided(tt_g.transpose(1, 0, 2).reshape(S, DIL_WIDTH))
    dq_s, dk_s, dv_s = _sw_bwd(sv["qs"], sv["ks"], sv["vs"], do_s, sv["lse_s"], tt_s, "sw_bwd")
    unstride = lambda t: _from_strided(t, S).transpose(1, 0, 2).reshape(S, DIL_WIDTH)
    dq_a, dk_a = _rope(unstride(dq_s), 0, unstride(dk_s), 0, ctab, stab, -1.0, "rope_bwd")
    dz = jnp.concatenate([dq_a, dq_f, dk_a, dk_f, unstride(dv_s), dv_f], axis=1)
    g_qkv = tn(sv["u"], dz, name="dw_qkv")
    g_f = tn(sv["u"], dzf, name="dw_f", tj=128)
    cols = W["w_in_cols"]
    g_in = jnp.concatenate([g_qkv, g_f[:, :N_FOX_HEADS]], axis=1)
    gw["w_in"] = jnp.stack([g_in[:, k * cols:(k + 1) * cols] for k in range(N_CHIPS)])
    du = _mm(dzf, W["w_f"], mode="nt", name="mm_du_f", out_dtypes=(F32,))
    du = _mm(dgp, W["w_gate"], mode="nt", name="mm_du_gate", b_chunked=True, out_dtypes=(F32,),
             extras=[(du, "tile")], epilogue=lambda acc, r: (acc + r,), tj=512)
    du = _mm(dz, W["w_qkv"], mode="nt", name="mm_du_qkv", b_cols=3 * ATTN_WIDTH, extras=[(du, "tile")],
             epilogue=lambda acc, r: (acc + r,), tj=512)
    dh0, _, gs["g_mix"] = _rms_bwd(sv["h0"], small["g_mix"], du, dh1, "rms_mix_bwd")
    return dh0, gw, gs


BIG = ("w_in", "w_gate", "w_br_a", "w_br_b", "w_o", "w_up", "w_down", "w_ple", "w_ple_gate")
EARLY = ("w_ple", "w_ple_gate", "w_down", "w_up")
SMALL = ("g_mix", "b_f", "b_gate", "g_mlp", "g_ple", "g_final")
ORDER = ("g_mix", "w_in", "b_f", "w_gate", "b_gate", "w_br_a", "w_br_b", "w_o", "g_mlp", "w_up", "w_down",
         "g_ple", "w_ple", "w_ple_gate", "g_final")


def _gathered_layer_weights(full, D):
    W = {}
    for name in ("w_gate", "w_br_a", "w_br_b", "w_up", "w_ple"):
        W[name] = full[name]
    for name in ("w_o", "w_down", "w_ple_gate"):
        t = full[name]
        W[name] = t.reshape(t.shape[0] * t.shape[1], t.shape[2])
    w_in = full["w_in"]
    cols = w_in.shape[2]
    w_in = jnp.concatenate([w_in[k] for k in range(N_CHIPS)], axis=1)
    W["w_qkv"] = w_in
    W["w_f"] = jnp.pad(w_in[:, 3 * ATTN_WIDTH:], ((0, 0), (0, LANES - N_FOX_HEADS)))
    W["w_in_cols"] = cols
    return W


def _pack_rows(vals):
    flat = jnp.concatenate([v.reshape(-1) for v in vals])
    rows = -(-flat.shape[0] // (8 * LANES)) * 8
    return jnp.pad(flat, (0, rows * LANES - flat.shape[0])).reshape(rows, LANES)


def _unpack_rows(packed, shapes):
    flat = packed.reshape(-1)
    out, pos = [], 0
    for s in shapes:
        size = 1
        for dim in s:
            size *= dim
        out.append(flat[pos:pos + size].reshape(s))
        pos += size
    return out


def _local_step(x, p, small_w, comm, loss_target):
    depth = p.shape[0]
    S, D = x.shape
    tabs = _rope_tables(S)
    h = x
    saved, weights, smalls = [], [], []
    state, _ = comm["gather_start"](0, x)
    full = comm["gather_finish"](comm["gather_mid"](state, x)[0], x)
    for l in range(depth):
        nxt = [None]
        g_mix = small_w["g_mix"][l][None]
        if l + 1 < depth:
            nxt[0], token = comm["gather_start"](l + 1, full["w_ple"])
            g_mix = g_mix + token[0, 0]

        def after_mlp(h2):
            if nxt[0] is None:
                return None
            nxt[0], token = comm["gather_mid"](nxt[0], h2)
            return token

        W = _gathered_layer_weights(full, D)
        sm = dict(g_mix=g_mix, g_mlp=small_w["g_mlp"][l][None],
                  g_ple=small_w["g_ple"][l][None], b_gate=small_w["b_gate"][l][None],
                  b_f=jnp.pad(small_w["b_f"][l][None], ((0, 0), (0, LANES - N_FOX_HEADS))))
        h, sv = _layer_fwd(h, p[l].astype(BF16), W, sm, tabs, after_mlp)
        if nxt[0] is not None:
            full = comm["gather_finish"](nxt[0], h)
        saved.append(sv)
        weights.append(W)
        smalls.append(sm)
    dh, loss_row, dg_final = _loss_head(h, small_w["g_final"][None], loss_target, "loss_head")
    gss = [None] * depth
    pending, token = [None], None
    for l in reversed(range(depth)):
        sm = smalls[l]
        if token is not None:
            sm = {**sm, "g_ple": sm["g_ple"] + token[0, 0]}

        def after_mlp_grad(da):
            if pending[0] is None:
                return None
            pending[0] = comm["reduce_mid"](pending[0], da)
            return pending[0][-1]

        hooks = dict(mlp_grad=after_mlp_grad)
        if l == 0 and "reduce_early_begin" in comm:
            hooks.update(mlp_weights=comm["reduce_early_begin"], merge_grad=comm["reduce_early_mid"])
        dh, gw, gss[l] = _layer_bwd(dh, saved[l], weights[l], sm, tabs, hooks)
        if pending[0] is not None:
            comm["reduce_end"](pending[0], dh)
        if l > 0:
            pending[0], token = comm["reduce_begin"](l, gw, dh)
    return loss_row, dh, gss, dg_final, gw


def _device_comm(w, m, v, depth, c_arr, me_arr):
    n = len(BIG)
    totals = {name: None for name in BIG}
    placed, results = [], {}

    def gather_start(l, after):
        if l == 0:
            placed.append([_place_shard(w[name], 0, me_arr, me_arr, "place_shard") for name in BIG])
        send, recv, bufs, token = _copies_start(_gather_ici_plan, placed[l], (3 * n,), after, f"gather_ici_start_{l}")
        if l == 0:
            placed.extend([_place_shard(w[name], k, me_arr, token, "place_shard") for name in BIG]
                          for k in range(1, depth))
        return (l, send, recv, bufs), token

    def gather_mid(state, after):
        l, send, recv, bufs = state
        if l == 0:
            after = placed[-1][-1]
        bufs = _copies_wait(_gather_ici_plan, send, recv, bufs, after, f"gather_ici_wait_{l}")
        send, recv, bufs, token = _copies_start(_gather_d2d_plan, bufs, (3 * n,), after, f"gather_d2d_start_{l}")
        return (l, send, recv, bufs), token

    def gather_finish(state, after):
        l, send, recv, bufs = state
        return dict(zip(BIG, _copies_wait(_gather_d2d_plan, send, recv, bufs, after, f"gather_d2d_wait_{l}")))

    def reduce_begin(l, gw, after, names=BIG, tag=""):
        grads = [gw[name] for name in names]
        landing = [lax.empty((g.shape[0], g.shape[1] // 2, g.shape[2]), g.dtype) for g in grads]
        send, recv, arrays, token = _copies_start(_pair_plan, grads + landing, (len(names),), after,
                                                  f"pair_start_{l}{tag}")
        return (l, tag, names, send, recv, arrays), token

    def reduce_mid(state, after):
        l, tag, names, send, recv, arrays = state
        k = len(names)
        arrays = _copies_wait(_pair_plan, send, recv, arrays, after, f"pair_wait_{l}{tag}")
        sums = [_pair_sum(g, t, c_arr, "rs_pair_sum") for g, t in zip(arrays[:k], arrays[k:])]
        landing = [lax.empty((3,) + s.shape[1:], s.dtype) for s in sums]
        send, recv, arrays, token = _copies_start(_scatter_ici_plan, sums + landing, (3 * k,), sums[0],
                                                  f"scatter_ici_start_{l}{tag}")
        return l, tag, names, send, recv, arrays, token

    def reduce_end(state, after):
        l, tag, names, send, recv, arrays, _ = state
        k = len(names)
        arrays = _copies_wait(_scatter_ici_plan, send, recv, arrays, after, f"scatter_ici_wait_{l}{tag}")
        done = [_chip_sum(s, o, me_arr, c_arr, totals[name], l, depth, "rs_chip_sum")
                for name, s, o in zip(names, arrays[:k], arrays[k:])]
        totals.update(zip(names, _half_exchange(done, l, "rs_half_exchange")))

    early = [None]

    def reduce_early_begin(gw):
        early[0], token = reduce_begin(0, gw, gw[EARLY[-1]], EARLY, "a")
        return token

    def reduce_early_mid(after):
        early[0] = reduce_mid(early[0], after)
        return early[0][-1]

    def reduce_last(gw, after):
        upper = {}

        def adamw_upper(names, token):
            for name in names:
                if depth > 1:
                    upper[name] = _adamw(w[name], totals[name], m[name], v[name], 1, depth, None, token,
                                         "adamw_upper")
                    token = upper[name][0]
            return token

        def adamw_first(names, token):
            for name in names:
                results[name] = _adamw(w[name], totals[name], m[name], v[name], 0, 1, upper.get(name), token,
                                       "adamw_first")
                token = results[name][0]
            return token

        late = BIG if early[0] is None else tuple(name for name in BIG if name not in EARLY)
        state, token = reduce_begin(0, gw, after, late, "b")
        state = reduce_mid(state, adamw_upper(("w_up",), token))
        token = adamw_upper([name for name in BIG if name != "w_up"], state[-1])
        if early[0] is not None:
            reduce_end(early[0], token)
            token = adamw_first(EARLY, totals[EARLY[0]])
        reduce_end(state, token)
        adamw_first(late, totals[late[0]])

    comm = dict(gather_start=gather_start, gather_mid=gather_mid, gather_finish=gather_finish,
                reduce_begin=reduce_begin, reduce_mid=reduce_mid, reduce_end=reduce_end, reduce_last=reduce_last,
                reduce_early_begin=reduce_early_begin, reduce_early_mid=reduce_early_mid)
    return comm, results


def kernel(x, p, g_mix, w_in, b_f, w_gate, b_gate, w_br_a, w_br_b, w_o, g_mlp, w_up, w_down, g_ple, w_ple, w_ple_gate, g_final, loss_target, m_g_mix, m_w_in, m_b_f, m_w_gate, m_b_gate, m_w_br_a, m_w_br_b, m_w_o, m_g_mlp, m_w_up, m_w_down, m_g_ple, m_w_ple, m_w_ple_gate, m_g_final, v_g_mix, v_w_in, v_b_f, v_w_gate, v_b_gate, v_w_br_a, v_w_br_b, v_w_o, v_g_mlp, v_w_up, v_w_down, v_g_ple, v_w_ple, v_w_ple_gate, v_g_final):
    w = dict(g_mix=g_mix, w_in=w_in, b_f=b_f, w_gate=w_gate, b_gate=b_gate, w_br_a=w_br_a, w_br_b=w_br_b,
             w_o=w_o, g_mlp=g_mlp, w_up=w_up, w_down=w_down, g_ple=g_ple, w_ple=w_ple, w_ple_gate=w_ple_gate,
             g_final=g_final)
    m = dict(g_mix=m_g_mix, w_in=m_w_in, b_f=m_b_f, w_gate=m_w_gate, b_gate=m_b_gate, w_br_a=m_w_br_a,
             w_br_b=m_w_br_b, w_o=m_w_o, g_mlp=m_g_mlp, w_up=m_w_up, w_down=m_w_down, g_ple=m_g_ple,
             w_ple=m_w_ple, w_ple_gate=m_w_ple_gate, g_final=m_g_final)
    v = dict(g_mix=v_g_mix, w_in=v_w_in, b_f=v_b_f, w_gate=v_w_gate, b_gate=v_b_gate, w_br_a=v_w_br_a,
             w_br_b=v_w_br_b, w_o=v_w_o, g_mlp=v_g_mlp, w_up=v_w_up, w_down=v_w_down, g_ple=v_g_ple,
             w_ple=v_w_ple, w_ple_gate=v_w_ple_gate, g_final=v_g_final)
    depth = p.shape[0]
    cx, cy, cc = lax.axis_index("x"), lax.axis_index("y"), lax.axis_index("c")
    c_arr = jnp.reshape(cc, (1,)).astype(jnp.int32)
    me_arr = jnp.reshape(2 * cx + cy, (1,)).astype(jnp.int32)

    comm, big_out = _device_comm(w, m, v, depth, c_arr, me_arr)
    loss_row, grad_x, gss, dg_final, gw0 = _local_step(x[0], p[:, 0], w, comm, loss_target[0])

    small_grads = [jnp.stack([gss[l][n][0] for l in range(depth)]) for n in SMALL[:-1]] + [dg_final[0]]
    shapes = [w[n].shape for n in SMALL]
    parts = _allgather_devices(_pack_rows(small_grads), "allgather_small")
    packed = _adamw_small(_pack_rows([w[n] for n in SMALL]), parts, _pack_rows([m[n] for n in SMALL]),
                          _pack_rows([v[n] for n in SMALL]), "adamw_small")
    small_out = {n: vals for n, vals in zip(SMALL, zip(*[_unpack_rows(t, shapes) for t in packed]))}
    comm["reduce_last"](gw0, packed[0])

    loss = lax.psum(loss_row[0, 0], ("x", "y", "c"))
    out = {**big_out, **small_out}
    return (loss, grad_x[None], *[out[n][0] for n in ORDER], *[out[n][1] for n in ORDER],
            *[out[n][2] for n in ORDER], *[out[n][3] for n in ORDER])
```

```python
import functools

import jax
import jax.numpy as jnp
from jax import lax
from jax.experimental import pallas as pl
from jax.experimental.pallas import tpu as pltpu

F32 = jnp.float32
BF16 = jnp.bfloat16

HEAD_DIM = 128
N_HEADS = 16
N_DIL_HEADS = 12
N_FOX_HEADS = 4
HEADS_PER_DIL = 4
DILATIONS = (1, 4, 16)
BLOCK = 128
ATTN_WIDTH = N_HEADS * HEAD_DIM
DIL_WIDTH = N_DIL_HEADS * HEAD_DIM
FOX_WIDTH = N_FOX_HEADS * HEAD_DIM
ROPE_THETA = 500000.0
ROPE_HALF = 16
NORM_EPS = 1e-6
SCALE = HEAD_DIM ** -0.5
NEG = -1e30

ADAM_LR = 0.001
ADAM_B1 = 0.9
ADAM_B2 = 0.999
ADAM_EPS = 1e-08
ADAM_WD = 0.01
ADAM_STEP = 10

N_CHIPS = 4
V7X_VMEM_LIMIT_BYTES = 56 * 1024 * 1024
LANES = 128
MESH = pl.DeviceIdType.MESH
ANY = pl.BlockSpec(memory_space=pl.ANY)


def _params(sem):
    return pltpu.CompilerParams(dimension_semantics=sem, vmem_limit_bytes=V7X_VMEM_LIMIT_BYTES)


def _tile(n, pref):
    if n <= pref:
        return n
    t = (pref // LANES) * LANES
    while t > LANES and n % t:
        t -= LANES
    assert n % t == 0, (n, pref)
    return t


def _mm(a, b, *, mode, name, out_dtypes=(BF16,), epilogue=None, extras=(), b_chunked=False,
        out_chunks=0, b_cols=None, ti=2048, tj=512, tc=2048):
    if mode == "tn":
        C, I = a.shape
    else:
        I, C = a.shape
    if b_chunked:
        nch, d0, n = b.shape
        if mode == "nn":
            assert d0 == C
            J = nch * n
        else:
            assert mode == "nt" and nch * n == C
            J = d0
    elif mode == "nt":
        J = b.shape[0]
        assert (b.shape[1] if b_cols is None else b_cols) == C
    else:
        assert b.shape[0] == C
        J = b.shape[1] if b_cols is None else b_cols
    ti, tc = _tile(I, ti), _tile(C, tc)
    if b_chunked and mode == "nn":
        tj = _tile(n, tj)
    elif out_chunks:
        tj = _tile(J // out_chunks, tj)
    else:
        tj = _tile(J, tj)
    if b_chunked and mode == "nt":
        tc = _tile(n, tc)
    ni, nj, nc = I // ti, J // tj, C // tc

    if mode == "tn":
        a_spec = pl.BlockSpec((tc, ti), lambda i, j, c: (c, i))
        dims = (((0,), (0,)), ((), ()))
    else:
        a_spec = pl.BlockSpec((ti, tc), lambda i, j, c: (i, c))
        dims = (((1,), (0,)), ((), ())) if mode == "nn" else (((1,), (1,)), ((), ()))
    if mode == "nt":
        if b_chunked:
            cb = n // tc
            b_spec = pl.BlockSpec((None, tj, tc), lambda i, j, c: (c // cb, j, c % cb))
        else:
            b_spec = pl.BlockSpec((tj, tc), lambda i, j, c: (j, c))
    else:
        if b_chunked:
            jb = n // tj
            b_spec = pl.BlockSpec((None, tc, tj), lambda i, j, c: (j // jb, c, j % jb))
        else:
            b_spec = pl.BlockSpec((tc, tj), lambda i, j, c: (c, j))
    extra_specs = []
    for arr, kind in extras:
        if kind == "tile":
            assert arr.shape == (I, J), (arr.shape, I, J)
            extra_specs.append(pl.BlockSpec((ti, tj), lambda i, j, c: (i, j)))
        else:
            assert arr.shape == (1, J)
            extra_specs.append(pl.BlockSpec((1, tj), lambda i, j, c: (0, j)))
    if out_chunks:
        ob = (J // out_chunks) // tj
        out_spec = pl.BlockSpec((None, ti, tj), lambda i, j, c: (j // ob, i, j % ob))
        out_shape = [jax.ShapeDtypeStruct((out_chunks, I, J // out_chunks), d) for d in out_dtypes]
    else:
        out_spec = pl.BlockSpec((ti, tj), lambda i, j, c: (i, j))
        out_shape = [jax.ShapeDtypeStruct((I, J), d) for d in out_dtypes]
    ne, no = len(extras), len(out_dtypes)
    if epilogue is None:
        epilogue = lambda acc: (acc,)

    def body(a_ref, b_ref, *rest):
        extra_refs, out_refs = rest[:ne], rest[ne:ne + no]

        def finish(acc):
            outs = epilogue(acc, *[r[...] for r in extra_refs])
            for o_ref, val in zip(out_refs, outs):
                o_ref[...] = val.astype(o_ref.dtype)

        part = lax.dot_general(a_ref[...], b_ref[...], dims, preferred_element_type=F32)
        if nc == 1:
            finish(part)
        else:
            acc_ref = rest[-1]
            k = pl.program_id(2)

            @pl.when(k == 0)
            def _():
                acc_ref[...] = part

            @pl.when(k > 0)
            def _():
                acc_ref[...] += part

            @pl.when(k == nc - 1)
            def _():
                finish(acc_ref[...])

    outs = pl.pallas_call(
        body, name=name, grid=(ni, nj, nc),
        in_specs=[a_spec, b_spec] + extra_specs,
        out_specs=[out_spec] * no, out_shape=out_shape,
        scratch_shapes=[pltpu.VMEM((ti, tj), F32)] if nc > 1 else [],
        compiler_params=_params(("parallel", "parallel", "arbitrary")),
    )(a, b, *[e[0] for e in extras])
    return outs[0] if no == 1 else tuple(outs)


def _ew(fn, ins, outs, grid, name):
    n_in = len(ins)
    has_acc = any(o[3] for o in outs)
    assert not has_acc or len(grid) == 1

    def body(*refs):
        vals = fn(*[r[...] for r in refs[:n_in]])
        for o_ref, o, val in zip(refs[n_in:], outs, vals):
            if o[3]:
                step = pl.program_id(0)

                @pl.when(step == 0)
                def _(o_ref=o_ref, val=val):
                    o_ref[...] = val

                @pl.when(step > 0)
                def _(o_ref=o_ref, val=val):
                    o_ref[...] += val
            else:
                o_ref[...] = val.astype(o_ref.dtype)

    sem = ("arbitrary",) if has_acc else ("parallel",) * len(grid)
    res = pl.pallas_call(
        body, name=name, grid=grid,
        in_specs=[i[1] for i in ins], out_specs=[o[2] for o in outs],
        out_shape=[jax.ShapeDtypeStruct(o[0], o[1]) for o in outs],
        compiler_params=_params(sem),
    )(*[i[0] for i in ins])
    return res[0] if len(outs) == 1 else tuple(res)


def _rows(tr, w):
    return pl.BlockSpec((tr, w), lambda i: (i, 0))


def _bcast(w):
    return pl.BlockSpec((1, w), lambda i: (0, 0))


ROW_BLOCK_BYTES = 4 * 1024 * 1024


def _row_tile(S, width_bytes):
    tr = 512
    while tr > 16 and tr * width_bytes > ROW_BLOCK_BYTES:
        tr //= 2
    return min(tr, S)


def _rms_fwd(h, g, name):
    S, D = h.shape
    tr = _row_tile(S, D * 4)

    def fn(x, gg):
        r = lax.rsqrt(jnp.mean(x * x, axis=-1, keepdims=True) + NORM_EPS)
        return (x * r * gg,)

    return _ew(fn, [(h, _rows(tr, D)), (g, _bcast(D))], [((S, D), BF16, _rows(tr, D), False)],
               (S // tr,), name)


def _rms_bwd(x, g, dy, dres, name):
    S, D = x.shape
    tr = _row_tile(S, D * 4)

    def fn(xv, gg, dyv, dr):
        r = lax.rsqrt(jnp.mean(xv * xv, axis=-1, keepdims=True) + NORM_EPS)
        dyf = dyv.astype(F32)
        gy = dyf * gg
        dx = r * gy - xv * (r * r * r) * jnp.mean(xv * gy, axis=-1, keepdims=True)
        tot = dr + dx
        dg = jnp.sum(dyf * xv * r, axis=0, keepdims=True)
        return tot, tot, dg

    return _ew(fn, [(x, _rows(tr, D)), (g, _bcast(D)), (dy, _rows(tr, D)), (dres, _rows(tr, D))],
               [((S, D), F32, _rows(tr, D), False), ((S, D), BF16, _rows(tr, D), False),
                ((1, D), F32, _bcast(D), True)], (S // tr,), name)


def _loss_head(h, g, target, name):
    S, D = h.shape
    tr = _row_tile(S, D * 4)

    def fn(xv, gg, tgt):
        r = lax.rsqrt(jnp.mean(xv * xv, axis=-1, keepdims=True) + NORM_EPS)
        y = xv * r * gg
        e = y - tgt
        loss = 0.5 * jnp.sum(jnp.mean(e * e, axis=-1, keepdims=True), axis=0, keepdims=True)
        dy = e * (1.0 / D)
        gy = dy * gg
        dx = r * gy - xv * (r * r * r) * jnp.mean(xv * gy, axis=-1, keepdims=True)
        dg = jnp.sum(dy * xv * r, axis=0, keepdims=True)
        return dx, jnp.broadcast_to(loss, (1, LANES)), dg

    return _ew(fn, [(h, _rows(tr, D)), (g, _bcast(D)), (target, _rows(tr, D))],
               [((S, D), F32, _rows(tr, D), False), ((1, LANES), F32, _bcast(LANES), True),
                ((1, D), F32, _bcast(D), True)], (S // tr,), name)


def _rope_tables(S):
    inv = ROPE_THETA ** (-jnp.arange(ROPE_HALF, dtype=F32) / ROPE_HALF)
    ang = jnp.arange(S, dtype=F32)[:, None] * inv[None, :]
    cos, sin = jnp.cos(ang), jnp.sin(ang)
    rest = HEAD_DIM - 2 * ROPE_HALF
    ctab = jnp.concatenate([cos, cos, jnp.ones((S, rest), F32)], axis=1)
    stab = jnp.concatenate([-sin, sin, jnp.zeros((S, rest), F32)], axis=1)
    return ctab, stab


def _swap_halves(x):
    lane = lax.broadcasted_iota(jnp.int32, x.shape, 1)
    return jnp.where(lane < ROPE_HALF, pltpu.roll(x, HEAD_DIM - ROPE_HALF, 1), pltpu.roll(x, ROPE_HALF, 1))


def _rope(q_src, q_col0, k_src, k_col0, ctab, stab, sign, name):
    S = q_src.shape[0]
    tr = min(512, S)
    width = HEADS_PER_DIL * HEAD_DIM

    def fn(q, k, ct, st):
        outs = []
        for v in (q, k):
            heads = []
            for h in range(HEADS_PER_DIL):
                vf = v[:, h * HEAD_DIM:(h + 1) * HEAD_DIM].astype(F32)
                heads.append(vf * ct + sign * _swap_halves(vf) * st)
            outs.append(jnp.concatenate(heads, axis=1))
        return tuple(outs)

    group = lambda c0: pl.BlockSpec((tr, width), lambda i, g: (i, c0 // HEADS_PER_DIL + g))
    tab = pl.BlockSpec((tr, HEAD_DIM), lambda i, g: (i, 0))
    out = ((S, DIL_WIDTH), BF16, group(0), False)
    return _ew(fn, [(q_src, group(q_col0)), (k_src, group(k_col0)), (ctab, tab), (stab, tab)],
               [out, out], (S // tr, len(DILATIONS)), name)


SW_BLOCKS_PER_STEP = 16


def _to_strided(x):
    S = x.shape[0]
    parts = []
    for g, d in enumerate(DILATIONS):
        xg = x[:, g * 512:(g + 1) * 512].reshape(S // d, d, HEADS_PER_DIL, HEAD_DIM)
        parts.append(xg.transpose(1, 2, 0, 3).reshape(-1, BLOCK, HEAD_DIM))
    return jnp.concatenate(parts, axis=0)


def _from_strided(y, S):
    per = y.shape[0] // len(DILATIONS)
    parts = []
    for g, d in enumerate(DILATIONS):
        yg = y[g * per:(g + 1) * per].reshape(d, HEADS_PER_DIL, S // d, HEAD_DIM)
        parts.append(yg.transpose(2, 0, 1, 3).reshape(S, HEADS_PER_DIL * HEAD_DIM))
    return jnp.stack(parts, axis=0)


def _seq_blocks(b0, per_group):
    g = b0 // per_group
    n0 = per_group // HEADS_PER_DIL
    return jnp.where(g == 0, n0, jnp.where(g == 1, n0 // 4, n0 // 16))


def _sw_masks():
    qi = lax.broadcasted_iota(jnp.int32, (BLOCK, BLOCK), 0)
    ki = lax.broadcasted_iota(jnp.int32, (BLOCK, BLOCK), 1)
    return qi >= ki, qi <= ki


def _sw_fwd(q, k, v, name):
    NB = q.shape[0]
    T = SW_BLOCKS_PER_STEP
    per_group = NB // len(DILATIONS)
    nt = (((1,), (1,)), ((), ()))

    def body(q_ref, k_ref, v_ref, kp_ref, vp_ref, o_ref, lse_ref):
        b0 = pl.program_id(0) * T
        nseq = _seq_blocks(b0, per_group)
        cur_mask, prev_mask = _sw_masks()
        for t in range(T):
            has_prev = ((b0 + t) & (nseq - 1)) != 0
            qt = q_ref[t]
            kp = kp_ref[0] if t == 0 else k_ref[t - 1]
            vp = vp_ref[0] if t == 0 else v_ref[t - 1]
            s_c = lax.dot_general(qt, k_ref[t], nt, preferred_element_type=F32) * SCALE
            s_p = lax.dot_general(qt, kp, nt, preferred_element_type=F32) * SCALE
            s_c = jnp.where(cur_mask, s_c, NEG)
            s_p = jnp.where(prev_mask, s_p, NEG) + jnp.where(has_prev, 0.0, NEG)
            m = jnp.maximum(jnp.max(s_c, axis=-1, keepdims=True), jnp.max(s_p, axis=-1, keepdims=True))
            p_c = jnp.exp(s_c - m)
            p_p = jnp.exp(s_p - m)
            l = jnp.sum(p_c, axis=-1, keepdims=True) + jnp.sum(p_p, axis=-1, keepdims=True)
            o = (jnp.dot(p_c.astype(BF16), v_ref[t], preferred_element_type=F32)
                 + jnp.dot(p_p.astype(BF16), vp, preferred_element_type=F32))
            o_ref[t] = (o / l).astype(o_ref.dtype)
            lse_ref[t] = jnp.broadcast_to(m + jnp.log(l), (BLOCK, HEAD_DIM))

    tile = pl.BlockSpec((T, BLOCK, HEAD_DIM), lambda i: (i, 0, 0))
    before = pl.BlockSpec((1, BLOCK, HEAD_DIM), lambda i: (jnp.maximum(i * T - 1, 0), 0, 0))
    return pl.pallas_call(
        body, name=name, grid=(NB // T,),
        in_specs=[tile, tile, tile, before, before], out_specs=[tile, tile],
        out_shape=[jax.ShapeDtypeStruct(q.shape, BF16), jax.ShapeDtypeStruct(q.shape, F32)],
        compiler_params=_params(("parallel",)),
    )(q, k, v, k, v)


def _sw_bwd(q, k, v, do, lse, tt, name):
    NB = q.shape[0]
    T = SW_BLOCKS_PER_STEP
    per_group = NB // len(DILATIONS)
    nt = (((1,), (1,)), ((), ()))
    tn = (((0,), (0,)), ((), ()))

    def body(q_ref, k_ref, v_ref, do_ref, lse_ref, tt_ref, kp_ref, vp_ref, qn_ref, don_ref, lsen_ref,
             ttn_ref, dq_ref, dk_ref, dv_ref):
        b0 = pl.program_id(0) * T
        nseq = _seq_blocks(b0, per_group)
        cur_mask, prev_mask = _sw_masks()

        def probs(qq, kk, lse_b, mask, gate):
            s = lax.dot_general(qq, kk, nt, preferred_element_type=F32) * SCALE
            return jnp.exp(jnp.where(mask, s, NEG) + gate - lse_b)

        for t in range(T):
            has_prev = jnp.where(((b0 + t) & (nseq - 1)) != 0, 0.0, NEG)
            has_next = jnp.where(((b0 + t + 1) & (nseq - 1)) != 0, 0.0, NEG)
            last = t == T - 1
            qt, kt, vt, dot = q_ref[t], k_ref[t], v_ref[t], do_ref[t]
            kp = kp_ref[0] if t == 0 else k_ref[t - 1]
            vp = vp_ref[0] if t == 0 else v_ref[t - 1]
            qn = qn_ref[0] if last else q_ref[t + 1]
            don = don_ref[0] if last else do_ref[t + 1]
            lsen = lsen_ref[0] if last else lse_ref[t + 1]
            ttn = ttn_ref[0] if last else tt_ref[t + 1]
            p_cc = probs(qt, kt, lse_ref[t], cur_mask, 0.0)
            p_cp = probs(qt, kp, lse_ref[t], prev_mask, has_prev)
            p_nc = probs(qn, kt, lsen, prev_mask, has_next)
            ds_cc = p_cc * (lax.dot_general(dot, vt, nt, preferred_element_type=F32) + tt_ref[t])
            ds_cp = p_cp * (lax.dot_general(dot, vp, nt, preferred_element_type=F32) + tt_ref[t])
            ds_nc = p_nc * (lax.dot_general(don, vt, nt, preferred_element_type=F32) + ttn)
            ds_cc, ds_cp, ds_nc = ds_cc.astype(BF16), ds_cp.astype(BF16), ds_nc.astype(BF16)
            dq = (jnp.dot(ds_cc, kt, preferred_element_type=F32)
                  + jnp.dot(ds_cp, kp, preferred_element_type=F32))
            dk = (lax.dot_general(ds_cc, qt, tn, preferred_element_type=F32)
                  + lax.dot_general(ds_nc, qn, tn, preferred_element_type=F32))
            dv = (lax.dot_general(p_cc.astype(BF16), dot, tn, preferred_element_type=F32)
                  + lax.dot_general(p_nc.astype(BF16), don, tn, preferred_element_type=F32))
            dq_ref[t] = (dq * SCALE).astype(BF16)
            dk_ref[t] = (dk * SCALE).astype(BF16)
            dv_ref[t] = dv.astype(BF16)

    tile = pl.BlockSpec((T, BLOCK, HEAD_DIM), lambda i: (i, 0, 0))
    before = pl.BlockSpec((1, BLOCK, HEAD_DIM), lambda i: (jnp.maximum(i * T - 1, 0), 0, 0))
    after = pl.BlockSpec((1, BLOCK, HEAD_DIM), lambda i: (jnp.minimum(i * T + T, NB - 1), 0, 0))
    out = jax.ShapeDtypeStruct(q.shape, BF16)
    return pl.pallas_call(
        body, name=name, grid=(NB // T,),
        in_specs=[tile] * 6 + [before, before, after, after, after, after],
        out_specs=[tile] * 3, out_shape=[out] * 3,
        compiler_params=_params(("parallel",)),
    )(q, k, v, do, lse, tt, k, v, q, do, lse, tt)


def _group_softmax(lse):
    m = jnp.max(lse, axis=0, keepdims=True)
    e = jnp.exp(lse - m)
    return e / jnp.sum(e, axis=0, keepdims=True)


def _mix_fwd(o, lse, name):
    G, S, W = o.shape
    tr = min(256, S)
    blk = pl.BlockSpec((G, tr, W), lambda i: (0, i, 0))

    def fn(ov, lv):
        return (jnp.sum(_group_softmax(lv) * ov.astype(F32), axis=0),)

    return _ew(fn, [(o, blk), (lse, blk)], [((S, W), BF16, _rows(tr, W), False)], (S // tr,), name)


def _mix_bwd(dya, ya, o, lse, name):
    G, S, W = o.shape
    tr = min(256, S)
    blk = pl.BlockSpec((G, tr, HEAD_DIM), lambda i, h: (0, i, h))
    row = pl.BlockSpec((tr, HEAD_DIM), lambda i, h: (i, h))

    def fn(dy, yv, ov, lv):
        w = _group_softmax(lv)
        dyf = dy.astype(F32)
        inner = jnp.sum(dyf * yv.astype(F32), axis=-1, keepdims=True)
        return w * dyf[None], -w * inner[None]

    return _ew(fn, [(dya, row), (ya, row), (o, blk), (lse, blk)],
               [((G, S, W), BF16, blk, False), ((G, S, W), F32, blk, False)],
               (S // tr, HEADS_PER_DIL), name)


CUM_BLOCK = 256


def _split3(x):
    hi = x.astype(BF16)
    r = x - hi.astype(F32)
    mid = r.astype(BF16)
    lo = (r - mid.astype(F32)).astype(BF16)
    return hi, mid, lo


def _tri_matmul(tri, x):
    return sum(jnp.dot(tri, part, preferred_element_type=F32) for part in _split3(x))


def _log_sigmoid(x):
    return jnp.minimum(x, 0.0) - jnp.log(1.0 + jnp.exp(-jnp.abs(x)))


def _fox_prep(f, b, name):
    S = f.shape[0]
    tb = min(CUM_BLOCK, S)

    def body(f_ref, b_ref, c_ref, carry):
        @pl.when(pl.program_id(0) == 0)
        def _():
            carry[...] = jnp.zeros_like(carry)

        ls = _log_sigmoid(f_ref[...] + b_ref[...])
        r = lax.broadcasted_iota(jnp.int32, (tb, tb), 0)
        cidx = lax.broadcasted_iota(jnp.int32, (tb, tb), 1)
        tri = jnp.where(r >= cidx, 1.0, 0.0).astype(BF16)
        c_ref[...] = _tri_matmul(tri, ls) + carry[...]
        carry[...] += jnp.sum(ls, axis=0, keepdims=True)

    return pl.pallas_call(
        body, name=name, grid=(S // tb,),
        in_specs=[_rows(tb, LANES), _bcast(LANES)], out_specs=_rows(tb, LANES),
        out_shape=jax.ShapeDtypeStruct((S, LANES), F32),
        scratch_shapes=[pltpu.VMEM((1, LANES), F32)],
        compiler_params=_params(("arbitrary",)),
    )(f, b)


def _fox_prep_bwd(dc, f, b, name):
    S = f.shape[0]
    tb = min(CUM_BLOCK, S)
    nb = S // tb

    def body(dc_ref, f_ref, b_ref, df_ref, db_ref, carry):
        @pl.when(pl.program_id(0) == 0)
        def _():
            carry[...] = jnp.zeros_like(carry)
            db_ref[...] = jnp.zeros_like(db_ref)

        r = lax.broadcasted_iota(jnp.int32, (tb, tb), 0)
        cidx = lax.broadcasted_iota(jnp.int32, (tb, tb), 1)
        tri = jnp.where(r <= cidx, 1.0, 0.0).astype(BF16)
        dcv = dc_ref[...]
        dls = _tri_matmul(tri, dcv) + carry[...]
        carry[...] += jnp.sum(dcv, axis=0, keepdims=True)
        z = f_ref[...] + b_ref[...]
        df = dls * (1.0 / (1.0 + jnp.exp(z)))
        df_ref[...] = df
        db_ref[...] += jnp.sum(df, axis=0, keepdims=True)

    rev = pl.BlockSpec((tb, LANES), lambda i: (nb - 1 - i, 0))
    return pl.pallas_call(
        body, name=name, grid=(nb,),
        in_specs=[rev, rev, _bcast(LANES)], out_specs=[rev, _bcast(LANES)],
        out_shape=[jax.ShapeDtypeStruct((S, LANES), F32), jax.ShapeDtypeStruct((1, LANES), F32)],
        scratch_shapes=[pltpu.VMEM((1, LANES), F32)],
        compiler_params=_params(("arbitrary",)),
    )(dc, f, b)


FOX_Q_TILE = 512
_FOX_Q0 = N_DIL_HEADS
_FOX_K0 = N_HEADS + N_DIL_HEADS
_FOX_V0 = 2 * N_HEADS + N_DIL_HEADS


def _fox_scores(q, k, cq, ck, q0):
    nt = (((1,), (1,)), ((), ()))
    s = lax.dot_general(q, k, nt, preferred_element_type=F32) * SCALE + cq - ck
    qpos = q0 + lax.broadcasted_iota(jnp.int32, s.shape, 0)
    kpos = lax.broadcasted_iota(jnp.int32, s.shape, 1)
    return jnp.where(kpos <= qpos, s, NEG)


def _per_query_tile(nq, tq, fn):
    step = pl.program_id(1)
    for n in range(nq):
        @pl.when(step == n)
        def _(n=n):
            fn(n, pl.ds(0, (n + 1) * tq))


def _fox_fwd(z, cq, ck, name):
    S = z.shape[0]
    tq = min(FOX_Q_TILE, S)

    def body(q_ref, k_ref, v_ref, cq_ref, ck_ref, o_ref, lse_ref):
        def tile(n, keys):
            s = _fox_scores(q_ref[...], k_ref[keys, :], cq_ref[...], ck_ref[:, keys], n * tq)
            m = jnp.max(s, axis=-1, keepdims=True)
            p = jnp.exp(s - m)
            l = jnp.sum(p, axis=-1, keepdims=True)
            o = jnp.dot(p.astype(BF16), v_ref[keys, :], preferred_element_type=F32)
            o_ref[...] = (o / l).astype(o_ref.dtype)
            lse_ref[...] = m + jnp.log(l)

        _per_query_tile(S // tq, tq, tile)

    qblk = lambda c0: pl.BlockSpec((tq, HEAD_DIM), lambda h, i: (i, c0 + h))
    full = lambda c0: pl.BlockSpec((S, HEAD_DIM), lambda h, i: (0, c0 + h))
    col = pl.BlockSpec((None, tq, 1), lambda h, i: (h, i, 0))
    rowv = pl.BlockSpec((None, 1, S), lambda h, i: (h, 0, 0))
    return pl.pallas_call(
        body, name=name, grid=(N_FOX_HEADS, S // tq),
        in_specs=[qblk(_FOX_Q0), full(_FOX_K0), full(_FOX_V0), col, rowv],
        out_specs=[qblk(0), col],
        out_shape=[jax.ShapeDtypeStruct((S, FOX_WIDTH), BF16),
                   jax.ShapeDtypeStruct((N_FOX_HEADS, S, 1), F32)],
        compiler_params=_params(("parallel", "parallel")),
    )(z, z, z, cq, ck)


def _fox_bwd(z, cq, ck, lse, yb, dyb, name):
    S = z.shape[0]
    tq = min(FOX_Q_TILE, S)
    nq = S // tq
    nt = (((1,), (1,)), ((), ()))
    tn = (((0,), (0,)), ((), ()))

    def body(q_ref, k_ref, v_ref, cq_ref, ck_ref, lse_ref, o_ref, do_ref,
             dq_ref, dk_ref, dv_ref, dc_ref, dk_acc, dv_acc):
        i = pl.program_id(1)

        @pl.when(i == 0)
        def _():
            dk_acc[...] = jnp.zeros_like(dk_acc)
            dv_acc[...] = jnp.zeros_like(dv_acc)
            dc_ref[...] = jnp.zeros_like(dc_ref)

        def tile(n, keys):
            q, k, v, do = q_ref[...], k_ref[keys, :], v_ref[keys, :], do_ref[...]
            s = _fox_scores(q, k, cq_ref[...], ck_ref[:, keys], n * tq)
            p = jnp.exp(s - lse_ref[...])
            dp = lax.dot_general(do, v, nt, preferred_element_type=F32)
            ds = p * (dp - jnp.sum(p * dp, axis=-1, keepdims=True))
            dsb = ds.astype(BF16)
            dq_ref[...] = (jnp.dot(dsb, k, preferred_element_type=F32) * SCALE).astype(BF16)
            dk_acc[keys, :] += lax.dot_general(dsb, q, tn, preferred_element_type=F32) * SCALE
            dv_acc[keys, :] += lax.dot_general(p.astype(BF16), do, tn, preferred_element_type=F32)
            dc_ref[:, keys] -= jnp.sum(ds, axis=0, keepdims=True)

        _per_query_tile(nq, tq, tile)

        @pl.when(i == nq - 1)
        def _():
            dk_ref[...] = dk_acc[...].astype(BF16)
            dv_ref[...] = dv_acc[...].astype(BF16)

    qblk = lambda c0: pl.BlockSpec((tq, HEAD_DIM), lambda h, i: (i, c0 + h))
    full = lambda c0: pl.BlockSpec((S, HEAD_DIM), lambda h, i: (0, c0 + h))
    col = pl.BlockSpec((None, tq, 1), lambda h, i: (h, i, 0))
    rowv = pl.BlockSpec((None, 1, S), lambda h, i: (h, 0, 0))
    wide = jax.ShapeDtypeStruct((S, FOX_WIDTH), BF16)
    return pl.pallas_call(
        body, name=name, grid=(N_FOX_HEADS, nq),
        in_specs=[qblk(_FOX_Q0), full(_FOX_K0), full(_FOX_V0), col, rowv, col, qblk(0), qblk(0)],
        out_specs=[qblk(0), full(0), full(0), rowv],
        out_shape=[wide, wide, wide, jax.ShapeDtypeStruct((N_FOX_HEADS, 1, S), F32)],
        scratch_shapes=[pltpu.VMEM((S, HEAD_DIM), F32), pltpu.VMEM((S, HEAD_DIM), F32)],
        compiler_params=_params(("parallel", "arbitrary")),
    )(z, z, z, cq, ck, lse, yb, dyb)


def _sigmoid(x):
    return 1.0 / (1.0 + jnp.exp(-x))


def _merge_fwd(gates, a, bm, name):
    S, D = a.shape
    tr = _row_tile(S, D * 4)
    g1 = pl.BlockSpec((tr, D), lambda i: (i, 0))
    g2 = pl.BlockSpec((tr, D), lambda i: (i, 1))

    def fn(x1, x2, av, bv):
        return (x1.astype(F32) * av.astype(F32) + x2.astype(F32) * bv.astype(F32),)

    return _ew(fn, [(gates, g1), (gates, g2), (a, _rows(tr, D)), (bm, _rows(tr, D))],
               [((S, D), BF16, _rows(tr, D), False)], (S // tr,), name)


def _merge_bwd(dmerged, gates, a, bm, name):
    S, D = a.shape
    tr = _row_tile(S, D * 8)
    g1 = pl.BlockSpec((tr, D), lambda i: (i, 0))
    g2 = pl.BlockSpec((tr, D), lambda i: (i, 1))

    def fn(dm, x1, x2, av, bv):
        dm, x1, x2 = dm.astype(F32), x1.astype(F32), x2.astype(F32)
        dg1 = dm * av.astype(F32) * x1 * (1.0 - x1)
        dg2 = dm * bv.astype(F32) * x2 * (1.0 - x2)
        dgp = jnp.concatenate([dg1, dg2], axis=1)
        return dm * x1, dm * x2, dgp, jnp.sum(dgp, axis=0, keepdims=True)

    return _ew(fn, [(dmerged, _rows(tr, D)), (gates, g1), (gates, g2), (a, _rows(tr, D)), (bm, _rows(tr, D))],
               [((S, D), BF16, _rows(tr, D), False), ((S, D), BF16, _rows(tr, D), False),
                ((S, 2 * D), BF16, _rows(tr, 2 * D), False), ((1, 2 * D), F32, _bcast(2 * D), True)],
               (S // tr,), name)


def _ple_bwd(dh, pg, pe, name):
    S, D = dh.shape
    tr = _row_tile(S, D * 4)

    def fn(d, g, e):
        g, e = g.astype(F32), e.astype(F32)
        return d * g, d * e * g * (1.0 - g)

    spec = _rows(tr, D)
    return _ew(fn, [(dh, spec), (pg, spec), (pe, spec)],
               [((S, D), BF16, spec, False), ((S, D), BF16, spec, False)], (S // tr,), name)


def _position():
    x, y, c = lax.axis_index("x"), lax.axis_index("y"), lax.axis_index("c")
    chips = [(1 - x, y), (x, 1 - y), (1 - x, 1 - y)]
    return x, y, c, chips


def _remote(src, dst, send_sem, recv_sem, target):
    return pltpu.make_async_remote_copy(src_ref=src, dst_ref=dst, send_sem=send_sem, recv_sem=recv_sem,
                                        device_id=target, device_id_type=MESH)


HBM = pl.BlockSpec(memory_space=pltpu.HBM)
SEM = pl.BlockSpec(memory_space=pltpu.SEMAPHORE)
EFFECT = pltpu.SideEffectType.DATAFLOW_SIDE_EFFECTING


def _copies_start(plan, arrays, sem_shape, after, name):
    n = len(arrays)

    def body(*refs):
        send_sems, recv_sems, token = refs[n + 1], refs[n + 2], refs[-1]
        for send, _ in plan(refs[:n], send_sems, recv_sems):
            send.start()
        token[...] = jnp.zeros_like(token)

    outs = pl.pallas_call(
        body, name=name,
        out_shape=(pltpu.SemaphoreType.DMA(sem_shape), pltpu.SemaphoreType.DMA(sem_shape),
                   *[pltpu.HBM(a.shape, a.dtype) for a in arrays], jax.ShapeDtypeStruct((8, LANES), F32)),
        in_specs=[HBM] * n + [ANY],
        out_specs=(SEM, SEM, *[HBM] * n, pl.BlockSpec(memory_space=pltpu.VMEM)),
        input_output_aliases={a: 2 + a for a in range(n)},
        compiler_params=pltpu.CompilerParams(has_side_effects=EFFECT),
    )(*[pltpu.with_memory_space_constraint(a, pltpu.HBM) for a in arrays], after)
    return outs[0], outs[1], list(outs[2:2 + n]), outs[-1]


def _copies_wait(plan, send_sems, recv_sems, arrays, after, name):
    n = len(arrays)

    def body(*refs):
        for send, recv in plan(refs[:n], refs[n], refs[n + 1]):
            send.wait_send()
            recv.wait_recv()

    return list(pl.pallas_call(
        body, name=name,
        out_shape=[pltpu.HBM(a.shape, a.dtype) for a in arrays],
        in_specs=[HBM] * n + [SEM, SEM, ANY], out_specs=[HBM] * n,
        input_output_aliases={a: a for a in range(n)},
        compiler_params=pltpu.CompilerParams(has_side_effects=EFFECT),
    )(*arrays, send_sems, recv_sems, after))


def _gather_ici_plan(refs, send_sems, recv_sems):
    x, y, c, chips = _position()
    plan = []
    for a, ref in enumerate(refs):
        rh = ref.shape[1] // 2
        mine = ref.at[2 * x + y, pl.ds(c * rh, rh)]
        for j, (cx, cy) in enumerate(chips):
            landed = ref.at[2 * cx + cy, pl.ds(c * rh, rh)]
            plan.append((_remote(mine, mine, send_sems.at[3 * a + j], recv_sems.at[3 * a + j], (cx, cy, c)),
                         _remote(landed, landed, send_sems.at[3 * a + j], recv_sems.at[3 * a + j], (cx, cy, c))))
    return plan


def _gather_d2d_plan(refs, send_sems, recv_sems):
    x, y, c, chips = _position()
    sibling = (x, y, 1 - c)
    plan = []
    for a, ref in enumerate(refs):
        rh = ref.shape[1] // 2
        for j, (cx, cy) in enumerate(chips):
            landed = ref.at[2 * cx + cy, pl.ds(c * rh, rh)]
            theirs = ref.at[2 * cx + cy, pl.ds((1 - c) * rh, rh)]
            plan.append((_remote(landed, landed, send_sems.at[3 * a + j], recv_sems.at[3 * a + j], sibling),
                         _remote(theirs, theirs, send_sems.at[3 * a + j], recv_sems.at[3 * a + j], sibling)))
    return plan


def _scatter_ici_plan(refs, send_sems, recv_sems):
    x, y, c, chips = _position()
    n = len(refs) // 2
    plan = []
    for a in range(n):
        for j, (cx, cy) in enumerate(chips):
            cp = _remote(refs[a].at[2 * cx + cy], refs[n + a].at[j], send_sems.at[3 * a + j], recv_sems.at[3 * a + j],
                         (cx, cy, c))
            plan.append((cp, cp))
    return plan


def _place_shard(w, layer, me, after, name):
    _, r, cc = w.shape
    tr = _row_tile(r, cc * 4)

    def body(me_ref, w_ref, *rest):
        rest[-1][...] = w_ref[...].astype(rest[-1].dtype)

    return pl.pallas_call(
        body, name=name,
        grid_spec=pltpu.PrefetchScalarGridSpec(
            num_scalar_prefetch=1, grid=(r // tr,),
            in_specs=[pl.BlockSpec((None, tr, cc), lambda i, me_ref: (layer, i, 0))] + [ANY] * len(after),
            out_specs=pl.BlockSpec((None, tr, cc), lambda i, me_ref: (me_ref[0], i, 0))),
        out_shape=jax.ShapeDtypeStruct((N_CHIPS, r, cc), BF16),
        compiler_params=_params(("parallel",)),
    )(me, w, *after)


def _pair_plan(refs, send_sems, recv_sems):
    x, y, c, _ = _position()
    n = len(refs) // 2
    plan = []
    for a in range(n):
        rh = refs[a].shape[1] // 2
        cp = _remote(refs[a].at[:, pl.ds((1 - c) * rh, rh)], refs[n + a], send_sems.at[a], recv_sems.at[a],
                     (x, y, 1 - c))
        plan.append((cp, cp))
    return plan


def _pair_sum(mine, theirs, c, name):
    nch, rh, cc = theirs.shape
    tr = _row_tile(rh, cc * 4)
    nb = rh // tr

    def body(c_ref, m_ref, t_ref, o_ref):
        o_ref[...] = (m_ref[...].astype(F32) + t_ref[...].astype(F32)).astype(o_ref.dtype)

    return pl.pallas_call(
        body, name=name,
        grid_spec=pltpu.PrefetchScalarGridSpec(
            num_scalar_prefetch=1, grid=(nch, nb),
            in_specs=[pl.BlockSpec((1, tr, cc), lambda k, i, c_ref: (k, c_ref[0] * nb + i, 0)),
                      pl.BlockSpec((1, tr, cc), lambda k, i, c_ref: (k, i, 0))],
            out_specs=pl.BlockSpec((1, tr, cc), lambda k, i, c_ref: (k, i, 0))),
        out_shape=jax.ShapeDtypeStruct(theirs.shape, BF16),
        compiler_params=_params(("parallel", "parallel")),
    )(c, mine, theirs)


def _chip_sum(own, others, me, c, total, layer, depth, name):
    _, rh, cc = own.shape
    tr = _row_tile(rh, cc * 4)
    nb = rh // tr
    chained = total is not None

    def body(me_ref, c_ref, o_ref, r_ref, *rest):
        g_ref = rest[-1]
        g_ref[...] = (o_ref[0].astype(F32) + r_ref[0].astype(F32)) + (r_ref[1].astype(F32) + r_ref[2].astype(F32))

    in_specs = [pl.BlockSpec((1, tr, cc), lambda i, me_ref, c_ref: (me_ref[0], i, 0)),
                pl.BlockSpec((3, tr, cc), lambda i, me_ref, c_ref: (0, i, 0))]
    args = [me, c, own, others]
    if chained:
        in_specs.append(ANY)
        args.append(total)
    return pl.pallas_call(
        body, name=name,
        grid_spec=pltpu.PrefetchScalarGridSpec(
            num_scalar_prefetch=2, grid=(nb,), in_specs=in_specs,
            out_specs=pl.BlockSpec((None, tr, cc), lambda i, me_ref, c_ref: (layer, c_ref[0] * nb + i, 0))),
        out_shape=jax.ShapeDtypeStruct((depth, 2 * rh, cc), F32),
        input_output_aliases={4: 0} if chained else {},
        compiler_params=_params(("parallel",)),
    )(*args)


def _half_exchange(totals, layer, name):
    n = len(totals)

    def body(*refs):
        outs = refs[n:2 * n]
        send_sems, recv_sems = refs[2 * n:]
        x, y, c, _ = _position()
        copies = []
        for a in range(n):
            rh = outs[a].shape[1] // 2
            mine = outs[a].at[layer, pl.ds(c * rh, rh)]
            cp = _remote(mine, mine, send_sems.at[a], recv_sems.at[a], (x, y, 1 - c))
            cp.start()
            copies.append(cp)
        for a, cp in enumerate(copies):
            rh = outs[a].shape[1] // 2
            theirs = outs[a].at[layer, pl.ds((1 - c) * rh, rh)]
            cp.wait_send()
            _remote(theirs, theirs, send_sems.at[a], recv_sems.at[a], (x, y, 1 - c)).wait_recv()

    return pl.pallas_call(
        body, name=name, in_specs=[ANY] * n, out_specs=[ANY] * n,
        out_shape=[jax.ShapeDtypeStruct(t.shape, t.dtype) for t in totals],
        input_output_aliases={a: a for a in range(n)},
        scratch_shapes=[pltpu.SemaphoreType.DMA((n,)), pltpu.SemaphoreType.DMA((n,))],
    )(*totals)


def _allgather_devices(v, name):
    m_per, n = v.shape

    def body(x_ref, out_ref, send_sems, recv_sems, local_sem):
        x, y, c, chips = _position()
        me, sibling = (x, y, c), (x, y, 1 - c)

        def rows(px, py, pc):
            return out_ref.at[pl.ds((4 * px + 2 * py + pc) * m_per, m_per), :]

        def copy(k, block, to, src=None):
            return _remote(rows(*block) if src is None else src, rows(*block), send_sems.at[k], recv_sems.at[k], to)

        mine = pltpu.make_async_copy(x_ref, rows(*me), local_sem)
        mine.start()
        first = [copy(0, me, sibling, src=x_ref)]
        first += [copy(1 + j, me, (*chip, c), src=x_ref) for j, chip in enumerate(chips)]
        for cp in first:
            cp.start()
        passed = [copy(4 + j, (*chip, c), sibling) for j, chip in enumerate(chips)]
        for j, chip in enumerate(chips):
            copy(1 + j, (*chip, c), me).wait_recv()
            passed[j].start()
        copy(0, sibling, me).wait_recv()
        for j, chip in enumerate(chips):
            copy(4 + j, (*chip, 1 - c), me).wait_recv()
        for cp in first + passed:
            cp.wait_send()
        mine.wait()

    vm = pl.BlockSpec(memory_space=pltpu.VMEM)
    return pl.pallas_call(
        body, name=name, in_specs=[vm], out_specs=vm,
        out_shape=jax.ShapeDtypeStruct((8 * m_per, n), v.dtype),
        scratch_shapes=[pltpu.SemaphoreType.DMA((7,)), pltpu.SemaphoreType.DMA((7,)), pltpu.SemaphoreType.DMA],
    )(v)


def _adamw_math(w, g, m, v):
    m = ADAM_B1 * m + (1.0 - ADAM_B1) * g
    v = ADAM_B2 * v + (1.0 - ADAM_B2) * (g * g)
    m_hat = m / (1.0 - ADAM_B1 ** ADAM_STEP)
    v_hat = v / (1.0 - ADAM_B2 ** ADAM_STEP)
    delta = -ADAM_LR * (m_hat / (jnp.sqrt(v_hat) + ADAM_EPS) + ADAM_WD * w)
    return delta, m, v


def _adamw(w, g, m, v, lo, hi, prev, after, name):
    depth, r, cc = w.shape
    tr = _row_tile(r, cc * 4 * 2)
    spec = pl.BlockSpec((1, tr, cc), lambda l, i: (lo + l, i, 0))

    def body(w_ref, g_ref, m_ref, v_ref, *rest):
        outs = rest[-4:]
        gv = g_ref[...]
        for o_ref, val in zip(outs, (gv,) + _adamw_math(w_ref[...], gv, m_ref[...], v_ref[...])):
            o_ref[...] = val

    prev = list(prev) if prev is not None else []
    return tuple(pl.pallas_call(
        body, name=name, grid=(hi - lo, r // tr),
        in_specs=[spec] * 4 + [ANY] * (1 + len(prev)), out_specs=[spec] * 4,
        out_shape=[jax.ShapeDtypeStruct(w.shape, F32)] * 4,
        input_output_aliases={5 + k: k for k in range(len(prev))},
        compiler_params=_params(("parallel", "parallel")),
    )(w, g, m, v, after, *prev))


def _adamw_small(w, parts, m, v, name):
    M = w.shape[0]

    def body(w_ref, p_ref, m_ref, v_ref, g_ref, d_ref, nm_ref, nv_ref):
        g = p_ref[pl.ds(0, M), :]
        for k in range(1, 8):
            g = g + p_ref[pl.ds(k * M, M), :]
        d, nm, nv = _adamw_math(w_ref[...], g, m_ref[...], v_ref[...])
        g_ref[...] = g
        d_ref[...] = d
        nm_ref[...] = nm
        nv_ref[...] = nv

    vm = pl.BlockSpec(memory_space=pltpu.VMEM)
    return pl.pallas_call(
        body, name=name, in_specs=[vm] * 4, out_specs=[vm] * 4,
        out_shape=[jax.ShapeDtypeStruct(w.shape, F32)] * 4,
    )(w, parts, m, v)


def _layer_fwd(h0, p_l, W, small, tabs, after_mlp):
    S, D = h0.shape
    ctab, stab = tabs
    u = _rms_fwd(h0, small["g_mix"], "rms_mix")
    z = _mm(u, W["w_qkv"], mode="nn", name="mm_qkv", b_cols=3 * ATTN_WIDTH)
    f = _mm(u, W["w_f"], mode="nn", name="mm_f", out_dtypes=(F32,))
    gates = _mm(u, W["w_gate"], mode="nn", name="mm_gate", b_chunked=True, extras=[(small["b_gate"], "row")],
                epilogue=lambda acc, b: (_sigmoid(acc + b),))
    qr, kr = _rope(z, 0, z, N_HEADS, ctab, stab, 1.0, "rope_fwd")
    qs, ks, vs = _to_strided(qr), _to_strided(kr), _to_strided(z[:, 2 * ATTN_WIDTH:2 * ATTN_WIDTH + DIL_WIDTH])
    o_s, lse_s = _sw_fwd(qs, ks, vs, "sw_fwd")
    o_g, lse_g = _from_strided(o_s, S), _from_strided(lse_s, S)
    ya = _mix_fwd(o_g, lse_g, "mix_fwd")
    a = _mm(ya, W["w_br_a"], mode="nn", name="mm_br_a", b_chunked=True)
    cum = _fox_prep(f, small["b_f"], "fox_prep")
    cq = cum[:, :N_FOX_HEADS].T[:, :, None]
    ck = cum[:, :N_FOX_HEADS].T[:, None, :]
    yb, lse_f = _fox_fwd(z, cq, ck, "fox_fwd")
    bm = _mm(yb, W["w_br_b"], mode="nn", name="mm_br_b", b_chunked=True)
    merged = _merge_fwd(gates, a, bm, "merge_fwd")
    h1 = _mm(merged, W["w_o"], mode="nn", name="mm_o", out_dtypes=(F32,), extras=[(h0, "tile")],
             epilogue=lambda acc, r: (acc + r,), tj=512)
    m = _rms_fwd(h1, small["g_mlp"], "rms_mlp")
    ra, act = _mm(m, W["w_up"], mode="nn", name="mm_up", b_chunked=True, out_dtypes=(BF16, BF16),
                  epilogue=lambda acc: (jnp.maximum(acc, 0.0), jnp.square(jnp.maximum(acc, 0.0))))
    h2 = _mm(act, W["w_down"], mode="nn", name="mm_down", out_dtypes=(F32,), extras=[(h1, "tile")],
             epilogue=lambda acc, r: (acc + r,), ti=1024, tj=512, tc=4096)
    token = after_mlp(h2)
    n = _rms_fwd(h2, small["g_ple"] if token is None else small["g_ple"] + token[0, 0], "rms_ple")
    pg = _mm(n, W["w_ple_gate"], mode="nn", name="mm_ple_gate", epilogue=lambda acc: (_sigmoid(acc),))
    h3, pe = _mm(p_l, W["w_ple"], mode="nn", name="mm_ple", b_chunked=True, out_dtypes=(F32, BF16),
                 extras=[(h2, "tile"), (pg, "tile")], tj=256,
                 epilogue=lambda acc, r, g: (r + g.astype(F32) * acc, acc))
    saved = dict(h0=h0, u=u, z=z, f=f, gates=gates, qs=qs, ks=ks, vs=vs, lse_s=lse_s, o_g=o_g, lse_g=lse_g,
                 ya=ya, a=a, cq=cq, ck=ck, yb=yb, lse_f=lse_f, bm=bm, merged=merged, h1=h1, m=m, ra=ra,
                 act=act, h2=h2, n=n, pg=pg, pe=pe, p_l=p_l)
    return h3, saved


def _after(hooks, name, small_value, *args):
    token = hooks[name](*args) if name in hooks else None
    return small_value if token is None else small_value + token[0, 0]


def _layer_bwd(dh3, sv, W, small, tabs, hooks):
    S, D = dh3.shape
    ctab, stab = tabs
    gw, gs = {}, {}
    tn = functools.partial(_mm, mode="tn", ti=1024, tj=2048)
    dpe, dpg = _ple_bwd(dh3, sv["pg"], sv["pe"], "ple_bwd")
    gw["w_ple"] = tn(sv["p_l"], dpe, name="dw_ple", out_chunks=N_CHIPS)
    gw["w_ple_gate"] = tn(sv["n"], dpg, name="dw_ple_gate").reshape(N_CHIPS, D // N_CHIPS, D)
    dn = _mm(dpg, W["w_ple_gate"], mode="nt", name="mm_dn")
    dh2, dh2b, gs["g_ple"] = _rms_bwd(sv["h2"], small["g_ple"], dn, dh3, "rms_ple_bwd")
    da = _mm(dh2b, W["w_down"], mode="nt", name="mm_dact", extras=[(sv["ra"], "tile")],
             epilogue=lambda acc, r: (acc * (2.0 * r.astype(F32)),))
    FF = da.shape[1]
    g_mlp = _after(hooks, "mlp_grad", small["g_mlp"], da)
    gw["w_down"] = tn(sv["act"], dh2b, name="dw_down").reshape(N_CHIPS, FF // N_CHIPS, D)
    gw["w_up"] = tn(sv["m"], da, name="dw_up", out_chunks=N_CHIPS)
    g_mlp = _after(hooks, "mlp_weights", g_mlp, dict(gw))
    dm = _mm(da, W["w_up"], mode="nt", name="mm_dm", b_chunked=True)
    dh1, dh1b, gs["g_mlp"] = _rms_bwd(sv["h1"], g_mlp, dm, dh2, "rms_mlp_bwd")
    dmerged = _mm(dh1b, W["w_o"], mode="nt", name="mm_dmerged")
    b_f = _after(hooks, "merge_grad", small["b_f"], dmerged)
    gw["w_o"] = tn(sv["merged"], dh1b, name="dw_o").reshape(N_CHIPS, D // N_CHIPS, D)
    d_a, d_b, dgp, gs["b_gate"] = _merge_bwd(dmerged, sv["gates"], sv["a"], sv["bm"], "merge_bwd")
    gw["w_gate"] = tn(sv["u"], dgp, name="dw_gate", out_chunks=N_CHIPS)
    gw["w_br_a"] = tn(sv["ya"], d_a, name="dw_br_a", out_chunks=N_CHIPS, tj=512)
    gw["w_br_b"] = tn(sv["yb"], d_b, name="dw_br_b", out_chunks=N_CHIPS, tj=512)
    dya = _mm(d_a, W["w_br_a"], mode="nt", name="mm_dya", b_chunked=True)
    dyb = _mm(d_b, W["w_br_b"], mode="nt", name="mm_dyb", b_chunked=True)
    z = sv["z"]
    dq_f, dk_f, dv_f, dck = _fox_bwd(z, sv["cq"], sv["ck"], sv["lse_f"], sv["yb"], dyb, "fox_bwd")
    dc = jnp.pad(dck[:, 0, :].T, ((0, 0), (0, LANES - N_FOX_HEADS)))
    df, dbf = _fox_prep_bwd(dc, sv["f"], b_f, "fox_prep_bwd")
    gs["b_f"] = dbf[:, :N_FOX_HEADS]
    lane = jnp.arange(LANES)[None, :] < N_FOX_HEADS
    dzf = jnp.where(lane, df, 0.0).astype(BF16)
    do_g, tt_g = _mix_bwd(dya, sv["ya"], sv["o_g"], sv["lse_g"], "mix_bwd")
    do_s = _to_strided(do_g.transpose(1, 0, 2).reshape(S, DIL_WIDTH))
    tt_s = _to_strided(tt_g.transpose(1, 0, 2).reshape(S, DIL_WIDTH))
    dq_s, dk_s, dv_s = _sw_bwd(sv["qs"], sv["ks"], sv["vs"], do_s, sv["lse_s"], tt_s, "sw_bwd")
    unstride = lambda t: _from_strided(t, S).transpose(1, 0, 2).reshape(S, DIL_WIDTH)
    dq_a, dk_a = _rope(unstride(dq_s), 0, unstride(dk_s), 0, ctab, stab, -1.0, "rope_bwd")
    dz = jnp.concatenate([dq_a, dq_f, dk_a, dk_f, unstride(dv_s), dv_f], axis=1)
    g_qkv = tn(sv["u"], dz, name="dw_qkv")
    g_f = tn(sv["u"], dzf, name="dw_f", tj=128)
    cols = W["w_in_cols"]
    g_in = jnp.concatenate([g_qkv, g_f[:, :N_FOX_HEADS]], axis=1)
    gw["w_in"] = jnp.stack([g_in[:, k * cols:(k + 1) * cols] for k in range(N_CHIPS)])
    du = _mm(dzf, W["w_f"], mode="nt", name="mm_du_f", out_dtypes=(F32,))
    du = _mm(dgp, W["w_gate"], mode="nt", name="mm_du_gate", b_chunked=True, out_dtypes=(F32,),
             extras=[(du, "tile")], epilogue=lambda acc, r: (acc + r,), tj=512)
    du = _mm(dz, W["w_qkv"], mode="nt", name="mm_du_qkv", b_cols=3 * ATTN_WIDTH, extras=[(du, "tile")],
             epilogue=lambda acc, r: (acc + r,), tj=512)
    dh0, _, gs["g_mix"] = _rms_bwd(sv["h0"], small["g_mix"], du, dh1, "rms_mix_bwd")
    return dh0, gw, gs


BIG = ("w_in", "w_gate", "w_br_a", "w_br_b", "w_o", "w_up", "w_down", "w_ple", "w_ple_gate")
EARLY = ("w_ple", "w_ple_gate", "w_down", "w_up")
SMALL = ("g_mix", "b_f", "b_gate", "g_mlp", "g_ple", "g_final")
ORDER = ("g_mix", "w_in", "b_f", "w_gate", "b_gate", "w_br_a", "w_br_b", "w_o", "g_mlp", "w_up", "w_down",
         "g_ple", "w_ple", "w_ple_gate", "g_final")


def _gathered_layer_weights(full, D):
    W = {}
    for name in ("w_gate", "w_br_a", "w_br_b", "w_up", "w_ple"):
        W[name] = full[name]
    for name in ("w_o", "w_down", "w_ple_gate"):
        t = full[name]
        W[name] = t.reshape(t.shape[0] * t.shape[1], t.shape[2])
    w_in = full["w_in"]
    cols = w_in.shape[2]
    w_in = jnp.concatenate([w_in[k] for k in range(N_CHIPS)], axis=1)
    W["w_qkv"] = w_in
    W["w_f"] = jnp.pad(w_in[:, 3 * ATTN_WIDTH:], ((0, 0), (0, LANES - N_FOX_HEADS)))
    W["w_in_cols"] = cols
    return W


def _pack_rows(vals):
    flat = jnp.concatenate([v.reshape(-1) for v in vals])
    rows = -(-flat.shape[0] // (8 * LANES)) * 8
    return jnp.pad(flat, (0, rows * LANES - flat.shape[0])).reshape(rows, LANES)


def _unpack_rows(packed, shapes):
    flat = packed.reshape(-1)
    out, pos = [], 0
    for s in shapes:
        size = 1
        for dim in s:
            size *= dim
        out.append(flat[pos:pos + size].reshape(s))
        pos += size
    return out


def _local_step(x, p, small_w, comm, loss_target):
    depth = p.shape[0]
    S, D = x.shape
    tabs = _rope_tables(S)
    h = x
    saved, weights, smalls = [], [], []
    state, _ = comm["gather_start"](0, x)
    full = comm["gather_finish"](comm["gather_mid"](state, x)[0], x)
    for l in range(depth):
        nxt = [None]
        g_mix = small_w["g_mix"][l][None]
        if l + 1 < depth:
            nxt[0], token = comm["gather_start"](l + 1, full["w_ple"])
            g_mix = g_mix + token[0, 0]

        def after_mlp(h2):
            if nxt[0] is None:
                return None
            nxt[0], token = comm["gather_mid"](nxt[0], h2)
            return token

        W = _gathered_layer_weights(full, D)
        sm = dict(g_mix=g_mix, g_mlp=small_w["g_mlp"][l][None],
                  g_ple=small_w["g_ple"][l][None], b_gate=small_w["b_gate"][l][None],
                  b_f=jnp.pad(small_w["b_f"][l][None], ((0, 0), (0, LANES - N_FOX_HEADS))))
        h, sv = _layer_fwd(h, p[l].astype(BF16), W, sm, tabs, after_mlp)
        if nxt[0] is not None:
            full = comm["gather_finish"](nxt[0], h)
        saved.append(sv)
        weights.append(W)
        smalls.append(sm)
    dh, loss_row, dg_final = _loss_head(h, small_w["g_final"][None], loss_target, "loss_head")
    gss = [None] * depth
    pending, token = [None], None
    for l in reversed(range(depth)):
        sm = smalls[l]
        if token is not None:
            sm = {**sm, "g_ple": sm["g_ple"] + token[0, 0]}

        def after_mlp_grad(da):
            if pending[0] is None:
                return None
            pending[0] = comm["reduce_mid"](pending[0], da)
            return pending[0][-1]

        hooks = dict(mlp_grad=after_mlp_grad)
        if l == 0 and "reduce_early_begin" in comm:
            hooks.update(mlp_weights=comm["reduce_early_begin"], merge_grad=comm["reduce_early_mid"])
        dh, gw, gss[l] = _layer_bwd(dh, saved[l], weights[l], sm, tabs, hooks)
        if pending[0] is not None:
            comm["reduce_end"](pending[0], dh)
        if l > 0:
            pending[0], token = comm["reduce_begin"](l, gw, dh)
    return loss_row, dh, gss, dg_final, gw


def _device_comm(w, m, v, depth, c_arr, me_arr):
    n = len(BIG)
    totals = {name: None for name in BIG}
    placed, results = [], {}

    def gather_start(l, after):
        if l == 0:
            placed.append([_place_shard(w[name], 0, me_arr, [me_arr], "place_shard") for name in BIG])
        send, recv, bufs, token = _copies_start(_gather_ici_plan, placed[l], (3 * n,), after, f"gather_ici_start_{l}")
        if l == 0:
            for k in range(1, depth):
                placed.append([_place_shard(w[name], k, me_arr,
                                            [token, m["w_in"], v["w_in"]] if (k, name) == (1, BIG[0]) else [token],
                                            "place_shard") for name in BIG])
        return (l, send, recv, bufs), token

    def gather_mid(state, after):
        l, send, recv, bufs = state
        if l == 0:
            after = placed[-1][-1]
        bufs = _copies_wait(_gather_ici_plan, send, recv, bufs, after, f"gather_ici_wait_{l}")
        send, recv, bufs, token = _copies_start(_gather_d2d_plan, bufs, (3 * n,), after, f"gather_d2d_start_{l}")
        return (l, send, recv, bufs), token

    def gather_finish(state, after):
        l, send, recv, bufs = state
        return dict(zip(BIG, _copies_wait(_gather_d2d_plan, send, recv, bufs, after, f"gather_d2d_wait_{l}")))

    def reduce_begin(l, gw, after, names=BIG, tag=""):
        grads = [gw[name] for name in names]
        landing = [lax.empty((g.shape[0], g.shape[1] // 2, g.shape[2]), g.dtype) for g in grads]
        send, recv, arrays, token = _copies_start(_pair_plan, grads + landing, (len(names),), after,
                                                  f"pair_start_{l}{tag}")
        return (l, tag, names, send, recv, arrays), token

    def reduce_mid(state, after):
        l, tag, names, send, recv, arrays = state
        k = len(names)
        arrays = _copies_wait(_pair_plan, send, recv, arrays, after, f"pair_wait_{l}{tag}")
        sums = [_pair_sum(g, t, c_arr, "rs_pair_sum") for g, t in zip(arrays[:k], arrays[k:])]
        landing = [lax.empty((3,) + s.shape[1:], s.dtype) for s in sums]
        send, recv, arrays, token = _copies_start(_scatter_ici_plan, sums + landing, (3 * k,), sums[0],
                                                  f"scatter_ici_start_{l}{tag}")
        return l, tag, names, send, recv, arrays, token

    def reduce_end(state, after):
        l, tag, names, send, recv, arrays, _ = state
        k = len(names)
        arrays = _copies_wait(_scatter_ici_plan, send, recv, arrays, after, f"scatter_ici_wait_{l}{tag}")
        done = [_chip_sum(s, o, me_arr, c_arr, totals[name], l, depth, "rs_chip_sum")
                for name, s, o in zip(names, arrays[:k], arrays[k:])]
        totals.update(zip(names, _half_exchange(done, l, "rs_half_exchange")))

    early = [None]

    def reduce_early_begin(gw):
        early[0], token = reduce_begin(0, gw, gw[EARLY[-1]], EARLY, "a")
        return token

    def reduce_early_mid(after):
        early[0] = reduce_mid(early[0], after)
        return early[0][-1]

    def reduce_last(gw, after):
        upper = {}

        def adamw_upper(names, token):
            for name in names:
                if depth > 1:
                    upper[name] = _adamw(w[name], totals[name], m[name], v[name], 1, depth, None, token,
                                         "adamw_upper")
                    token = upper[name][0]
            return token

        def adamw_first(names, token):
            for name in names:
                results[name] = _adamw(w[name], totals[name], m[name], v[name], 0, 1, upper.get(name), token,
                                       "adamw_first")
                token = results[name][0]
            return token

        late = BIG if early[0] is None else tuple(name for name in BIG if name not in EARLY)
        state, token = reduce_begin(0, gw, after, late, "b")
        state = reduce_mid(state, adamw_upper(("w_up",), token))
        token = adamw_upper([name for name in BIG if name != "w_up"], state[-1])
        if early[0] is not None:
            reduce_end(early[0], token)
            token = adamw_first(EARLY, totals[EARLY[0]])
        reduce_end(state, token)
        adamw_first(late, totals[late[0]])

    comm = dict(gather_start=gather_start, gather_mid=gather_mid, gather_finish=gather_finish,
                reduce_begin=reduce_begin, reduce_mid=reduce_mid, reduce_end=reduce_end, reduce_last=reduce_last,
                reduce_early_begin=reduce_early_begin, reduce_early_mid=reduce_early_mid)
    return comm, results


def kernel(x, p, g_mix, w_in, b_f, w_gate, b_gate, w_br_a, w_br_b, w_o, g_mlp, w_up, w_down, g_ple, w_ple, w_ple_gate, g_final, loss_target, m_g_mix, m_w_in, m_b_f, m_w_gate, m_b_gate, m_w_br_a, m_w_br_b, m_w_o, m_g_mlp, m_w_up, m_w_down, m_g_ple, m_w_ple, m_w_ple_gate, m_g_final, v_g_mix, v_w_in, v_b_f, v_w_gate, v_b_gate, v_w_br_a, v_w_br_b, v_w_o, v_g_mlp, v_w_up, v_w_down, v_g_ple, v_w_ple, v_w_ple_gate, v_g_final):
    w = dict(g_mix=g_mix, w_in=w_in, b_f=b_f, w_gate=w_gate, b_gate=b_gate, w_br_a=w_br_a, w_br_b=w_br_b,
             w_o=w_o, g_mlp=g_mlp, w_up=w_up, w_down=w_down, g_ple=g_ple, w_ple=w_ple, w_ple_gate=w_ple_gate,
             g_final=g_final)
    m = dict(g_mix=m_g_mix, w_in=m_w_in, b_f=m_b_f, w_gate=m_w_gate, b_gate=m_b_gate, w_br_a=m_w_br_a,
             w_br_b=m_w_br_b, w_o=m_w_o, g_mlp=m_g_mlp, w_up=m_w_up, w_down=m_w_down, g_ple=m_g_ple,
             w_ple=m_w_ple, w_ple_gate=m_w_ple_gate, g_final=m_g_final)
    v = dict(g_mix=v_g_mix, w_in=v_w_in, b_f=v_b_f, w_gate=v_w_gate, b_gate=v_b_gate, w_br_a=v_w_br_a,
             w_br_b=v_w_br_b, w_o=v_w_o, g_mlp=v_g_mlp, w_up=v_w_up, w_down=v_w_down, g_ple=v_g_ple,
             w_ple=v_w_ple, w_ple_gate=v_w_ple_gate, g_final=v_g_final)
    depth = p.shape[0]
    cx, cy, cc = lax.axis_index("x"), lax.axis_index("y"), lax.axis_index("c")
    c_arr = jnp.reshape(cc, (1,)).astype(jnp.int32)
    me_arr = jnp.reshape(2 * cx + cy, (1,)).astype(jnp.int32)

    comm, big_out = _device_comm(w, m, v, depth, c_arr, me_arr)
    loss_row, grad_x, gss, dg_final, gw0 = _local_step(x[0], p[:, 0], w, comm, loss_target[0])

    small_grads = [jnp.stack([gss[l][n][0] for l in range(depth)]) for n in SMALL[:-1]] + [dg_final[0]]
    shapes = [w[n].shape for n in SMALL]
    parts = _allgather_devices(_pack_rows(small_grads), "allgather_small")
    packed = _adamw_small(_pack_rows([w[n] for n in SMALL]), parts, _pack_rows([m[n] for n in SMALL]),
                          _pack_rows([v[n] for n in SMALL]), "adamw_small")
    small_out = {n: vals for n, vals in zip(SMALL, zip(*[_unpack_rows(t, shapes) for t in packed]))}
    comm["reduce_last"](gw0, packed[0])

    loss = lax.psum(loss_row[0, 0], ("x", "y", "c"))
    out = {**big_out, **small_out}
    return (loss, grad_x[None], *[out[n][0] for n in ORDER], *[out[n][1] for n in ORDER],
            *[out[n][2] for n in ORDER], *[out[n][3] for n in ORDER])
```

```python
import functools

import jax
import jax.numpy as jnp
from jax import lax
from jax.experimental import pallas as pl
from jax.experimental.pallas import tpu as pltpu

F32 = jnp.float32
BF16 = jnp.bfloat16

HEAD_DIM = 128
N_HEADS = 16
N_DIL_HEADS = 12
N_FOX_HEADS = 4
HEADS_PER_DIL = 4
DILATIONS = (1, 4, 16)
BLOCK = 128
ATTN_WIDTH = N_HEADS * HEAD_DIM
DIL_WIDTH = N_DIL_HEADS * HEAD_DIM
FOX_WIDTH = N_FOX_HEADS * HEAD_DIM
ROPE_THETA = 500000.0
ROPE_HALF = 16
NORM_EPS = 1e-6
SCALE = HEAD_DIM ** -0.5
NEG = -1e30

ADAM_LR = 0.001
ADAM_B1 = 0.9
ADAM_B2 = 0.999
ADAM_EPS = 1e-08
ADAM_WD = 0.01
ADAM_STEP = 10

N_CHIPS = 4
V7X_VMEM_LIMIT_BYTES = 56 * 1024 * 1024
LANES = 128
MESH = pl.DeviceIdType.MESH
ANY = pl.BlockSpec(memory_space=pl.ANY)


def _params(sem):
    return pltpu.CompilerParams(dimension_semantics=sem, vmem_limit_bytes=V7X_VMEM_LIMIT_BYTES)


def _tile(n, pref):
    if n <= pref:
        return n
    t = (pref // LANES) * LANES
    while t > LANES and n % t:
        t -= LANES
    assert n % t == 0, (n, pref)
    return t


def _mm(a, b, *, mode, name, out_dtypes=(BF16,), epilogue=None, extras=(), b_chunked=False,
        out_chunks=0, b_cols=None, ti=2048, tj=512, tc=2048):
    if mode == "tn":
        C, I = a.shape
    else:
        I, C = a.shape
    if b_chunked:
        nch, d0, n = b.shape
        if mode == "nn":
            assert d0 == C
            J = nch * n
        else:
            assert mode == "nt" and nch * n == C
            J = d0
    elif mode == "nt":
        J = b.shape[0]
        assert (b.shape[1] if b_cols is None else b_cols) == C
    else:
        assert b.shape[0] == C
        J = b.shape[1] if b_cols is None else b_cols
    ti, tc = _tile(I, ti), _tile(C, tc)
    if b_chunked and mode == "nn":
        tj = _tile(n, tj)
    elif out_chunks:
        tj = _tile(J // out_chunks, tj)
    else:
        tj = _tile(J, tj)
    if b_chunked and mode == "nt":
        tc = _tile(n, tc)
    ni, nj, nc = I // ti, J // tj, C // tc

    if mode == "tn":
        a_spec = pl.BlockSpec((tc, ti), lambda i, j, c: (c, i))
        dims = (((0,), (0,)), ((), ()))
    else:
        a_spec = pl.BlockSpec((ti, tc), lambda i, j, c: (i, c))
        dims = (((1,), (0,)), ((), ())) if mode == "nn" else (((1,), (1,)), ((), ()))
    if mode == "nt":
        if b_chunked:
            cb = n // tc
            b_spec = pl.BlockSpec((None, tj, tc), lambda i, j, c: (c // cb, j, c % cb))
        else:
            b_spec = pl.BlockSpec((tj, tc), lambda i, j, c: (j, c))
    else:
        if b_chunked:
            jb = n // tj
            b_spec = pl.BlockSpec((None, tc, tj), lambda i, j, c: (j // jb, c, j % jb))
        else:
            b_spec = pl.BlockSpec((tc, tj), lambda i, j, c: (c, j))
    extra_specs = []
    for arr, kind in extras:
        if kind == "tile":
            assert arr.shape == (I, J), (arr.shape, I, J)
            extra_specs.append(pl.BlockSpec((ti, tj), lambda i, j, c: (i, j)))
        else:
            assert arr.shape == (1, J)
            extra_specs.append(pl.BlockSpec((1, tj), lambda i, j, c: (0, j)))
    if out_chunks:
        ob = (J // out_chunks) // tj
        out_spec = pl.BlockSpec((None, ti, tj), lambda i, j, c: (j // ob, i, j % ob))
        out_shape = [jax.ShapeDtypeStruct((out_chunks, I, J // out_chunks), d) for d in out_dtypes]
    else:
        out_spec = pl.BlockSpec((ti, tj), lambda i, j, c: (i, j))
        out_shape = [jax.ShapeDtypeStruct((I, J), d) for d in out_dtypes]
    ne, no = len(extras), len(out_dtypes)
    if epilogue is None:
        epilogue = lambda acc: (acc,)

    def body(a_ref, b_ref, *rest):
        extra_refs, out_refs = rest[:ne], rest[ne:ne + no]

        def finish(acc):
            outs = epilogue(acc, *[r[...] for r in extra_refs])
            for o_ref, val in zip(out_refs, outs):
                o_ref[...] = val.astype(o_ref.dtype)

        part = lax.dot_general(a_ref[...], b_ref[...], dims, preferred_element_type=F32)
        if nc == 1:
            finish(part)
        else:
            acc_ref = rest[-1]
            k = pl.program_id(2)

            @pl.when(k == 0)
            def _():
                acc_ref[...] = part

            @pl.when(k > 0)
            def _():
                acc_ref[...] += part

            @pl.when(k == nc - 1)
            def _():
                finish(acc_ref[...])

    outs = pl.pallas_call(
        body, name=name, grid=(ni, nj, nc),
        in_specs=[a_spec, b_spec] + extra_specs,
        out_specs=[out_spec] * no, out_shape=out_shape,
        scratch_shapes=[pltpu.VMEM((ti, tj), F32)] if nc > 1 else [],
        compiler_params=_params(("parallel", "parallel", "arbitrary")),
    )(a, b, *[e[0] for e in extras])
    return outs[0] if no == 1 else tuple(outs)


def _ew(fn, ins, outs, grid, name):
    n_in = len(ins)
    has_acc = any(o[3] for o in outs)
    assert not has_acc or len(grid) == 1

    def body(*refs):
        vals = fn(*[r[...] for r in refs[:n_in]])
        for o_ref, o, val in zip(refs[n_in:], outs, vals):
            if o[3]:
                step = pl.program_id(0)

                @pl.when(step == 0)
                def _(o_ref=o_ref, val=val):
                    o_ref[...] = val

                @pl.when(step > 0)
                def _(o_ref=o_ref, val=val):
                    o_ref[...] += val
            else:
                o_ref[...] = val.astype(o_ref.dtype)

    sem = ("arbitrary",) if has_acc else ("parallel",) * len(grid)
    res = pl.pallas_call(
        body, name=name, grid=grid,
        in_specs=[i[1] for i in ins], out_specs=[o[2] for o in outs],
        out_shape=[jax.ShapeDtypeStruct(o[0], o[1]) for o in outs],
        compiler_params=_params(sem),
    )(*[i[0] for i in ins])
    return res[0] if len(outs) == 1 else tuple(res)


def _rows(tr, w):
    return pl.BlockSpec((tr, w), lambda i: (i, 0))


def _bcast(w):
    return pl.BlockSpec((1, w), lambda i: (0, 0))


ROW_BLOCK_BYTES = 4 * 1024 * 1024


def _row_tile(S, width_bytes):
    tr = 512
    while tr > 16 and tr * width_bytes > ROW_BLOCK_BYTES:
        tr //= 2
    return min(tr, S)


def _rms_fwd(h, g, name):
    S, D = h.shape
    tr = _row_tile(S, D * 4)

    def fn(x, gg):
        r = lax.rsqrt(jnp.mean(x * x, axis=-1, keepdims=True) + NORM_EPS)
        return (x * r * gg,)

    return _ew(fn, [(h, _rows(tr, D)), (g, _bcast(D))], [((S, D), BF16, _rows(tr, D), False)],
               (S // tr,), name)


def _rms_bwd(x, g, dy, dres, name):
    S, D = x.shape
    tr = _row_tile(S, D * 4)

    def fn(xv, gg, dyv, dr):
        r = lax.rsqrt(jnp.mean(xv * xv, axis=-1, keepdims=True) + NORM_EPS)
        dyf = dyv.astype(F32)
        gy = dyf * gg
        dx = r * gy - xv * (r * r * r) * jnp.mean(xv * gy, axis=-1, keepdims=True)
        tot = dr + dx
        dg = jnp.sum(dyf * xv * r, axis=0, keepdims=True)
        return tot, tot, dg

    return _ew(fn, [(x, _rows(tr, D)), (g, _bcast(D)), (dy, _rows(tr, D)), (dres, _rows(tr, D))],
               [((S, D), F32, _rows(tr, D), False), ((S, D), BF16, _rows(tr, D), False),
                ((1, D), F32, _bcast(D), True)], (S // tr,), name)


def _loss_head(h, g, target, name):
    S, D = h.shape
    tr = _row_tile(S, D * 4)

    def fn(xv, gg, tgt):
        r = lax.rsqrt(jnp.mean(xv * xv, axis=-1, keepdims=True) + NORM_EPS)
        y = xv * r * gg
        e = y - tgt
        loss = 0.5 * jnp.sum(jnp.mean(e * e, axis=-1, keepdims=True), axis=0, keepdims=True)
        dy = e * (1.0 / D)
        gy = dy * gg
        dx = r * gy - xv * (r * r * r) * jnp.mean(xv * gy, axis=-1, keepdims=True)
        dg = jnp.sum(dy * xv * r, axis=0, keepdims=True)
        return dx, jnp.broadcast_to(loss, (1, LANES)), dg

    return _ew(fn, [(h, _rows(tr, D)), (g, _bcast(D)), (target, _rows(tr, D))],
               [((S, D), F32, _rows(tr, D), False), ((1, LANES), F32, _bcast(LANES), True),
                ((1, D), F32, _bcast(D), True)], (S // tr,), name)


def _rope_tables(S):
    inv = ROPE_THETA ** (-jnp.arange(ROPE_HALF, dtype=F32) / ROPE_HALF)
    ang = jnp.arange(S, dtype=F32)[:, None] * inv[None, :]
    cos, sin = jnp.cos(ang), jnp.sin(ang)
    rest = HEAD_DIM - 2 * ROPE_HALF
    ctab = jnp.concatenate([cos, cos, jnp.ones((S, rest), F32)], axis=1)
    stab = jnp.concatenate([-sin, sin, jnp.zeros((S, rest), F32)], axis=1)
    return ctab, stab


def _swap_halves(x):
    lane = lax.broadcasted_iota(jnp.int32, x.shape, 1)
    return jnp.where(lane < ROPE_HALF, pltpu.roll(x, HEAD_DIM - ROPE_HALF, 1), pltpu.roll(x, ROPE_HALF, 1))


def _rope(q_src, q_col0, k_src, k_col0, ctab, stab, sign, name):
    S = q_src.shape[0]
    tr = min(512, S)
    width = HEADS_PER_DIL * HEAD_DIM

    def fn(q, k, ct, st):
        outs = []
        for v in (q, k):
            heads = []
            for h in range(HEADS_PER_DIL):
                vf = v[:, h * HEAD_DIM:(h + 1) * HEAD_DIM].astype(F32)
                heads.append(vf * ct + sign * _swap_halves(vf) * st)
            outs.append(jnp.concatenate(heads, axis=1))
        return tuple(outs)

    group = lambda c0: pl.BlockSpec((tr, width), lambda i, g: (i, c0 // HEADS_PER_DIL + g))
    tab = pl.BlockSpec((tr, HEAD_DIM), lambda i, g: (i, 0))
    out = ((S, DIL_WIDTH), BF16, group(0), False)
    return _ew(fn, [(q_src, group(q_col0)), (k_src, group(k_col0)), (ctab, tab), (stab, tab)],
               [out, out], (S // tr, len(DILATIONS)), name)


SW_BLOCKS_PER_STEP = 16


def _to_strided(x):
    S = x.shape[0]
    parts = []
    for g, d in enumerate(DILATIONS):
        xg = x[:, g * 512:(g + 1) * 512].reshape(S // d, d, HEADS_PER_DIL, HEAD_DIM)
        parts.append(xg.transpose(1, 2, 0, 3).reshape(-1, BLOCK, HEAD_DIM))
    return jnp.concatenate(parts, axis=0)


def _from_strided(y, S):
    per = y.shape[0] // len(DILATIONS)
    parts = []
    for g, d in enumerate(DILATIONS):
        yg = y[g * per:(g + 1) * per].reshape(d, HEADS_PER_DIL, S // d, HEAD_DIM)
        parts.append(yg.transpose(2, 0, 1, 3).reshape(S, HEADS_PER_DIL * HEAD_DIM))
    return jnp.stack(parts, axis=0)


def _seq_blocks(b0, per_group):
    g = b0 // per_group
    n0 = per_group // HEADS_PER_DIL
    return jnp.where(g == 0, n0, jnp.where(g == 1, n0 // 4, n0 // 16))


def _sw_masks():
    qi = lax.broadcasted_iota(jnp.int32, (BLOCK, BLOCK), 0)
    ki = lax.broadcasted_iota(jnp.int32, (BLOCK, BLOCK), 1)
    return qi >= ki, qi <= ki


def _sw_fwd(q, k, v, name):
    NB = q.shape[0]
    T = SW_BLOCKS_PER_STEP
    per_group = NB // len(DILATIONS)
    nt = (((1,), (1,)), ((), ()))

    def body(q_ref, k_ref, v_ref, kp_ref, vp_ref, o_ref, lse_ref):
        b0 = pl.program_id(0) * T
        nseq = _seq_blocks(b0, per_group)
        cur_mask, prev_mask = _sw_masks()
        for t in range(T):
            has_prev = ((b0 + t) & (nseq - 1)) != 0
            qt = q_ref[t]
            kp = kp_ref[0] if t == 0 else k_ref[t - 1]
            vp = vp_ref[0] if t == 0 else v_ref[t - 1]
            s_c = lax.dot_general(qt, k_ref[t], nt, preferred_element_type=F32) * SCALE
            s_p = lax.dot_general(qt, kp, nt, preferred_element_type=F32) * SCALE
            s_c = jnp.where(cur_mask, s_c, NEG)
            s_p = jnp.where(prev_mask, s_p, NEG) + jnp.where(has_prev, 0.0, NEG)
            m = jnp.maximum(jnp.max(s_c, axis=-1, keepdims=True), jnp.max(s_p, axis=-1, keepdims=True))
            p_c = jnp.exp(s_c - m)
            p_p = jnp.exp(s_p - m)
            l = jnp.sum(p_c, axis=-1, keepdims=True) + jnp.sum(p_p, axis=-1, keepdims=True)
            o = (jnp.dot(p_c.astype(BF16), v_ref[t], preferred_element_type=F32)
                 + jnp.dot(p_p.astype(BF16), vp, preferred_element_type=F32))
            o_ref[t] = (o / l).astype(o_ref.dtype)
            lse_ref[t] = jnp.broadcast_to(m + jnp.log(l), (BLOCK, HEAD_DIM))

    tile = pl.BlockSpec((T, BLOCK, HEAD_DIM), lambda i: (i, 0, 0))
    before = pl.BlockSpec((1, BLOCK, HEAD_DIM), lambda i: (jnp.maximum(i * T - 1, 0), 0, 0))
    return pl.pallas_call(
        body, name=name, grid=(NB // T,),
        in_specs=[tile, tile, tile, before, before], out_specs=[tile, tile],
        out_shape=[jax.ShapeDtypeStruct(q.shape, BF16), jax.ShapeDtypeStruct(q.shape, F32)],
        compiler_params=_params(("parallel",)),
    )(q, k, v, k, v)


def _sw_bwd(q, k, v, do, lse, tt, name):
    NB = q.shape[0]
    T = SW_BLOCKS_PER_STEP
    per_group = NB // len(DILATIONS)
    nt = (((1,), (1,)), ((), ()))
    tn = (((0,), (0,)), ((), ()))

    def body(q_ref, k_ref, v_ref, do_ref, lse_ref, tt_ref, kp_ref, vp_ref, qn_ref, don_ref, lsen_ref,
             ttn_ref, dq_ref, dk_ref, dv_ref):
        b0 = pl.program_id(0) * T
        nseq = _seq_blocks(b0, per_group)
        cur_mask, prev_mask = _sw_masks()

        def probs(qq, kk, lse_b, mask, gate):
            s = lax.dot_general(qq, kk, nt, preferred_element_type=F32) * SCALE
            return jnp.exp(jnp.where(mask, s, NEG) + gate - lse_b)

        for t in range(T):
            has_prev = jnp.where(((b0 + t) & (nseq - 1)) != 0, 0.0, NEG)
            has_next = jnp.where(((b0 + t + 1) & (nseq - 1)) != 0, 0.0, NEG)
            last = t == T - 1
            qt, kt, vt, dot = q_ref[t], k_ref[t], v_ref[t], do_ref[t]
            kp = kp_ref[0] if t == 0 else k_ref[t - 1]
            vp = vp_ref[0] if t == 0 else v_ref[t - 1]
            qn = qn_ref[0] if last else q_ref[t + 1]
            don = don_ref[0] if last else do_ref[t + 1]
            lsen = lsen_ref[0] if last else lse_ref[t + 1]
            ttn = ttn_ref[0] if last else tt_ref[t + 1]
            p_cc = probs(qt, kt, lse_ref[t], cur_mask, 0.0)
            p_cp = probs(qt, kp, lse_ref[t], prev_mask, has_prev)
            p_nc = probs(qn, kt, lsen, prev_mask, has_next)
            ds_cc = p_cc * (lax.dot_general(dot, vt, nt, preferred_element_type=F32) + tt_ref[t])
            ds_cp = p_cp * (lax.dot_general(dot, vp, nt, preferred_element_type=F32) + tt_ref[t])
            ds_nc = p_nc * (lax.dot_general(don, vt, nt, preferred_element_type=F32) + ttn)
            ds_cc, ds_cp, ds_nc = ds_cc.astype(BF16), ds_cp.astype(BF16), ds_nc.astype(BF16)
            dq = (jnp.dot(ds_cc, kt, preferred_element_type=F32)
                  + jnp.dot(ds_cp, kp, preferred_element_type=F32))
            dk = (lax.dot_general(ds_cc, qt, tn, preferred_element_type=F32)
                  + lax.dot_general(ds_nc, qn, tn, preferred_element_type=F32))
            dv = (lax.dot_general(p_cc.astype(BF16), dot, tn, preferred_element_type=F32)
                  + lax.dot_general(p_nc.astype(BF16), don, tn, preferred_element_type=F32))
            dq_ref[t] = (dq * SCALE).astype(BF16)
            dk_ref[t] = (dk * SCALE).astype(BF16)
            dv_ref[t] = dv.astype(BF16)

    tile = pl.BlockSpec((T, BLOCK, HEAD_DIM), lambda i: (i, 0, 0))
    before = pl.BlockSpec((1, BLOCK, HEAD_DIM), lambda i: (jnp.maximum(i * T - 1, 0), 0, 0))
    after = pl.BlockSpec((1, BLOCK, HEAD_DIM), lambda i: (jnp.minimum(i * T + T, NB - 1), 0, 0))
    out = jax.ShapeDtypeStruct(q.shape, BF16)
    return pl.pallas_call(
        body, name=name, grid=(NB // T,),
        in_specs=[tile] * 6 + [before, before, after, after, after, after],
        out_specs=[tile] * 3, out_shape=[out] * 3,
        compiler_params=_params(("parallel",)),
    )(q, k, v, do, lse, tt, k, v, q, do, lse, tt)


def _group_softmax(lse):
    m = jnp.max(lse, axis=0, keepdims=True)
    e = jnp.exp(lse - m)
    return e / jnp.sum(e, axis=0, keepdims=True)


def _mix_fwd(o, lse, name):
    G, S, W = o.shape
    tr = min(256, S)
    blk = pl.BlockSpec((G, tr, W), lambda i: (0, i, 0))

    def fn(ov, lv):
        return (jnp.sum(_group_softmax(lv) * ov.astype(F32), axis=0),)

    return _ew(fn, [(o, blk), (lse, blk)], [((S, W), BF16, _rows(tr, W), False)], (S // tr,), name)


def _mix_bwd(dya, ya, o, lse, name):
    G, S, W = o.shape
    tr = min(256, S)
    blk = pl.BlockSpec((G, tr, HEAD_DIM), lambda i, h: (0, i, h))
    row = pl.BlockSpec((tr, HEAD_DIM), lambda i, h: (i, h))

    def fn(dy, yv, ov, lv):
        w = _group_softmax(lv)
        dyf = dy.astype(F32)
        inner = jnp.sum(dyf * yv.astype(F32), axis=-1, keepdims=True)
        return w * dyf[None], -w * inner[None]

    return _ew(fn, [(dya, row), (ya, row), (o, blk), (lse, blk)],
               [((G, S, W), BF16, blk, False), ((G, S, W), F32, blk, False)],
               (S // tr, HEADS_PER_DIL), name)


CUM_BLOCK = 256


def _split3(x):
    hi = x.astype(BF16)
    r = x - hi.astype(F32)
    mid = r.astype(BF16)
    lo = (r - mid.astype(F32)).astype(BF16)
    return hi, mid, lo


def _tri_matmul(tri, x):
    return sum(jnp.dot(tri, part, preferred_element_type=F32) for part in _split3(x))


def _log_sigmoid(x):
    return jnp.minimum(x, 0.0) - jnp.log(1.0 + jnp.exp(-jnp.abs(x)))


def _fox_prep(f, b, name):
    S = f.shape[0]
    tb = min(CUM_BLOCK, S)

    def body(f_ref, b_ref, c_ref, carry):
        @pl.when(pl.program_id(0) == 0)
        def _():
            carry[...] = jnp.zeros_like(carry)

        ls = _log_sigmoid(f_ref[...] + b_ref[...])
        r = lax.broadcasted_iota(jnp.int32, (tb, tb), 0)
        cidx = lax.broadcasted_iota(jnp.int32, (tb, tb), 1)
        tri = jnp.where(r >= cidx, 1.0, 0.0).astype(BF16)
        c_ref[...] = _tri_matmul(tri, ls) + carry[...]
        carry[...] += jnp.sum(ls, axis=0, keepdims=True)

    return pl.pallas_call(
        body, name=name, grid=(S // tb,),
        in_specs=[_rows(tb, LANES), _bcast(LANES)], out_specs=_rows(tb, LANES),
        out_shape=jax.ShapeDtypeStruct((S, LANES), F32),
        scratch_shapes=[pltpu.VMEM((1, LANES), F32)],
        compiler_params=_params(("arbitrary",)),
    )(f, b)


def _fox_prep_bwd(dc, f, b, name):
    S = f.shape[0]
    tb = min(CUM_BLOCK, S)
    nb = S // tb

    def body(dc_ref, f_ref, b_ref, df_ref, db_ref, carry):
        @pl.when(pl.program_id(0) == 0)
        def _():
            carry[...] = jnp.zeros_like(carry)
            db_ref[...] = jnp.zeros_like(db_ref)

        r = lax.broadcasted_iota(jnp.int32, (tb, tb), 0)
        cidx = lax.broadcasted_iota(jnp.int32, (tb, tb), 1)
        tri = jnp.where(r <= cidx, 1.0, 0.0).astype(BF16)
        dcv = dc_ref[...]
        dls = _tri_matmul(tri, dcv) + carry[...]
        carry[...] += jnp.sum(dcv, axis=0, keepdims=True)
        z = f_ref[...] + b_ref[...]
        df = dls * (1.0 / (1.0 + jnp.exp(z)))
        df_ref[...] = df
        db_ref[...] += jnp.sum(df, axis=0, keepdims=True)

    rev = pl.BlockSpec((tb, LANES), lambda i: (nb - 1 - i, 0))
    return pl.pallas_call(
        body, name=name, grid=(nb,),
        in_specs=[rev, rev, _bcast(LANES)], out_specs=[rev, _bcast(LANES)],
        out_shape=[jax.ShapeDtypeStruct((S, LANES), F32), jax.ShapeDtypeStruct((1, LANES), F32)],
        scratch_shapes=[pltpu.VMEM((1, LANES), F32)],
        compiler_params=_params(("arbitrary",)),
    )(dc, f, b)


FOX_Q_TILE = 512
_FOX_Q0 = N_DIL_HEADS
_FOX_K0 = N_HEADS + N_DIL_HEADS
_FOX_V0 = 2 * N_HEADS + N_DIL_HEADS


def _fox_scores(q, k, cq, ck, q0):
    nt = (((1,), (1,)), ((), ()))
    s = lax.dot_general(q, k, nt, preferred_element_type=F32) * SCALE + cq - ck
    qpos = q0 + lax.broadcasted_iota(jnp.int32, s.shape, 0)
    kpos = lax.broadcasted_iota(jnp.int32, s.shape, 1)
    return jnp.where(kpos <= qpos, s, NEG)


def _per_query_tile(nq, tq, fn):
    step = pl.program_id(1)
    for n in range(nq):
        @pl.when(step == n)
        def _(n=n):
            fn(n, pl.ds(0, (n + 1) * tq))


def _fox_fwd(z, cq, ck, name):
    S = z.shape[0]
    tq = min(FOX_Q_TILE, S)

    def body(q_ref, k_ref, v_ref, cq_ref, ck_ref, o_ref, lse_ref):
        def tile(n, keys):
            s = _fox_scores(q_ref[...], k_ref[keys, :], cq_ref[...], ck_ref[:, keys], n * tq)
            m = jnp.max(s, axis=-1, keepdims=True)
            p = jnp.exp(s - m)
            l = jnp.sum(p, axis=-1, keepdims=True)
            o = jnp.dot(p.astype(BF16), v_ref[keys, :], preferred_element_type=F32)
            o_ref[...] = (o / l).astype(o_ref.dtype)
            lse_ref[...] = m + jnp.log(l)

        _per_query_tile(S // tq, tq, tile)

    qblk = lambda c0: pl.BlockSpec((tq, HEAD_DIM), lambda h, i: (i, c0 + h))
    full = lambda c0: pl.BlockSpec((S, HEAD_DIM), lambda h, i: (0, c0 + h))
    col = pl.BlockSpec((None, tq, 1), lambda h, i: (h, i, 0))
    rowv = pl.BlockSpec((None, 1, S), lambda h, i: (h, 0, 0))
    return pl.pallas_call(
        body, name=name, grid=(N_FOX_HEADS, S // tq),
        in_specs=[qblk(_FOX_Q0), full(_FOX_K0), full(_FOX_V0), col, rowv],
        out_specs=[qblk(0), col],
        out_shape=[jax.ShapeDtypeStruct((S, FOX_WIDTH), BF16),
                   jax.ShapeDtypeStruct((N_FOX_HEADS, S, 1), F32)],
        compiler_params=_params(("parallel", "parallel")),
    )(z, z, z, cq, ck)


def _fox_bwd(z, cq, ck, lse, yb, dyb, name):
    S = z.shape[0]
    tq = min(FOX_Q_TILE, S)
    nq = S // tq
    nt = (((1,), (1,)), ((), ()))
    tn = (((0,), (0,)), ((), ()))

    def body(q_ref, k_ref, v_ref, cq_ref, ck_ref, lse_ref, o_ref, do_ref,
             dq_ref, dk_ref, dv_ref, dc_ref, dk_acc, dv_acc):
        i = pl.program_id(1)

        @pl.when(i == 0)
        def _():
            dk_acc[...] = jnp.zeros_like(dk_acc)
            dv_acc[...] = jnp.zeros_like(dv_acc)
            dc_ref[...] = jnp.zeros_like(dc_ref)

        def tile(n, keys):
            q, k, v, do = q_ref[...], k_ref[keys, :], v_ref[keys, :], do_ref[...]
            s = _fox_scores(q, k, cq_ref[...], ck_ref[:, keys], n * tq)
            p = jnp.exp(s - lse_ref[...])
            dp = lax.dot_general(do, v, nt, preferred_element_type=F32)
            ds = p * (dp - jnp.sum(p * dp, axis=-1, keepdims=True))
            dsb = ds.astype(BF16)
            dq_ref[...] = (jnp.dot(dsb, k, preferred_element_type=F32) * SCALE).astype(BF16)
            dk_acc[keys, :] += lax.dot_general(dsb, q, tn, preferred_element_type=F32) * SCALE
            dv_acc[keys, :] += lax.dot_general(p.astype(BF16), do, tn, preferred_element_type=F32)
            dc_ref[:, keys] -= jnp.sum(ds, axis=0, keepdims=True)

        _per_query_tile(nq, tq, tile)

        @pl.when(i == nq - 1)
        def _():
            dk_ref[...] = dk_acc[...].astype(BF16)
            dv_ref[...] = dv_acc[...].astype(BF16)

    qblk = lambda c0: pl.BlockSpec((tq, HEAD_DIM), lambda h, i: (i, c0 + h))
    full = lambda c0: pl.BlockSpec((S, HEAD_DIM), lambda h, i: (0, c0 + h))
    col = pl.BlockSpec((None, tq, 1), lambda h, i: (h, i, 0))
    rowv = pl.BlockSpec((None, 1, S), lambda h, i: (h, 0, 0))
    wide = jax.ShapeDtypeStruct((S, FOX_WIDTH), BF16)
    return pl.pallas_call(
        body, name=name, grid=(N_FOX_HEADS, nq),
        in_specs=[qblk(_FOX_Q0), full(_FOX_K0), full(_FOX_V0), col, rowv, col, qblk(0), qblk(0)],
        out_specs=[qblk(0), full(0), full(0), rowv],
        out_shape=[wide, wide, wide, jax.ShapeDtypeStruct((N_FOX_HEADS, 1, S), F32)],
        scratch_shapes=[pltpu.VMEM((S, HEAD_DIM), F32), pltpu.VMEM((S, HEAD_DIM), F32)],
        compiler_params=_params(("parallel", "arbitrary")),
    )(z, z, z, cq, ck, lse, yb, dyb)


def _sigmoid(x):
    return 1.0 / (1.0 + jnp.exp(-x))


def _merge_fwd(gates, a, bm, name):
    S, D = a.shape
    tr = _row_tile(S, D * 4)
    g1 = pl.BlockSpec((tr, D), lambda i: (i, 0))
    g2 = pl.BlockSpec((tr, D), lambda i: (i, 1))

    def fn(x1, x2, av, bv):
        return (x1.astype(F32) * av.astype(F32) + x2.astype(F32) * bv.astype(F32),)

    return _ew(fn, [(gates, g1), (gates, g2), (a, _rows(tr, D)), (bm, _rows(tr, D))],
               [((S, D), BF16, _rows(tr, D), False)], (S // tr,), name)


def _merge_bwd(dmerged, gates, a, bm, name):
    S, D = a.shape
    tr = _row_tile(S, D * 8)
    g1 = pl.BlockSpec((tr, D), lambda i: (i, 0))
    g2 = pl.BlockSpec((tr, D), lambda i: (i, 1))

    def fn(dm, x1, x2, av, bv):
        dm, x1, x2 = dm.astype(F32), x1.astype(F32), x2.astype(F32)
        dg1 = dm * av.astype(F32) * x1 * (1.0 - x1)
        dg2 = dm * bv.astype(F32) * x2 * (1.0 - x2)
        dgp = jnp.concatenate([dg1, dg2], axis=1)
        return dm * x1, dm * x2, dgp, jnp.sum(dgp, axis=0, keepdims=True)

    return _ew(fn, [(dmerged, _rows(tr, D)), (gates, g1), (gates, g2), (a, _rows(tr, D)), (bm, _rows(tr, D))],
               [((S, D), BF16, _rows(tr, D), False), ((S, D), BF16, _rows(tr, D), False),
                ((S, 2 * D), BF16, _rows(tr, 2 * D), False), ((1, 2 * D), F32, _bcast(2 * D), True)],
               (S // tr,), name)


def _ple_bwd(dh, pg, pe, name):
    S, D = dh.shape
    tr = _row_tile(S, D * 4)

    def fn(d, g, e):
        g, e = g.astype(F32), e.astype(F32)
        return d * g, d * e * g * (1.0 - g)

    spec = _rows(tr, D)
    return _ew(fn, [(dh, spec), (pg, spec), (pe, spec)],
               [((S, D), BF16, spec, False), ((S, D), BF16, spec, False)], (S // tr,), name)


def _position():
    x, y, c = lax.axis_index("x"), lax.axis_index("y"), lax.axis_index("c")
    chips = [(1 - x, y), (x, 1 - y), (1 - x, 1 - y)]
    return x, y, c, chips


def _remote(src, dst, send_sem, recv_sem, target):
    return pltpu.make_async_remote_copy(src_ref=src, dst_ref=dst, send_sem=send_sem, recv_sem=recv_sem,
                                        device_id=target, device_id_type=MESH)


HBM = pl.BlockSpec(memory_space=pltpu.HBM)
SEM = pl.BlockSpec(memory_space=pltpu.SEMAPHORE)
EFFECT = pltpu.SideEffectType.DATAFLOW_SIDE_EFFECTING


def _copies_start(plan, arrays, sem_shape, after, name):
    n = len(arrays)

    def body(*refs):
        send_sems, recv_sems, token = refs[n + 1], refs[n + 2], refs[-1]
        for send, _ in plan(refs[:n], send_sems, recv_sems):
            send.start()
        token[...] = jnp.zeros_like(token)

    outs = pl.pallas_call(
        body, name=name,
        out_shape=(pltpu.SemaphoreType.DMA(sem_shape), pltpu.SemaphoreType.DMA(sem_shape),
                   *[pltpu.HBM(a.shape, a.dtype) for a in arrays], jax.ShapeDtypeStruct((8, LANES), F32)),
        in_specs=[HBM] * n + [ANY],
        out_specs=(SEM, SEM, *[HBM] * n, pl.BlockSpec(memory_space=pltpu.VMEM)),
        input_output_aliases={a: 2 + a for a in range(n)},
        compiler_params=pltpu.CompilerParams(has_side_effects=EFFECT),
    )(*[pltpu.with_memory_space_constraint(a, pltpu.HBM) for a in arrays], after)
    return outs[0], outs[1], list(outs[2:2 + n]), outs[-1]


def _copies_wait(plan, send_sems, recv_sems, arrays, after, name):
    n = len(arrays)

    def body(*refs):
        for send, recv in plan(refs[:n], refs[n], refs[n + 1]):
            send.wait_send()
            recv.wait_recv()

    return list(pl.pallas_call(
        body, name=name,
        out_shape=[pltpu.HBM(a.shape, a.dtype) for a in arrays],
        in_specs=[HBM] * n + [SEM, SEM, ANY], out_specs=[HBM] * n,
        input_output_aliases={a: a for a in range(n)},
        compiler_params=pltpu.CompilerParams(has_side_effects=EFFECT),
    )(*arrays, send_sems, recv_sems, after))


def _gather_ici_plan(refs, send_sems, recv_sems):
    x, y, c, chips = _position()
    plan = []
    for a, ref in enumerate(refs):
        rh = ref.shape[1] // 2
        mine = ref.at[2 * x + y, pl.ds(c * rh, rh)]
        for j, (cx, cy) in enumerate(chips):
            landed = ref.at[2 * cx + cy, pl.ds(c * rh, rh)]
            plan.append((_remote(mine, mine, send_sems.at[3 * a + j], recv_sems.at[3 * a + j], (cx, cy, c)),
                         _remote(landed, landed, send_sems.at[3 * a + j], recv_sems.at[3 * a + j], (cx, cy, c))))
    return plan


def _gather_d2d_plan(refs, send_sems, recv_sems):
    x, y, c, chips = _position()
    sibling = (x, y, 1 - c)
    plan = []
    for a, ref in enumerate(refs):
        rh = ref.shape[1] // 2
        for j, (cx, cy) in enumerate(chips):
            landed = ref.at[2 * cx + cy, pl.ds(c * rh, rh)]
            theirs = ref.at[2 * cx + cy, pl.ds((1 - c) * rh, rh)]
            plan.append((_remote(landed, landed, send_sems.at[3 * a + j], recv_sems.at[3 * a + j], sibling),
                         _remote(theirs, theirs, send_sems.at[3 * a + j], recv_sems.at[3 * a + j], sibling)))
    return plan


def _scatter_ici_plan(refs, send_sems, recv_sems):
    x, y, c, chips = _position()
    n = len(refs) // 2
    plan = []
    for a in range(n):
        for j, (cx, cy) in enumerate(chips):
            cp = _remote(refs[a].at[2 * cx + cy], refs[n + a].at[j], send_sems.at[3 * a + j], recv_sems.at[3 * a + j],
                         (cx, cy, c))
            plan.append((cp, cp))
    return plan


def _place_shard(w, layer, me, after, name):
    _, r, cc = w.shape
    tr = _row_tile(r, cc * 4)

    def body(me_ref, w_ref, *rest):
        rest[-1][...] = w_ref[...].astype(rest[-1].dtype)

    return pl.pallas_call(
        body, name=name,
        grid_spec=pltpu.PrefetchScalarGridSpec(
            num_scalar_prefetch=1, grid=(r // tr,),
            in_specs=[pl.BlockSpec((None, tr, cc), lambda i, me_ref: (layer, i, 0))] + [ANY] * len(after),
            out_specs=pl.BlockSpec((None, tr, cc), lambda i, me_ref: (me_ref[0], i, 0))),
        out_shape=jax.ShapeDtypeStruct((N_CHIPS, r, cc), BF16),
        compiler_params=_params(("parallel",)),
    )(me, w, *after)


def _pair_plan(refs, send_sems, recv_sems):
    x, y, c, _ = _position()
    n = len(refs) // 2
    plan = []
    for a in range(n):
        rh = refs[a].shape[1] // 2
        cp = _remote(refs[a].at[:, pl.ds((1 - c) * rh, rh)], refs[n + a], send_sems.at[a], recv_sems.at[a],
                     (x, y, 1 - c))
        plan.append((cp, cp))
    return plan


def _pair_sum(mine, theirs, c, name):
    nch, rh, cc = theirs.shape
    tr = _row_tile(rh, cc * 4)
    nb = rh // tr

    def body(c_ref, m_ref, t_ref, o_ref):
        o_ref[...] = (m_ref[...].astype(F32) + t_ref[...].astype(F32)).astype(o_ref.dtype)

    return pl.pallas_call(
        body, name=name,
        grid_spec=pltpu.PrefetchScalarGridSpec(
            num_scalar_prefetch=1, grid=(nch, nb),
            in_specs=[pl.BlockSpec((1, tr, cc), lambda k, i, c_ref: (k, c_ref[0] * nb + i, 0)),
                      pl.BlockSpec((1, tr, cc), lambda k, i, c_ref: (k, i, 0))],
            out_specs=pl.BlockSpec((1, tr, cc), lambda k, i, c_ref: (k, i, 0))),
        out_shape=jax.ShapeDtypeStruct(theirs.shape, BF16),
        compiler_params=_params(("parallel", "parallel")),
    )(c, mine, theirs)


def _chip_sum(own, others, me, c, total, layer, depth, name):
    _, rh, cc = own.shape
    tr = _row_tile(rh, cc * 4)
    nb = rh // tr
    chained = total is not None

    def body(me_ref, c_ref, o_ref, r_ref, *rest):
        g_ref = rest[-1]
        g_ref[...] = (o_ref[0].astype(F32) + r_ref[0].astype(F32)) + (r_ref[1].astype(F32) + r_ref[2].astype(F32))

    in_specs = [pl.BlockSpec((1, tr, cc), lambda i, me_ref, c_ref: (me_ref[0], i, 0)),
                pl.BlockSpec((3, tr, cc), lambda i, me_ref, c_ref: (0, i, 0))]
    args = [me, c, own, others]
    if chained:
        in_specs.append(ANY)
        args.append(total)
    return pl.pallas_call(
        body, name=name,
        grid_spec=pltpu.PrefetchScalarGridSpec(
            num_scalar_prefetch=2, grid=(nb,), in_specs=in_specs,
            out_specs=pl.BlockSpec((None, tr, cc), lambda i, me_ref, c_ref: (layer, c_ref[0] * nb + i, 0))),
        out_shape=jax.ShapeDtypeStruct((depth, 2 * rh, cc), F32),
        input_output_aliases={4: 0} if chained else {},
        compiler_params=_params(("parallel",)),
    )(*args)


def _half_exchange(totals, layer, name):
    n = len(totals)

    def body(*refs):
        outs = refs[n:2 * n]
        send_sems, recv_sems = refs[2 * n:]
        x, y, c, _ = _position()
        copies = []
        for a in range(n):
            rh = outs[a].shape[1] // 2
            mine = outs[a].at[layer, pl.ds(c * rh, rh)]
            cp = _remote(mine, mine, send_sems.at[a], recv_sems.at[a], (x, y, 1 - c))
            cp.start()
            copies.append(cp)
        for a, cp in enumerate(copies):
            rh = outs[a].shape[1] // 2
            theirs = outs[a].at[layer, pl.ds((1 - c) * rh, rh)]
            cp.wait_send()
            _remote(theirs, theirs, send_sems.at[a], recv_sems.at[a], (x, y, 1 - c)).wait_recv()

    return pl.pallas_call(
        body, name=name, in_specs=[ANY] * n, out_specs=[ANY] * n,
        out_shape=[jax.ShapeDtypeStruct(t.shape, t.dtype) for t in totals],
        input_output_aliases={a: a for a in range(n)},
        scratch_shapes=[pltpu.SemaphoreType.DMA((n,)), pltpu.SemaphoreType.DMA((n,))],
    )(*totals)


def _allgather_devices(v, name):
    m_per, n = v.shape

    def body(x_ref, out_ref, send_sems, recv_sems, local_sem):
        x, y, c, chips = _position()
        me, sibling = (x, y, c), (x, y, 1 - c)

        def rows(px, py, pc):
            return out_ref.at[pl.ds((4 * px + 2 * py + pc) * m_per, m_per), :]

        def copy(k, block, to, src=None):
            return _remote(rows(*block) if src is None else src, rows(*block), send_sems.at[k], recv_sems.at[k], to)

        mine = pltpu.make_async_copy(x_ref, rows(*me), local_sem)
        mine.start()
        first = [copy(0, me, sibling, src=x_ref)]
        first += [copy(1 + j, me, (*chip, c), src=x_ref) for j, chip in enumerate(chips)]
        for cp in first:
            cp.start()
        passed = [copy(4 + j, (*chip, c), sibling) for j, chip in enumerate(chips)]
        for j, chip in enumerate(chips):
            copy(1 + j, (*chip, c), me).wait_recv()
            passed[j].start()
        copy(0, sibling, me).wait_recv()
        for j, chip in enumerate(chips):
            copy(4 + j, (*chip, 1 - c), me).wait_recv()
        for cp in first + passed:
            cp.wait_send()
        mine.wait()

    vm = pl.BlockSpec(memory_space=pltpu.VMEM)
    return pl.pallas_call(
        body, name=name, in_specs=[vm], out_specs=vm,
        out_shape=jax.ShapeDtypeStruct((8 * m_per, n), v.dtype),
        scratch_shapes=[pltpu.SemaphoreType.DMA((7,)), pltpu.SemaphoreType.DMA((7,)), pltpu.SemaphoreType.DMA],
    )(v)


def _adamw_math(w, g, m, v):
    m = ADAM_B1 * m + (1.0 - ADAM_B1) * g
    v = ADAM_B2 * v + (1.0 - ADAM_B2) * (g * g)
    m_hat = m / (1.0 - ADAM_B1 ** ADAM_STEP)
    v_hat = v / (1.0 - ADAM_B2 ** ADAM_STEP)
    delta = -ADAM_LR * (m_hat / (jnp.sqrt(v_hat) + ADAM_EPS) + ADAM_WD * w)
    return delta, m, v


def _adamw(w, g, m, v, lo, hi, prev, after, name):
    depth, r, cc = w.shape
    tr = _row_tile(r, cc * 4 * 2)
    spec = pl.BlockSpec((1, tr, cc), lambda l, i: (lo + l, i, 0))

    def body(w_ref, g_ref, m_ref, v_ref, *rest):
        outs = rest[-4:]
        gv = g_ref[...]
        for o_ref, val in zip(outs, (gv,) + _adamw_math(w_ref[...], gv, m_ref[...], v_ref[...])):
            o_ref[...] = val

    prev = list(prev) if prev is not None else []
    return tuple(pl.pallas_call(
        body, name=name, grid=(hi - lo, r // tr),
        in_specs=[spec] * 4 + [ANY] * (1 + len(prev)), out_specs=[spec] * 4,
        out_shape=[jax.ShapeDtypeStruct(w.shape, F32)] * 4,
        input_output_aliases={5 + k: k for k in range(len(prev))},
        compiler_params=_params(("parallel", "parallel")),
    )(w, g, m, v, after, *prev))


def _adamw_small(w, parts, m, v, name):
    M = w.shape[0]

    def body(w_ref, p_ref, m_ref, v_ref, g_ref, d_ref, nm_ref, nv_ref):
        g = p_ref[pl.ds(0, M), :]
        for k in range(1, 8):
            g = g + p_ref[pl.ds(k * M, M), :]
        d, nm, nv = _adamw_math(w_ref[...], g, m_ref[...], v_ref[...])
        g_ref[...] = g
        d_ref[...] = d
        nm_ref[...] = nm
        nv_ref[...] = nv

    vm = pl.BlockSpec(memory_space=pltpu.VMEM)
    return pl.pallas_call(
        body, name=name, in_specs=[vm] * 4, out_specs=[vm] * 4,
        out_shape=[jax.ShapeDtypeStruct(w.shape, F32)] * 4,
    )(w, parts, m, v)


def _layer_fwd(h0, p_l, W, small, tabs, after_mlp):
    S, D = h0.shape
    ctab, stab = tabs
    u = _rms_fwd(h0, small["g_mix"], "rms_mix")
    z = _mm(u, W["w_qkv"], mode="nn", name="mm_qkv", b_cols=3 * ATTN_WIDTH)
    f = _mm(u, W["w_f"], mode="nn", name="mm_f", out_dtypes=(F32,))
    gates = _mm(u, W["w_gate"], mode="nn", name="mm_gate", b_chunked=True, extras=[(small["b_gate"], "row")],
                epilogue=lambda acc, b: (_sigmoid(acc + b),))
    qr, kr = _rope(z, 0, z, N_HEADS, ctab, stab, 1.0, "rope_fwd")
    qs, ks, vs = _to_strided(qr), _to_strided(kr), _to_strided(z[:, 2 * ATTN_WIDTH:2 * ATTN_WIDTH + DIL_WIDTH])
    o_s, lse_s = _sw_fwd(qs, ks, vs, "sw_fwd")
    o_g, lse_g = _from_strided(o_s, S), _from_strided(lse_s, S)
    ya = _mix_fwd(o_g, lse_g, "mix_fwd")
    a = _mm(ya, W["w_br_a"], mode="nn", name="mm_br_a", b_chunked=True)
    cum = _fox_prep(f, small["b_f"], "fox_prep")
    cq = cum[:, :N_FOX_HEADS].T[:, :, None]
    ck = cum[:, :N_FOX_HEADS].T[:, None, :]
    yb, lse_f = _fox_fwd(z, cq, ck, "fox_fwd")
    bm = _mm(yb, W["w_br_b"], mode="nn", name="mm_br_b", b_chunked=True)
    merged = _merge_fwd(gates, a, bm, "merge_fwd")
    h1 = _mm(merged, W["w_o"], mode="nn", name="mm_o", out_dtypes=(F32,), extras=[(h0, "tile")],
             epilogue=lambda acc, r: (acc + r,), tj=512)
    m = _rms_fwd(h1, small["g_mlp"], "rms_mlp")
    ra, act = _mm(m, W["w_up"], mode="nn", name="mm_up", b_chunked=True, out_dtypes=(BF16, BF16),
                  epilogue=lambda acc: (jnp.maximum(acc, 0.0), jnp.square(jnp.maximum(acc, 0.0))))
    h2 = _mm(act, W["w_down"], mode="nn", name="mm_down", out_dtypes=(F32,), extras=[(h1, "tile")],
             epilogue=lambda acc, r: (acc + r,), ti=1024, tj=512, tc=4096)
    token = after_mlp(h2)
    n = _rms_fwd(h2, small["g_ple"] if token is None else small["g_ple"] + token[0, 0], "rms_ple")
    pg = _mm(n, W["w_ple_gate"], mode="nn", name="mm_ple_gate", epilogue=lambda acc: (_sigmoid(acc),))
    h3, pe = _mm(p_l, W["w_ple"], mode="nn", name="mm_ple", b_chunked=True, out_dtypes=(F32, BF16),
                 extras=[(h2, "tile"), (pg, "tile")], tj=256,
                 epilogue=lambda acc, r, g: (r + g.astype(F32) * acc, acc))
    saved = dict(h0=h0, u=u, z=z, f=f, gates=gates, qs=qs, ks=ks, vs=vs, lse_s=lse_s, o_g=o_g, lse_g=lse_g,
                 ya=ya, a=a, cq=cq, ck=ck, yb=yb, lse_f=lse_f, bm=bm, merged=merged, h1=h1, m=m, ra=ra,
                 act=act, h2=h2, n=n, pg=pg, pe=pe, p_l=p_l)
    return h3, saved


def _after(hooks, name, small_value, *args):
    token = hooks[name](*args) if name in hooks else None
    return small_value if token is None else small_value + token[0, 0]


def _layer_bwd(dh3, sv, W, small, tabs, hooks):
    S, D = dh3.shape
    ctab, stab = tabs
    gw, gs = {}, {}
    tn = functools.partial(_mm, mode="tn", ti=1024, tj=2048)
    dpe, dpg = _ple_bwd(dh3, sv["pg"], sv["pe"], "ple_bwd")
    gw["w_ple"] = tn(sv["p_l"], dpe, name="dw_ple", out_chunks=N_CHIPS)
    gw["w_ple_gate"] = tn(sv["n"], dpg, name="dw_ple_gate").reshape(N_CHIPS, D // N_CHIPS, D)
    dn = _mm(dpg, W["w_ple_gate"], mode="nt", name="mm_dn")
    dh2, dh2b, gs["g_ple"] = _rms_bwd(sv["h2"], small["g_ple"], dn, dh3, "rms_ple_bwd")
    da = _mm(dh2b, W["w_down"], mode="nt", name="mm_dact", extras=[(sv["ra"], "tile")],
             epilogue=lambda acc, r: (acc * (2.0 * r.astype(F32)),))
    FF = da.shape[1]
    g_mlp = _after(hooks, "mlp_grad", small["g_mlp"], da)
    gw["w_down"] = tn(sv["act"], dh2b, name="dw_down").reshape(N_CHIPS, FF // N_CHIPS, D)
    gw["w_up"] = tn(sv["m"], da, name="dw_up", out_chunks=N_CHIPS)
    g_mlp = _after(hooks, "mlp_weights", g_mlp, dict(gw))
    dm = _mm(da, W["w_up"], mode="nt", name="mm_dm", b_chunked=True)
    dh1, dh1b, gs["g_mlp"] = _rms_bwd(sv["h1"], g_mlp, dm, dh2, "rms_mlp_bwd")
    dmerged = _mm(dh1b, W["w_o"], mode="nt", name="mm_dmerged")
    b_f = _after(hooks, "merge_grad", small["b_f"], dmerged)
    gw["w_o"] = tn(sv["merged"], dh1b, name="dw_o").reshape(N_CHIPS, D // N_CHIPS, D)
    d_a, d_b, dgp, gs["b_gate"] = _merge_bwd(dmerged, sv["gates"], sv["a"], sv["bm"], "merge_bwd")
    gw["w_gate"] = tn(sv["u"], dgp, name="dw_gate", out_chunks=N_CHIPS)
    gw["w_br_a"] = tn(sv["ya"], d_a, name="dw_br_a", out_chunks=N_CHIPS, tj=512)
    gw["w_br_b"] = tn(sv["yb"], d_b, name="dw_br_b", out_chunks=N_CHIPS, tj=512)
    dya = _mm(d_a, W["w_br_a"], mode="nt", name="mm_dya", b_chunked=True)
    dyb = _mm(d_b, W["w_br_b"], mode="nt", name="mm_dyb", b_chunked=True)
    z = sv["z"]
    dq_f, dk_f, dv_f, dck = _fox_bwd(z, sv["cq"], sv["ck"], sv["lse_f"], sv["yb"], dyb, "fox_bwd")
    dc = jnp.pad(dck[:, 0, :].T, ((0, 0), (0, LANES - N_FOX_HEADS)))
    df, dbf = _fox_prep_bwd(dc, sv["f"], b_f, "fox_prep_bwd")
    gs["b_f"] = dbf[:, :N_FOX_HEADS]
    lane = jnp.arange(LANES)[None, :] < N_FOX_HEADS
    dzf = jnp.where(lane, df, 0.0).astype(BF16)
    do_g, tt_g = _mix_bwd(dya, sv["ya"], sv["o_g"], sv["lse_g"], "mix_bwd")
    do_s = _to_strided(do_g.transpose(1, 0, 2).reshape(S, DIL_WIDTH))
    tt_s = _to_strided(tt_g.transpose(1, 0, 2).reshape(S, DIL_WIDTH))
    dq_s, dk_s, dv_s = _sw_bwd(sv["qs"], sv["ks"], sv["vs"], do_s, sv["lse_s"], tt_s, "sw_bwd")
    unstride = lambda t: _from_strided(t, S).transpose(1, 0, 2).reshape(S, DIL_WIDTH)
    dq_a, dk_a = _rope(unstride(dq_s), 0, unstride(dk_s), 0, ctab, stab, -1.0, "rope_bwd")
    dz = jnp.concatenate([dq_a, dq_f, dk_a, dk_f, unstride(dv_s), dv_f], axis=1)
    g_qkv = tn(sv["u"], dz, name="dw_qkv")
    g_f = tn(sv["u"], dzf, name="dw_f", tj=128)
    cols = W["w_in_cols"]
    g_in = jnp.concatenate([g_qkv, g_f[:, :N_FOX_HEADS]], axis=1)
    gw["w_in"] = jnp.stack([g_in[:, k * cols:(k + 1) * cols] for k in range(N_CHIPS)])
    du = _mm(dzf, W["w_f"], mode="nt", name="mm_du_f", out_dtypes=(F32,))
    du = _mm(dgp, W["w_gate"], mode="nt", name="mm_du_gate", b_chunked=True, out_dtypes=(F32,),
             extras=[(du, "tile")], epilogue=lambda acc, r: (acc + r,), tj=512)
    du = _mm(dz, W["w_qkv"], mode="nt", name="mm_du_qkv", b_cols=3 * ATTN_WIDTH, extras=[(du, "tile")],
             epilogue=lambda acc, r: (acc + r,), tj=512)
    dh0, _, gs["g_mix"] = _rms_bwd(sv["h0"], small["g_mix"], du, dh1, "rms_mix_bwd")
    return dh0, gw, gs


BIG = ("w_in", "w_gate", "w_br_a", "w_br_b", "w_o", "w_up", "w_down", "w_ple", "w_ple_gate")
EARLY = ("w_ple", "w_ple_gate", "w_down", "w_up")
SMALL = ("g_mix", "b_f", "b_gate", "g_mlp", "g_ple", "g_final")
ORDER = ("g_mix", "w_in", "b_f", "w_gate", "b_gate", "w_br_a", "w_br_b", "w_o", "g_mlp", "w_up", "w_down",
         "g_ple", "w_ple", "w_ple_gate", "g_final")


def _gathered_layer_weights(full, D):
    W = {}
    for name in ("w_gate", "w_br_a", "w_br_b", "w_up", "w_ple"):
        W[name] = full[name]
    for name in ("w_o", "w_down", "w_ple_gate"):
        t = full[name]
        W[name] = t.reshape(t.shape[0] * t.shape[1], t.shape[2])
    w_in = full["w_in"]
    cols = w_in.shape[2]
    w_in = jnp.concatenate([w_in[k] for k in range(N_CHIPS)], axis=1)
    W["w_qkv"] = w_in
    W["w_f"] = jnp.pad(w_in[:, 3 * ATTN_WIDTH:], ((0, 0), (0, LANES - N_FOX_HEADS)))
    W["w_in_cols"] = cols
    return W


def _pack_rows(vals):
    flat = jnp.concatenate([v.reshape(-1) for v in vals])
    rows = -(-flat.shape[0] // (8 * LANES)) * 8
    return jnp.pad(flat, (0, rows * LANES - flat.shape[0])).reshape(rows, LANES)


def _unpack_rows(packed, shapes):
    flat = packed.reshape(-1)
    out, pos = [], 0
    for s in shapes:
        size = 1
        for dim in s:
            size *= dim
        out.append(flat[pos:pos + size].reshape(s))
        pos += size
    return out


def _local_step(x, p, small_w, comm, loss_target):
    depth = p.shape[0]
    S, D = x.shape
    tabs = _rope_tables(S)
    h = x
    saved, weights, smalls = [], [], []
    state, _ = comm["gather_start"](0, x)
    full = comm["gather_finish"](comm["gather_mid"](state, x)[0], x)
    for l in range(depth):
        nxt = [None]
        g_mix = small_w["g_mix"][l][None]
        if l + 1 < depth:
            nxt[0], token = comm["gather_start"](l + 1, full["w_ple"])
            g_mix = g_mix + token[0, 0]

        def after_mlp(h2):
            if nxt[0] is None:
                return None
            nxt[0], token = comm["gather_mid"](nxt[0], h2)
            return token

        W = _gathered_layer_weights(full, D)
        sm = dict(g_mix=g_mix, g_mlp=small_w["g_mlp"][l][None],
                  g_ple=small_w["g_ple"][l][None], b_gate=small_w["b_gate"][l][None],
                  b_f=jnp.pad(small_w["b_f"][l][None], ((0, 0), (0, LANES - N_FOX_HEADS))))
        h, sv = _layer_fwd(h, p[l].astype(BF16), W, sm, tabs, after_mlp)
        if nxt[0] is not None:
            full = comm["gather_finish"](nxt[0], h)
        saved.append(sv)
        weights.append(W)
        smalls.append(sm)
    dh, loss_row, dg_final = _loss_head(h, small_w["g_final"][None], loss_target, "loss_head")
    gss = [None] * depth
    pending, token = [None], None
    for l in reversed(range(depth)):
        sm = smalls[l]
        if token is not None:
            sm = {**sm, "g_ple": sm["g_ple"] + token[0, 0]}

        def after_mlp_grad(da):
            if pending[0] is None:
                return None
            pending[0] = comm["reduce_mid"](pending[0], da)
            return pending[0][-1]

        hooks = dict(mlp_grad=after_mlp_grad)
        if l == 0 and "reduce_early_begin" in comm:
            hooks.update(mlp_weights=comm["reduce_early_begin"], merge_grad=comm["reduce_early_mid"])
        dh, gw, gss[l] = _layer_bwd(dh, saved[l], weights[l], sm, tabs, hooks)
        if pending[0] is not None:
            comm["reduce_end"](pending[0], dh)
        if l > 0:
            pending[0], token = comm["reduce_begin"](l, gw, dh)
    return loss_row, dh, gss, dg_final, gw


def _device_comm(w, m, v, depth, c_arr, me_arr):
    n = len(BIG)
    totals = {name: None for name in BIG}
    placed, results = [], {}

    def gather_start(l, after):
        if l == 0:
            placed.append([_place_shard(w[name], 0, me_arr, [me_arr], "place_shard") for name in BIG])
        send, recv, bufs, token = _copies_start(_gather_ici_plan, placed[l], (3 * n,), after, f"gather_ici_start_{l}")
        if l == 0:
            for k in range(1, depth):
                placed.append([_place_shard(w[name], k, me_arr,
                                            [token, m["w_in"], v["w_in"]] if (k, name) == (1, BIG[0]) else [token],
                                            "place_shard") for name in BIG])
        return (l, send, recv, bufs), token

    def gather_mid(state, after):
        l, send, recv, bufs = state
        if l == 0:
            after = placed[-1][-1]
        bufs = _copies_wait(_gather_ici_plan, send, recv, bufs, after, f"gather_ici_wait_{l}")
        send, recv, bufs, token = _copies_start(_gather_d2d_plan, bufs, (3 * n,), after, f"gather_d2d_start_{l}")
        return (l, send, recv, bufs), token

    def gather_finish(state, after):
        l, send, recv, bufs = state
        return dict(zip(BIG, _copies_wait(_gather_d2d_plan, send, recv, bufs, after, f"gather_d2d_wait_{l}")))

    def reduce_begin(l, gw, after, names=BIG, tag=""):
        grads = [gw[name] for name in names]
        landing = [lax.empty((g.shape[0], g.shape[1] // 2, g.shape[2]), g.dtype) for g in grads]
        send, recv, arrays, token = _copies_start(_pair_plan, grads + landing, (len(names),), after,
                                                  f"pair_start_{l}{tag}")
        return (l, tag, names, send, recv, arrays), token

    def reduce_mid(state, after):
        l, tag, names, send, recv, arrays = state
        k = len(names)
        arrays = _copies_wait(_pair_plan, send, recv, arrays, after, f"pair_wait_{l}{tag}")
        sums = [_pair_sum(g, t, c_arr, "rs_pair_sum") for g, t in zip(arrays[:k], arrays[k:])]
        landing = [lax.empty((3,) + s.shape[1:], s.dtype) for s in sums]
        send, recv, arrays, token = _copies_start(_scatter_ici_plan, sums + landing, (3 * k,), sums[0],
                                                  f"scatter_ici_start_{l}{tag}")
        return l, tag, names, send, recv, arrays, token

    halves = [None]

    def finish_halves(after):
        if halves[0] is not None:
            plan, names, send, recv, arrays, label = halves[0]
            totals.update(zip(names, _copies_wait(plan, send, recv, arrays, after, f"half_wait_{label}")))
            halves[0] = None

    def reduce_end(state, after):
        l, tag, names, send, recv, arrays, _ = state
        k = len(names)
        finish_halves(after)
        arrays = _copies_wait(_scatter_ici_plan, send, recv, arrays, after, f"scatter_ici_wait_{l}{tag}")
        done = [_chip_sum(s, o, me_arr, c_arr, totals[name], l, depth, "rs_chip_sum")
                for name, s, o in zip(names, arrays[:k], arrays[k:])]

        def plan(refs, send_sems, recv_sems):
            x, y, c, _ = _position()
            copies = []
            for a, ref in enumerate(refs):
                rh = ref.shape[1] // 2
                mine, theirs = ref.at[l, pl.ds(c * rh, rh)], ref.at[l, pl.ds((1 - c) * rh, rh)]
                copies.append((_remote(mine, mine, send_sems.at[a], recv_sems.at[a], (x, y, 1 - c)),
                               _remote(theirs, theirs, send_sems.at[a], recv_sems.at[a], (x, y, 1 - c))))
            return copies

        send, recv, arrays, _ = _copies_start(plan, done, (k,), done[0], f"half_start_{l}{tag}")
        halves[0] = (plan, names, send, recv, arrays, f"{l}{tag}")

    early = [None]

    def reduce_early_begin(gw):
        early[0], token = reduce_begin(0, gw, gw[EARLY[-1]], EARLY, "a")
        return token

    def reduce_early_mid(after):
        early[0] = reduce_mid(early[0], after)
        return early[0][-1]

    def reduce_last(gw, after):
        upper = {}

        def adamw_upper(names, token):
            for name in names:
                if depth > 1:
                    upper[name] = _adamw(w[name], totals[name], m[name], v[name], 1, depth, None, token,
                                         "adamw_upper")
                    token = upper[name][0]
            return token

        def adamw_first(names, token):
            for name in names:
                results[name] = _adamw(w[name], totals[name], m[name], v[name], 0, 1, upper.get(name), token,
                                       "adamw_first")
                token = results[name][0]
            return token

        late = BIG if early[0] is None else tuple(name for name in BIG if name not in EARLY)
        state, token = reduce_begin(0, gw, after, late, "b")
        finish_halves(token)
        state = reduce_mid(state, adamw_upper(("w_up",), token))
        token = adamw_upper([name for name in BIG if name != "w_up"], state[-1])
        if early[0] is not None:
            reduce_end(early[0], token)
            finish_halves(token)
            token = adamw_first(EARLY, totals[EARLY[0]])
        reduce_end(state, token)
        finish_halves(token)
        adamw_first(late, totals[late[0]])

    comm = dict(gather_start=gather_start, gather_mid=gather_mid, gather_finish=gather_finish,
                reduce_begin=reduce_begin, reduce_mid=reduce_mid, reduce_end=reduce_end, reduce_last=reduce_last,
                reduce_early_begin=reduce_early_begin, reduce_early_mid=reduce_early_mid)
    return comm, results


def kernel(x, p, g_mix, w_in, b_f, w_gate, b_gate, w_br_a, w_br_b, w_o, g_mlp, w_up, w_down, g_ple, w_ple, w_ple_gate, g_final, loss_target, m_g_mix, m_w_in, m_b_f, m_w_gate, m_b_gate, m_w_br_a, m_w_br_b, m_w_o, m_g_mlp, m_w_up, m_w_down, m_g_ple, m_w_ple, m_w_ple_gate, m_g_final, v_g_mix, v_w_in, v_b_f, v_w_gate, v_b_gate, v_w_br_a, v_w_br_b, v_w_o, v_g_mlp, v_w_up, v_w_down, v_g_ple, v_w_ple, v_w_ple_gate, v_g_final):
    w = dict(g_mix=g_mix, w_in=w_in, b_f=b_f, w_gate=w_gate, b_gate=b_gate, w_br_a=w_br_a, w_br_b=w_br_b,
             w_o=w_o, g_mlp=g_mlp, w_up=w_up, w_down=w_down, g_ple=g_ple, w_ple=w_ple, w_ple_gate=w_ple_gate,
             g_final=g_final)
    m = dict(g_mix=m_g_mix, w_in=m_w_in, b_f=m_b_f, w_gate=m_w_gate, b_gate=m_b_gate, w_br_a=m_w_br_a,
             w_br_b=m_w_br_b, w_o=m_w_o, g_mlp=m_g_mlp, w_up=m_w_up, w_down=m_w_down, g_ple=m_g_ple,
             w_ple=m_w_ple, w_ple_gate=m_w_ple_gate, g_final=m_g_final)
    v = dict(g_mix=v_g_mix, w_in=v_w_in, b_f=v_b_f, w_gate=v_w_gate, b_gate=v_b_gate, w_br_a=v_w_br_a,
             w_br_b=v_w_br_b, w_o=v_w_o, g_mlp=v_g_mlp, w_up=v_w_up, w_down=v_w_down, g_ple=v_g_ple,
             w_ple=v_w_ple, w_ple_gate=v_w_ple_gate, g_final=v_g_final)
    depth = p.shape[0]
    cx, cy, cc = lax.axis_index("x"), lax.axis_index("y"), lax.axis_index("c")
    c_arr = jnp.reshape(cc, (1,)).astype(jnp.int32)
    me_arr = jnp.reshape(2 * cx + cy, (1,)).astype(jnp.int32)

    comm, big_out = _device_comm(w, m, v, depth, c_arr, me_arr)
    loss_row, grad_x, gss, dg_final, gw0 = _local_step(x[0], p[:, 0], w, comm, loss_target[0])

    small_grads = [jnp.stack([gss[l][n][0] for l in range(depth)]) for n in SMALL[:-1]] + [dg_final[0]]
    shapes = [w[n].shape for n in SMALL]
    parts = _allgather_devices(_pack_rows(small_grads), "allgather_small")
    packed = _adamw_small(_pack_rows([w[n] for n in SMALL]), parts, _pack_rows([m[n] for n in SMALL]),
                          _pack_rows([v[n] for n in SMALL]), "adamw_small")
    small_out = {n: vals for n, vals in zip(SMALL, zip(*[_unpack_rows(t, shapes) for t in packed]))}
    comm["reduce_last"](gw0, packed[0])

    loss = lax.psum(loss_row[0, 0], ("x", "y", "c"))
    out = {**big_out, **small_out}
    return (loss, grad_x[None], *[out[n][0] for n in ORDER], *[out[n][1] for n in ORDER],
            *[out[n][2] for n in ORDER], *[out[n][3] for n in ORDER])
```
